```python
import numpy as np
import jax
import jax.numpy as jnp
from jax import lax


D_MODEL = 4096
BATCH = 4
SEQ = 2048
DEPTH = 2

NSA_HEADS = 16
NSA_KV_HEADS = 4
NSA_QK_DIM = 192
NSA_V_DIM = 128
CMP_BLOCK = 32
CMP_STRIDE = 16
SLC_BLOCK = 64
SLC_TOPN = 8
NSA_WINDOW = 512
SLC_Q_BLOCK = 64
MLA_HEADS = 16
MLA_Q_RANK = 1024
MLA_KV_RANK = 512
MLA_NOPE_DIM = 128
MLA_ROPE_DIM = 64
MLA_V_DIM = 128
ROPE_THETA = 10000.0
SWA_HEADS = 64
SWA_KV_HEADS = 8
SWA_HEAD_DIM = 64
SWA_WINDOW = 128
N_EXPERTS = 32
TOP_K = 4
EXPERT_FF = 768
SWIGLU_ALPHA = 1.702
SWIGLU_LIMIT = 7.0
MOE_ROW_BLOCK = 128
Q_BLOCK = 128
EPS = 1e-6

EVEN_IN_SPLITS = (NSA_HEADS * NSA_QK_DIM,
                  NSA_KV_HEADS * NSA_QK_DIM, NSA_KV_HEADS * NSA_V_DIM,
                  NSA_KV_HEADS * NSA_QK_DIM, NSA_KV_HEADS * NSA_V_DIM,
                  NSA_KV_HEADS * NSA_QK_DIM, NSA_KV_HEADS * NSA_V_DIM,
                  3 * NSA_HEADS, MLA_Q_RANK, MLA_KV_RANK + MLA_ROPE_DIM)
EVEN_IN_W = sum(EVEN_IN_SPLITS)
EVEN_OUT_W = NSA_HEADS * NSA_V_DIM + MLA_HEADS * MLA_V_DIM
ODD_IN_SPLITS = (SWA_HEADS * SWA_HEAD_DIM, SWA_KV_HEADS * SWA_HEAD_DIM, SWA_KV_HEADS * SWA_HEAD_DIM)
ODD_IN_W = sum(ODD_IN_SPLITS)
N_EVEN = (DEPTH + 1) // 2
N_ODD = DEPTH // 2

kernel_name = 'hybrid_nsa_mla_swa_moe_adaln'


def split_cols(h, widths):
    return jnp.split(h, np.cumsum(widths)[:-1].tolist(), axis=-1)


def rms_norm(x, gain):
    xf = x.astype(jnp.float32)
    y = xf * lax.rsqrt(jnp.mean(xf * xf, axis=-1, keepdims=True) + EPS)
    return (y * gain.astype(jnp.float32)).astype(x.dtype)


def alibi_slopes(n):
    return jnp.exp2(-8.0 * jnp.arange(1, n + 1, dtype=jnp.float32) / n)


def masked_softmax(s, mask, sink=None):
    s = jnp.where(mask, s, -jnp.inf)
    m = jnp.max(s, axis=-1, keepdims=True)
    if sink is not None:
        m = jnp.maximum(m, sink)
    m = jnp.where(jnp.isfinite(m), m, 0.0)
    p = jnp.exp(s - m)
    denom = jnp.sum(p, axis=-1, keepdims=True)
    if sink is not None:
        denom = denom + jnp.exp(sink - m)
    return p / jnp.maximum(denom, 1e-30)


def rope(x, pos):
    half = x.shape[-1] // 2
    inv_freq = ROPE_THETA ** (-jnp.arange(half, dtype=jnp.float32) / half)
    ang = pos.astype(jnp.float32)[:, None] * inv_freq[None, :]
    cos = jnp.cos(ang)[None, :, None, :]
    sin = jnp.sin(ang)[None, :, None, :]
    xf = x.astype(jnp.float32)
    x1, x2 = xf[..., :half], xf[..., half:]
    return jnp.concatenate([x1 * cos - x2 * sin, x2 * cos + x1 * sin], axis=-1).astype(x.dtype)


def banded_attention(q, k, v, window, slopes, sinks):
    B, T, G, R, dq = q.shape
    dv = v.shape[-1]
    nb = T // Q_BLOCK
    span = Q_BLOCK + window
    scale = dq ** -0.5
    kp = jnp.pad(k, ((0, 0), (window, 0), (0, 0), (0, 0)))
    vp = jnp.pad(v, ((0, 0), (window, 0), (0, 0), (0, 0)))
    qb = q.reshape(B, nb, Q_BLOCK, G, R, dq).swapaxes(0, 1)
    r = jnp.arange(span)
    rel = jnp.arange(Q_BLOCK)[:, None] + window - r[None, :]
    in_band = (rel >= 0) & (rel < window)
    bias = -slopes.astype(jnp.float32)[:, :, None, None] * rel.astype(jnp.float32)
    sink = None if sinks is None else sinks.astype(jnp.float32)[:, :, None, None]

    def block(args):
        i, qi = args
        kb = lax.dynamic_slice_in_dim(kp, i * Q_BLOCK, span, axis=1)
        vb = lax.dynamic_slice_in_dim(vp, i * Q_BLOCK, span, axis=1)
        s = jnp.einsum('bqgrd,bkgd->bgrqk', qi, kb).astype(jnp.float32) * scale + bias
        mask = in_band & (i * Q_BLOCK - window + r >= 0)[None, :]
        p = masked_softmax(s, mask, sink)
        return jnp.einsum('bgrqk,bkgd->bqgrd', p.astype(vb.dtype), vb)

    out = lax.map(block, (jnp.arange(nb), qb))
    return out.swapaxes(0, 1).reshape(B, T, G, R, dv)


def causal_block_attention(q, k, v):
    B, T, H, d = q.shape
    nb = T // Q_BLOCK
    scale = d ** -0.5
    qb = q.reshape(B, nb, Q_BLOCK, H, d).swapaxes(0, 1)
    s_idx = jnp.arange(T)

    def block(args):
        i, qi = args
        t_idx = i * Q_BLOCK + jnp.arange(Q_BLOCK)
        s = jnp.einsum('bqhd,bkhd->bhqk', qi, k).astype(jnp.float32) * scale
        p = masked_softmax(s, s_idx[None, :] <= t_idx[:, None])
        return jnp.einsum('bhqk,bkhd->bqhd', p.astype(v.dtype), v)

    out = lax.map(block, (jnp.arange(nb), qb))
    return out.swapaxes(0, 1).reshape(B, T, H, v.shape[-1])


def compress_blocks(x, blk_idx, pe, w1, w2):
    B, T, G, d = x.shape
    n, l = blk_idx.shape
    blocks = x[:, blk_idx] + pe[:, None, :]
    flat = blocks.transpose(0, 1, 3, 2, 4).reshape(B, n, G, l * d)
    return jax.nn.silu(flat @ w1) @ w2


def selected_attention(q, k, v, sel, slopes):
    B, T, G, R, dk = q.shape
    dv = v.shape[-1]
    n_slc = T // SLC_BLOCK
    n_sel = sel.shape[-1]
    nqc = T // SLC_Q_BLOCK
    kb = k.reshape(B, n_slc, SLC_BLOCK, G, dk).transpose(0, 3, 1, 2, 4)
    vb = v.reshape(B, n_slc, SLC_BLOCK, G, dv).transpose(0, 3, 1, 2, 4)
    qc = q.reshape(B, nqc, SLC_Q_BLOCK, G, R, dk).swapaxes(0, 1)
    selc = sel.reshape(B, G, nqc, SLC_Q_BLOCK, n_sel).transpose(2, 0, 1, 3, 4)
    bi = jnp.arange(B)[:, None, None, None]
    gi = jnp.arange(G)[None, :, None, None]
    offs = jnp.arange(SLC_BLOCK)
    slope = slopes.astype(jnp.float32)[None, :, :, None, None, None]
    scale = dk ** -0.5

    def chunk(args):
        ci, qi, si = args
        kg = kb[bi, gi, si]
        vg = vb[bi, gi, si]
        tq = ci * SLC_Q_BLOCK + jnp.arange(SLC_Q_BLOCK)
        dist = (tq[None, None, :, None, None] - (si[..., None] * SLC_BLOCK + offs))[:, :, None]
        dist = dist.astype(jnp.float32)
        s = jnp.einsum('bqgrd,bgqkld->bgrqkl', qi, kg).astype(jnp.float32) * scale - slope * dist
        shp = s.shape
        flat_shape = shp[:4] + (-1,)
        p = masked_softmax(s.reshape(flat_shape), jnp.broadcast_to(dist >= 0, shp).reshape(flat_shape)).reshape(shp)
        return jnp.einsum('bgrqkl,bgqkld->bqgrd', p.astype(vg.dtype), vg)

    out = lax.map(chunk, (jnp.arange(nqc), qc, selc))
    return out.swapaxes(0, 1).reshape(B, T, G, R, dv)


def nsa_mixer(q, k_cmp, v_cmp, k_slc, v_slc, k_win, v_win, gate_logits,
              q_gain, k_gain, pe_k, pe_v, w_ck1, w_ck2, w_cv1, w_cv2):
    B, T, _ = q.shape
    G, R = NSA_KV_HEADS, NSA_HEADS // NSA_KV_HEADS
    dk, dv = NSA_QK_DIM, NSA_V_DIM
    scale = dk ** -0.5
    slopes = alibi_slopes(NSA_HEADS).reshape(G, R)
    t = jnp.arange(T)
    q = rms_norm(q.reshape(B, T, G, R, dk), q_gain)

    n_cmp = (T - CMP_BLOCK) // CMP_STRIDE + 1
    cmp_start = jnp.arange(n_cmp) * CMP_STRIDE
    blk_idx = cmp_start[:, None] + jnp.arange(CMP_BLOCK)[None, :]
    kc = rms_norm(compress_blocks(k_cmp.reshape(B, T, G, dk), blk_idx, pe_k, w_ck1, w_ck2), k_gain[0])
    vc = compress_blocks(v_cmp.reshape(B, T, G, dv), blk_idx, pe_v, w_cv1, w_cv2)
    cmp_end = cmp_start + CMP_BLOCK - 1
    dist = (t[:, None] - cmp_end[None, :]).astype(jnp.float32)
    s = jnp.einsum('btgrd,bngd->bgrtn', q, kc).astype(jnp.float32) * scale - slopes[:, :, None, None] * dist
    p_cmp = masked_softmax(s, dist >= 0)
    o_cmp = jnp.einsum('bgrtn,bngd->btgrd', p_cmp.astype(vc.dtype), vc)

    n_slc = T // SLC_BLOCK
    slc_start = jnp.arange(n_slc) * SLC_BLOCK
    overlap = ((cmp_start[:, None] < slc_start[None, :] + SLC_BLOCK)
               & (cmp_start[:, None] + CMP_BLOCK > slc_start[None, :])).astype(jnp.float32)
    imp = jnp.einsum('bgrtn,ns->bgts', p_cmp, overlap)
    cur = t // SLC_BLOCK
    j = jnp.arange(n_slc)
    forced = (j[None, :] == 0) | (j[None, :] == cur[:, None]) | (j[None, :] == cur[:, None] - 1)
    future = j[None, :] > cur[:, None]
    imp = jnp.where(forced, jnp.inf, jnp.where(future, -jnp.inf, imp))
    _, sel = lax.top_k(imp, min(SLC_TOPN, n_slc))
    o_slc = selected_attention(q, rms_norm(k_slc.reshape(B, T, G, dk), k_gain[1]),
                               v_slc.reshape(B, T, G, dv), sel, slopes)

    o_win = banded_attention(q, rms_norm(k_win.reshape(B, T, G, dk), k_gain[2]),
                             v_win.reshape(B, T, G, dv), NSA_WINDOW, slopes, None)

    g = jax.nn.sigmoid(gate_logits.astype(jnp.float32)).reshape(B, T, G, R, 3).astype(o_win.dtype)
    o = g[..., 0:1] * o_cmp + g[..., 1:2] * o_slc + g[..., 2:3] * o_win
    return o.reshape(B, T, NSA_HEADS * dv)


def mla_mixer(cq_raw, kva_raw, g_cq, g_ckv, w_uq, w_ukv, q_gain, k_gain, pos):
    B, T, _ = cq_raw.shape
    H = MLA_HEADS
    dqk = MLA_NOPE_DIM + MLA_ROPE_DIM
    q = (rms_norm(cq_raw, g_cq) @ w_uq).reshape(B, T, H, dqk)
    c_kv = rms_norm(kva_raw[..., :MLA_KV_RANK], g_ckv)
    k_rope = kva_raw[..., MLA_KV_RANK:]
    kv = (c_kv @ w_ukv).reshape(B, T, H, MLA_NOPE_DIM + MLA_V_DIM)
    k_nope, v = kv[..., :MLA_NOPE_DIM], kv[..., MLA_NOPE_DIM:]
    k = jnp.concatenate([k_nope, jnp.broadcast_to(k_rope[:, :, None, :], (B, T, H, MLA_ROPE_DIM))], axis=-1)
    q = rms_norm(q, q_gain)
    k = rms_norm(k, k_gain)
    q = jnp.concatenate([q[..., :MLA_NOPE_DIM], rope(q[..., MLA_NOPE_DIM:], pos)], axis=-1)
    k = jnp.concatenate([k[..., :MLA_NOPE_DIM], rope(k[..., MLA_NOPE_DIM:], pos)], axis=-1)
    o = causal_block_attention(q, k, v)
    return o.reshape(B, T, H * MLA_V_DIM)


def swa_mixer(h, q_gain, k_gain, sinks):
    B, T, _ = h.shape
    G, R, dh = SWA_KV_HEADS, SWA_HEADS // SWA_KV_HEADS, SWA_HEAD_DIM
    q, k, v = split_cols(h, ODD_IN_SPLITS)
    q = rms_norm(q.reshape(B, T, G, R, dh), q_gain)
    k = rms_norm(k.reshape(B, T, G, dh), k_gain)
    v = v.reshape(B, T, G, dh)
    o = banded_attention(q, k, v, SWA_WINDOW, alibi_slopes(SWA_HEADS).reshape(G, R), sinks.reshape(G, R))
    return o.reshape(B, T, SWA_HEADS * dh)


def clamped_swiglu(g, u):
    g = jnp.minimum(g, SWIGLU_LIMIT)
    u = jnp.clip(u, -SWIGLU_LIMIT, SWIGLU_LIMIT)
    return g * jax.nn.sigmoid(SWIGLU_ALPHA * g) * (u + 1.0)


def moe_ffn(xn, router_w, router_b, w_gate, b_gate, w_up, b_up, w_down, b_down):
    N, D = xn.shape
    RB = MOE_ROW_BLOCK
    logits = (xn @ router_w + router_b).astype(jnp.float32)
    top_v, top_e = lax.top_k(logits, TOP_K)
    top_w = jax.nn.softmax(top_v, axis=-1)
    nk = N * TOP_K
    flat_e = top_e.reshape(-1)
    flat_tok = jnp.arange(nk, dtype=jnp.int32) // TOP_K
    flat_w = top_w.reshape(-1)
    order = jnp.argsort(flat_e)
    sorted_e = flat_e[order]
    counts = jnp.bincount(flat_e, length=N_EXPERTS)
    padded = (counts + RB - 1) // RB * RB
    start = jnp.cumsum(counts) - counts
    pad_end = jnp.cumsum(padded)
    pad_start = pad_end - padded
    dest = pad_start[sorted_e] + jnp.arange(nk) - start[sorted_e]
    cap = (-(-nk // RB)) * RB + N_EXPERTS * RB
    n_blk = cap // RB
    buf_tok = jnp.zeros((cap,), jnp.int32).at[dest].set(flat_tok[order])
    buf_w = jnp.zeros((cap,), xn.dtype).at[dest].set(flat_w[order].astype(xn.dtype))
    blk_e = jnp.minimum(jnp.searchsorted(pad_end, jnp.arange(n_blk) * RB, side='right'), N_EXPERTS - 1)
    xb = xn[buf_tok].reshape(n_blk, RB, D)

    def expert_block(args):
        e, xr = args
        g = xr @ w_gate[e] + b_gate[e]
        u = xr @ w_up[e] + b_up[e]
        return clamped_swiglu(g, u) @ w_down[e] + b_down[e]

    yb = lax.map(expert_block, (blk_e, xb)).reshape(cap, D)
    return jax.ops.segment_sum(yb * buf_w[:, None], buf_tok, num_segments=N)


def setup_inputs(seed: int = 0) -> dict:
    key = jax.random.key(seed)
    keys = iter(jax.random.split(key, 64))

    def nrm(shape, s):
        return s * jax.random.normal(next(keys), shape, jnp.float32)

    def gain(shape):
        return 1.0 + 0.1 * jax.random.normal(next(keys), shape, jnp.float32)

    D, F, E = D_MODEL, EXPERT_FF, N_EXPERTS
    lk, lv = CMP_BLOCK * NSA_QK_DIM, CMP_BLOCK * NSA_V_DIM
    mla_qk = MLA_NOPE_DIM + MLA_ROPE_DIM
    swa_w = SWA_HEADS * SWA_HEAD_DIM
    return {
        'x': nrm((BATCH, SEQ, D), 1.0),
        'c': nrm((BATCH, D), 1.0),
        'w_mod': nrm((D, 6 * D), 0.2 * D ** -0.5),
        'mod_table': nrm((DEPTH, 6 * D), 0.1),
        'norm_attn': gain((DEPTH, D)),
        'norm_ffn': gain((DEPTH, D)),
        'w_in_even': nrm((N_EVEN, D, EVEN_IN_W), D ** -0.5),
        'w_out_even': nrm((N_EVEN, EVEN_OUT_W, D), EVEN_OUT_W ** -0.5),
        'nsa_q_gain': gain((N_EVEN, NSA_QK_DIM)),
        'nsa_k_gain': gain((N_EVEN, 3, NSA_QK_DIM)),
        'nsa_pe_k': nrm((N_EVEN, CMP_BLOCK, NSA_QK_DIM), 0.2),
        'nsa_pe_v': nrm((N_EVEN, CMP_BLOCK, NSA_V_DIM), 0.2),
        'nsa_w_ck1': nrm((N_EVEN, lk, NSA_QK_DIM), lk ** -0.5),
        'nsa_w_ck2': nrm((N_EVEN, NSA_QK_DIM, NSA_QK_DIM), NSA_QK_DIM ** -0.5),
        'nsa_w_cv1': nrm((N_EVEN, lv, NSA_V_DIM), lv ** -0.5),
        'nsa_w_cv2': nrm((N_EVEN, NSA_V_DIM, NSA_V_DIM), NSA_V_DIM ** -0.5),
        'mla_g_cq': gain((N_EVEN, MLA_Q_RANK)),
        'mla_g_ckv': gain((N_EVEN, MLA_KV_RANK)),
        'mla_w_uq': nrm((N_EVEN, MLA_Q_RANK, MLA_HEADS * mla_qk), MLA_Q_RANK ** -0.5),
        'mla_w_ukv': nrm((N_EVEN, MLA_KV_RANK, MLA_HEADS * (MLA_NOPE_DIM + MLA_V_DIM)), MLA_KV_RANK ** -0.5),
        'mla_q_gain': gain((N_EVEN, mla_qk)),
        'mla_k_gain': gain((N_EVEN, mla_qk)),
        'w_in_odd': nrm((N_ODD, D, ODD_IN_W), D ** -0.5),
        'b_in_odd': nrm((N_ODD, ODD_IN_W), 0.02),
        'w_out_odd': nrm((N_ODD, swa_w, D), swa_w ** -0.5),
        'b_out_odd': nrm((N_ODD, D), 0.02),
        'swa_q_gain': gain((N_ODD, SWA_HEAD_DIM)),
        'swa_k_gain': gain((N_ODD, SWA_HEAD_DIM)),
        'swa_sinks': nrm((N_ODD, SWA_HEADS), 1.0),
        'router_w': nrm((DEPTH, D, E), D ** -0.5),
        'router_b': nrm((DEPTH, E), 0.01),
        'moe_w_gate': nrm((DEPTH, E, D, F), D ** -0.5),
        'moe_b_gate': nrm((DEPTH, E, F), 0.01),
        'moe_w_up': nrm((DEPTH, E, D, F), D ** -0.5),
        'moe_b_up': nrm((DEPTH, E, F), 0.01),
        'moe_w_down': nrm((DEPTH, E, F, D), F ** -0.5),
        'moe_b_down': nrm((DEPTH, E, D), 0.01),
    }


def reference(x, c, w_mod, mod_table, norm_attn, norm_ffn,
              w_in_even, w_out_even, nsa_q_gain, nsa_k_gain, nsa_pe_k, nsa_pe_v,
              nsa_w_ck1, nsa_w_ck2, nsa_w_cv1, nsa_w_cv2,
              mla_g_cq, mla_g_ckv, mla_w_uq, mla_w_ukv, mla_q_gain, mla_k_gain,
              w_in_odd, b_in_odd, w_out_odd, b_out_odd, swa_q_gain, swa_k_gain, swa_sinks,
              router_w, router_b, moe_w_gate, moe_b_gate, moe_w_up, moe_b_up, moe_w_down, moe_b_down):
    B, T, D = x.shape
    pos = jnp.arange(T)
    cond = jax.nn.silu(c) @ w_mod
    for layer in range(DEPTH):
        mod = (cond + mod_table[layer])[:, None, :]
        sh_a, sc_a, g_a, sh_f, sc_f, g_f = jnp.split(mod, 6, axis=-1)
        xm = rms_norm(x, norm_attn[layer]) * (1.0 + sc_a) + sh_a
        i = layer // 2
        if layer % 2 == 0:
            (q_a, k_c, v_c, k_s, v_s, k_w, v_w, g_nsa, cq_raw, kva_raw) = split_cols(xm @ w_in_even[i], EVEN_IN_SPLITS)
            o_a = nsa_mixer(q_a, k_c, v_c, k_s, v_s, k_w, v_w, g_nsa,
                            nsa_q_gain[i], nsa_k_gain[i], nsa_pe_k[i], nsa_pe_v[i],
                            nsa_w_ck1[i], nsa_w_ck2[i], nsa_w_cv1[i], nsa_w_cv2[i])
            o_b = mla_mixer(cq_raw, kva_raw, mla_g_cq[i], mla_g_ckv[i], mla_w_uq[i], mla_w_ukv[i],
                            mla_q_gain[i], mla_k_gain[i], pos)
            mix = jnp.concatenate([o_a, o_b], axis=-1) @ w_out_even[i]
        else:
            o_c = swa_mixer(xm @ w_in_odd[i] + b_in_odd[i], swa_q_gain[i], swa_k_gain[i], swa_sinks[i])
            mix = o_c @ w_out_odd[i] + b_out_odd[i]
        x = x + g_a * mix
        xm = rms_norm(x, norm_ffn[layer]) * (1.0 + sc_f) + sh_f
        y = moe_ffn(xm.reshape(B * T, D), router_w[layer], router_b[layer],
                    moe_w_gate[layer], moe_b_gate[layer], moe_w_up[layer], moe_b_up[layer],
                    moe_w_down[layer], moe_b_down[layer])
        x = x + g_f * y.reshape(B, T, D)
    return x
```

```python
import functools
import math

import numpy as np
import jax
import jax.numpy as jnp
from jax import lax
from jax.experimental import pallas as pl
from jax.experimental.pallas import tpu as pltpu

BF16 = jnp.bfloat16
F32 = jnp.float32

NSA_HEADS = 16
NSA_KV_HEADS = 4
NSA_QK_DIM = 192
NSA_V_DIM = 128
CMP_BLOCK = 32
CMP_STRIDE = 16
SLC_BLOCK = 64
SLC_TOPN = 8
NSA_WINDOW = 512
MLA_HEADS = 16
MLA_Q_RANK = 1024
MLA_KV_RANK = 512
MLA_NOPE_DIM = 128
MLA_ROPE_DIM = 64
MLA_V_DIM = 128
ROPE_THETA = 10000.0
SWA_HEADS = 64
SWA_KV_HEADS = 8
SWA_HEAD_DIM = 64
SWA_WINDOW = 128
N_EXPERTS = 32
TOP_K = 4
SWIGLU_ALPHA = 1.702
SWIGLU_LIMIT = 7.0
MOE_ROW_BLOCK = 128
EPS = 1e-6

LANES = 128
HEAD_PAD = 256
NEG = -1e30
M_FLOOR = -1e29

EVEN_IN_SPLITS = (NSA_HEADS * NSA_QK_DIM,
                  NSA_KV_HEADS * NSA_QK_DIM, NSA_KV_HEADS * NSA_V_DIM,
                  NSA_KV_HEADS * NSA_QK_DIM, NSA_KV_HEADS * NSA_V_DIM,
                  NSA_KV_HEADS * NSA_QK_DIM, NSA_KV_HEADS * NSA_V_DIM,
                  3 * NSA_HEADS, MLA_Q_RANK, MLA_KV_RANK + MLA_ROPE_DIM)

_NQ = NSA_HEADS * HEAD_PAD
_NK = NSA_KV_HEADS * HEAD_PAD
_NV = NSA_KV_HEADS * NSA_V_DIM
EV_Q = 0
EV_KC = EV_Q + _NQ
EV_KS = EV_KC + _NK
EV_KW = EV_KS + _NK
EV_CQ = EV_KW + _NK
EV_CKV = EV_CQ + MLA_Q_RANK
EV_KR = EV_CKV + MLA_KV_RANK
EV_VC = EV_KR + 2 * MLA_ROPE_DIM
EV_VS = EV_VC + _NV
EV_VW = EV_VS + _NV
EV_GATE = EV_VW + _NV
EV_END = EV_GATE + LANES
EV_WIDTH = -(-EV_END // 512) * 512


def _tile(dim, want):
    t = min(dim, want)
    while dim % t:
        t //= 2
    return t


def _alibi(n):
    return jnp.exp2(-8.0 * jnp.arange(1, n + 1, dtype=F32) / n)


def _head_norm(res, hd, real_d):
    tm, tn = res.shape
    x2 = res * res
    pieces = []
    if hd == HEAD_PAD:
        for s in range(tn // hd):
            ss = jnp.sum(x2[:, s * hd:s * hd + LANES] + x2[:, s * hd + LANES:(s + 1) * hd],
                         axis=1, keepdims=True)
            r = lax.rsqrt(ss * (1.0 / real_d) + EPS)
            pieces.append(jnp.broadcast_to(r, (tm, hd)))
    else:
        lo = lax.broadcasted_iota(jnp.int32, (tm, LANES), 1) < hd
        for s in range(tn // LANES):
            c = x2[:, s * LANES:(s + 1) * LANES]
            ss_lo = jnp.sum(jnp.where(lo, c, 0.0), axis=1, keepdims=True)
            ss_hi = jnp.sum(jnp.where(lo, 0.0, c), axis=1, keepdims=True)
            r_lo = lax.rsqrt(ss_lo * (1.0 / real_d) + EPS)
            r_hi = lax.rsqrt(ss_hi * (1.0 / real_d) + EPS)
            pieces.append(jnp.where(lo, r_lo, r_hi))
    return jnp.concatenate(pieces, axis=1) if len(pieces) > 1 else pieces[0]


def _mm_kernel(*refs, nk, a_pro, has_bias, hd, real_d, has_resid):
    it = iter(refs)
    a_ref = next(it)
    w_ref = next(it)
    again_ref = next(it) if a_pro == "rms" else None
    b_ref = next(it) if has_bias else None
    gain_ref = flag_ref = None
    if hd:
        gain_ref = next(it)
        flag_ref = next(it)
    x_ref = gate_ref = None
    if has_resid:
        x_ref = next(it)
        gate_ref = next(it)
    o_ref = next(it)
    acc_ref = next(it) if nk > 1 else None

    a = a_ref[...]
    if a_pro == "silu":
        af = a.astype(F32)
        a = af * jax.nn.sigmoid(af)
    elif a_pro == "rms":
        af = a.astype(F32)
        r = lax.rsqrt(jnp.mean(af * af, axis=1, keepdims=True) + EPS)
        a = af * r * again_ref[...]
    part = jnp.dot(a.astype(BF16), w_ref[...].astype(BF16), preferred_element_type=F32)

    def finish(res):
        if has_bias:
            res = res + b_ref[...]
        if hd:
            r = _head_norm(res, hd, real_d)
            res = res * jnp.where(flag_ref[...] > 0.0, r, 1.0) * gain_ref[...]
        if has_resid:
            res = x_ref[...] + gate_ref[...] * res
        o_ref[...] = res.astype(o_ref.dtype)

    if nk == 1:
        finish(part)
    else:
        k = pl.program_id(2)

        @pl.when(k == 0)
        def _():
            acc_ref[...] = part

        @pl.when(k > 0)
        def _():
            acc_ref[...] += part

        @pl.when(k == nk - 1)
        def _():
            finish(acc_ref[...])


def matmul(a, w, *, a_col0=0, k_dim=None, a_pro=None, a_gain=None, bias=None,
           head_norm=None, resid=None, out_dtype=BF16, tm=512, tn=512, tk=1024, name="mm"):
    M = a.shape[0]
    K, N = w.shape
    k_dim = K if k_dim is None else k_dim
    tm, tn, tk = _tile(M, tm), _tile(N, tn), _tile(K, tk)
    if a_pro == "rms":
        tk = K
    assert a_col0 % tk == 0 and M % tm == 0 and N % tn == 0 and K % tk == 0
    nk = K // tk
    koff = a_col0 // tk
    hd = head_norm[0] if head_norm else 0
    real_d = head_norm[1] if head_norm else 0
    if hd:
        assert tn % max(hd, LANES) == 0

    ins = [a, w]
    specs = [pl.BlockSpec((tm, tk), lambda i, j, k: (i, koff + k)),
             pl.BlockSpec((tk, tn), lambda i, j, k: (k, j))]
    if a_pro == "rms":
        ins.append(a_gain.reshape(1, K).astype(F32))
        specs.append(pl.BlockSpec((1, tk), lambda i, j, k: (0, k)))
    if bias is not None:
        ins.append(bias.reshape(1, N).astype(F32))
        specs.append(pl.BlockSpec((1, tn), lambda i, j, k: (0, j)))
    if hd:
        ins += [head_norm[2].reshape(1, N).astype(F32), head_norm[3].reshape(1, N).astype(F32)]
        specs += [pl.BlockSpec((1, tn), lambda i, j, k: (0, j))] * 2
    if resid is not None:
        x, gate, rows_per_batch = resid
        assert rows_per_batch % tm == 0
        bpb = rows_per_batch // tm
        ins += [x, gate]
        specs += [pl.BlockSpec((tm, tn), lambda i, j, k: (i, j)),
                  pl.BlockSpec((None, 1, tn), lambda i, j, k: (i // bpb, 0, j))]
    kern = functools.partial(_mm_kernel, nk=nk, a_pro=a_pro, has_bias=bias is not None,
                             hd=hd, real_d=real_d, has_resid=resid is not None)
    return pl.pallas_call(
        kern,
        out_shape=jax.ShapeDtypeStruct((M, N), out_dtype),
        grid=(M // tm, N // tn, nk),
        in_specs=specs,
        out_specs=pl.BlockSpec((tm, tn), lambda i, j, k: (i, j)),
        scratch_shapes=[pltpu.VMEM((tm, tn), F32)] if nk > 1 else [],
        compiler_params=pltpu.CompilerParams(
            dimension_semantics=("parallel", "parallel", "arbitrary")),
        name=name,
    )(*ins)


def _split_bf16(v):
    hi = v.astype(BF16)
    lo = (v - hi.astype(F32)).astype(BF16)
    return hi, lo


def _norm_mod_kernel(x_ref, g_ref, sc_ref, sh_ref, *rest, route):
    x = x_ref[...]
    r = lax.rsqrt(jnp.mean(x * x, axis=1, keepdims=True) + EPS)
    xm = (x * r * g_ref[...]) * (1.0 + sc_ref[...]) + sh_ref[...]
    if not route:
        (o_ref,) = rest
        o_ref[...] = xm.astype(o_ref.dtype)
        return
    rw_ref, rb_ref, o_ref, e_ref, p_ref = rest
    o_ref[...] = xm.astype(o_ref.dtype)
    a_hi, a_lo = _split_bf16(xm)
    w = rw_ref[...]
    w_hi, w_lo = _split_bf16(w)
    logits = (jnp.dot(a_hi, w_hi, preferred_element_type=F32)
              + jnp.dot(a_hi, w_lo, preferred_element_type=F32)
              + jnp.dot(a_lo, w_hi, preferred_element_type=F32)) + rb_ref[...]
    tt = logits.shape[0]
    lane = lax.broadcasted_iota(jnp.int32, (tt, LANES), 1)
    work = jnp.where(lane < N_EXPERTS, logits, -jnp.inf)
    e_out = jnp.zeros((tt, LANES), jnp.int32)
    v_out = jnp.full((tt, LANES), -jnp.inf, F32)
    for kk in range(TOP_K):
        m = jnp.max(work, axis=1, keepdims=True)
        idx = jnp.min(jnp.where(work == m, lane, LANES), axis=1, keepdims=True)
        e_out = jnp.where(lane == kk, idx, e_out)
        v_out = jnp.where(lane == kk, m, v_out)
        work = jnp.where(lane == idx, -jnp.inf, work)
    vmax = jnp.max(v_out, axis=1, keepdims=True)
    pe = jnp.exp(v_out - vmax)
    p_ref[...] = pe / jnp.sum(pe, axis=1, keepdims=True)
    e_ref[...] = e_out


def norm_mod(x2d, gain, sc, sh, rows_per_batch, *, out_dtype=BF16, router=None, tt=256):
    N, D = x2d.shape
    tt = _tile(rows_per_batch, tt)
    bpb = rows_per_batch // tt
    ins = [x2d, gain.reshape(1, D), sc, sh]
    specs = [pl.BlockSpec((tt, D), lambda i: (i, 0)),
             pl.BlockSpec((1, D), lambda i: (0, 0)),
             pl.BlockSpec((None, 1, D), lambda i: (i // bpb, 0, 0)),
             pl.BlockSpec((None, 1, D), lambda i: (i // bpb, 0, 0))]
    out_shape = [jax.ShapeDtypeStruct((N, D), out_dtype)]
    out_specs = [pl.BlockSpec((tt, D), lambda i: (i, 0))]
    if router is not None:
        rw, rb = router
        E = rw.shape[1]
        rw_p = jnp.pad(rw, ((0, 0), (0, LANES - E)))
        rb_p = jnp.pad(rb, (0, LANES - E)).reshape(1, LANES)
        ins += [rw_p, rb_p]
        specs += [pl.BlockSpec((D, LANES), lambda i: (0, 0)),
                  pl.BlockSpec((1, LANES), lambda i: (0, 0))]
        out_shape += [jax.ShapeDtypeStruct((N, LANES), jnp.int32),
                      jax.ShapeDtypeStruct((N, LANES), F32)]
        out_specs += [pl.BlockSpec((tt, LANES), lambda i: (i, 0))] * 2
    res = pl.pallas_call(
        functools.partial(_norm_mod_kernel, route=router is not None),
        out_shape=out_shape, grid=(N // tt,), in_specs=specs, out_specs=out_specs,
        compiler_params=pltpu.CompilerParams(dimension_semantics=("parallel",)),
        name="norm_mod_route" if router is not None else "norm_mod",
    )(*ins)
    return res if router is not None else res[0]


def _compress_kernel(h_ref, pet_ref, peb_ref, wt_ref, wb_ref, w2_ref, g_ref, o_ref, *, real_d, norm):
    halves = h_ref[...].astype(F32)
    top = jnp.dot((halves + pet_ref[...]).astype(BF16), wt_ref[...], preferred_element_type=F32)
    bot = jnp.dot((halves + peb_ref[...]).astype(BF16), wb_ref[...], preferred_element_type=F32)
    nh = top.shape[0]
    hid = top + pltpu.roll(bot, nh - 1, 0)
    hid = hid * jax.nn.sigmoid(hid)
    y = jnp.dot(hid.astype(BF16), w2_ref[...], preferred_element_type=F32)
    if norm:
        r = lax.rsqrt(jnp.sum(y * y, axis=1, keepdims=True) * (1.0 / real_d) + EPS)
        y = y * r * g_ref[...]
    o_ref[...] = y.astype(o_ref.dtype)


def compress(halves, pe, w1, w2, gain, d, dp):
    BG, nh, wdt = halves.shape
    half = CMP_BLOCK // 2
    pe_p = jnp.pad(pe, ((0, 0), (0, dp - d)))
    w1_p = jnp.pad(w1.reshape(CMP_BLOCK, d, d), ((0, 0), (0, dp - d), (0, dp - d)))
    w2_p = jnp.pad(w2, ((0, dp - d), (0, dp - d))).astype(BF16)
    g_p = (jnp.ones((dp,), F32) if gain is None else jnp.pad(gain, (0, dp - d))).reshape(1, dp)
    full = lambda shape: pl.BlockSpec(shape, lambda i: (0,) * len(shape))
    return pl.pallas_call(
        functools.partial(_compress_kernel, real_d=d, norm=gain is not None),
        out_shape=jax.ShapeDtypeStruct((BG, nh, dp), BF16),
        grid=(BG,),
        in_specs=[pl.BlockSpec((None, nh, wdt), lambda i: (i, 0, 0)),
                  full((1, wdt)), full((1, wdt)), full((wdt, dp)), full((wdt, dp)),
                  full((dp, dp)), full((1, dp))],
        out_specs=pl.BlockSpec((None, nh, dp), lambda i: (i, 0, 0)),
        compiler_params=pltpu.CompilerParams(dimension_semantics=("parallel",)),
        name="nsa_compress",
    )(halves, pe_p[:half].reshape(1, wdt), pe_p[half:].reshape(1, wdt),
      w1_p[:half].reshape(wdt, dp).astype(BF16), w1_p[half:].reshape(wdt, dp).astype(BF16), w2_p, g_p)


def _stack_heads(q_ref, n, w):
    return jnp.concatenate([q_ref[:, r * w:(r + 1) * w] for r in range(n)], axis=0)


def _row_scalars(vals, tq):
    rows = len(vals) * tq
    rid = lax.broadcasted_iota(jnp.int32, (rows, 1), 0) // tq
    col = jnp.full((rows, 1), vals[-1], F32)
    for r in range(len(vals) - 2, -1, -1):
        col = jnp.where(rid == r, vals[r], col)
    return col


def _nt_dot(a, b):
    return lax.dot_general(a, b, (((1,), (1,)), ((), ())), preferred_element_type=F32)


def _cmp_attn_kernel(slope_ref, q_ref, kc_ref, vc_ref, ovl_ref, o_ref, bits_ref, *, tq, R, n_slc):
    g = pl.program_id(1)
    t0 = pl.program_id(2) * tq
    rows = R * tq
    q = _stack_heads(q_ref, R, HEAD_PAD)
    s = _nt_dot(q, kc_ref[...])
    ncol = s.shape[1]
    row_t = t0 + lax.broadcasted_iota(jnp.int32, (rows, ncol), 0) % tq
    n_id = lax.broadcasted_iota(jnp.int32, (rows, ncol), 1)
    dist = row_t - (n_id * CMP_STRIDE + (CMP_BLOCK - 1))
    slope = _row_scalars([slope_ref[g * R + r] for r in range(R)], tq)
    valid = dist >= 0
    s = jnp.where(valid, s - slope * dist.astype(F32), NEG)
    m = jnp.max(s, axis=1, keepdims=True)
    p = jnp.where(valid, jnp.exp(s - m), 0.0)
    p = p / jnp.maximum(jnp.sum(p, axis=1, keepdims=True), 1e-30)
    o = jnp.dot(p.astype(BF16), vc_ref[...], preferred_element_type=F32)
    for r in range(R):
        o_ref[:, r * NSA_V_DIM:(r + 1) * NSA_V_DIM] = o[r * tq:(r + 1) * tq].astype(o_ref.dtype)

    psum = p[0:tq]
    for r in range(1, R):
        psum = psum + p[r * tq:(r + 1) * tq]
    p_hi, p_lo = _split_bf16(psum)
    imp = (jnp.dot(p_hi, ovl_ref[...], preferred_element_type=F32)
           + jnp.dot(p_lo, ovl_ref[...], preferred_element_type=F32))
    lane = lax.broadcasted_iota(jnp.int32, (tq, LANES), 1)
    cur = (t0 + lax.broadcasted_iota(jnp.int32, (tq, LANES), 0)) // SLC_BLOCK
    forced = (lane == 0) | (lane == cur) | (lane == cur - 1)
    work = jnp.where(forced, jnp.inf, jnp.where(lane > cur, -jnp.inf, imp))
    removed = -3.0e38
    work = jnp.where(lane < n_slc, jnp.where(work == -jnp.inf, -2.0e38, work), removed)
    sel = jnp.zeros((tq, LANES), jnp.bool_)
    for _ in range(min(SLC_TOPN, n_slc)):
        mx = jnp.max(work, axis=1, keepdims=True)
        idx = jnp.min(jnp.where(work == mx, lane, LANES), axis=1, keepdims=True)
        hit = lane == idx
        sel = sel | hit
        work = jnp.where(hit, removed, work)
    half = 16
    w_lo = jnp.where(sel & (lane < half), jnp.left_shift(1, jnp.minimum(lane, half - 1)), 0)
    w_hi = jnp.where(sel & (lane >= half), jnp.left_shift(1, jnp.clip(lane - half, 0, half - 1)), 0)
    b_lo = jnp.sum(w_lo.astype(F32), axis=1, keepdims=True).astype(jnp.int32)
    b_hi = jnp.sum(w_hi.astype(F32), axis=1, keepdims=True).astype(jnp.int32)
    bits = b_lo | jnp.left_shift(b_hi, half)
    bits_ref[...] = jnp.broadcast_to(bits, (tq, LANES))


def cmp_attention(h, kc, vc, B, T, tq=256):
    G, R = NSA_KV_HEADS, NSA_HEADS // NSA_KV_HEADS
    tq = _tile(T, tq)
    nq = T // tq
    n_slc = T // SLC_BLOCK
    n_cmp = kc.shape[1]
    assert n_slc <= 32 and n_cmp <= LANES and n_cmp % 8 == 0
    cmp_start = np.arange(n_cmp) * CMP_STRIDE
    slc_start = np.arange(LANES) * SLC_BLOCK
    ovl = ((cmp_start[:, None] < slc_start[None, :] + SLC_BLOCK)
           & (cmp_start[:, None] + CMP_BLOCK > slc_start[None, :])
           & (np.arange(LANES)[None, :] < n_slc) & (np.arange(n_cmp)[:, None] < n_cmp - 1))
    ovl = jnp.asarray(ovl, BF16)
    qw = R * HEAD_PAD
    return pl.pallas_call(
        functools.partial(_cmp_attn_kernel, tq=tq, R=R, n_slc=n_slc),
        out_shape=[jax.ShapeDtypeStruct((B * T, NSA_HEADS * NSA_V_DIM), BF16),
                   jax.ShapeDtypeStruct((B, G, T, LANES), jnp.int32)],
        grid=(B, G, nq),
        in_specs=[pl.BlockSpec(memory_space=pltpu.SMEM),
                  pl.BlockSpec((tq, qw), lambda b, g, i: (b * nq + i, EV_Q // qw + g)),
                  pl.BlockSpec((None, n_cmp, HEAD_PAD), lambda b, g, i: (b * G + g, 0, 0)),
                  pl.BlockSpec((None, n_cmp, NSA_V_DIM), lambda b, g, i: (b * G + g, 0, 0)),
                  pl.BlockSpec((n_cmp, LANES), lambda b, g, i: (0, 0))],
        out_specs=[pl.BlockSpec((tq, R * NSA_V_DIM), lambda b, g, i: (b * nq + i, g)),
                   pl.BlockSpec((None, None, tq, LANES), lambda b, g, i: (b, g, i, 0))],
        compiler_params=pltpu.CompilerParams(dimension_semantics=("parallel", "parallel", "parallel")),
        name="nsa_cmp_attn",
    )(_alibi(NSA_HEADS), h, kc, vc, ovl)


def _flash_kernel(*refs, tq, tk, R, vd, window, alibi, use_bits):
    it = iter(refs)
    slope_ref = next(it) if alibi else None
    q_ref = next(it)
    k_ref = next(it)
    v_ref = next(it)
    bits_ref = next(it) if use_bits else None
    o_ref = next(it)
    m_ref = next(it)
    l_ref = next(it)
    acc_ref = next(it)

    g = pl.program_id(1)
    i = pl.program_id(2)
    t0 = i * tq
    rows = R * tq
    q = _stack_heads(q_ref, R, HEAD_PAD)
    rel0 = (lax.broadcasted_iota(jnp.int32, (rows, tk), 0) % tq
            - lax.broadcasted_iota(jnp.int32, (rows, tk), 1))
    if alibi:
        slope = _row_scalars([slope_ref[g * R + r] for r in range(R)], tq)
        bias0 = -slope * rel0.astype(F32)
    if use_bits:
        b1 = bits_ref[...]
        b1 = jnp.concatenate([b1] * R, axis=0) if R > 1 else b1
        bits = jnp.concatenate([b1] * (tk // LANES), axis=1) if tk > LANES else b1
        blk0 = lax.broadcasted_iota(jnp.int32, (rows, tk), 1) // SLC_BLOCK

    m_ref[...] = jnp.full((rows, 1), M_FLOOR, F32)
    l_ref[...] = jnp.zeros((rows, 1), F32)
    acc_ref[...] = jnp.zeros((rows, vd), F32)

    def body(c, carry):
        s0 = pl.multiple_of(c * tk, tk)
        kc = k_ref[pl.ds(s0, tk), :]
        vc = v_ref[pl.ds(s0, tk), :]
        off = t0 - s0
        s = _nt_dot(q, kc)
        if alibi:
            s = s + (bias0 - slope * off.astype(F32))
        valid = rel0 >= -off
        if window is not None:
            valid = valid & (rel0 < window - off)
        if use_bits:
            valid = valid & ((jnp.right_shift(bits, blk0 + s0 // SLC_BLOCK) & 1) != 0)
        s = jnp.where(valid, s, NEG)
        m_old = m_ref[...]
        m_new = jnp.maximum(m_old, jnp.max(s, axis=1, keepdims=True))
        alpha = jnp.exp(m_old - m_new)
        p = jnp.exp(s - m_new)
        l_ref[...] = alpha * l_ref[...] + jnp.sum(p, axis=1, keepdims=True)
        acc_ref[...] = alpha * acc_ref[...] + jnp.dot(p.astype(BF16), vc, preferred_element_type=F32)
        m_ref[...] = m_new
        return carry

    c_hi = (t0 + tq - 1) // tk
    c_lo = 0 if window is None else jnp.maximum((t0 - window + 1) // tk, 0)
    lax.fori_loop(c_lo, c_hi + 1, body, 0)
    o = acc_ref[...] / jnp.maximum(l_ref[...], 1e-30)
    for r in range(R):
        o_ref[:, r * vd:(r + 1) * vd] = o[r * tq:(r + 1) * tq].astype(o_ref.dtype)


def flash_attention(q_arr, q_col0, k_arr, k_col0, v_arr, v_col0, *, B, T, G, R, out_cols, out_col0=0,
                    v_stride=NSA_V_DIM, window=None, slopes=None, bits=None, tq=256, tk=256, name="flash"):
    vd = NSA_V_DIM
    tq = _tile(T, tq)
    tk = _tile(T, tk)
    nq = T // tq
    qw = R * HEAD_PAD
    assert q_col0 % qw == 0 and k_col0 % HEAD_PAD == 0 and v_col0 % vd == 0 and out_col0 % (R * vd) == 0
    assert v_stride % vd == 0
    qb, kb, vb, ob = q_col0 // qw, k_col0 // HEAD_PAD, v_col0 // vd, out_col0 // (R * vd)
    vs = v_stride // vd
    ins, specs = [], []
    if slopes is not None:
        ins.append(slopes)
        specs.append(pl.BlockSpec(memory_space=pltpu.SMEM))
    ins += [q_arr, k_arr, v_arr]
    specs += [pl.BlockSpec((tq, qw), lambda b, g, i: (b * nq + i, qb + g)),
              pl.BlockSpec((T, HEAD_PAD), lambda b, g, i: (b, kb + g)),
              pl.BlockSpec((T, vd), lambda b, g, i: (b, vb + g * vs))]
    if bits is not None:
        ins.append(bits)
        specs.append(pl.BlockSpec((None, None, tq, LANES), lambda b, g, i: (b, g, i, 0)))
    rows = R * tq
    return pl.pallas_call(
        functools.partial(_flash_kernel, tq=tq, tk=tk, R=R, vd=vd, window=window,
                          alibi=slopes is not None, use_bits=bits is not None),
        out_shape=jax.ShapeDtypeStruct((B * T, out_cols), BF16),
        grid=(B, G, nq),
        in_specs=specs,
        out_specs=pl.BlockSpec((tq, R * vd), lambda b, g, i: (b * nq + i, ob + g)),
        scratch_shapes=[pltpu.VMEM((rows, 1), F32), pltpu.VMEM((rows, 1), F32),
                        pltpu.VMEM((rows, vd), F32)],
        compiler_params=pltpu.CompilerParams(dimension_semantics=("parallel", "parallel", "parallel")),
        name=name,
    )(*ins)


def _nsa_combine_kernel(gate_ref, oc_ref, os_ref, ow_ref, o_ref):
    gates = jax.nn.sigmoid(gate_ref[...].astype(F32))
    for h in range(NSA_HEADS):
        sl = slice(h * NSA_V_DIM, (h + 1) * NSA_V_DIM)
        acc = None
        for br, ref in enumerate((oc_ref, os_ref, ow_ref)):
            gcol = gates[:, 3 * h + br:3 * h + br + 1]
            term = gcol * ref[:, sl].astype(F32)
            acc = term if acc is None else acc + term
        o_ref[:, sl] = acc.astype(o_ref.dtype)


def nsa_combine(h, o_cmp, o_slc, o_win, tt=256):
    N, W = o_cmp.shape
    tt = _tile(N, tt)
    gb = EV_GATE // LANES
    blk = pl.BlockSpec((tt, W), lambda i: (i, 0))
    return pl.pallas_call(
        _nsa_combine_kernel,
        out_shape=jax.ShapeDtypeStruct((N, W), BF16),
        grid=(N // tt,),
        in_specs=[pl.BlockSpec((tt, LANES), lambda i: (i, gb)), blk, blk, blk],
        out_specs=blk,
        compiler_params=pltpu.CompilerParams(dimension_semantics=("parallel",)),
        name="nsa_combine",
    )(h, o_cmp, o_slc, o_win)


def _mla_prep_kernel(lo_ref, hi_ref, alo_ref, ahi_ref, bhi_ref, o_ref):
    x_lo = lo_ref[...].astype(F32)
    x_hi = hi_ref[...].astype(F32)
    tt = x_lo.shape[0]
    is_rope = lax.broadcasted_iota(jnp.int32, (tt, LANES), 1) < MLA_ROPE_DIM
    ss = jnp.sum(x_lo * x_lo + jnp.where(is_rope, x_hi * x_hi, 0.0), axis=1, keepdims=True)
    r = lax.rsqrt(ss * (1.0 / (MLA_NOPE_DIM + MLA_ROPE_DIM)) + EPS)
    y_hi = x_hi * ahi_ref[...] + pltpu.roll(x_hi * bhi_ref[...], MLA_ROPE_DIM, 1)
    o_ref[:, :LANES] = (x_lo * r * alo_ref[...]).astype(o_ref.dtype)
    o_ref[:, LANES:] = (y_hi * r).astype(o_ref.dtype)


def mla_prep(lo_arr, lo_col0, lo_stride, hi_arr, hi_col0, hi_stride, a_lo, a_hi, b_hi, T, tt=256):
    N = lo_arr.shape[0]
    H = MLA_HEADS
    tt = _tile(T, tt)
    nt = T // tt
    l0, ls, h0, hs = lo_col0 // LANES, lo_stride // LANES, hi_col0 // LANES, hi_stride // LANES
    return pl.pallas_call(
        _mla_prep_kernel,
        out_shape=jax.ShapeDtypeStruct((N, H * HEAD_PAD), BF16),
        grid=(N // tt, H),
        in_specs=[pl.BlockSpec((tt, LANES), lambda i, h: (i, l0 + h * ls)),
                  pl.BlockSpec((tt, LANES), lambda i, h: (i, h0 + h * hs)),
                  pl.BlockSpec((1, LANES), lambda i, h: (0, 0)),
                  pl.BlockSpec((tt, LANES), lambda i, h: (i % nt, 0)),
                  pl.BlockSpec((tt, LANES), lambda i, h: (i % nt, 0))],
        out_specs=pl.BlockSpec((tt, HEAD_PAD), lambda i, h: (i, h)),
        compiler_params=pltpu.CompilerParams(dimension_semantics=("parallel", "parallel")),
        name="mla_prep",
    )(lo_arr, hi_arr, a_lo, a_hi, b_hi)


def _rope_tables(gain, T, scale):
    half = MLA_ROPE_DIM // 2
    inv_freq = ROPE_THETA ** (-jnp.arange(half, dtype=F32) / half)
    ang = jnp.arange(T, dtype=F32)[:, None] * inv_freq[None, :]
    cos2 = jnp.concatenate([jnp.cos(ang), jnp.cos(ang)], axis=1)
    sin_s = jnp.concatenate([-jnp.sin(ang), jnp.sin(ang)], axis=1)
    g_nope, g_rope = gain[:MLA_NOPE_DIM], gain[MLA_NOPE_DIM:]
    g_perm = jnp.concatenate([g_rope[half:], g_rope[:half]])
    zeros = jnp.zeros((T, MLA_ROPE_DIM), F32)
    a_lo = (g_nope * scale).reshape(1, LANES)
    a_hi = jnp.concatenate([g_rope[None, :] * cos2 * scale, zeros], axis=1)
    b_hi = jnp.concatenate([zeros, g_perm[None, :] * sin_s * scale], axis=1)
    return a_lo, a_hi, b_hi


def _swa_kernel(slope_ref, sink_ref, q_ref, k_ref, v_ref, o_ref, *, tq, tk, R):
    g = pl.program_id(1)
    t0 = pl.program_id(2) * tq
    npair = R // 2
    rows = npair * tq
    qp = _stack_heads(q_ref, npair, LANES)
    lane = lax.broadcasted_iota(jnp.int32, (rows, LANES), 1)
    first = lane < SWA_HEAD_DIM
    start = pl.multiple_of(jnp.maximum(t0 - (tk - tq), 0), tq)
    kc = k_ref[pl.ds(start, tk), :]
    vc = v_ref[pl.ds(start, tk), :]
    rel = (t0 - start + lax.broadcasted_iota(jnp.int32, (rows, tk), 0) % tq
           - lax.broadcasted_iota(jnp.int32, (rows, tk), 1))
    valid = (rel >= 0) & (rel < SWA_WINDOW)
    relf = rel.astype(F32)
    outs = []
    for u in range(2):
        qu = jnp.where(first if u == 0 else ~first, qp, jnp.zeros_like(qp))
        heads = [g * R + 2 * p + u for p in range(npair)]
        slope = _row_scalars([slope_ref[h] for h in heads], tq)
        sink = _row_scalars([sink_ref[h] for h in heads], tq)
        s = jnp.where(valid, _nt_dot(qu, kc) - slope * relf, NEG)
        m = jnp.maximum(jnp.max(s, axis=1, keepdims=True), sink)
        p = jnp.exp(s - m)
        denom = jnp.sum(p, axis=1, keepdims=True) + jnp.exp(sink - m)
        p = p / jnp.maximum(denom, 1e-30)
        outs.append(jnp.dot(p.astype(BF16), vc, preferred_element_type=F32))
    o = jnp.where(first, outs[0], outs[1])
    for p_ in range(npair):
        o_ref[:, p_ * LANES:(p_ + 1) * LANES] = o[p_ * tq:(p_ + 1) * tq].astype(o_ref.dtype)


def swa_attention(h, sinks, B, T, k_col0, v_col0, tq=128):
    G, R = SWA_KV_HEADS, SWA_HEADS // SWA_KV_HEADS
    tq = _tile(T, tq)
    tk = min(T, tq + SWA_WINDOW)
    assert tk - tq in (0, SWA_WINDOW) and SWA_WINDOW % tq == 0 or tk == T
    nq = T // tq
    qw = R * SWA_HEAD_DIM
    kb, vb = k_col0 // LANES, v_col0 // LANES
    return pl.pallas_call(
        functools.partial(_swa_kernel, tq=tq, tk=tk, R=R),
        out_shape=jax.ShapeDtypeStruct((B * T, SWA_HEADS * SWA_HEAD_DIM), BF16),
        grid=(B, G, nq),
        in_specs=[pl.BlockSpec(memory_space=pltpu.SMEM), pl.BlockSpec(memory_space=pltpu.SMEM),
                  pl.BlockSpec((tq, qw), lambda b, g, i: (b * nq + i, g)),
                  pl.BlockSpec((T, LANES), lambda b, g, i: (b, kb + g)),
                  pl.BlockSpec((T, LANES), lambda b, g, i: (b, vb + g))],
        out_specs=pl.BlockSpec((tq, qw), lambda b, g, i: (b * nq + i, g)),
        compiler_params=pltpu.CompilerParams(dimension_semantics=("parallel", "parallel", "parallel")),
        name="swa_attn",
    )(_alibi(SWA_HEADS), sinks.astype(F32), h, h, h)


def _expert_kernel(be_ref, nu_ref, tok_ref, x_hbm, wg_ref, bg_ref, wu_ref, bu_ref, wd_ref, bd_ref,
                   o_ref, xbuf, sem, *, rb):
    blk = pl.program_id(0)

    @pl.when(blk < nu_ref[0])
    def _():
        def row_copy(r):
            return pltpu.make_async_copy(x_hbm.at[pl.ds(tok_ref[0, r], 1)], xbuf.at[pl.ds(r, 1)], sem)

        def issue(r, c):
            row_copy(r).start()
            return c

        def drain(r, c):
            row_copy(r).wait()
            return c

        lax.fori_loop(0, rb, issue, 0)
        lax.fori_loop(0, rb, drain, 0)
        x = xbuf[...].astype(BF16)
        gg = jnp.dot(x, wg_ref[...].astype(BF16), preferred_element_type=F32) + bg_ref[...]
        uu = jnp.dot(x, wu_ref[...].astype(BF16), preferred_element_type=F32) + bu_ref[...]
        gg = jnp.minimum(gg, SWIGLU_LIMIT)
        uu = jnp.clip(uu, -SWIGLU_LIMIT, SWIGLU_LIMIT)
        act = gg * jax.nn.sigmoid(SWIGLU_ALPHA * gg) * (uu + 1.0)
        y = jnp.dot(act.astype(BF16), wd_ref[...].astype(BF16), preferred_element_type=F32) + bd_ref[...]
        o_ref[...] = y

    @pl.when(blk >= nu_ref[0])
    def _():
        o_ref[...] = jnp.zeros_like(o_ref)


def moe_experts(xm, blk_e, n_used, buf_tok, wg, bg, wu, bu, wd, bd):
    N, D = xm.shape
    E, _, F = wg.shape
    n_blk, _, rb = buf_tok.shape
    grid_spec = pltpu.PrefetchScalarGridSpec(
        num_scalar_prefetch=2,
        grid=(n_blk,),
        in_specs=[pl.BlockSpec((None, 1, rb), lambda i, be, nu: (i, 0, 0), memory_space=pltpu.SMEM),
                  pl.BlockSpec(memory_space=pl.ANY),
                  pl.BlockSpec((None, D, F), lambda i, be, nu: (be[i], 0, 0)),
                  pl.BlockSpec((None, 1, F), lambda i, be, nu: (be[i], 0, 0)),
                  pl.BlockSpec((None, D, F), lambda i, be, nu: (be[i], 0, 0)),
                  pl.BlockSpec((None, 1, F), lambda i, be, nu: (be[i], 0, 0)),
                  pl.BlockSpec((None, F, D), lambda i, be, nu: (be[i], 0, 0)),
                  pl.BlockSpec((None, 1, D), lambda i, be, nu: (be[i], 0, 0))],
        out_specs=pl.BlockSpec((rb, D), lambda i, be, nu: (i, 0)),
        scratch_shapes=[pltpu.VMEM((rb, D), F32), pltpu.SemaphoreType.DMA(())],
    )
    return pl.pallas_call(
        functools.partial(_expert_kernel, rb=rb),
        out_shape=jax.ShapeDtypeStruct((n_blk * rb, D), F32),
        grid_spec=grid_spec,
        compiler_params=pltpu.CompilerParams(dimension_semantics=("arbitrary",)),
        name="moe_experts",
    )(blk_e, n_used, buf_tok, xm, wg, bg.reshape(E, 1, F), wu, bu.reshape(E, 1, F), wd, bd.reshape(E, 1, D))


def _moe_combine_kernel(pos_ref, y_hbm, x_ref, w_ref, gate_ref, o_ref, ybuf, sem, *, tt):
    def row_copy(t, k):
        return pltpu.make_async_copy(y_hbm.at[pl.ds(pos_ref[0, t * TOP_K + k], 1)],
                                     ybuf.at[k, pl.ds(t, 1)], sem)

    def issue(t, c):
        for k in range(TOP_K):
            row_copy(t, k).start()
        return c

    def drain(t, c):
        for k in range(TOP_K):
            row_copy(t, k).wait()
        return c

    lax.fori_loop(0, tt, issue, 0)
    lax.fori_loop(0, tt, drain, 0)
    w = w_ref[...]
    y = w[:, 0:1] * ybuf[0]
    for k in range(1, TOP_K):
        y = y + w[:, k:k + 1] * ybuf[k]
    o_ref[...] = x_ref[...] + gate_ref[...] * y


def moe_combine(yb, pos, top_w, x2d, gate, rows_per_batch, tt=64):
    N, D = x2d.shape
    tt = _tile(rows_per_batch, tt)
    bpb = rows_per_batch // tt
    pos3 = pos.reshape(N // tt, 1, tt * TOP_K)
    return pl.pallas_call(
        functools.partial(_moe_combine_kernel, tt=tt),
        out_shape=jax.ShapeDtypeStruct((N, D), F32),
        grid=(N // tt,),
        in_specs=[pl.BlockSpec((None, 1, tt * TOP_K), lambda i: (i, 0, 0), memory_space=pltpu.SMEM),
                  pl.BlockSpec(memory_space=pl.ANY),
                  pl.BlockSpec((tt, D), lambda i: (i, 0)),
                  pl.BlockSpec((tt, LANES), lambda i: (i, 0)),
                  pl.BlockSpec((None, 1, D), lambda i: (i // bpb, 0, 0))],
        out_specs=pl.BlockSpec((tt, D), lambda i: (i, 0)),
        scratch_shapes=[pltpu.VMEM((TOP_K, tt, D), F32), pltpu.SemaphoreType.DMA(())],
        compiler_params=pltpu.CompilerParams(dimension_semantics=("arbitrary",)),
        name="moe_combine",
    )(pos3, yb, x2d, top_w, gate)


def moe_layer(x2d, gain, sc, sh, gate, rows_per_batch, router_w, router_b, wg, bg, wu, bu, wd, bd):
    N, D = x2d.shape
    RB = MOE_ROW_BLOCK
    xm, top_e, top_w = norm_mod(x2d, gain, sc, sh, rows_per_batch, out_dtype=F32,
                                router=(router_w, router_b))
    nk = N * TOP_K
    flat_e = top_e[:, :TOP_K].reshape(-1)
    flat_tok = jnp.arange(nk, dtype=jnp.int32) // TOP_K
    order = jnp.argsort(flat_e)
    sorted_e = flat_e[order]
    counts = jnp.bincount(flat_e, length=N_EXPERTS)
    padded = (counts + RB - 1) // RB * RB
    start = jnp.cumsum(counts) - counts
    pad_end = jnp.cumsum(padded)
    pad_start = pad_end - padded
    dest = (pad_start[sorted_e] + jnp.arange(nk) - start[sorted_e]).astype(jnp.int32)
    cap = (-(-nk // RB)) * RB + N_EXPERTS * RB
    n_blk = cap // RB
    buf_tok = jnp.zeros((cap,), jnp.int32).at[dest].set(flat_tok[order])
    blk_e = jnp.minimum(jnp.searchsorted(pad_end, jnp.arange(n_blk) * RB, side='right'),
                        N_EXPERTS - 1).astype(jnp.int32)
    pos = jnp.zeros((nk,), jnp.int32).at[order].set(dest)
    n_used = (pad_end[-1] // RB).astype(jnp.int32).reshape(1)
    yb = moe_experts(xm, blk_e, n_used, buf_tok.reshape(n_blk, 1, RB), wg.astype(BF16), bg,
                     wu.astype(BF16), bu, wd.astype(BF16), bd)
    return moe_combine(yb, pos, top_w, x2d, gate, rows_per_batch)


def _pad_heads(w, n_heads, d, dp):
    lead = w.shape[:-1]
    w = w.reshape(lead + (n_heads, d))
    w = jnp.pad(w, [(0, 0)] * len(lead) + [(0, 0), (0, dp - d)])
    return w.reshape(lead + (n_heads * dp,))


def _even_in_weights(w_in, q_gain, k_gain):
    D = w_in.shape[0]
    cuts = np.cumsum(EVEN_IN_SPLITS)[:-1].tolist()
    q, kc, vc, ks, vs, kw, vw, gates, cq, kva = jnp.split(w_in, cuts, axis=1)
    half = MLA_ROPE_DIM // 2
    kr = kva[:, MLA_KV_RANK:]
    kr_perm = jnp.concatenate([kr[:, half:], kr[:, :half]], axis=1)
    G = NSA_KV_HEADS
    cols = [_pad_heads(q, NSA_HEADS, NSA_QK_DIM, HEAD_PAD),
            _pad_heads(kc, G, NSA_QK_DIM, HEAD_PAD), _pad_heads(ks, G, NSA_QK_DIM, HEAD_PAD),
            _pad_heads(kw, G, NSA_QK_DIM, HEAD_PAD), cq, kva[:, :MLA_KV_RANK], kr, kr_perm,
            vc, vs, vw, gates]
    w = jnp.concatenate(cols, axis=1)
    w = jnp.pad(w, ((0, 0), (0, EV_WIDTH - w.shape[1]))).astype(BF16)
    scale = NSA_QK_DIM ** -0.5
    pad_g = lambda g: jnp.pad(g, (0, HEAD_PAD - NSA_QK_DIM))
    gain = jnp.ones((EV_WIDTH,), F32)
    gain = gain.at[EV_Q:EV_Q + _NQ].set(jnp.tile(pad_g(q_gain * scale), NSA_HEADS))
    gain = gain.at[EV_KS:EV_KS + _NK].set(jnp.tile(pad_g(k_gain[1]), G))
    gain = gain.at[EV_KW:EV_KW + _NK].set(jnp.tile(pad_g(k_gain[2]), G))
    col = np.arange(EV_WIDTH)
    flag = ((col < EV_Q + _NQ) | ((col >= EV_KS) & (col < EV_KW + _NK))).astype(np.float32)
    return w, gain, jnp.asarray(flag)


def _even_mixer(xm, x2d, g_a, B, T, w_in, w_out, q_gain, k_gain, pe_k, pe_v, w_ck1, w_ck2, w_cv1, w_cv2,
                g_cq, g_ckv, w_uq, w_ukv, mq_gain, mk_gain):
    N, D = x2d.shape
    G, R = NSA_KV_HEADS, NSA_HEADS // NSA_KV_HEADS
    w_p, gain, flag = _even_in_weights(w_in, q_gain, k_gain)
    h = matmul(xm, w_p, head_norm=(HEAD_PAD, NSA_QK_DIM, gain, flag), name="in_proj_even")

    n_half = T // CMP_STRIDE
    kraw = h[:, EV_KC:EV_KC + _NK].reshape(B, n_half, CMP_STRIDE, G, HEAD_PAD)
    kraw = kraw.transpose(0, 3, 1, 2, 4).reshape(B * G, n_half, CMP_STRIDE * HEAD_PAD)
    vraw = h[:, EV_VC:EV_VC + _NV].reshape(B, n_half, CMP_STRIDE, G, NSA_V_DIM)
    vraw = vraw.transpose(0, 3, 1, 2, 4).reshape(B * G, n_half, CMP_STRIDE * NSA_V_DIM)
    kc = compress(kraw, pe_k, w_ck1, w_ck2, k_gain[0], NSA_QK_DIM, HEAD_PAD)
    vc = compress(vraw, pe_v, w_cv1, w_cv2, None, NSA_V_DIM, NSA_V_DIM)

    o_cmp, bits = cmp_attention(h, kc, vc, B, T)
    slopes = _alibi(NSA_HEADS)
    HV = NSA_HEADS * NSA_V_DIM
    o_slc = flash_attention(h, EV_Q, h, EV_KS, h, EV_VS, B=B, T=T, G=G, R=R, out_cols=HV,
                            slopes=slopes, bits=bits, name="nsa_slc_attn")
    o_win = flash_attention(h, EV_Q, h, EV_KW, h, EV_VW, B=B, T=T, G=G, R=R, out_cols=HV,
                            window=NSA_WINDOW, slopes=slopes, name="nsa_win_attn")
    o_a = nsa_combine(h, o_cmp, o_slc, o_win)

    H = MLA_HEADS
    dqk = MLA_NOPE_DIM + MLA_ROPE_DIM
    half = MLA_ROPE_DIM // 2
    wq = w_uq.reshape(MLA_Q_RANK, H, dqk)
    wq_rope = wq[:, :, MLA_NOPE_DIM:]
    wq_p = jnp.concatenate([wq, wq_rope[:, :, half:], wq_rope[:, :, :half]], axis=2)
    wq_p = wq_p.reshape(MLA_Q_RANK, H * HEAD_PAD).astype(BF16)
    q_raw = matmul(h, wq_p, a_col0=EV_CQ, a_pro="rms", a_gain=g_cq, name="mla_q_up")
    kv_raw = matmul(h, w_ukv.astype(BF16), a_col0=EV_CKV, a_pro="rms", a_gain=g_ckv, name="mla_kv_up")
    qa_lo, qa_hi, qb_hi = _rope_tables(mq_gain, T, dqk ** -0.5)
    ka_lo, ka_hi, kb_hi = _rope_tables(mk_gain, T, 1.0)
    q_m = mla_prep(q_raw, 0, HEAD_PAD, q_raw, LANES, HEAD_PAD, qa_lo, qa_hi, qb_hi, T)
    k_m = mla_prep(kv_raw, 0, HEAD_PAD, h, EV_KR, 0, ka_lo, ka_hi, kb_hi, T)
    o_b = flash_attention(q_m, 0, k_m, 0, kv_raw, MLA_NOPE_DIM, B=B, T=T, G=H, R=1,
                          out_cols=H * MLA_V_DIM, v_stride=MLA_NOPE_DIM + MLA_V_DIM, name="mla_attn")
    o_ab = jnp.concatenate([o_a, o_b], axis=1)
    return matmul(o_ab, w_out.astype(BF16), resid=(x2d, g_a, T), out_dtype=F32, name="out_proj_even")


def _odd_mixer(xm, x2d, g_a, B, T, w_in, b_in, w_out, b_out, q_gain, k_gain, sinks):
    D = w_in.shape[0]
    G, hd = SWA_KV_HEADS, SWA_HEAD_DIM
    nq = SWA_HEADS * hd

    def dup(t):
        lead = t.shape[:-1]
        t = t.reshape(lead + (G, 1, hd))
        return jnp.broadcast_to(t, lead + (G, 2, hd)).reshape(lead + (G * 2 * hd,))

    kw = G * hd
    w_p = jnp.concatenate([w_in[:, :nq], dup(w_in[:, nq:nq + kw]), dup(w_in[:, nq + kw:])], axis=1)
    b_p = jnp.concatenate([b_in[:nq], dup(b_in[nq:nq + kw]), dup(b_in[nq + kw:])])
    gain = jnp.concatenate([jnp.tile(q_gain * hd ** -0.5, SWA_HEADS), jnp.tile(k_gain, 2 * G),
                            jnp.ones((2 * kw,), F32)])
    flag = jnp.concatenate([jnp.ones((nq + 2 * kw,), F32), jnp.zeros((2 * kw,), F32)])
    h = matmul(xm, w_p.astype(BF16), bias=b_p, head_norm=(hd, hd, gain, flag), name="in_proj_odd")
    o_c = swa_attention(h, sinks, B, T, nq, nq + 2 * kw)
    return matmul(o_c, w_out.astype(BF16), bias=b_out, resid=(x2d, g_a, T), out_dtype=F32,
                  name="out_proj_odd")


def kernel(x, c, w_mod, mod_table, norm_attn, norm_ffn, w_in_even, w_out_even, nsa_q_gain, nsa_k_gain,
           nsa_pe_k, nsa_pe_v, nsa_w_ck1, nsa_w_ck2, nsa_w_cv1, nsa_w_cv2, mla_g_cq, mla_g_ckv, mla_w_uq,
           mla_w_ukv, mla_q_gain, mla_k_gain, w_in_odd, b_in_odd, w_out_odd, b_out_odd, swa_q_gain,
           swa_k_gain, swa_sinks, router_w, router_b, moe_w_gate, moe_b_gate, moe_w_up, moe_b_up,
           moe_w_down, moe_b_down):
    B, T, D = x.shape
    N = B * T
    depth = mod_table.shape[0]
    c_pad = jnp.pad(c, ((0, 8 - B % 8 if B % 8 else 0), (0, 0)))
    cond = matmul(c_pad, w_mod, a_pro="silu", out_dtype=F32, tn=1024, tk=1024, name="adaln_proj")[:B]
    x2d = x.reshape(N, D)
    for layer in range(depth):
        mod = (cond + mod_table[layer]).reshape(B, 6, 1, D)
        sh_a, sc_a, g_a, sh_f, sc_f, g_f = (mod[:, j] for j in range(6))
        xm = norm_mod(x2d, norm_attn[layer], sc_a, sh_a, T)
        i = layer // 2
        if layer % 2 == 0:
            x2d = _even_mixer(xm, x2d, g_a, B, T, w_in_even[i], w_out_even[i], nsa_q_gain[i], nsa_k_gain[i],
                              nsa_pe_k[i], nsa_pe_v[i], nsa_w_ck1[i], nsa_w_ck2[i], nsa_w_cv1[i],
                              nsa_w_cv2[i], mla_g_cq[i], mla_g_ckv[i], mla_w_uq[i], mla_w_ukv[i],
                              mla_q_gain[i], mla_k_gain[i])
        else:
            x2d = _odd_mixer(xm, x2d, g_a, B, T, w_in_odd[i], b_in_odd[i], w_out_odd[i], b_out_odd[i],
                             swa_q_gain[i], swa_k_gain[i], swa_sinks[i])
        x2d = moe_layer(x2d, norm_ffn[layer], sc_f, sh_f, g_f, T, router_w[layer], router_b[layer],
                        moe_w_gate[layer], moe_b_gate[layer], moe_w_up[layer], moe_b_up[layer],
                        moe_w_down[layer], moe_b_down[layer])
    return x2d.reshape(B, T, D)
```

```python
import functools
import math

import numpy as np
import jax
import jax.numpy as jnp
from jax import lax
from jax.experimental import pallas as pl
from jax.experimental.pallas import tpu as pltpu

BF16 = jnp.bfloat16
F32 = jnp.float32

NSA_HEADS = 16
NSA_KV_HEADS = 4
NSA_QK_DIM = 192
NSA_V_DIM = 128
CMP_BLOCK = 32
CMP_STRIDE = 16
SLC_BLOCK = 64
SLC_TOPN = 8
NSA_WINDOW = 512
MLA_HEADS = 16
MLA_Q_RANK = 1024
MLA_KV_RANK = 512
MLA_NOPE_DIM = 128
MLA_ROPE_DIM = 64
MLA_V_DIM = 128
ROPE_THETA = 10000.0
SWA_HEADS = 64
SWA_KV_HEADS = 8
SWA_HEAD_DIM = 64
SWA_WINDOW = 128
N_EXPERTS = 32
TOP_K = 4
SWIGLU_ALPHA = 1.702
SWIGLU_LIMIT = 7.0
MOE_ROW_BLOCK = 128
EPS = 1e-6

LANES = 128
HEAD_PAD = 256
NEG = -1e30
M_FLOOR = -1e29

EVEN_IN_SPLITS = (NSA_HEADS * NSA_QK_DIM,
                  NSA_KV_HEADS * NSA_QK_DIM, NSA_KV_HEADS * NSA_V_DIM,
                  NSA_KV_HEADS * NSA_QK_DIM, NSA_KV_HEADS * NSA_V_DIM,
                  NSA_KV_HEADS * NSA_QK_DIM, NSA_KV_HEADS * NSA_V_DIM,
                  3 * NSA_HEADS, MLA_Q_RANK, MLA_KV_RANK + MLA_ROPE_DIM)

_NQ = NSA_HEADS * HEAD_PAD
_NK = NSA_KV_HEADS * HEAD_PAD
_NV = NSA_KV_HEADS * NSA_V_DIM
EV_Q = 0
EV_KC = EV_Q + _NQ
EV_KS = EV_KC + _NK
EV_KW = EV_KS + _NK
EV_CQ = EV_KW + _NK
EV_CKV = EV_CQ + MLA_Q_RANK
EV_KR = EV_CKV + MLA_KV_RANK
EV_VC = EV_KR + 2 * MLA_ROPE_DIM
EV_VS = EV_VC + _NV
EV_VW = EV_VS + _NV
EV_GATE = EV_VW + _NV
EV_END = EV_GATE + LANES
EV_WIDTH = -(-EV_END // 512) * 512


def _tile(dim, want):
    t = min(dim, want)
    while dim % t:
        t //= 2
    return t


def _alibi(n):
    return jnp.exp2(-8.0 * jnp.arange(1, n + 1, dtype=F32) / n)


def _head_norm(res, hd, real_d):
    tm, tn = res.shape
    x2 = res * res
    pieces = []
    if hd == HEAD_PAD:
        for s in range(tn // hd):
            ss = jnp.sum(x2[:, s * hd:s * hd + LANES] + x2[:, s * hd + LANES:(s + 1) * hd],
                         axis=1, keepdims=True)
            r = lax.rsqrt(ss * (1.0 / real_d) + EPS)
            pieces.append(jnp.broadcast_to(r, (tm, hd)))
    else:
        lo = lax.broadcasted_iota(jnp.int32, (tm, LANES), 1) < hd
        for s in range(tn // LANES):
            c = x2[:, s * LANES:(s + 1) * LANES]
            ss_lo = jnp.sum(jnp.where(lo, c, 0.0), axis=1, keepdims=True)
            ss_hi = jnp.sum(jnp.where(lo, 0.0, c), axis=1, keepdims=True)
            r_lo = lax.rsqrt(ss_lo * (1.0 / real_d) + EPS)
            r_hi = lax.rsqrt(ss_hi * (1.0 / real_d) + EPS)
            pieces.append(jnp.where(lo, r_lo, r_hi))
    return jnp.concatenate(pieces, axis=1) if len(pieces) > 1 else pieces[0]


def _mm_kernel(*refs, nk, a_pro, has_bias, hd, real_d, has_resid):
    it = iter(refs)
    a_ref = next(it)
    w_ref = next(it)
    again_ref = next(it) if a_pro == "rms" else None
    b_ref = next(it) if has_bias else None
    gain_ref = flag_ref = None
    if hd:
        gain_ref = next(it)
        flag_ref = next(it)
    x_ref = gate_ref = None
    if has_resid:
        x_ref = next(it)
        gate_ref = next(it)
    o_ref = next(it)
    acc_ref = next(it) if nk > 1 else None

    a = a_ref[...]
    if a_pro == "silu":
        af = a.astype(F32)
        a = af * jax.nn.sigmoid(af)
    elif a_pro == "rms":
        af = a.astype(F32)
        r = lax.rsqrt(jnp.mean(af * af, axis=1, keepdims=True) + EPS)
        a = af * r * again_ref[...]
    part = jnp.dot(a.astype(BF16), w_ref[...].astype(BF16), preferred_element_type=F32)

    def finish(res):
        if has_bias:
            res = res + b_ref[...]
        if hd:
            r = _head_norm(res, hd, real_d)
            res = res * jnp.where(flag_ref[...] > 0.0, r, 1.0) * gain_ref[...]
        if has_resid:
            res = x_ref[...] + gate_ref[...] * res
        o_ref[...] = res.astype(o_ref.dtype)

    if nk == 1:
        finish(part)
    else:
        k = pl.program_id(2)

        @pl.when(k == 0)
        def _():
            acc_ref[...] = part

        @pl.when(k > 0)
        def _():
            acc_ref[...] += part

        @pl.when(k == nk - 1)
        def _():
            finish(acc_ref[...])


def matmul(a, w, *, a_col0=0, k_dim=None, a_pro=None, a_gain=None, bias=None,
           head_norm=None, resid=None, out_dtype=BF16, tm=1024, tn=512, tk=4096, name="mm"):
    M = a.shape[0]
    K, N = w.shape
    k_dim = K if k_dim is None else k_dim
    tm, tn, tk = _tile(M, tm), _tile(N, tn), _tile(K, tk)
    if a_pro == "rms":
        tk = K
    assert a_col0 % tk == 0 and M % tm == 0 and N % tn == 0 and K % tk == 0
    nk = K // tk
    koff = a_col0 // tk
    hd = head_norm[0] if head_norm else 0
    real_d = head_norm[1] if head_norm else 0
    if hd:
        assert tn % max(hd, LANES) == 0

    ins = [a, w]
    specs = [pl.BlockSpec((tm, tk), lambda i, j, k: (i, koff + k)),
             pl.BlockSpec((tk, tn), lambda i, j, k: (k, j))]
    if a_pro == "rms":
        ins.append(a_gain.reshape(1, K).astype(F32))
        specs.append(pl.BlockSpec((1, tk), lambda i, j, k: (0, k)))
    if bias is not None:
        ins.append(bias.reshape(1, N).astype(F32))
        specs.append(pl.BlockSpec((1, tn), lambda i, j, k: (0, j)))
    if hd:
        ins += [head_norm[2].reshape(1, N).astype(F32), head_norm[3].reshape(1, N).astype(F32)]
        specs += [pl.BlockSpec((1, tn), lambda i, j, k: (0, j))] * 2
    if resid is not None:
        x, gate, rows_per_batch = resid
        assert rows_per_batch % tm == 0
        bpb = rows_per_batch // tm
        ins += [x, gate]
        specs += [pl.BlockSpec((tm, tn), lambda i, j, k: (i, j)),
                  pl.BlockSpec((None, 1, tn), lambda i, j, k: (i // bpb, 0, j))]
    kern = functools.partial(_mm_kernel, nk=nk, a_pro=a_pro, has_bias=bias is not None,
                             hd=hd, real_d=real_d, has_resid=resid is not None)
    return pl.pallas_call(
        kern,
        out_shape=jax.ShapeDtypeStruct((M, N), out_dtype),
        grid=(M // tm, N // tn, nk),
        in_specs=specs,
        out_specs=pl.BlockSpec((tm, tn), lambda i, j, k: (i, j)),
        scratch_shapes=[pltpu.VMEM((tm, tn), F32)] if nk > 1 else [],
        compiler_params=pltpu.CompilerParams(
            dimension_semantics=("parallel", "parallel", "arbitrary")),
        name=name,
    )(*ins)


def _split_bf16(v):
    hi = v.astype(BF16)
    lo = (v - hi.astype(F32)).astype(BF16)
    return hi, lo


def _pack_bf16_pairs(v):
    half = v.shape[1] // 2
    vb = v.astype(BF16).astype(F32)
    lo = lax.shift_right_logical(lax.bitcast_convert_type(vb[:, :half], jnp.int32), 16)
    hi = lax.bitcast_convert_type(vb[:, half:], jnp.int32) & jnp.int32(-65536)
    return hi | lo


def _unpack_bf16_pairs(u):
    lo = lax.bitcast_convert_type(lax.shift_left(u, 16), F32)
    hi = lax.bitcast_convert_type(u & jnp.int32(-65536), F32)
    return jnp.concatenate([lo, hi], axis=1).astype(BF16)


def _norm_mod_kernel(x_ref, g_ref, sc_ref, sh_ref, *rest, route):
    x = x_ref[...]
    r = lax.rsqrt(jnp.mean(x * x, axis=1, keepdims=True) + EPS)
    xm = (x * r * g_ref[...]) * (1.0 + sc_ref[...]) + sh_ref[...]
    if not route:
        (o_ref,) = rest
        o_ref[...] = xm.astype(o_ref.dtype)
        return
    rw_ref, rb_ref, o_ref, e_ref, p_ref, rank_ref, cnt_ref, run_ref = rest
    o_ref[...] = _pack_bf16_pairs(xm)
    a_hi, a_lo = _split_bf16(xm)
    w = rw_ref[...]
    w_hi, w_lo = _split_bf16(w)
    logits = (jnp.dot(a_hi, w_hi, preferred_element_type=F32)
              + jnp.dot(a_hi, w_lo, preferred_element_type=F32)
              + jnp.dot(a_lo, w_hi, preferred_element_type=F32)) + rb_ref[...]
    tt = logits.shape[0]
    lane = lax.broadcasted_iota(jnp.int32, (tt, LANES), 1)
    work = jnp.where(lane < N_EXPERTS, logits, -jnp.inf)
    e_out = jnp.zeros((tt, LANES), jnp.int32)
    v_out = jnp.full((tt, LANES), -jnp.inf, F32)
    hits = []
    for kk in range(TOP_K):
        m = jnp.max(work, axis=1, keepdims=True)
        idx = jnp.min(jnp.where(work == m, lane, LANES), axis=1, keepdims=True)
        e_out = jnp.where(lane == kk, idx, e_out)
        v_out = jnp.where(lane == kk, m, v_out)
        hits.append(lane == idx)
        work = jnp.where(hits[-1], -jnp.inf, work)
    vmax = jnp.max(v_out, axis=1, keepdims=True)
    pe = jnp.exp(v_out - vmax)
    p_ref[...] = pe / jnp.sum(pe, axis=1, keepdims=True)
    e_ref[...] = e_out

    @pl.when(pl.program_id(0) == 0)
    def _():
        run_ref[...] = jnp.zeros_like(run_ref)

    onehot = jnp.zeros((tt, LANES), F32)
    for hit in hits:
        onehot = jnp.where(hit, 1.0, onehot)
    earlier = (lax.broadcasted_iota(jnp.int32, (tt, tt), 1)
               < lax.broadcasted_iota(jnp.int32, (tt, tt), 0))
    prefix = jnp.dot(jnp.where(earlier, 1.0, 0.0).astype(BF16), onehot.astype(BF16),
                     preferred_element_type=F32)
    base = run_ref[...] + prefix
    rank = jnp.zeros((tt, LANES), F32)
    for kk, hit in enumerate(hits):
        rk = jnp.sum(jnp.where(hit, base, 0.0), axis=1, keepdims=True)
        rank = jnp.where(lane == kk, rk, rank)
    rank_ref[...] = rank.astype(jnp.int32)
    total = run_ref[...] + jnp.sum(onehot, axis=0, keepdims=True)
    run_ref[...] = total
    cnt_ref[...] = total.astype(jnp.int32)


def norm_mod(x2d, gain, sc, sh, rows_per_batch, *, out_dtype=BF16, router=None, tt=256):
    N, D = x2d.shape
    tt = _tile(rows_per_batch, tt)
    bpb = rows_per_batch // tt
    ins = [x2d, gain.reshape(1, D), sc, sh]
    specs = [pl.BlockSpec((tt, D), lambda i: (i, 0)),
             pl.BlockSpec((1, D), lambda i: (0, 0)),
             pl.BlockSpec((None, 1, D), lambda i: (i // bpb, 0, 0)),
             pl.BlockSpec((None, 1, D), lambda i: (i // bpb, 0, 0))]
    if router is None:
        out_shape = [jax.ShapeDtypeStruct((N, D), out_dtype)]
        out_specs = [pl.BlockSpec((tt, D), lambda i: (i, 0))]
        scratch = []
    else:
        rw, rb = router
        E = rw.shape[1]
        rw_p = jnp.pad(rw, ((0, 0), (0, LANES - E)))
        rb_p = jnp.pad(rb, (0, LANES - E)).reshape(1, LANES)
        ins += [rw_p, rb_p]
        specs += [pl.BlockSpec((D, LANES), lambda i: (0, 0)),
                  pl.BlockSpec((1, LANES), lambda i: (0, 0))]
        out_shape = [jax.ShapeDtypeStruct((N, D // 2), jnp.int32),
                     jax.ShapeDtypeStruct((N, LANES), jnp.int32),
                     jax.ShapeDtypeStruct((N, LANES), F32),
                     jax.ShapeDtypeStruct((N, LANES), jnp.int32),
                     jax.ShapeDtypeStruct((1, LANES), jnp.int32)]
        row = pl.BlockSpec((tt, LANES), lambda i: (i, 0))
        out_specs = [pl.BlockSpec((tt, D // 2), lambda i: (i, 0)), row, row, row,
                     pl.BlockSpec((1, LANES), lambda i: (0, 0))]
        scratch = [pltpu.VMEM((1, LANES), F32)]
    res = pl.pallas_call(
        functools.partial(_norm_mod_kernel, route=router is not None),
        out_shape=out_shape, grid=(N // tt,), in_specs=specs, out_specs=out_specs,
        scratch_shapes=scratch,
        compiler_params=pltpu.CompilerParams(
            dimension_semantics=("parallel",) if router is None else ("arbitrary",)),
        name="norm_mod_route" if router is not None else "norm_mod",
    )(*ins)
    return res if router is not None else res[0]


def _compress_kernel(h_ref, pe_ref, w1_ref, w2_ref, g_ref, o_ref, xf_ref, *, real_d, norm, nh):
    n_tiles = xf_ref.shape[0]
    for j in range(n_tiles):
        xf_ref[j] = h_ref[:, j * LANES:(j + 1) * LANES].astype(F32)
    half = CMP_BLOCK // 2
    top = bot = None
    for l in range(half):
        parts = [xf_ref[j, pl.ds(l, nh, stride=CMP_STRIDE), :] for j in range(n_tiles)]
        xl = jnp.concatenate(parts, axis=1) if n_tiles > 1 else parts[0]
        t = jnp.dot((xl + pe_ref[l:l + 1, :]).astype(BF16), w1_ref[l], preferred_element_type=F32)
        b = jnp.dot((xl + pe_ref[half + l:half + l + 1, :]).astype(BF16), w1_ref[half + l],
                    preferred_element_type=F32)
        top = t if top is None else top + t
        bot = b if bot is None else bot + b
    hid = top + pltpu.roll(bot, nh - 1, 0)
    hid = hid * jax.nn.sigmoid(hid)
    y = jnp.dot(hid.astype(BF16), w2_ref[...], preferred_element_type=F32)
    if norm:
        r = lax.rsqrt(jnp.sum(y * y, axis=1, keepdims=True) * (1.0 / real_d) + EPS)
        y = y * r * g_ref[...]
    o_ref[...] = y.astype(o_ref.dtype)


def compress(h, col0, B, T, pe, w1, w2, gain, d, dp):
    G = NSA_KV_HEADS
    nh = T // CMP_STRIDE
    cb = col0 // dp
    pe_p = jnp.pad(pe, ((0, 0), (0, dp - d)))
    w1_p = jnp.pad(w1.reshape(CMP_BLOCK, d, d), ((0, 0), (0, dp - d), (0, dp - d))).astype(BF16)
    w2_p = jnp.pad(w2, ((0, dp - d), (0, dp - d))).astype(BF16)
    g_p = (jnp.ones((dp,), F32) if gain is None else jnp.pad(gain, (0, dp - d))).reshape(1, dp)
    full = lambda shape: pl.BlockSpec(shape, lambda b, g: (0,) * len(shape))
    return pl.pallas_call(
        functools.partial(_compress_kernel, real_d=d, norm=gain is not None, nh=nh),
        out_shape=jax.ShapeDtypeStruct((B * G, nh, dp), BF16),
        grid=(B, G),
        in_specs=[pl.BlockSpec((T, dp), lambda b, g: (b, cb + g)),
                  full((CMP_BLOCK, dp)), full((CMP_BLOCK, dp, dp)), full((dp, dp)), full((1, dp))],
        out_specs=pl.BlockSpec((None, nh, dp), lambda b, g: (b * G + g, 0, 0)),
        scratch_shapes=[pltpu.VMEM((dp // LANES, T, LANES), F32)],
        compiler_params=pltpu.CompilerParams(dimension_semantics=("parallel", "parallel")),
        name="nsa_compress",
    )(h, pe_p, w1_p, w2_p, g_p)


def _stack_heads(q_ref, n, w):
    return jnp.concatenate([q_ref[:, r * w:(r + 1) * w] for r in range(n)], axis=0)


def _row_scalars(vals, tq):
    rows = len(vals) * tq
    rid = lax.broadcasted_iota(jnp.int32, (rows, 1), 0) // tq
    col = jnp.full((rows, 1), vals[-1], F32)
    for r in range(len(vals) - 2, -1, -1):
        col = jnp.where(rid == r, vals[r], col)
    return col


def _nt_dot(a, b):
    return lax.dot_general(a, b, (((1,), (1,)), ((), ())), preferred_element_type=F32)


def _cmp_attn_kernel(slope_ref, q_ref, kc_ref, vc_ref, ovl_ref, o_ref, bits_ref, *, tq, R, n_slc):
    g = pl.program_id(1)
    t0 = pl.program_id(2) * tq
    rows = R * tq
    q = _stack_heads(q_ref, R, HEAD_PAD)
    s = _nt_dot(q, kc_ref[...])
    ncol = s.shape[1]
    row_t = t0 + lax.broadcasted_iota(jnp.int32, (rows, ncol), 0) % tq
    n_id = lax.broadcasted_iota(jnp.int32, (rows, ncol), 1)
    dist = row_t - (n_id * CMP_STRIDE + (CMP_BLOCK - 1))
    slope = _row_scalars([slope_ref[g * R + r] for r in range(R)], tq)
    valid = dist >= 0
    s = jnp.where(valid, s - slope * dist.astype(F32), NEG)
    m = jnp.max(s, axis=1, keepdims=True)
    p = jnp.where(valid, jnp.exp(s - m), 0.0)
    p = p / jnp.maximum(jnp.sum(p, axis=1, keepdims=True), 1e-30)
    o = jnp.dot(p.astype(BF16), vc_ref[...], preferred_element_type=F32)
    for r in range(R):
        o_ref[:, r * NSA_V_DIM:(r + 1) * NSA_V_DIM] = o[r * tq:(r + 1) * tq].astype(o_ref.dtype)

    psum = p[0:tq]
    for r in range(1, R):
        psum = psum + p[r * tq:(r + 1) * tq]
    p_hi, p_lo = _split_bf16(psum)
    imp = (jnp.dot(p_hi, ovl_ref[...], preferred_element_type=F32)
           + jnp.dot(p_lo, ovl_ref[...], preferred_element_type=F32))
    lane = lax.broadcasted_iota(jnp.int32, (tq, LANES), 1)
    cur = (t0 + lax.broadcasted_iota(jnp.int32, (tq, LANES), 0)) // SLC_BLOCK
    forced = (lane == 0) | (lane == cur) | (lane == cur - 1)
    work = jnp.where(forced, jnp.inf, jnp.where(lane > cur, -jnp.inf, imp))
    removed = -3.0e38
    work = jnp.where(lane < n_slc, jnp.where(work == -jnp.inf, -2.0e38, work), removed)
    sel = jnp.zeros((tq, LANES), jnp.bool_)
    for _ in range(min(SLC_TOPN, n_slc)):
        mx = jnp.max(work, axis=1, keepdims=True)
        idx = jnp.min(jnp.where(work == mx, lane, LANES), axis=1, keepdims=True)
        hit = lane == idx
        sel = sel | hit
        work = jnp.where(hit, removed, work)
    half = 16
    w_lo = jnp.where(sel & (lane < half), jnp.left_shift(1, jnp.minimum(lane, half - 1)), 0)
    w_hi = jnp.where(sel & (lane >= half), jnp.left_shift(1, jnp.clip(lane - half, 0, half - 1)), 0)
    b_lo = jnp.sum(w_lo.astype(F32), axis=1, keepdims=True).astype(jnp.int32)
    b_hi = jnp.sum(w_hi.astype(F32), axis=1, keepdims=True).astype(jnp.int32)
    bits = b_lo | jnp.left_shift(b_hi, half)
    bits_ref[...] = jnp.broadcast_to(bits, (tq, LANES))


def cmp_attention(h, kc, vc, B, T, tq=256):
    G, R = NSA_KV_HEADS, NSA_HEADS // NSA_KV_HEADS
    tq = _tile(T, tq)
    nq = T // tq
    n_slc = T // SLC_BLOCK
    n_cmp = kc.shape[1]
    assert n_slc <= 32 and n_cmp <= LANES and n_cmp % 8 == 0
    cmp_start = np.arange(n_cmp) * CMP_STRIDE
    slc_start = np.arange(LANES) * SLC_BLOCK
    ovl = ((cmp_start[:, None] < slc_start[None, :] + SLC_BLOCK)
           & (cmp_start[:, None] + CMP_BLOCK > slc_start[None, :])
           & (np.arange(LANES)[None, :] < n_slc) & (np.arange(n_cmp)[:, None] < n_cmp - 1))
    ovl = jnp.asarray(ovl, BF16)
    qw = R * HEAD_PAD
    return pl.pallas_call(
        functools.partial(_cmp_attn_kernel, tq=tq, R=R, n_slc=n_slc),
        out_shape=[jax.ShapeDtypeStruct((B * T, NSA_HEADS * NSA_V_DIM), BF16),
                   jax.ShapeDtypeStruct((B, G, T, LANES), jnp.int32)],
        grid=(B, G, nq),
        in_specs=[pl.BlockSpec(memory_space=pltpu.SMEM),
                  pl.BlockSpec((tq, qw), lambda b, g, i: (b * nq + i, EV_Q // qw + g)),
                  pl.BlockSpec((None, n_cmp, HEAD_PAD), lambda b, g, i: (b * G + g, 0, 0)),
                  pl.BlockSpec((None, n_cmp, NSA_V_DIM), lambda b, g, i: (b * G + g, 0, 0)),
                  pl.BlockSpec((n_cmp, LANES), lambda b, g, i: (0, 0))],
        out_specs=[pl.BlockSpec((tq, R * NSA_V_DIM), lambda b, g, i: (b * nq + i, g)),
                   pl.BlockSpec((None, None, tq, LANES), lambda b, g, i: (b, g, i, 0))],
        compiler_params=pltpu.CompilerParams(dimension_semantics=("parallel", "parallel", "parallel")),
        name="nsa_cmp_attn",
    )(_alibi(NSA_HEADS), h, kc, vc, ovl)


def _flash_kernel(*refs, tq, tk, R, vd, window, alibi, use_bits):
    it = iter(refs)
    slope_ref = next(it) if alibi else None
    q_ref = next(it)
    k_ref = next(it)
    v_ref = next(it)
    bits_ref = next(it) if use_bits else None
    o_ref = next(it)
    m_ref = next(it)
    l_ref = next(it)
    acc_ref = next(it)

    g = pl.program_id(1)
    i = pl.program_id(2)
    t0 = i * tq
    rows = R * tq
    q = _stack_heads(q_ref, R, HEAD_PAD)
    rel0 = (lax.broadcasted_iota(jnp.int32, (rows, tk), 0) % tq
            - lax.broadcasted_iota(jnp.int32, (rows, tk), 1))
    if alibi:
        slope = _row_scalars([slope_ref[g * R + r] for r in range(R)], tq)
        bias0 = -slope * rel0.astype(F32)
    if use_bits:
        b1 = bits_ref[...]
        b1 = jnp.concatenate([b1] * R, axis=0) if R > 1 else b1
        bits = jnp.concatenate([b1] * (tk // LANES), axis=1) if tk > LANES else b1
        blk0 = lax.broadcasted_iota(jnp.int32, (rows, tk), 1) // SLC_BLOCK

    m_ref[...] = jnp.full((rows, 1), M_FLOOR, F32)
    l_ref[...] = jnp.zeros((rows, 1), F32)
    acc_ref[...] = jnp.zeros((rows, vd), F32)

    def body(c, carry):
        s0 = pl.multiple_of(c * tk, tk)
        kc = k_ref[pl.ds(s0, tk), :]
        vc = v_ref[pl.ds(s0, tk), :]
        off = t0 - s0
        s = _nt_dot(q, kc)
        if alibi:
            s = s + (bias0 - slope * off.astype(F32))
        valid = rel0 >= -off
        if window is not None:
            valid = valid & (rel0 < window - off)
        if use_bits:
            valid = valid & ((jnp.right_shift(bits, blk0 + s0 // SLC_BLOCK) & 1) != 0)
        s = jnp.where(valid, s, NEG)
        m_old = m_ref[...]
        m_new = jnp.maximum(m_old, jnp.max(s, axis=1, keepdims=True))
        alpha = jnp.exp(m_old - m_new)
        p = jnp.exp(s - m_new)
        l_ref[...] = alpha * l_ref[...] + jnp.sum(p, axis=1, keepdims=True)
        acc_ref[...] = alpha * acc_ref[...] + jnp.dot(p.astype(BF16), vc, preferred_element_type=F32)
        m_ref[...] = m_new
        return carry

    c_hi = (t0 + tq - 1) // tk
    c_lo = 0 if window is None else jnp.maximum((t0 - window + 1) // tk, 0)
    lax.fori_loop(c_lo, c_hi + 1, body, 0)
    o = acc_ref[...] / jnp.maximum(l_ref[...], 1e-30)
    for r in range(R):
        o_ref[:, r * vd:(r + 1) * vd] = o[r * tq:(r + 1) * tq].astype(o_ref.dtype)


def flash_attention(q_arr, q_col0, k_arr, k_col0, v_arr, v_col0, *, B, T, G, R, out_cols, out_col0=0,
                    v_stride=NSA_V_DIM, window=None, slopes=None, bits=None, tq=256, tk=256, name="flash"):
    vd = NSA_V_DIM
    tq = _tile(T, tq)
    tk = _tile(T, tk)
    nq = T // tq
    qw = R * HEAD_PAD
    assert q_col0 % qw == 0 and k_col0 % HEAD_PAD == 0 and v_col0 % vd == 0 and out_col0 % (R * vd) == 0
    assert v_stride % vd == 0
    qb, kb, vb, ob = q_col0 // qw, k_col0 // HEAD_PAD, v_col0 // vd, out_col0 // (R * vd)
    vs = v_stride // vd
    ins, specs = [], []
    if slopes is not None:
        ins.append(slopes)
        specs.append(pl.BlockSpec(memory_space=pltpu.SMEM))
    ins += [q_arr, k_arr, v_arr]
    specs += [pl.BlockSpec((tq, qw), lambda b, g, i: (b * nq + i, qb + g)),
              pl.BlockSpec((T, HEAD_PAD), lambda b, g, i: (b, kb + g)),
              pl.BlockSpec((T, vd), lambda b, g, i: (b, vb + g * vs))]
    if bits is not None:
        ins.append(bits)
        specs.append(pl.BlockSpec((None, None, tq, LANES), lambda b, g, i: (b, g, i, 0)))
    rows = R * tq
    return pl.pallas_call(
        functools.partial(_flash_kernel, tq=tq, tk=tk, R=R, vd=vd, window=window,
                          alibi=slopes is not None, use_bits=bits is not None),
        out_shape=jax.ShapeDtypeStruct((B * T, out_cols), BF16),
        grid=(B, G, nq),
        in_specs=specs,
        out_specs=pl.BlockSpec((tq, R * vd), lambda b, g, i: (b * nq + i, ob + g)),
        scratch_shapes=[pltpu.VMEM((rows, 1), F32), pltpu.VMEM((rows, 1), F32),
                        pltpu.VMEM((rows, vd), F32)],
        compiler_params=pltpu.CompilerParams(dimension_semantics=("parallel", "parallel", "parallel")),
        name=name,
    )(*ins)


def _nsa_combine_kernel(gate_ref, oc_ref, os_ref, ow_ref, o_ref):
    gates = jax.nn.sigmoid(gate_ref[...].astype(F32))
    for h in range(NSA_HEADS):
        sl = slice(h * NSA_V_DIM, (h + 1) * NSA_V_DIM)
        acc = None
        for br, ref in enumerate((oc_ref, os_ref, ow_ref)):
            gcol = gates[:, 3 * h + br:3 * h + br + 1]
            term = gcol * ref[:, sl].astype(F32)
            acc = term if acc is None else acc + term
        o_ref[:, sl] = acc.astype(o_ref.dtype)


def nsa_combine(h, o_cmp, o_slc, o_win, tt=256):
    N, W = o_cmp.shape
    tt = _tile(N, tt)
    gb = EV_GATE // LANES
    blk = pl.BlockSpec((tt, W), lambda i: (i, 0))
    return pl.pallas_call(
        _nsa_combine_kernel,
        out_shape=jax.ShapeDtypeStruct((N, W), BF16),
        grid=(N // tt,),
        in_specs=[pl.BlockSpec((tt, LANES), lambda i: (i, gb)), blk, blk, blk],
        out_specs=blk,
        compiler_params=pltpu.CompilerParams(dimension_semantics=("parallel",)),
        name="nsa_combine",
    )(h, o_cmp, o_slc, o_win)


def _mla_prep_kernel(*refs, shared_rope):
    if shared_rope:
        x_ref, hi_ref, alo_ref, ahi_ref, bhi_ref, o_ref = refs
    else:
        x_ref, alo_ref, ahi_ref, bhi_ref, o_ref = refs
    tt = x_ref.shape[0]
    is_rope = lax.broadcasted_iota(jnp.int32, (tt, LANES), 1) < MLA_ROPE_DIM
    a_hi, b_hi, a_lo = ahi_ref[...], bhi_ref[...], alo_ref[...]

    def rope_part(x_hi):
        ss = jnp.sum(jnp.where(is_rope, x_hi * x_hi, 0.0), axis=1, keepdims=True)
        return x_hi * a_hi + pltpu.roll(x_hi * b_hi, MLA_ROPE_DIM, 1), ss

    if shared_rope:
        y_hi, ss_hi = rope_part(hi_ref[...].astype(F32))
    for h in range(MLA_HEADS):
        c = h * HEAD_PAD
        x_lo = x_ref[:, c:c + LANES].astype(F32)
        if not shared_rope:
            y_hi, ss_hi = rope_part(x_ref[:, c + LANES:c + HEAD_PAD].astype(F32))
        ss = jnp.sum(x_lo * x_lo, axis=1, keepdims=True) + ss_hi
        r = lax.rsqrt(ss * (1.0 / (MLA_NOPE_DIM + MLA_ROPE_DIM)) + EPS)
        o_ref[:, c:c + LANES] = (x_lo * r * a_lo).astype(o_ref.dtype)
        o_ref[:, c + LANES:c + HEAD_PAD] = (y_hi * r).astype(o_ref.dtype)


def mla_prep(x_arr, rope_arr, rope_col0, a_lo, a_hi, b_hi, T, tt=256):
    N = x_arr.shape[0]
    W = MLA_HEADS * HEAD_PAD
    tt = _tile(T, tt)
    nt = T // tt
    shared = rope_arr is not None
    ins = [x_arr]
    specs = [pl.BlockSpec((tt, W), lambda i: (i, 0))]
    if shared:
        rb = rope_col0 // LANES
        ins.append(rope_arr)
        specs.append(pl.BlockSpec((tt, LANES), lambda i: (i, rb)))
    ins += [a_lo, a_hi, b_hi]
    specs += [pl.BlockSpec((1, LANES), lambda i: (0, 0)),
              pl.BlockSpec((tt, LANES), lambda i: (i % nt, 0)),
              pl.BlockSpec((tt, LANES), lambda i: (i % nt, 0))]
    return pl.pallas_call(
        functools.partial(_mla_prep_kernel, shared_rope=shared),
        out_shape=jax.ShapeDtypeStruct((N, W), BF16),
        grid=(N // tt,),
        in_specs=specs,
        out_specs=pl.BlockSpec((tt, W), lambda i: (i, 0)),
        compiler_params=pltpu.CompilerParams(dimension_semantics=("parallel",)),
        name="mla_prep",
    )(*ins)


def _rope_tables(gain, T, scale):
    half = MLA_ROPE_DIM // 2
    inv_freq = ROPE_THETA ** (-jnp.arange(half, dtype=F32) / half)
    ang = jnp.arange(T, dtype=F32)[:, None] * inv_freq[None, :]
    cos2 = jnp.concatenate([jnp.cos(ang), jnp.cos(ang)], axis=1)
    sin_s = jnp.concatenate([-jnp.sin(ang), jnp.sin(ang)], axis=1)
    g_nope, g_rope = gain[:MLA_NOPE_DIM], gain[MLA_NOPE_DIM:]
    g_perm = jnp.concatenate([g_rope[half:], g_rope[:half]])
    zeros = jnp.zeros((T, MLA_ROPE_DIM), F32)
    a_lo = (g_nope * scale).reshape(1, LANES)
    a_hi = jnp.concatenate([g_rope[None, :] * cos2 * scale, zeros], axis=1)
    b_hi = jnp.concatenate([zeros, g_perm[None, :] * sin_s * scale], axis=1)
    return a_lo, a_hi, b_hi


def _swa_kernel(slope_ref, sink_ref, q_ref, k_ref, v_ref, o_ref, *, tq, tk, R):
    g = pl.program_id(1)
    t0 = pl.program_id(2) * tq
    npair = R // 2
    rows = npair * tq
    qp = _stack_heads(q_ref, npair, LANES)
    lane = lax.broadcasted_iota(jnp.int32, (rows, LANES), 1)
    first = lane < SWA_HEAD_DIM
    start = pl.multiple_of(jnp.maximum(t0 - (tk - tq), 0), tq)
    kc = k_ref[pl.ds(start, tk), :]
    vc = v_ref[pl.ds(start, tk), :]
    rel = (t0 - start + lax.broadcasted_iota(jnp.int32, (rows, tk), 0) % tq
           - lax.broadcasted_iota(jnp.int32, (rows, tk), 1))
    valid = (rel >= 0) & (rel < SWA_WINDOW)
    relf = rel.astype(F32)
    outs = []
    for u in range(2):
        qu = jnp.where(first if u == 0 else ~first, qp, jnp.zeros_like(qp))
        heads = [g * R + 2 * p + u for p in range(npair)]
        slope = _row_scalars([slope_ref[h] for h in heads], tq)
        sink = _row_scalars([sink_ref[h] for h in heads], tq)
        s = jnp.where(valid, _nt_dot(qu, kc) - slope * relf, NEG)
        m = jnp.maximum(jnp.max(s, axis=1, keepdims=True), sink)
        p = jnp.exp(s - m)
        denom = jnp.sum(p, axis=1, keepdims=True) + jnp.exp(sink - m)
        p = p / jnp.maximum(denom, 1e-30)
        outs.append(jnp.dot(p.astype(BF16), vc, preferred_element_type=F32))
    o = jnp.where(first, outs[0], outs[1])
    for p_ in range(npair):
        o_ref[:, p_ * LANES:(p_ + 1) * LANES] = o[p_ * tq:(p_ + 1) * tq].astype(o_ref.dtype)


def swa_attention(h, sinks, B, T, k_col0, v_col0, tq=128):
    G, R = SWA_KV_HEADS, SWA_HEADS // SWA_KV_HEADS
    tq = _tile(T, tq)
    tk = min(T, tq + SWA_WINDOW)
    assert tk - tq in (0, SWA_WINDOW) and SWA_WINDOW % tq == 0 or tk == T
    nq = T // tq
    qw = R * SWA_HEAD_DIM
    kb, vb = k_col0 // LANES, v_col0 // LANES
    return pl.pallas_call(
        functools.partial(_swa_kernel, tq=tq, tk=tk, R=R),
        out_shape=jax.ShapeDtypeStruct((B * T, SWA_HEADS * SWA_HEAD_DIM), BF16),
        grid=(B, G, nq),
        in_specs=[pl.BlockSpec(memory_space=pltpu.SMEM), pl.BlockSpec(memory_space=pltpu.SMEM),
                  pl.BlockSpec((tq, qw), lambda b, g, i: (b * nq + i, g)),
                  pl.BlockSpec((T, LANES), lambda b, g, i: (b, kb + g)),
                  pl.BlockSpec((T, LANES), lambda b, g, i: (b, vb + g))],
        out_specs=pl.BlockSpec((tq, qw), lambda b, g, i: (b * nq + i, g)),
        compiler_params=pltpu.CompilerParams(dimension_semantics=("parallel", "parallel", "parallel")),
        name="swa_attn",
    )(_alibi(SWA_HEADS), sinks.astype(F32), h, h, h)


def _dispatch_kernel(pos_ref, src_hbm, zero_hbm, dst_hbm, sem, *, td):
    del zero_hbm
    t0 = pl.program_id(0) * td

    def row_copy(t, k):
        return pltpu.make_async_copy(src_hbm.at[pl.ds(t0 + t, 1)],
                                     dst_hbm.at[pl.ds(pos_ref[0, t * TOP_K + k], 1)], sem)

    def issue(t, c):
        for k in range(TOP_K):
            row_copy(t, k).start()
        return c

    def drain(t, c):
        for k in range(TOP_K):
            row_copy(t, k).wait()
        return c

    lax.fori_loop(0, td, issue, 0)
    lax.fori_loop(0, td, drain, 0)


def moe_dispatch(xm_packed, pos, cap, td=256):
    N, W = xm_packed.shape
    td = _tile(N, td)
    pos3 = pos.reshape(N // td, 1, td * TOP_K)
    return pl.pallas_call(
        functools.partial(_dispatch_kernel, td=td),
        out_shape=jax.ShapeDtypeStruct((cap, W), jnp.int32),
        grid=(N // td,),
        in_specs=[pl.BlockSpec((None, 1, td * TOP_K), lambda i: (i, 0, 0), memory_space=pltpu.SMEM),
                  pl.BlockSpec(memory_space=pl.ANY), pl.BlockSpec(memory_space=pl.ANY)],
        out_specs=pl.BlockSpec(memory_space=pl.ANY),
        scratch_shapes=[pltpu.SemaphoreType.DMA(())],
        input_output_aliases={2: 0},
        compiler_params=pltpu.CompilerParams(dimension_semantics=("arbitrary",), has_side_effects=True),
        name="moe_dispatch",
    )(pos3, xm_packed, jnp.zeros((cap, W), jnp.int32))


def _expert_kernel(be_ref, first_ref, nu_ref, xs_ref, wg_hbm, bg_ref, wu_hbm, bu_ref, wd_hbm, bd_ref,
                   o_ref, wg_s, wu_s, wd_s, stg_a, stg_d, sem, *, ca, cd):
    blk = pl.program_id(0)
    e = be_ref[blk]
    D, F = wg_s.shape

    chunks = []
    for src, dst in ((wg_hbm, wg_s), (wu_hbm, wu_s)):
        chunks += [(src, dst, c * ca, ca, stg_a, 0) for c in range(D // ca)]
    chunks += [(wd_hbm, wd_s, c * cd, cd, stg_d, 2) for c in range(F // cd)]

    def chunk_copy(i):
        src, _, r0, n, stg, s0 = chunks[i]
        slot = i % 2
        return pltpu.make_async_copy(src.at[e, pl.ds(r0, n), :], stg.at[slot], sem.at[s0 + slot])

    @pl.when((blk < nu_ref[0]) & (first_ref[blk] == 1))
    def _():
        chunk_copy(0).start()
        for i, (_, dst, r0, n, stg, _) in enumerate(chunks):
            if i + 1 < len(chunks):
                chunk_copy(i + 1).start()
            chunk_copy(i).wait()
            dst[pl.ds(r0, n), :] = stg[i % 2].astype(BF16)

    @pl.when(blk < nu_ref[0])
    def _():
        x = _unpack_bf16_pairs(xs_ref[...])
        gg = jnp.dot(x, wg_s[...], preferred_element_type=F32) + bg_ref[...]
        uu = jnp.dot(x, wu_s[...], preferred_element_type=F32) + bu_ref[...]
        gg = jnp.minimum(gg, SWIGLU_LIMIT)
        uu = jnp.clip(uu, -SWIGLU_LIMIT, SWIGLU_LIMIT)
        act = gg * jax.nn.sigmoid(SWIGLU_ALPHA * gg) * (uu + 1.0)
        o_ref[...] = jnp.dot(act.astype(BF16), wd_s[...], preferred_element_type=F32) + bd_ref[...]

    @pl.when(blk >= nu_ref[0])
    def _():
        o_ref[...] = jnp.zeros_like(o_ref)


def moe_experts(xs, blk_e, first, n_used, wg, bg, wu, bu, wd, bd, rb=MOE_ROW_BLOCK):
    cap, W = xs.shape
    E, D, F = wg.shape
    n_blk = cap // rb
    ca, cd = _tile(D, 512), _tile(F, 128)
    by_expert = lambda shape: pl.BlockSpec((None,) + shape, lambda i, be, fi, nu: (be[i], 0, 0))
    grid_spec = pltpu.PrefetchScalarGridSpec(
        num_scalar_prefetch=3,
        grid=(n_blk,),
        in_specs=[pl.BlockSpec((rb, W), lambda i, be, fi, nu: (i, 0)),
                  pl.BlockSpec(memory_space=pl.ANY), by_expert((1, F)),
                  pl.BlockSpec(memory_space=pl.ANY), by_expert((1, F)),
                  pl.BlockSpec(memory_space=pl.ANY), by_expert((1, D))],
        out_specs=pl.BlockSpec((rb, D), lambda i, be, fi, nu: (i, 0)),
        scratch_shapes=[pltpu.VMEM((D, F), BF16), pltpu.VMEM((D, F), BF16), pltpu.VMEM((F, D), BF16),
                        pltpu.VMEM((2, ca, F), F32), pltpu.VMEM((2, cd, D), F32),
                        pltpu.SemaphoreType.DMA((4,))],
    )
    return pl.pallas_call(
        functools.partial(_expert_kernel, ca=ca, cd=cd),
        out_shape=jax.ShapeDtypeStruct((cap, D), F32),
        grid_spec=grid_spec,
        compiler_params=pltpu.CompilerParams(dimension_semantics=("arbitrary",)),
        name="moe_experts",
    )(blk_e, first, n_used, xs, wg, bg.reshape(E, 1, F), wu, bu.reshape(E, 1, F), wd, bd.reshape(E, 1, D))


def _moe_combine_kernel(pos_ref, y_hbm, x_ref, w_ref, gate_ref, o_ref, ybuf, sem, *, tt):
    def row_copy(t, k):
        return pltpu.make_async_copy(y_hbm.at[pl.ds(pos_ref[0, t * TOP_K + k], 1)],
                                     ybuf.at[k, pl.ds(t, 1)], sem)

    def issue(t, c):
        for k in range(TOP_K):
            row_copy(t, k).start()
        return c

    def drain(t, c):
        for k in range(TOP_K):
            row_copy(t, k).wait()
        return c

    lax.fori_loop(0, tt, issue, 0)
    lax.fori_loop(0, tt, drain, 0)
    w = w_ref[...]
    y = w[:, 0:1] * ybuf[0]
    for k in range(1, TOP_K):
        y = y + w[:, k:k + 1] * ybuf[k]
    o_ref[...] = x_ref[...] + gate_ref[...] * y


def moe_combine(yb, pos, top_w, x2d, gate, rows_per_batch, tt=64):
    N, D = x2d.shape
    tt = _tile(rows_per_batch, tt)
    bpb = rows_per_batch // tt
    pos3 = pos.reshape(N // tt, 1, tt * TOP_K)
    return pl.pallas_call(
        functools.partial(_moe_combine_kernel, tt=tt),
        out_shape=jax.ShapeDtypeStruct((N, D), F32),
        grid=(N // tt,),
        in_specs=[pl.BlockSpec((None, 1, tt * TOP_K), lambda i: (i, 0, 0), memory_space=pltpu.SMEM),
                  pl.BlockSpec(memory_space=pl.ANY),
                  pl.BlockSpec((tt, D), lambda i: (i, 0)),
                  pl.BlockSpec((tt, LANES), lambda i: (i, 0)),
                  pl.BlockSpec((None, 1, D), lambda i: (i // bpb, 0, 0))],
        out_specs=pl.BlockSpec((tt, D), lambda i: (i, 0)),
        scratch_shapes=[pltpu.VMEM((TOP_K, tt, D), F32), pltpu.SemaphoreType.DMA(())],
        compiler_params=pltpu.CompilerParams(dimension_semantics=("arbitrary",)),
        name="moe_combine",
    )(pos3, yb, x2d, top_w, gate)


def moe_layer(x2d, gain, sc, sh, gate, rows_per_batch, router_w, router_b, wg, bg, wu, bu, wd, bd):
    N, D = x2d.shape
    RB = MOE_ROW_BLOCK
    E = N_EXPERTS
    xm_packed, top_e, top_w, rank, counts = norm_mod(x2d, gain, sc, sh, rows_per_batch,
                                                     router=(router_w, router_b))
    nk = N * TOP_K
    counts = counts[0, :E]
    padded = (counts + RB - 1) // RB * RB
    pad_end = jnp.cumsum(padded)
    pad_start = pad_end - padded
    e4 = top_e[:, :TOP_K]
    start_of = jnp.sum(jnp.where(e4[:, :, None] == jnp.arange(E)[None, None, :],
                                 pad_start[None, None, :], 0), axis=2)
    pos = (start_of + rank[:, :TOP_K]).astype(jnp.int32)
    cap = (-(-nk // RB)) * RB + E * RB
    n_blk = cap // RB
    blk_start = jnp.arange(n_blk, dtype=jnp.int32) * RB
    blk_e = jnp.minimum(jnp.sum(pad_end[None, :] <= blk_start[:, None], axis=1), E - 1).astype(jnp.int32)
    first = jnp.concatenate([jnp.ones((1,), jnp.int32), (blk_e[1:] != blk_e[:-1]).astype(jnp.int32)])
    n_used = (pad_end[-1] // RB).astype(jnp.int32).reshape(1)
    xs = moe_dispatch(xm_packed, pos, cap)
    yb = moe_experts(xs, blk_e, first, n_used, wg, bg, wu, bu, wd, bd)
    return moe_combine(yb, pos, top_w, x2d, gate, rows_per_batch)


def _pad_heads(w, n_heads, d, dp):
    lead = w.shape[:-1]
    w = w.reshape(lead + (n_heads, d))
    w = jnp.pad(w, [(0, 0)] * len(lead) + [(0, 0), (0, dp - d)])
    return w.reshape(lead + (n_heads * dp,))


def _even_in_weights(w_in, q_gain, k_gain):
    D = w_in.shape[0]
    cuts = np.cumsum(EVEN_IN_SPLITS)[:-1].tolist()
    q, kc, vc, ks, vs, kw, vw, gates, cq, kva = jnp.split(w_in, cuts, axis=1)
    half = MLA_ROPE_DIM // 2
    kr = kva[:, MLA_KV_RANK:]
    kr_perm = jnp.concatenate([kr[:, half:], kr[:, :half]], axis=1)
    G = NSA_KV_HEADS
    cols = [_pad_heads(q, NSA_HEADS, NSA_QK_DIM, HEAD_PAD),
            _pad_heads(kc, G, NSA_QK_DIM, HEAD_PAD), _pad_heads(ks, G, NSA_QK_DIM, HEAD_PAD),
            _pad_heads(kw, G, NSA_QK_DIM, HEAD_PAD), cq, kva[:, :MLA_KV_RANK], kr, kr_perm,
            vc, vs, vw, gates]
    w = jnp.concatenate(cols, axis=1)
    w = jnp.pad(w, ((0, 0), (0, EV_WIDTH - w.shape[1]))).astype(BF16)
    scale = NSA_QK_DIM ** -0.5
    pad_g = lambda g: jnp.pad(g, (0, HEAD_PAD - NSA_QK_DIM))
    gain = jnp.ones((EV_WIDTH,), F32)
    gain = gain.at[EV_Q:EV_Q + _NQ].set(jnp.tile(pad_g(q_gain * scale), NSA_HEADS))
    gain = gain.at[EV_KS:EV_KS + _NK].set(jnp.tile(pad_g(k_gain[1]), G))
    gain = gain.at[EV_KW:EV_KW + _NK].set(jnp.tile(pad_g(k_gain[2]), G))
    col = np.arange(EV_WIDTH)
    flag = ((col < EV_Q + _NQ) | ((col >= EV_KS) & (col < EV_KW + _NK))).astype(np.float32)
    return w, gain, jnp.asarray(flag)


def _even_mixer(xm, x2d, g_a, B, T, w_in, w_out, q_gain, k_gain, pe_k, pe_v, w_ck1, w_ck2, w_cv1, w_cv2,
                g_cq, g_ckv, w_uq, w_ukv, mq_gain, mk_gain):
    N, D = x2d.shape
    G, R = NSA_KV_HEADS, NSA_HEADS // NSA_KV_HEADS
    w_p, gain, flag = _even_in_weights(w_in, q_gain, k_gain)
    h = matmul(xm, w_p, head_norm=(HEAD_PAD, NSA_QK_DIM, gain, flag), name="in_proj_even")

    kc = compress(h, EV_KC, B, T, pe_k, w_ck1, w_ck2, k_gain[0], NSA_QK_DIM, HEAD_PAD)
    vc = compress(h, EV_VC, B, T, pe_v, w_cv1, w_cv2, None, NSA_V_DIM, NSA_V_DIM)

    o_cmp, bits = cmp_attention(h, kc, vc, B, T)
    slopes = _alibi(NSA_HEADS)
    HV = NSA_HEADS * NSA_V_DIM
    o_slc = flash_attention(h, EV_Q, h, EV_KS, h, EV_VS, B=B, T=T, G=G, R=R, out_cols=HV,
                            slopes=slopes, bits=bits, name="nsa_slc_attn")
    o_win = flash_attention(h, EV_Q, h, EV_KW, h, EV_VW, B=B, T=T, G=G, R=R, out_cols=HV,
                            window=NSA_WINDOW, slopes=slopes, name="nsa_win_attn")
    o_a = nsa_combine(h, o_cmp, o_slc, o_win)

    H = MLA_HEADS
    dqk = MLA_NOPE_DIM + MLA_ROPE_DIM
    half = MLA_ROPE_DIM // 2
    wq = w_uq.reshape(MLA_Q_RANK, H, dqk)
    wq_rope = wq[:, :, MLA_NOPE_DIM:]
    wq_p = jnp.concatenate([wq, wq_rope[:, :, half:], wq_rope[:, :, :half]], axis=2)
    wq_p = wq_p.reshape(MLA_Q_RANK, H * HEAD_PAD).astype(BF16)
    q_raw = matmul(h, wq_p, a_col0=EV_CQ, a_pro="rms", a_gain=g_cq, name="mla_q_up")
    kv_raw = matmul(h, w_ukv.astype(BF16), a_col0=EV_CKV, a_pro="rms", a_gain=g_ckv, name="mla_kv_up")
    qa_lo, qa_hi, qb_hi = _rope_tables(mq_gain, T, dqk ** -0.5)
    ka_lo, ka_hi, kb_hi = _rope_tables(mk_gain, T, 1.0)
    q_m = mla_prep(q_raw, None, 0, qa_lo, qa_hi, qb_hi, T)
    k_m = mla_prep(kv_raw, h, EV_KR, ka_lo, ka_hi, kb_hi, T)
    o_b = flash_attention(q_m, 0, k_m, 0, kv_raw, MLA_NOPE_DIM, B=B, T=T, G=H, R=1,
                          out_cols=H * MLA_V_DIM, v_stride=MLA_NOPE_DIM + MLA_V_DIM, name="mla_attn")
    o_ab = jnp.concatenate([o_a, o_b], axis=1)
    return matmul(o_ab, w_out.astype(BF16), resid=(x2d, g_a, T), out_dtype=F32, name="out_proj_even")


def _odd_mixer(xm, x2d, g_a, B, T, w_in, b_in, w_out, b_out, q_gain, k_gain, sinks):
    D = w_in.shape[0]
    G, hd = SWA_KV_HEADS, SWA_HEAD_DIM
    nq = SWA_HEADS * hd

    def dup(t):
        lead = t.shape[:-1]
        t = t.reshape(lead + (G, 1, hd))
        return jnp.broadcast_to(t, lead + (G, 2, hd)).reshape(lead + (G * 2 * hd,))

    kw = G * hd
    w_p = jnp.concatenate([w_in[:, :nq], dup(w_in[:, nq:nq + kw]), dup(w_in[:, nq + kw:])], axis=1)
    b_p = jnp.concatenate([b_in[:nq], dup(b_in[nq:nq + kw]), dup(b_in[nq + kw:])])
    gain = jnp.concatenate([jnp.tile(q_gain * hd ** -0.5, SWA_HEADS), jnp.tile(k_gain, 2 * G),
                            jnp.ones((2 * kw,), F32)])
    flag = jnp.concatenate([jnp.ones((nq + 2 * kw,), F32), jnp.zeros((2 * kw,), F32)])
    h = matmul(xm, w_p.astype(BF16), bias=b_p, head_norm=(hd, hd, gain, flag), name="in_proj_odd")
    o_c = swa_attention(h, sinks, B, T, nq, nq + 2 * kw)
    return matmul(o_c, w_out.astype(BF16), bias=b_out, resid=(x2d, g_a, T), out_dtype=F32,
                  name="out_proj_odd")


def kernel(x, c, w_mod, mod_table, norm_attn, norm_ffn, w_in_even, w_out_even, nsa_q_gain, nsa_k_gain,
           nsa_pe_k, nsa_pe_v, nsa_w_ck1, nsa_w_ck2, nsa_w_cv1, nsa_w_cv2, mla_g_cq, mla_g_ckv, mla_w_uq,
           mla_w_ukv, mla_q_gain, mla_k_gain, w_in_odd, b_in_odd, w_out_odd, b_out_odd, swa_q_gain,
           swa_k_gain, swa_sinks, router_w, router_b, moe_w_gate, moe_b_gate, moe_w_up, moe_b_up,
           moe_w_down, moe_b_down):
    B, T, D = x.shape
    N = B * T
    depth = mod_table.shape[0]
    c_pad = jnp.pad(c, ((0, 8 - B % 8 if B % 8 else 0), (0, 0)))
    cond = matmul(c_pad, w_mod, a_pro="silu", out_dtype=F32, tn=1024, tk=1024, name="adaln_proj")[:B]
    x2d = x.reshape(N, D)
    for layer in range(depth):
        mod = (cond + mod_table[layer]).reshape(B, 6, 1, D)
        sh_a, sc_a, g_a, sh_f, sc_f, g_f = (mod[:, j] for j in range(6))
        xm = norm_mod(x2d, norm_attn[layer], sc_a, sh_a, T)
        i = layer // 2
        if layer % 2 == 0:
            x2d = _even_mixer(xm, x2d, g_a, B, T, w_in_even[i], w_out_even[i], nsa_q_gain[i], nsa_k_gain[i],
                              nsa_pe_k[i], nsa_pe_v[i], nsa_w_ck1[i], nsa_w_ck2[i], nsa_w_cv1[i],
                              nsa_w_cv2[i], mla_g_cq[i], mla_g_ckv[i], mla_w_uq[i], mla_w_ukv[i],
                              mla_q_gain[i], mla_k_gain[i])
        else:
            x2d = _odd_mixer(xm, x2d, g_a, B, T, w_in_odd[i], b_in_odd[i], w_out_odd[i], b_out_odd[i],
                             swa_q_gain[i], swa_k_gain[i], swa_sinks[i])
        x2d = moe_layer(x2d, norm_ffn[layer], sc_f, sh_f, g_f, T, router_w[layer], router_b[layer],
                        moe_w_gate[layer], moe_b_gate[layer], moe_w_up[layer], moe_b_up[layer],
                        moe_w_down[layer], moe_b_down[layer])
    return x2d.reshape(B, T, D)
```

```python
import functools
import math

import numpy as np
import jax
import jax.numpy as jnp
from jax import lax
from jax.experimental import pallas as pl
from jax.experimental.pallas import tpu as pltpu

BF16 = jnp.bfloat16
F32 = jnp.float32

NSA_HEADS = 16
NSA_KV_HEADS = 4
NSA_QK_DIM = 192
NSA_V_DIM = 128
CMP_BLOCK = 32
CMP_STRIDE = 16
SLC_BLOCK = 64
SLC_TOPN = 8
NSA_WINDOW = 512
MLA_HEADS = 16
MLA_Q_RANK = 1024
MLA_KV_RANK = 512
MLA_NOPE_DIM = 128
MLA_ROPE_DIM = 64
MLA_V_DIM = 128
ROPE_THETA = 10000.0
SWA_HEADS = 64
SWA_KV_HEADS = 8
SWA_HEAD_DIM = 64
SWA_WINDOW = 128
N_EXPERTS = 32
TOP_K = 4
SWIGLU_ALPHA = 1.702
SWIGLU_LIMIT = 7.0
MOE_ROW_BLOCK = 128
EPS = 1e-6

LANES = 128
HEAD_PAD = 256
NEG = -1e30
M_FLOOR = -1e29

EVEN_IN_SPLITS = (NSA_HEADS * NSA_QK_DIM,
                  NSA_KV_HEADS * NSA_QK_DIM, NSA_KV_HEADS * NSA_V_DIM,
                  NSA_KV_HEADS * NSA_QK_DIM, NSA_KV_HEADS * NSA_V_DIM,
                  NSA_KV_HEADS * NSA_QK_DIM, NSA_KV_HEADS * NSA_V_DIM,
                  3 * NSA_HEADS, MLA_Q_RANK, MLA_KV_RANK + MLA_ROPE_DIM)

_NQ = NSA_HEADS * HEAD_PAD
_NK = NSA_KV_HEADS * HEAD_PAD
_NV = NSA_KV_HEADS * NSA_V_DIM
EV_Q = 0
EV_KC = EV_Q + _NQ
EV_KS = EV_KC + _NK
EV_KW = EV_KS + _NK
EV_CQ = EV_KW + _NK
EV_CKV = EV_CQ + MLA_Q_RANK
EV_KR = EV_CKV + MLA_KV_RANK
EV_VC = EV_KR + 2 * MLA_ROPE_DIM
EV_VS = EV_VC + _NV
EV_VW = EV_VS + _NV
EV_GATE = EV_VW + _NV
EV_END = EV_GATE + LANES
EV_WIDTH = -(-EV_END // 512) * 512


def _tile(dim, want):
    t = min(dim, want)
    while dim % t:
        t //= 2
    return t


def _alibi_np(n):
    return np.exp2(-8.0 * np.arange(1, n + 1, dtype=np.float32) / n).astype(np.float32)


def _alibi(n):
    return jnp.asarray(_alibi_np(n))


def _head_norm(res, hd, real_d):
    tm, tn = res.shape
    x2 = res * res
    pieces = []
    if hd == HEAD_PAD:
        for s in range(tn // hd):
            ss = jnp.sum(x2[:, s * hd:s * hd + LANES] + x2[:, s * hd + LANES:(s + 1) * hd],
                         axis=1, keepdims=True)
            r = lax.rsqrt(ss * (1.0 / real_d) + EPS)
            pieces.append(jnp.broadcast_to(r, (tm, hd)))
    else:
        lo = lax.broadcasted_iota(jnp.int32, (tm, LANES), 1) < hd
        for s in range(tn // LANES):
            c = x2[:, s * LANES:(s + 1) * LANES]
            ss_lo = jnp.sum(jnp.where(lo, c, 0.0), axis=1, keepdims=True)
            ss_hi = jnp.sum(jnp.where(lo, 0.0, c), axis=1, keepdims=True)
            r_lo = lax.rsqrt(ss_lo * (1.0 / real_d) + EPS)
            r_hi = lax.rsqrt(ss_hi * (1.0 / real_d) + EPS)
            pieces.append(jnp.where(lo, r_lo, r_hi))
    return jnp.concatenate(pieces, axis=1) if len(pieces) > 1 else pieces[0]


def _mm_kernel(*refs, nk, a_pro, has_a2, has_bias, hd, real_d, has_resid):
    it = iter(refs)
    a_ref = next(it)
    a2_ref = next(it) if has_a2 else None
    w_ref = next(it)
    again_ref = next(it) if a_pro == "rms" else None
    b_ref = next(it) if has_bias else None
    gain_ref = flag_ref = None
    if hd:
        gain_ref = next(it)
        flag_ref = next(it)
    x_ref = gate_ref = None
    if has_resid:
        x_ref = next(it)
        gate_ref = next(it)
    o_ref = next(it)
    acc_ref = next(it) if nk > 1 else None

    a = a_ref[...]
    if a_pro == "silu":
        af = a.astype(F32)
        a = af * jax.nn.sigmoid(af)
    elif a_pro == "rms":
        af = a.astype(F32)
        r = lax.rsqrt(jnp.mean(af * af, axis=1, keepdims=True) + EPS)
        a = af * r * again_ref[...]
    if has_a2:
        k1 = a.shape[1]
        part = (jnp.dot(a.astype(BF16), w_ref[:k1, :].astype(BF16), preferred_element_type=F32)
                + jnp.dot(a2_ref[...].astype(BF16), w_ref[k1:, :].astype(BF16), preferred_element_type=F32))
    else:
        part = jnp.dot(a.astype(BF16), w_ref[...].astype(BF16), preferred_element_type=F32)

    def finish(res):
        if has_bias:
            res = res + b_ref[...]
        if hd:
            r = _head_norm(res, hd, real_d)
            res = res * jnp.where(flag_ref[...] > 0.0, r, 1.0) * gain_ref[...]
        if has_resid:
            res = x_ref[...] + gate_ref[...] * res
        o_ref[...] = res.astype(o_ref.dtype)

    if nk == 1:
        finish(part)
    else:
        k = pl.program_id(2)

        @pl.when(k == 0)
        def _():
            acc_ref[...] = part

        @pl.when(k > 0)
        def _():
            acc_ref[...] += part

        @pl.when(k == nk - 1)
        def _():
            finish(acc_ref[...])


def matmul(a, w, *, a2=None, a_col0=0, a_pro=None, a_gain=None, bias=None,
           head_norm=None, resid=None, out_dtype=BF16, tm=1024, tn=512, tk=4096, name="mm"):
    M = a.shape[0]
    K, N = w.shape
    tm, tn, tk = _tile(M, tm), _tile(N, tn), _tile(K, tk)
    if a_pro == "rms" or a2 is not None:
        tk = K
    assert a_col0 % tk == 0 and M % tm == 0 and N % tn == 0 and K % tk == 0
    nk = K // tk
    koff = a_col0 // tk
    hd = head_norm[0] if head_norm else 0
    real_d = head_norm[1] if head_norm else 0
    if hd:
        assert tn % max(hd, LANES) == 0

    if a2 is None:
        ins = [a, w]
        specs = [pl.BlockSpec((tm, tk), lambda i, j, k: (i, koff + k))]
    else:
        assert a_col0 == 0 and a_pro is None and a.shape[1] + a2.shape[1] == K
        ins = [a, a2, w]
        specs = [pl.BlockSpec((tm, a.shape[1]), lambda i, j, k: (i, 0)),
                 pl.BlockSpec((tm, a2.shape[1]), lambda i, j, k: (i, 0))]
    specs.append(pl.BlockSpec((tk, tn), lambda i, j, k: (k, j)))
    if a_pro == "rms":
        ins.append(a_gain.reshape(1, K).astype(F32))
        specs.append(pl.BlockSpec((1, tk), lambda i, j, k: (0, k)))
    if bias is not None:
        ins.append(bias.reshape(1, N).astype(F32))
        specs.append(pl.BlockSpec((1, tn), lambda i, j, k: (0, j)))
    if hd:
        ins += [head_norm[2].reshape(1, N).astype(F32), head_norm[3].reshape(1, N).astype(F32)]
        specs += [pl.BlockSpec((1, tn), lambda i, j, k: (0, j))] * 2
    if resid is not None:
        x, gate, rows_per_batch = resid
        assert rows_per_batch % tm == 0
        bpb = rows_per_batch // tm
        ins += [x, gate]
        specs += [pl.BlockSpec((tm, tn), lambda i, j, k: (i, j)),
                  pl.BlockSpec((None, 1, tn), lambda i, j, k: (i // bpb, 0, j))]
    kern = functools.partial(_mm_kernel, nk=nk, a_pro=a_pro, has_a2=a2 is not None, has_bias=bias is not None,
                             hd=hd, real_d=real_d, has_resid=resid is not None)
    return pl.pallas_call(
        kern,
        out_shape=jax.ShapeDtypeStruct((M, N), out_dtype),
        grid=(M // tm, N // tn, nk),
        in_specs=specs,
        out_specs=pl.BlockSpec((tm, tn), lambda i, j, k: (i, j)),
        scratch_shapes=[pltpu.VMEM((tm, tn), F32)] if nk > 1 else [],
        compiler_params=pltpu.CompilerParams(
            dimension_semantics=("parallel", "parallel", "arbitrary")),
        name=name,
    )(*ins)


def _split_bf16(v):
    hi = v.astype(BF16)
    lo = (v - hi.astype(F32)).astype(BF16)
    return hi, lo


def _pack_bf16_pairs(v):
    half = v.shape[1] // 2
    vb = v.astype(BF16).astype(F32)
    lo = lax.shift_right_logical(lax.bitcast_convert_type(vb[:, :half], jnp.int32), 16)
    hi = lax.bitcast_convert_type(vb[:, half:], jnp.int32) & jnp.int32(-65536)
    return hi | lo


def _unpack_bf16_pairs(u):
    lo = lax.bitcast_convert_type(lax.shift_left(u, 16), F32)
    hi = lax.bitcast_convert_type(u & jnp.int32(-65536), F32)
    return jnp.concatenate([lo, hi], axis=1).astype(BF16)


def _norm_mod_kernel(x_ref, g_ref, sc_ref, sh_ref, *rest, route):
    x = x_ref[...]
    r = lax.rsqrt(jnp.mean(x * x, axis=1, keepdims=True) + EPS)
    xm = (x * r * g_ref[...]) * (1.0 + sc_ref[...]) + sh_ref[...]
    if not route:
        (o_ref,) = rest
        o_ref[...] = xm.astype(o_ref.dtype)
        return
    rw_ref, rb_ref, o_ref, e_ref, p_ref, rank_ref, cnt_ref, run_ref = rest
    o_ref[...] = _pack_bf16_pairs(xm)
    a_hi, a_lo = _split_bf16(xm)
    w = rw_ref[...]
    w_hi, w_lo = _split_bf16(w)
    logits = (jnp.dot(a_hi, w_hi, preferred_element_type=F32)
              + jnp.dot(a_hi, w_lo, preferred_element_type=F32)
              + jnp.dot(a_lo, w_hi, preferred_element_type=F32)) + rb_ref[...]
    tt = logits.shape[0]
    lane = lax.broadcasted_iota(jnp.int32, (tt, LANES), 1)
    work = jnp.where(lane < N_EXPERTS, logits, -jnp.inf)
    e_out = jnp.zeros((tt, LANES), jnp.int32)
    v_out = jnp.full((tt, LANES), -jnp.inf, F32)
    hits = []
    for kk in range(TOP_K):
        m = jnp.max(work, axis=1, keepdims=True)
        idx = jnp.min(jnp.where(work == m, lane, LANES), axis=1, keepdims=True)
        e_out = jnp.where(lane == kk, idx, e_out)
        v_out = jnp.where(lane == kk, m, v_out)
        hits.append(lane == idx)
        work = jnp.where(hits[-1], -jnp.inf, work)
    vmax = jnp.max(v_out, axis=1, keepdims=True)
    pe = jnp.exp(v_out - vmax)
    p_ref[...] = pe / jnp.sum(pe, axis=1, keepdims=True)
    e_ref[...] = e_out

    @pl.when(pl.program_id(0) == 0)
    def _():
        run_ref[...] = jnp.zeros_like(run_ref)

    onehot = jnp.zeros((tt, LANES), F32)
    for hit in hits:
        onehot = jnp.where(hit, 1.0, onehot)
    earlier = (lax.broadcasted_iota(jnp.int32, (tt, tt), 1)
               < lax.broadcasted_iota(jnp.int32, (tt, tt), 0))
    prefix = jnp.dot(jnp.where(earlier, 1.0, 0.0).astype(BF16), onehot.astype(BF16),
                     preferred_element_type=F32)
    base = run_ref[...] + prefix
    rank = jnp.zeros((tt, LANES), F32)
    for kk, hit in enumerate(hits):
        rk = jnp.sum(jnp.where(hit, base, 0.0), axis=1, keepdims=True)
        rank = jnp.where(lane == kk, rk, rank)
    rank_ref[...] = rank.astype(jnp.int32)
    total = run_ref[...] + jnp.sum(onehot, axis=0, keepdims=True)
    run_ref[...] = total
    cnt_ref[...] = total.astype(jnp.int32)


def norm_mod(x2d, gain, sc, sh, rows_per_batch, *, out_dtype=BF16, router=None, tt=256):
    N, D = x2d.shape
    tt = _tile(rows_per_batch, tt)
    bpb = rows_per_batch // tt
    ins = [x2d, gain.reshape(1, D), sc, sh]
    specs = [pl.BlockSpec((tt, D), lambda i: (i, 0)),
             pl.BlockSpec((1, D), lambda i: (0, 0)),
             pl.BlockSpec((None, 1, D), lambda i: (i // bpb, 0, 0)),
             pl.BlockSpec((None, 1, D), lambda i: (i // bpb, 0, 0))]
    if router is None:
        out_shape = [jax.ShapeDtypeStruct((N, D), out_dtype)]
        out_specs = [pl.BlockSpec((tt, D), lambda i: (i, 0))]
        scratch = []
    else:
        rw, rb = router
        E = rw.shape[1]
        rw_p = jnp.pad(rw, ((0, 0), (0, LANES - E)))
        rb_p = jnp.pad(rb, (0, LANES - E)).reshape(1, LANES)
        ins += [rw_p, rb_p]
        specs += [pl.BlockSpec((D, LANES), lambda i: (0, 0)),
                  pl.BlockSpec((1, LANES), lambda i: (0, 0))]
        out_shape = [jax.ShapeDtypeStruct((N, D // 2), jnp.int32),
                     jax.ShapeDtypeStruct((N, LANES), jnp.int32),
                     jax.ShapeDtypeStruct((N, LANES), F32),
                     jax.ShapeDtypeStruct((N, LANES), jnp.int32),
                     jax.ShapeDtypeStruct((1, LANES), jnp.int32)]
        row = pl.BlockSpec((tt, LANES), lambda i: (i, 0))
        out_specs = [pl.BlockSpec((tt, D // 2), lambda i: (i, 0)), row, row, row,
                     pl.BlockSpec((1, LANES), lambda i: (0, 0))]
        scratch = [pltpu.VMEM((1, LANES), F32)]
    res = pl.pallas_call(
        functools.partial(_norm_mod_kernel, route=router is not None),
        out_shape=out_shape, grid=(N // tt,), in_specs=specs, out_specs=out_specs,
        scratch_shapes=scratch,
        compiler_params=pltpu.CompilerParams(
            dimension_semantics=("parallel",) if router is None else ("arbitrary",)),
        name="norm_mod_route" if router is not None else "norm_mod",
    )(*ins)
    return res if router is not None else res[0]


def _compress_kernel(h_ref, pe_ref, w1_ref, w2_ref, g_ref, o_ref, xf_ref, *, real_d, norm, nh):
    n_tiles = xf_ref.shape[0]
    for j in range(n_tiles):
        xf_ref[j] = h_ref[:, j * LANES:(j + 1) * LANES].astype(F32)
    half = CMP_BLOCK // 2
    top = bot = None
    for l in range(half):
        parts = [xf_ref[j, pl.ds(l, nh, stride=CMP_STRIDE), :] for j in range(n_tiles)]
        xl = jnp.concatenate(parts, axis=1) if n_tiles > 1 else parts[0]
        t = jnp.dot((xl + pe_ref[l:l + 1, :]).astype(BF16), w1_ref[l], preferred_element_type=F32)
        b = jnp.dot((xl + pe_ref[half + l:half + l + 1, :]).astype(BF16), w1_ref[half + l],
                    preferred_element_type=F32)
        top = t if top is None else top + t
        bot = b if bot is None else bot + b
    hid = top + pltpu.roll(bot, nh - 1, 0)
    hid = hid * jax.nn.sigmoid(hid)
    y = jnp.dot(hid.astype(BF16), w2_ref[...], preferred_element_type=F32)
    if norm:
        r = lax.rsqrt(jnp.sum(y * y, axis=1, keepdims=True) * (1.0 / real_d) + EPS)
        y = y * r * g_ref[...]
    o_ref[...] = y.astype(o_ref.dtype)


def compress(h, col0, B, T, pe, w1, w2, gain, d, dp):
    G = NSA_KV_HEADS
    nh = T // CMP_STRIDE
    cb = col0 // dp
    pe_p = jnp.pad(pe, ((0, 0), (0, dp - d)))
    w1_p = jnp.pad(w1.reshape(CMP_BLOCK, d, d), ((0, 0), (0, dp - d), (0, dp - d))).astype(BF16)
    w2_p = jnp.pad(w2, ((0, dp - d), (0, dp - d))).astype(BF16)
    g_p = (jnp.ones((dp,), F32) if gain is None else jnp.pad(gain, (0, dp - d))).reshape(1, dp)
    full = lambda shape: pl.BlockSpec(shape, lambda b, g: (0,) * len(shape))
    return pl.pallas_call(
        functools.partial(_compress_kernel, real_d=d, norm=gain is not None, nh=nh),
        out_shape=jax.ShapeDtypeStruct((B * G, nh, dp), BF16),
        grid=(B, G),
        in_specs=[pl.BlockSpec((T, dp), lambda b, g: (b, cb + g)),
                  full((CMP_BLOCK, dp)), full((CMP_BLOCK, dp, dp)), full((dp, dp)), full((1, dp))],
        out_specs=pl.BlockSpec((None, nh, dp), lambda b, g: (b * G + g, 0, 0)),
        scratch_shapes=[pltpu.VMEM((dp // LANES, T, LANES), F32)],
        compiler_params=pltpu.CompilerParams(dimension_semantics=("parallel", "parallel")),
        name="nsa_compress",
    )(h, pe_p, w1_p, w2_p, g_p)


def _stack_heads(q_ref, n, w):
    return jnp.concatenate([q_ref[:, r * w:(r + 1) * w] for r in range(n)], axis=0)


def _row_scalars(vals, tq):
    rows = len(vals) * tq
    rid = lax.broadcasted_iota(jnp.int32, (rows, 1), 0) // tq
    col = jnp.full((rows, 1), vals[-1], F32)
    for r in range(len(vals) - 2, -1, -1):
        col = jnp.where(rid == r, vals[r], col)
    return col


def _nt_dot(a, b):
    return lax.dot_general(a, b, (((1,), (1,)), ((), ())), preferred_element_type=F32)


def _cmp_attn_kernel(slope_ref, q_ref, kc_ref, vc_ref, ovl_ref, o_ref, bits_ref, *, tq, R, n_slc):
    g = pl.program_id(1)
    t0 = pl.program_id(2) * tq
    rows = R * tq
    q = _stack_heads(q_ref, R, HEAD_PAD)
    s = _nt_dot(q, kc_ref[...])
    ncol = s.shape[1]
    row_t = t0 + lax.broadcasted_iota(jnp.int32, (rows, ncol), 0) % tq
    n_id = lax.broadcasted_iota(jnp.int32, (rows, ncol), 1)
    dist = row_t - (n_id * CMP_STRIDE + (CMP_BLOCK - 1))
    slope = _row_scalars([slope_ref[g * R + r] for r in range(R)], tq)
    valid = dist >= 0
    s = jnp.where(valid, s - slope * dist.astype(F32), NEG)
    m = jnp.max(s, axis=1, keepdims=True)
    p = jnp.where(valid, jnp.exp(s - m), 0.0)
    p = p / jnp.maximum(jnp.sum(p, axis=1, keepdims=True), 1e-30)
    o = jnp.dot(p.astype(BF16), vc_ref[...], preferred_element_type=F32)
    for r in range(R):
        o_ref[:, r * NSA_V_DIM:(r + 1) * NSA_V_DIM] = o[r * tq:(r + 1) * tq].astype(o_ref.dtype)

    psum = p[0:tq]
    for r in range(1, R):
        psum = psum + p[r * tq:(r + 1) * tq]
    p_hi, p_lo = _split_bf16(psum)
    imp = (jnp.dot(p_hi, ovl_ref[...], preferred_element_type=F32)
           + jnp.dot(p_lo, ovl_ref[...], preferred_element_type=F32))
    lane = lax.broadcasted_iota(jnp.int32, (tq, LANES), 1)
    cur = (t0 + lax.broadcasted_iota(jnp.int32, (tq, LANES), 0)) // SLC_BLOCK
    forced = (lane == 0) | (lane == cur) | (lane == cur - 1)
    work = jnp.where(forced, jnp.inf, jnp.where(lane > cur, -jnp.inf, imp))
    removed = -3.0e38
    work = jnp.where(lane < n_slc, jnp.where(work == -jnp.inf, -2.0e38, work), removed)
    sel = jnp.zeros((tq, LANES), jnp.bool_)
    for _ in range(min(SLC_TOPN, n_slc)):
        mx = jnp.max(work, axis=1, keepdims=True)
        idx = jnp.min(jnp.where(work == mx, lane, LANES), axis=1, keepdims=True)
        hit = lane == idx
        sel = sel | hit
        work = jnp.where(hit, removed, work)
    half = 16
    w_lo = jnp.where(sel & (lane < half), jnp.left_shift(1, jnp.minimum(lane, half - 1)), 0)
    w_hi = jnp.where(sel & (lane >= half), jnp.left_shift(1, jnp.clip(lane - half, 0, half - 1)), 0)
    b_lo = jnp.sum(w_lo.astype(F32), axis=1, keepdims=True).astype(jnp.int32)
    b_hi = jnp.sum(w_hi.astype(F32), axis=1, keepdims=True).astype(jnp.int32)
    bits = b_lo | jnp.left_shift(b_hi, half)
    bits_ref[...] = jnp.broadcast_to(bits, (tq, LANES))


def cmp_attention(h, kc, vc, B, T, tq=256):
    G, R = NSA_KV_HEADS, NSA_HEADS // NSA_KV_HEADS
    tq = _tile(T, tq)
    nq = T // tq
    n_slc = T // SLC_BLOCK
    n_cmp = kc.shape[1]
    assert n_slc <= 32 and n_cmp <= LANES and n_cmp % 8 == 0
    cmp_start = np.arange(n_cmp) * CMP_STRIDE
    slc_start = np.arange(LANES) * SLC_BLOCK
    ovl = ((cmp_start[:, None] < slc_start[None, :] + SLC_BLOCK)
           & (cmp_start[:, None] + CMP_BLOCK > slc_start[None, :])
           & (np.arange(LANES)[None, :] < n_slc) & (np.arange(n_cmp)[:, None] < n_cmp - 1))
    ovl = jnp.asarray(ovl, BF16)
    qw = R * HEAD_PAD
    return pl.pallas_call(
        functools.partial(_cmp_attn_kernel, tq=tq, R=R, n_slc=n_slc),
        out_shape=[jax.ShapeDtypeStruct((B * T, NSA_HEADS * NSA_V_DIM), BF16),
                   jax.ShapeDtypeStruct((B, G, T, LANES), jnp.int32)],
        grid=(B, G, nq),
        in_specs=[pl.BlockSpec(memory_space=pltpu.SMEM),
                  pl.BlockSpec((tq, qw), lambda b, g, i: (b * nq + i, EV_Q // qw + g)),
                  pl.BlockSpec((None, n_cmp, HEAD_PAD), lambda b, g, i: (b * G + g, 0, 0)),
                  pl.BlockSpec((None, n_cmp, NSA_V_DIM), lambda b, g, i: (b * G + g, 0, 0)),
                  pl.BlockSpec((n_cmp, LANES), lambda b, g, i: (0, 0))],
        out_specs=[pl.BlockSpec((tq, R * NSA_V_DIM), lambda b, g, i: (b * nq + i, g)),
                   pl.BlockSpec((None, None, tq, LANES), lambda b, g, i: (b, g, i, 0))],
        compiler_params=pltpu.CompilerParams(dimension_semantics=("parallel", "parallel", "parallel")),
        name="nsa_cmp_attn",
    )(_alibi(NSA_HEADS), h, kc, vc, ovl)


ALIBI_LANE0 = 64
SEL_LANE0 = 70


def _key_table(T):
    s = np.arange(T)
    tab = np.zeros((T, LANES), np.float32)
    tab[:, ALIBI_LANE0:ALIBI_LANE0 + 3] = (s // 256 * 256)[:, None]
    tab[:, ALIBI_LANE0 + 3:ALIBI_LANE0 + 6] = (s % 256)[:, None]
    tab[s, SEL_LANE0 + s // SLC_BLOCK] = 1.0
    return jnp.asarray(tab, BF16)


def _slope_parts(n_heads):
    s = _alibi_np(n_heads)
    h1 = s.astype(BF16).astype(np.float32)
    h2 = (s - h1).astype(BF16).astype(np.float32)
    h3 = (s - h1 - h2).astype(BF16).astype(np.float32)
    return jnp.asarray(np.stack([h1, h2, h3], axis=1).reshape(-1))


def _flash_kernel(*refs, tq, tk, R, HP, v_off, v_step, window, alibi, use_bits):
    it = iter(refs)
    slope_ref = next(it) if alibi else None
    q_ref = next(it)
    k_ref = next(it)
    v_ref = next(it)
    ktab_ref = next(it) if alibi else None
    bits_ref = next(it) if use_bits else None
    o_ref = next(it)
    m_ref = next(it)
    acc_ref = next(it)

    g = pl.program_id(1)
    t0 = pl.program_id(2) * tq
    rows = R * tq
    vd = NSA_V_DIM
    lane = lax.broadcasted_iota(jnp.int32, (tq, LANES), 1)
    sel_add = None
    if use_bits:
        j = jnp.clip(lane - SEL_LANE0, 0, 31)
        in_sel = (lane >= SEL_LANE0) & (lane < SEL_LANE0 + 32)
        picked = (jnp.right_shift(bits_ref[...], j) & 1) != 0
        sel_add = jnp.where(in_sel & jnp.logical_not(picked), NEG, 0.0)

    qs = []
    for hp in range(HP):
        pieces = []
        for r in range(R):
            c0 = (hp * R + r) * HEAD_PAD
            q_lo = q_ref[:, c0:c0 + LANES]
            q_hi = q_ref[:, c0 + LANES:c0 + HEAD_PAD]
            if alibi:
                hd = (g * HP + hp) * R + r
                add = jnp.zeros((tq, LANES), F32) if sel_add is None else sel_add
                for part in range(3):
                    sp = slope_ref[3 * hd + part]
                    add = jnp.where((lane == ALIBI_LANE0 + part) | (lane == ALIBI_LANE0 + 3 + part), sp, add)
                q_hi = (q_hi.astype(F32) + add).astype(BF16)
            pieces.append(jnp.concatenate([q_lo, q_hi], axis=1))
        qs.append(jnp.concatenate(pieces, axis=0) if R > 1 else pieces[0])

    rel0 = (lax.broadcasted_iota(jnp.int32, (rows, tk), 0) % tq
            - lax.broadcasted_iota(jnp.int32, (rows, tk), 1))
    ones = jnp.ones((tk, LANES), BF16)
    m_ref[...] = jnp.full(m_ref.shape, M_FLOOR, F32)
    acc_ref[...] = jnp.zeros(acc_ref.shape, F32)

    def step(c, masked):
        s0 = pl.multiple_of(c * tk, tk)
        off = t0 - s0
        if masked:
            valid = rel0 >= -off
            if window is not None:
                valid = valid & (rel0 < window - off)
        for hp in range(HP):
            kc = k_ref[pl.ds(s0, tk), hp * HEAD_PAD:(hp + 1) * HEAD_PAD]
            if alibi:
                kc = jnp.concatenate([kc[:, :LANES], kc[:, LANES:] + ktab_ref[pl.ds(s0, tk), :]], axis=1)
            vc = v_ref[pl.ds(s0, tk), v_off + hp * v_step:v_off + hp * v_step + vd]
            s = _nt_dot(qs[hp], kc)
            if masked:
                s = jnp.where(valid, s, NEG)
            tiles = [s[:, j * LANES:(j + 1) * LANES] for j in range(tk // LANES)]
            mx = tiles[0]
            for t in tiles[1:]:
                mx = jnp.maximum(mx, t)
            m_old = m_ref[hp]
            m_new = jnp.maximum(m_old, jnp.max(mx, axis=1, keepdims=True))
            alpha = jnp.exp(m_old - m_new)
            p = jnp.concatenate([jnp.exp(t - m_new).astype(BF16) for t in tiles], axis=1)
            pv = jnp.dot(p, jnp.concatenate([vc, ones], axis=1), preferred_element_type=F32)
            acc_ref[hp] = jnp.concatenate([alpha, alpha], axis=1) * acc_ref[hp] + pv
            m_ref[hp] = m_new

    def body(c, carry):
        s0 = c * tk
        full = s0 + tk - 1 <= t0
        if window is not None:
            full = full & (s0 >= t0 + tq - window)
        lax.cond(full, lambda: step(c, False), lambda: step(c, True))
        return carry

    c_hi = (t0 + tq - 1) // tk
    c_lo = 0 if window is None else jnp.maximum((t0 - window + 1) // tk, 0)
    lax.fori_loop(c_lo, c_hi + 1, body, 0)
    for hp in range(HP):
        acc = acc_ref[hp]
        o = acc[:, :vd] / jnp.maximum(acc[:, vd:], 1e-30)
        for r in range(R):
            c0 = (hp * R + r) * vd
            o_ref[:, c0:c0 + vd] = o[r * tq:(r + 1) * tq].astype(o_ref.dtype)


def flash_attention(q_arr, q_col0, k_arr, k_col0, v_arr, v_col0, *, B, T, G, R, HP=1, v_width=NSA_V_DIM,
                    v_off=0, v_step=0, out_cols, out_col0=0, window=None, n_alibi_heads=0, bits=None,
                    tq=256, tk=256, name="flash"):
    vd = NSA_V_DIM
    tq = _tile(T, tq)
    tk = _tile(T, tk)
    nq = T // tq
    qw, kw, ow = HP * R * HEAD_PAD, HP * HEAD_PAD, HP * R * vd
    assert q_col0 % qw == 0 and k_col0 % kw == 0 and v_col0 % v_width == 0 and out_col0 % ow == 0
    assert bits is None or n_alibi_heads
    qb, kb, vb, ob = q_col0 // qw, k_col0 // kw, v_col0 // v_width, out_col0 // ow
    alibi = n_alibi_heads > 0
    ins, specs = [], []
    if alibi:
        ins.append(_slope_parts(n_alibi_heads))
        specs.append(pl.BlockSpec(memory_space=pltpu.SMEM))
    ins += [q_arr, k_arr, v_arr]
    specs += [pl.BlockSpec((tq, qw), lambda b, g, i: (b * nq + i, qb + g)),
              pl.BlockSpec((T, kw), lambda b, g, i: (b, kb + g)),
              pl.BlockSpec((T, v_width), lambda b, g, i: (b, vb + g))]
    if alibi:
        ins.append(_key_table(T))
        specs.append(pl.BlockSpec((T, LANES), lambda b, g, i: (0, 0)))
    if bits is not None:
        ins.append(bits)
        specs.append(pl.BlockSpec((None, None, tq, LANES), lambda b, g, i: (b, g, i, 0)))
    rows = R * tq
    return pl.pallas_call(
        functools.partial(_flash_kernel, tq=tq, tk=tk, R=R, HP=HP, v_off=v_off, v_step=v_step,
                          window=window, alibi=alibi, use_bits=bits is not None),
        out_shape=jax.ShapeDtypeStruct((B * T, out_cols), BF16),
        grid=(B, G // HP, nq),
        in_specs=specs,
        out_specs=pl.BlockSpec((tq, ow), lambda b, g, i: (b * nq + i, ob + g)),
        scratch_shapes=[pltpu.VMEM((HP, rows, LANES), F32), pltpu.VMEM((HP, rows, 2 * vd), F32)],
        compiler_params=pltpu.CompilerParams(dimension_semantics=("parallel", "parallel", "parallel")),
        name=name,
    )(*ins)


def _nsa_combine_kernel(gate_ref, oc_ref, os_ref, ow_ref, o_ref):
    gates = jax.nn.sigmoid(gate_ref[...].astype(F32))
    for h in range(NSA_HEADS):
        sl = slice(h * NSA_V_DIM, (h + 1) * NSA_V_DIM)
        acc = None
        for br, ref in enumerate((oc_ref, os_ref, ow_ref)):
            gcol = gates[:, 3 * h + br:3 * h + br + 1]
            term = gcol * ref[:, sl].astype(F32)
            acc = term if acc is None else acc + term
        o_ref[:, sl] = acc.astype(o_ref.dtype)


def nsa_combine(h, o_cmp, o_slc, o_win, tt=256):
    N, W = o_cmp.shape
    tt = _tile(N, tt)
    gb = EV_GATE // LANES
    blk = pl.BlockSpec((tt, W), lambda i: (i, 0))
    return pl.pallas_call(
        _nsa_combine_kernel,
        out_shape=jax.ShapeDtypeStruct((N, W), BF16),
        grid=(N // tt,),
        in_specs=[pl.BlockSpec((tt, LANES), lambda i: (i, gb)), blk, blk, blk],
        out_specs=blk,
        compiler_params=pltpu.CompilerParams(dimension_semantics=("parallel",)),
        name="nsa_combine",
    )(h, o_cmp, o_slc, o_win)


def _mla_prep_kernel(*refs, shared_rope):
    if shared_rope:
        x_ref, hi_ref, alo_ref, ahi_ref, bhi_ref, o_ref = refs
    else:
        x_ref, alo_ref, ahi_ref, bhi_ref, o_ref = refs
    tt = x_ref.shape[0]
    is_rope = lax.broadcasted_iota(jnp.int32, (tt, LANES), 1) < MLA_ROPE_DIM
    a_hi, b_hi, a_lo = ahi_ref[...], bhi_ref[...], alo_ref[...]

    def rope_part(x_hi):
        ss = jnp.sum(jnp.where(is_rope, x_hi * x_hi, 0.0), axis=1, keepdims=True)
        return x_hi * a_hi + pltpu.roll(x_hi * b_hi, MLA_ROPE_DIM, 1), ss

    if shared_rope:
        y_hi, ss_hi = rope_part(hi_ref[...].astype(F32))
    for h in range(MLA_HEADS):
        c = h * HEAD_PAD
        x_lo = x_ref[:, c:c + LANES].astype(F32)
        if not shared_rope:
            y_hi, ss_hi = rope_part(x_ref[:, c + LANES:c + HEAD_PAD].astype(F32))
        ss = jnp.sum(x_lo * x_lo, axis=1, keepdims=True) + ss_hi
        r = lax.rsqrt(ss * (1.0 / (MLA_NOPE_DIM + MLA_ROPE_DIM)) + EPS)
        o_ref[:, c:c + LANES] = (x_lo * r * a_lo).astype(o_ref.dtype)
        o_ref[:, c + LANES:c + HEAD_PAD] = (y_hi * r).astype(o_ref.dtype)


def mla_prep(x_arr, rope_arr, rope_col0, a_lo, a_hi, b_hi, T, tt=256):
    N = x_arr.shape[0]
    W = MLA_HEADS * HEAD_PAD
    tt = _tile(T, tt)
    nt = T // tt
    shared = rope_arr is not None
    ins = [x_arr]
    specs = [pl.BlockSpec((tt, W), lambda i: (i, 0))]
    if shared:
        rb = rope_col0 // LANES
        ins.append(rope_arr)
        specs.append(pl.BlockSpec((tt, LANES), lambda i: (i, rb)))
    ins += [a_lo, a_hi, b_hi]
    specs += [pl.BlockSpec((1, LANES), lambda i: (0, 0)),
              pl.BlockSpec((tt, LANES), lambda i: (i % nt, 0)),
              pl.BlockSpec((tt, LANES), lambda i: (i % nt, 0))]
    return pl.pallas_call(
        functools.partial(_mla_prep_kernel, shared_rope=shared),
        out_shape=jax.ShapeDtypeStruct((N, W), BF16),
        grid=(N // tt,),
        in_specs=specs,
        out_specs=pl.BlockSpec((tt, W), lambda i: (i, 0)),
        compiler_params=pltpu.CompilerParams(dimension_semantics=("parallel",)),
        name="mla_prep",
    )(*ins)


def _rope_tables(gain, T, scale):
    half = MLA_ROPE_DIM // 2
    inv_freq = ROPE_THETA ** (-jnp.arange(half, dtype=F32) / half)
    ang = jnp.arange(T, dtype=F32)[:, None] * inv_freq[None, :]
    cos2 = jnp.concatenate([jnp.cos(ang), jnp.cos(ang)], axis=1)
    sin_s = jnp.concatenate([-jnp.sin(ang), jnp.sin(ang)], axis=1)
    g_nope, g_rope = gain[:MLA_NOPE_DIM], gain[MLA_NOPE_DIM:]
    g_perm = jnp.concatenate([g_rope[half:], g_rope[:half]])
    zeros = jnp.zeros((T, MLA_ROPE_DIM), F32)
    a_lo = (g_nope * scale).reshape(1, LANES)
    a_hi = jnp.concatenate([g_rope[None, :] * cos2 * scale, zeros], axis=1)
    b_hi = jnp.concatenate([zeros, g_perm[None, :] * sin_s * scale], axis=1)
    return a_lo, a_hi, b_hi


def _swa_kernel(slope_ref, parts_ref, sink_ref, q_ref, k_ref, v_ref, ktab_ref, o_ref, mask_ref,
                *, tq, tk, R):
    g = pl.program_id(1)
    t0 = pl.program_id(2) * tq
    npair = R // 2
    rows = npair * tq
    qp = _stack_heads(q_ref, npair, LANES)
    lane = lax.broadcasted_iota(jnp.int32, (rows, LANES), 1)
    first = lane < SWA_HEAD_DIM
    start = pl.multiple_of(jnp.maximum(t0 - SWA_WINDOW, 0), LANES)
    kc = jnp.concatenate([k_ref[pl.ds(start, tk), :], ktab_ref[pl.ds(start, tk), :]], axis=1)
    v_aug = jnp.concatenate([v_ref[pl.ds(start, tk), :], jnp.ones((tk, LANES), BF16)], axis=1)
    @pl.when(pl.program_id(2) <= 1)
    def _():
        row_in = lax.broadcasted_iota(jnp.int32, (rows, tk), 0) % tq
        rel = t0 - start + row_in - lax.broadcasted_iota(jnp.int32, (rows, tk), 1)
        mask_ref[...] = jnp.where((rel >= 0) & (rel < SWA_WINDOW), 0.0, NEG)

    t_row = (t0 + lax.broadcasted_iota(jnp.int32, (rows, LANES), 0) % tq).astype(F32)
    pair_id = lax.broadcasted_iota(jnp.int32, (rows, LANES), 0) // tq
    outs = []
    for u in range(2):
        heads = [g * R + 2 * p + u for p in range(npair)]
        add = jnp.zeros((rows, LANES), F32)
        for part in range(3):
            at_lane = (lane == ALIBI_LANE0 + part) | (lane == ALIBI_LANE0 + 3 + part)
            for p, hd in enumerate(heads):
                add = jnp.where(at_lane & (pair_id == p), parts_ref[3 * hd + part], add)
        qu = jnp.where(first if u == 0 else jnp.logical_not(first), qp, jnp.zeros_like(qp))
        qa = jnp.concatenate([qu, add.astype(BF16)], axis=1)
        slope = _row_scalars([slope_ref[hd] for hd in heads], tq)
        sink = _row_scalars([sink_ref[hd] for hd in heads], tq)
        sink_t = sink + slope * t_row
        s = _nt_dot(qa, kc) + mask_ref[...]
        tiles = [s[:, j * LANES:(j + 1) * LANES] for j in range(tk // LANES)]
        mx = tiles[0]
        for t in tiles[1:]:
            mx = jnp.maximum(mx, t)
        m = jnp.maximum(jnp.max(mx, axis=1, keepdims=True), sink_t)
        p = jnp.concatenate([jnp.exp(t - m).astype(BF16) for t in tiles], axis=1)
        pv = jnp.dot(p, v_aug, preferred_element_type=F32)
        denom = pv[:, LANES:] + jnp.exp(sink_t - m)
        outs.append(pv[:, :LANES] / jnp.maximum(denom, 1e-30))
    o = jnp.where(first, outs[0], outs[1])
    for p_ in range(npair):
        o_ref[:, p_ * LANES:(p_ + 1) * LANES] = o[p_ * tq:(p_ + 1) * tq].astype(o_ref.dtype)


def swa_attention(h, sinks, B, T, k_col0, v_col0, tq=256):
    G, R = SWA_KV_HEADS, SWA_HEADS // SWA_KV_HEADS
    tq = _tile(T, tq)
    tk = min(T, tq + SWA_WINDOW)
    assert tk % LANES == 0 and tq % LANES == 0
    nq = T // tq
    qw = R * SWA_HEAD_DIM
    kb, vb = k_col0 // LANES, v_col0 // LANES
    smem = pl.BlockSpec(memory_space=pltpu.SMEM)
    return pl.pallas_call(
        functools.partial(_swa_kernel, tq=tq, tk=tk, R=R),
        out_shape=jax.ShapeDtypeStruct((B * T, SWA_HEADS * SWA_HEAD_DIM), BF16),
        grid=(B, G, nq),
        in_specs=[smem, smem, smem,
                  pl.BlockSpec((tq, qw), lambda b, g, i: (b * nq + i, g)),
                  pl.BlockSpec((T, LANES), lambda b, g, i: (b, kb + g)),
                  pl.BlockSpec((T, LANES), lambda b, g, i: (b, vb + g)),
                  pl.BlockSpec((T, LANES), lambda b, g, i: (0, 0))],
        out_specs=pl.BlockSpec((tq, qw), lambda b, g, i: (b * nq + i, g)),
        scratch_shapes=[pltpu.VMEM((R // 2 * tq, tk), F32)],
        compiler_params=pltpu.CompilerParams(dimension_semantics=("parallel", "parallel", "arbitrary")),
        name="swa_attn",
    )(_alibi(SWA_HEADS), _slope_parts(SWA_HEADS), sinks.astype(F32), h, h, h, _key_table(T))


def _dispatch_kernel(pos_ref, src_ref, zero_hbm, dst_hbm, sem, *, td):
    del zero_hbm

    def row_copy(t, k):
        return pltpu.make_async_copy(src_ref.at[pl.ds(t, 1)],
                                     dst_hbm.at[pl.ds(pos_ref[0, t * TOP_K + k], 1)], sem)

    def issue(t, c):
        for k in range(TOP_K):
            row_copy(t, k).start()
        return c

    def drain(t, c):
        for k in range(TOP_K):
            row_copy(t, k).wait()
        return c

    lax.fori_loop(0, td, issue, 0)
    lax.fori_loop(0, td, drain, 0)


def moe_dispatch(xm_packed, pos, cap, td=128):
    N, W = xm_packed.shape
    td = _tile(N, td)
    pos3 = pos.reshape(N // td, 1, td * TOP_K)
    return pl.pallas_call(
        functools.partial(_dispatch_kernel, td=td),
        out_shape=jax.ShapeDtypeStruct((cap, W), jnp.int32),
        grid=(N // td,),
        in_specs=[pl.BlockSpec((None, 1, td * TOP_K), lambda i: (i, 0, 0), memory_space=pltpu.SMEM),
                  pl.BlockSpec((td, W), lambda i: (i, 0)), pl.BlockSpec(memory_space=pl.ANY)],
        out_specs=pl.BlockSpec(memory_space=pl.ANY),
        scratch_shapes=[pltpu.SemaphoreType.DMA(())],
        input_output_aliases={2: 0},
        compiler_params=pltpu.CompilerParams(dimension_semantics=("arbitrary",), has_side_effects=True),
        name="moe_dispatch",
    )(pos3, xm_packed, jnp.zeros((cap, W), jnp.int32))


def _expert_kernel(be_ref, first_ref, nu_ref, xs_ref, wg_hbm, bg_ref, wu_hbm, bu_ref, wd_hbm, bd_ref,
                   o_ref, wg_s, wu_s, wd_s, stg_a, stg_d, sem, *, ca, cd, layer):
    blk = pl.program_id(0)
    e = be_ref[blk]
    D, F = wg_s.shape

    chunks = []
    for src, dst in ((wg_hbm, wg_s), (wu_hbm, wu_s)):
        chunks += [(src, dst, c * ca, ca, stg_a, 0) for c in range(D // ca)]
    chunks += [(wd_hbm, wd_s, c * cd, cd, stg_d, 2) for c in range(F // cd)]

    def chunk_copy(i):
        src, _, r0, n, stg, s0 = chunks[i]
        slot = i % 2
        return pltpu.make_async_copy(src.at[layer, e, pl.ds(r0, n), :], stg.at[slot], sem.at[s0 + slot])

    @pl.when((blk < nu_ref[0]) & (first_ref[blk] == 1))
    def _():
        chunk_copy(0).start()
        for i, (_, dst, r0, n, stg, _) in enumerate(chunks):
            if i + 1 < len(chunks):
                chunk_copy(i + 1).start()
            chunk_copy(i).wait()
            dst[pl.ds(r0, n), :] = stg[i % 2].astype(BF16)

    @pl.when(blk < nu_ref[0])
    def _():
        x = _unpack_bf16_pairs(xs_ref[...])
        gg = jnp.dot(x, wg_s[...], preferred_element_type=F32) + bg_ref[...]
        uu = jnp.dot(x, wu_s[...], preferred_element_type=F32) + bu_ref[...]
        gg = jnp.minimum(gg, SWIGLU_LIMIT)
        uu = jnp.clip(uu, -SWIGLU_LIMIT, SWIGLU_LIMIT)
        act = gg * jax.nn.sigmoid(SWIGLU_ALPHA * gg) * (uu + 1.0)
        y = jnp.dot(act.astype(BF16), wd_s[...], preferred_element_type=F32) + bd_ref[...]
        o_ref[...] = _pack_bf16_pairs(y)

    @pl.when(blk >= nu_ref[0])
    def _():
        o_ref[...] = jnp.zeros_like(o_ref)


def moe_experts(xs, blk_e, first, n_used, layer, wg, bg, wu, bu, wd, bd, rb=MOE_ROW_BLOCK):
    cap, W = xs.shape
    depth, E, D, F = wg.shape
    n_blk = cap // rb
    ca, cd = _tile(D, 512), _tile(F, 128)
    by_expert = lambda shape: pl.BlockSpec((None,) + shape,
                                           lambda i, be, fi, nu: (layer * E + be[i], 0, 0))
    grid_spec = pltpu.PrefetchScalarGridSpec(
        num_scalar_prefetch=3,
        grid=(n_blk,),
        in_specs=[pl.BlockSpec((rb, W), lambda i, be, fi, nu: (i, 0)),
                  pl.BlockSpec(memory_space=pl.ANY), by_expert((1, F)),
                  pl.BlockSpec(memory_space=pl.ANY), by_expert((1, F)),
                  pl.BlockSpec(memory_space=pl.ANY), by_expert((1, D))],
        out_specs=pl.BlockSpec((rb, W), lambda i, be, fi, nu: (i, 0)),
        scratch_shapes=[pltpu.VMEM((D, F), BF16), pltpu.VMEM((D, F), BF16), pltpu.VMEM((F, D), BF16),
                        pltpu.VMEM((2, ca, F), F32), pltpu.VMEM((2, cd, D), F32),
                        pltpu.SemaphoreType.DMA((4,))],
    )
    return pl.pallas_call(
        functools.partial(_expert_kernel, ca=ca, cd=cd, layer=layer),
        out_shape=jax.ShapeDtypeStruct((cap, W), jnp.int32),
        grid_spec=grid_spec,
        compiler_params=pltpu.CompilerParams(dimension_semantics=("arbitrary",)),
        name="moe_experts",
    )(blk_e, first, n_used, xs, wg, bg.reshape(depth * E, 1, F), wu, bu.reshape(depth * E, 1, F),
      wd, bd.reshape(depth * E, 1, D))


def _moe_combine_kernel(pos_ref, pos_next_ref, y_hbm, x_ref, w_ref, gate_ref, o_ref, ybuf, sem,
                        *, tt, n_steps):
    i = pl.program_id(0)
    slot = i % 2

    def row_copy(p_ref, t, k, sl):
        return pltpu.make_async_copy(y_hbm.at[pl.ds(p_ref[0, t * TOP_K + k], 1)],
                                     ybuf.at[sl, k, pl.ds(t, 1)], sem.at[sl])

    def issue(p_ref, sl):
        def body(t, c):
            for k in range(TOP_K):
                row_copy(p_ref, t, k, sl).start()
            return c
        lax.fori_loop(0, tt, body, 0)

    @pl.when(i == 0)
    def _():
        issue(pos_ref, 0)

    @pl.when(i + 1 < n_steps)
    def _():
        issue(pos_next_ref, 1 - slot)

    def drain(t, c):
        for k in range(TOP_K):
            row_copy(pos_ref, t, k, slot).wait()
        return c

    lax.fori_loop(0, tt, drain, 0)
    w = w_ref[...]
    half = x_ref.shape[1] // 2
    y_lo = y_hi = None
    for k in range(TOP_K):
        u = ybuf[slot, k]
        wk = w[:, k:k + 1]
        lo = wk * lax.bitcast_convert_type(lax.shift_left(u, 16), F32)
        hi = wk * lax.bitcast_convert_type(u & jnp.int32(-65536), F32)
        y_lo = lo if y_lo is None else y_lo + lo
        y_hi = hi if y_hi is None else y_hi + hi
    o_ref[:, :half] = x_ref[:, :half] + gate_ref[:, :half] * y_lo
    o_ref[:, half:] = x_ref[:, half:] + gate_ref[:, half:] * y_hi


def moe_combine(yb, pos, top_w, x2d, gate, rows_per_batch, tt=64):
    N, D = x2d.shape
    tt = _tile(rows_per_batch, tt)
    bpb = rows_per_batch // tt
    n_steps = N // tt
    pos3 = pos.reshape(n_steps, 1, tt * TOP_K)
    pos_spec = lambda f: pl.BlockSpec((None, 1, tt * TOP_K), f, memory_space=pltpu.SMEM)
    return pl.pallas_call(
        functools.partial(_moe_combine_kernel, tt=tt, n_steps=n_steps),
        out_shape=jax.ShapeDtypeStruct((N, D), F32),
        grid=(n_steps,),
        in_specs=[pos_spec(lambda i: (i, 0, 0)),
                  pos_spec(lambda i: (jnp.minimum(i + 1, n_steps - 1), 0, 0)),
                  pl.BlockSpec(memory_space=pl.ANY),
                  pl.BlockSpec((tt, D), lambda i: (i, 0)),
                  pl.BlockSpec((tt, LANES), lambda i: (i, 0)),
                  pl.BlockSpec((None, 1, D), lambda i: (i // bpb, 0, 0))],
        out_specs=pl.BlockSpec((tt, D), lambda i: (i, 0)),
        scratch_shapes=[pltpu.VMEM((2, TOP_K, tt, D // 2), jnp.int32), pltpu.SemaphoreType.DMA((2,))],
        compiler_params=pltpu.CompilerParams(dimension_semantics=("arbitrary",)),
        name="moe_combine",
    )(pos3, pos3, yb, x2d, top_w, gate)


def moe_layer(x2d, gain, sc, sh, gate, rows_per_batch, router_w, router_b, layer, wg, bg, wu, bu, wd, bd):
    N, D = x2d.shape
    RB = MOE_ROW_BLOCK
    E = N_EXPERTS
    xm_packed, top_e, top_w, rank, counts = norm_mod(x2d, gain, sc, sh, rows_per_batch,
                                                     router=(router_w, router_b))
    nk = N * TOP_K
    counts = counts[0, :E]
    padded = (counts + RB - 1) // RB * RB
    pad_end = jnp.cumsum(padded)
    pad_start = pad_end - padded
    e4 = top_e[:, :TOP_K]
    start_of = jnp.sum(jnp.where(e4[:, :, None] == jnp.arange(E)[None, None, :],
                                 pad_start[None, None, :], 0), axis=2)
    pos = (start_of + rank[:, :TOP_K]).astype(jnp.int32)
    cap = (-(-nk // RB)) * RB + E * RB
    n_blk = cap // RB
    blk_start = jnp.arange(n_blk, dtype=jnp.int32) * RB
    blk_e = jnp.minimum(jnp.sum(pad_end[None, :] <= blk_start[:, None], axis=1), E - 1).astype(jnp.int32)
    first = jnp.concatenate([jnp.ones((1,), jnp.int32), (blk_e[1:] != blk_e[:-1]).astype(jnp.int32)])
    n_used = (pad_end[-1] // RB).astype(jnp.int32).reshape(1)
    xs = moe_dispatch(xm_packed, pos, cap)
    yb = moe_experts(xs, blk_e, first, n_used, layer, wg, bg, wu, bu, wd, bd)
    return moe_combine(yb, pos, top_w, x2d, gate, rows_per_batch)


def _pad_heads(w, n_heads, d, dp):
    lead = w.shape[:-1]
    w = w.reshape(lead + (n_heads, d))
    w = jnp.pad(w, [(0, 0)] * len(lead) + [(0, 0), (0, dp - d)])
    return w.reshape(lead + (n_heads * dp,))


def _even_in_weights(w_in, q_gain, k_gain):
    D = w_in.shape[0]
    cuts = np.cumsum(EVEN_IN_SPLITS)[:-1].tolist()
    q, kc, vc, ks, vs, kw, vw, gates, cq, kva = jnp.split(w_in, cuts, axis=1)
    half = MLA_ROPE_DIM // 2
    kr = kva[:, MLA_KV_RANK:]
    kr_perm = jnp.concatenate([kr[:, half:], kr[:, :half]], axis=1)
    G = NSA_KV_HEADS
    cols = [_pad_heads(q, NSA_HEADS, NSA_QK_DIM, HEAD_PAD),
            _pad_heads(kc, G, NSA_QK_DIM, HEAD_PAD), _pad_heads(ks, G, NSA_QK_DIM, HEAD_PAD),
            _pad_heads(kw, G, NSA_QK_DIM, HEAD_PAD), cq, kva[:, :MLA_KV_RANK], kr, kr_perm,
            vc, vs, vw, gates]
    w = jnp.concatenate(cols, axis=1)
    w = jnp.pad(w, ((0, 0), (0, EV_WIDTH - w.shape[1]))).astype(BF16)
    scale = NSA_QK_DIM ** -0.5
    pad_g = lambda g: jnp.pad(g, (0, HEAD_PAD - NSA_QK_DIM))
    gain = jnp.ones((EV_WIDTH,), F32)
    gain = gain.at[EV_Q:EV_Q + _NQ].set(jnp.tile(pad_g(q_gain * scale), NSA_HEADS))
    gain = gain.at[EV_KS:EV_KS + _NK].set(jnp.tile(pad_g(k_gain[1]), G))
    gain = gain.at[EV_KW:EV_KW + _NK].set(jnp.tile(pad_g(k_gain[2]), G))
    col = np.arange(EV_WIDTH)
    flag = ((col < EV_Q + _NQ) | ((col >= EV_KS) & (col < EV_KW + _NK))).astype(np.float32)
    return w, gain, jnp.asarray(flag)


def _even_mixer(xm, x2d, g_a, B, T, w_in, w_out, q_gain, k_gain, pe_k, pe_v, w_ck1, w_ck2, w_cv1, w_cv2,
                g_cq, g_ckv, w_uq, w_ukv, mq_gain, mk_gain):
    N, D = x2d.shape
    G, R = NSA_KV_HEADS, NSA_HEADS // NSA_KV_HEADS
    w_p, gain, flag = _even_in_weights(w_in, q_gain, k_gain)
    h = matmul(xm, w_p, head_norm=(HEAD_PAD, NSA_QK_DIM, gain, flag), name="in_proj_even")

    kc = compress(h, EV_KC, B, T, pe_k, w_ck1, w_ck2, k_gain[0], NSA_QK_DIM, HEAD_PAD)
    vc = compress(h, EV_VC, B, T, pe_v, w_cv1, w_cv2, None, NSA_V_DIM, NSA_V_DIM)

    o_cmp, bits = cmp_attention(h, kc, vc, B, T)
    HV = NSA_HEADS * NSA_V_DIM
    o_slc = flash_attention(h, EV_Q, h, EV_KS, h, EV_VS, B=B, T=T, G=G, R=R, out_cols=HV,
                            n_alibi_heads=NSA_HEADS, bits=bits, tk=512, name="nsa_slc_attn")
    o_win = flash_attention(h, EV_Q, h, EV_KW, h, EV_VW, B=B, T=T, G=G, R=R, out_cols=HV,
                            window=NSA_WINDOW, n_alibi_heads=NSA_HEADS, name="nsa_win_attn")
    o_a = nsa_combine(h, o_cmp, o_slc, o_win)

    H = MLA_HEADS
    dqk = MLA_NOPE_DIM + MLA_ROPE_DIM
    half = MLA_ROPE_DIM // 2
    wq = w_uq.reshape(MLA_Q_RANK, H, dqk)
    wq_rope = wq[:, :, MLA_NOPE_DIM:]
    wq_p = jnp.concatenate([wq, wq_rope[:, :, half:], wq_rope[:, :, :half]], axis=2)
    wq_p = wq_p.reshape(MLA_Q_RANK, H * HEAD_PAD).astype(BF16)
    q_raw = matmul(h, wq_p, a_col0=EV_CQ, a_pro="rms", a_gain=g_cq, name="mla_q_up")
    kv_raw = matmul(h, w_ukv.astype(BF16), a_col0=EV_CKV, a_pro="rms", a_gain=g_ckv, name="mla_kv_up")
    qa_lo, qa_hi, qb_hi = _rope_tables(mq_gain, T, dqk ** -0.5)
    ka_lo, ka_hi, kb_hi = _rope_tables(mk_gain, T, 1.0)
    q_m = mla_prep(q_raw, None, 0, qa_lo, qa_hi, qb_hi, T)
    k_m = mla_prep(kv_raw, h, EV_KR, ka_lo, ka_hi, kb_hi, T)
    hp = 4
    kv_w = MLA_NOPE_DIM + MLA_V_DIM
    o_b = flash_attention(q_m, 0, k_m, 0, kv_raw, 0, B=B, T=T, G=H, R=1, HP=hp, v_width=hp * kv_w,
                          v_off=MLA_NOPE_DIM, v_step=kv_w, out_cols=H * MLA_V_DIM, tk=512,
                          name="mla_attn")
    return matmul(o_a, w_out.astype(BF16), a2=o_b, resid=(x2d, g_a, T), out_dtype=F32, name="out_proj_even")


def _odd_mixer(xm, x2d, g_a, B, T, w_in, b_in, w_out, b_out, q_gain, k_gain, sinks):
    D = w_in.shape[0]
    G, hd = SWA_KV_HEADS, SWA_HEAD_DIM
    nq = SWA_HEADS * hd

    def dup(t):
        lead = t.shape[:-1]
        t = t.reshape(lead + (G, 1, hd))
        return jnp.broadcast_to(t, lead + (G, 2, hd)).reshape(lead + (G * 2 * hd,))

    kw = G * hd
    w_p = jnp.concatenate([w_in[:, :nq], dup(w_in[:, nq:nq + kw]), dup(w_in[:, nq + kw:])], axis=1)
    b_p = jnp.concatenate([b_in[:nq], dup(b_in[nq:nq + kw]), dup(b_in[nq + kw:])])
    gain = jnp.concatenate([jnp.tile(q_gain * hd ** -0.5, SWA_HEADS), jnp.tile(k_gain, 2 * G),
                            jnp.ones((2 * kw,), F32)])
    flag = jnp.concatenate([jnp.ones((nq + 2 * kw,), F32), jnp.zeros((2 * kw,), F32)])
    h = matmul(xm, w_p.astype(BF16), bias=b_p, head_norm=(hd, hd, gain, flag), name="in_proj_odd")
    o_c = swa_attention(h, sinks, B, T, nq, nq + 2 * kw)
    return matmul(o_c, w_out.astype(BF16), bias=b_out, resid=(x2d, g_a, T), out_dtype=F32,
                  name="out_proj_odd")


def kernel(x, c, w_mod, mod_table, norm_attn, norm_ffn, w_in_even, w_out_even, nsa_q_gain, nsa_k_gain,
           nsa_pe_k, nsa_pe_v, nsa_w_ck1, nsa_w_ck2, nsa_w_cv1, nsa_w_cv2, mla_g_cq, mla_g_ckv, mla_w_uq,
           mla_w_ukv, mla_q_gain, mla_k_gain, w_in_odd, b_in_odd, w_out_odd, b_out_odd, swa_q_gain,
           swa_k_gain, swa_sinks, router_w, router_b, moe_w_gate, moe_b_gate, moe_w_up, moe_b_up,
           moe_w_down, moe_b_down):
    B, T, D = x.shape
    N = B * T
    depth = mod_table.shape[0]
    c_pad = jnp.pad(c, ((0, 8 - B % 8 if B % 8 else 0), (0, 0)))
    cond = matmul(c_pad, w_mod, a_pro="silu", out_dtype=F32, tn=1024, tk=1024, name="adaln_proj")[:B]
    x2d = x.reshape(N, D)
    for layer in range(depth):
        mod = (cond + mod_table[layer]).reshape(B, 6, 1, D)
        sh_a, sc_a, g_a, sh_f, sc_f, g_f = (mod[:, j] for j in range(6))
        xm = norm_mod(x2d, norm_attn[layer], sc_a, sh_a, T)
        i = layer // 2
        if layer % 2 == 0:
            x2d = _even_mixer(xm, x2d, g_a, B, T, w_in_even[i], w_out_even[i], nsa_q_gain[i], nsa_k_gain[i],
                              nsa_pe_k[i], nsa_pe_v[i], nsa_w_ck1[i], nsa_w_ck2[i], nsa_w_cv1[i],
                              nsa_w_cv2[i], mla_g_cq[i], mla_g_ckv[i], mla_w_uq[i], mla_w_ukv[i],
                              mla_q_gain[i], mla_k_gain[i])
        else:
            x2d = _odd_mixer(xm, x2d, g_a, B, T, w_in_odd[i], b_in_odd[i], w_out_odd[i], b_out_odd[i],
                             swa_q_gain[i], swa_k_gain[i], swa_sinks[i])
        x2d = moe_layer(x2d, norm_ffn[layer], sc_f, sh_f, g_f, T, router_w[layer], router_b[layer],
                        layer, moe_w_gate, moe_b_gate, moe_w_up, moe_b_up, moe_w_down, moe_b_down)
    return x2d.reshape(B, T, D)
```

```python
import functools
import math

import numpy as np
import jax
import jax.numpy as jnp
from jax import lax
from jax.experimental import pallas as pl
from jax.experimental.pallas import tpu as pltpu

BF16 = jnp.bfloat16
F32 = jnp.float32

NSA_HEADS = 16
NSA_KV_HEADS = 4
NSA_QK_DIM = 192
NSA_V_DIM = 128
CMP_BLOCK = 32
CMP_STRIDE = 16
SLC_BLOCK = 64
SLC_TOPN = 8
NSA_WINDOW = 512
MLA_HEADS = 16
MLA_Q_RANK = 1024
MLA_KV_RANK = 512
MLA_NOPE_DIM = 128
MLA_ROPE_DIM = 64
MLA_V_DIM = 128
ROPE_THETA = 10000.0
SWA_HEADS = 64
SWA_KV_HEADS = 8
SWA_HEAD_DIM = 64
SWA_WINDOW = 128
N_EXPERTS = 32
TOP_K = 4
SWIGLU_ALPHA = 1.702
SWIGLU_LIMIT = 7.0
MOE_ROW_BLOCK = 128
EPS = 1e-6

LANES = 128
HEAD_PAD = 256
NEG = -1e30
M_FLOOR = -1e29

EVEN_IN_SPLITS = (NSA_HEADS * NSA_QK_DIM,
                  NSA_KV_HEADS * NSA_QK_DIM, NSA_KV_HEADS * NSA_V_DIM,
                  NSA_KV_HEADS * NSA_QK_DIM, NSA_KV_HEADS * NSA_V_DIM,
                  NSA_KV_HEADS * NSA_QK_DIM, NSA_KV_HEADS * NSA_V_DIM,
                  3 * NSA_HEADS, MLA_Q_RANK, MLA_KV_RANK + MLA_ROPE_DIM)

_NQ = NSA_HEADS * HEAD_PAD
_NK = NSA_KV_HEADS * HEAD_PAD
_NV = NSA_KV_HEADS * NSA_V_DIM
EV_Q = 0
EV_KC = EV_Q + _NQ
EV_KS = EV_KC + _NK
EV_KW = EV_KS + _NK
EV_CQ = EV_KW + _NK
EV_CKV = EV_CQ + MLA_Q_RANK
EV_KR = EV_CKV + MLA_KV_RANK
EV_VC = EV_KR + 2 * MLA_ROPE_DIM
EV_VS = EV_VC + _NV
EV_VW = EV_VS + _NV
EV_GATE = EV_VW + _NV
EV_END = EV_GATE + LANES
EV_WIDTH = -(-EV_END // 512) * 512


def _tile(dim, want):
    t = min(dim, want)
    while dim % t:
        t //= 2
    return t


def _alibi_np(n):
    return np.exp2(-8.0 * np.arange(1, n + 1, dtype=np.float32) / n).astype(np.float32)


def _alibi(n):
    return jnp.asarray(_alibi_np(n))


def _head_norm(res, hd, real_d):
    tm, tn = res.shape
    x2 = res * res
    pieces = []
    if hd == HEAD_PAD:
        for s in range(tn // hd):
            ss = jnp.sum(x2[:, s * hd:s * hd + LANES] + x2[:, s * hd + LANES:(s + 1) * hd],
                         axis=1, keepdims=True)
            r = lax.rsqrt(ss * (1.0 / real_d) + EPS)
            pieces.append(jnp.broadcast_to(r, (tm, hd)))
    else:
        lo = lax.broadcasted_iota(jnp.int32, (tm, LANES), 1) < hd
        for s in range(tn // LANES):
            c = x2[:, s * LANES:(s + 1) * LANES]
            ss_lo = jnp.sum(jnp.where(lo, c, 0.0), axis=1, keepdims=True)
            ss_hi = jnp.sum(jnp.where(lo, 0.0, c), axis=1, keepdims=True)
            r_lo = lax.rsqrt(ss_lo * (1.0 / real_d) + EPS)
            r_hi = lax.rsqrt(ss_hi * (1.0 / real_d) + EPS)
            pieces.append(jnp.where(lo, r_lo, r_hi))
    return jnp.concatenate(pieces, axis=1) if len(pieces) > 1 else pieces[0]


def _mm_kernel(*refs, nk, a_pro, has_a2, has_bias, hd, real_d, has_resid):
    it = iter(refs)
    a_ref = next(it)
    a2_ref = next(it) if has_a2 else None
    w_ref = next(it)
    again_ref = next(it) if a_pro == "rms" else None
    b_ref = next(it) if has_bias else None
    gain_ref = flag_ref = None
    if hd:
        gain_ref = next(it)
        flag_ref = next(it)
    x_ref = gate_ref = None
    if has_resid:
        x_ref = next(it)
        gate_ref = next(it)
    o_ref = next(it)
    acc_ref = next(it) if nk > 1 else None

    a = a_ref[...]
    if a_pro == "silu":
        af = a.astype(F32)
        a = af * jax.nn.sigmoid(af)
    elif a_pro == "rms":
        af = a.astype(F32)
        r = lax.rsqrt(jnp.mean(af * af, axis=1, keepdims=True) + EPS)
        a = af * r * again_ref[...]
    if has_a2:
        k1 = a.shape[1]
        part = (jnp.dot(a.astype(BF16), w_ref[:k1, :].astype(BF16), preferred_element_type=F32)
                + jnp.dot(a2_ref[...].astype(BF16), w_ref[k1:, :].astype(BF16), preferred_element_type=F32))
    else:
        part = jnp.dot(a.astype(BF16), w_ref[...].astype(BF16), preferred_element_type=F32)

    def finish(res):
        if has_bias:
            res = res + b_ref[...]
        if hd:
            r = _head_norm(res, hd, real_d)
            res = res * jnp.where(flag_ref[...] > 0.0, r, 1.0) * gain_ref[...]
        if has_resid:
            res = x_ref[...] + gate_ref[...] * res
        o_ref[...] = res.astype(o_ref.dtype)

    if nk == 1:
        finish(part)
    else:
        k = pl.program_id(2)

        @pl.when(k == 0)
        def _():
            acc_ref[...] = part

        @pl.when(k > 0)
        def _():
            acc_ref[...] += part

        @pl.when(k == nk - 1)
        def _():
            finish(acc_ref[...])


def matmul(a, w, *, a2=None, a_col0=0, a_pro=None, a_gain=None, bias=None,
           head_norm=None, resid=None, out_dtype=BF16, tm=1024, tn=512, tk=4096, name="mm"):
    M = a.shape[0]
    K, N = w.shape
    tm, tn, tk = _tile(M, tm), _tile(N, tn), _tile(K, tk)
    if a_pro == "rms" or a2 is not None:
        tk = K
    assert a_col0 % tk == 0 and M % tm == 0 and N % tn == 0 and K % tk == 0
    nk = K // tk
    koff = a_col0 // tk
    hd = head_norm[0] if head_norm else 0
    real_d = head_norm[1] if head_norm else 0
    if hd:
        assert tn % max(hd, LANES) == 0

    if a2 is None:
        ins = [a, w]
        specs = [pl.BlockSpec((tm, tk), lambda i, j, k: (i, koff + k))]
    else:
        assert a_col0 == 0 and a_pro is None and a.shape[1] + a2.shape[1] == K
        ins = [a, a2, w]
        specs = [pl.BlockSpec((tm, a.shape[1]), lambda i, j, k: (i, 0)),
                 pl.BlockSpec((tm, a2.shape[1]), lambda i, j, k: (i, 0))]
    specs.append(pl.BlockSpec((tk, tn), lambda i, j, k: (k, j)))
    if a_pro == "rms":
        ins.append(a_gain.reshape(1, K).astype(F32))
        specs.append(pl.BlockSpec((1, tk), lambda i, j, k: (0, k)))
    if bias is not None:
        ins.append(bias.reshape(1, N).astype(F32))
        specs.append(pl.BlockSpec((1, tn), lambda i, j, k: (0, j)))
    if hd:
        ins += [head_norm[2].reshape(1, N).astype(F32), head_norm[3].reshape(1, N).astype(F32)]
        specs += [pl.BlockSpec((1, tn), lambda i, j, k: (0, j))] * 2
    if resid is not None:
        x, gate, rows_per_batch = resid
        assert rows_per_batch % tm == 0
        bpb = rows_per_batch // tm
        ins += [x, gate]
        specs += [pl.BlockSpec((tm, tn), lambda i, j, k: (i, j)),
                  pl.BlockSpec((None, 1, tn), lambda i, j, k: (i // bpb, 0, j))]
    kern = functools.partial(_mm_kernel, nk=nk, a_pro=a_pro, has_a2=a2 is not None, has_bias=bias is not None,
                             hd=hd, real_d=real_d, has_resid=resid is not None)
    return pl.pallas_call(
        kern,
        out_shape=jax.ShapeDtypeStruct((M, N), out_dtype),
        grid=(M // tm, N // tn, nk),
        in_specs=specs,
        out_specs=pl.BlockSpec((tm, tn), lambda i, j, k: (i, j)),
        scratch_shapes=[pltpu.VMEM((tm, tn), F32)] if nk > 1 else [],
        compiler_params=pltpu.CompilerParams(
            dimension_semantics=("parallel", "parallel", "arbitrary")),
        name=name,
    )(*ins)


def _split_bf16(v):
    hi = v.astype(BF16)
    lo = (v - hi.astype(F32)).astype(BF16)
    return hi, lo


def _pack_bf16_pairs(v):
    half = v.shape[1] // 2
    vb = v.astype(BF16).astype(F32)
    lo = lax.shift_right_logical(lax.bitcast_convert_type(vb[:, :half], jnp.int32), 16)
    hi = lax.bitcast_convert_type(vb[:, half:], jnp.int32) & jnp.int32(-65536)
    return hi | lo


def _unpack_bf16_pairs(u):
    lo = lax.bitcast_convert_type(lax.shift_left(u, 16), F32)
    hi = lax.bitcast_convert_type(u & jnp.int32(-65536), F32)
    return jnp.concatenate([lo, hi], axis=1).astype(BF16)


def _norm_mod_kernel(x_ref, g_ref, sc_ref, sh_ref, *rest, route):
    x = x_ref[...]
    r = lax.rsqrt(jnp.mean(x * x, axis=1, keepdims=True) + EPS)
    xm = (x * r * g_ref[...]) * (1.0 + sc_ref[...]) + sh_ref[...]
    if not route:
        (o_ref,) = rest
        o_ref[...] = xm.astype(o_ref.dtype)
        return
    rw_ref, rb_ref, o_ref, e_ref, p_ref, rank_ref, cnt_ref, run_ref = rest
    o_ref[...] = _pack_bf16_pairs(xm)
    a_hi, a_lo = _split_bf16(xm)
    w = rw_ref[...]
    w_hi, w_lo = _split_bf16(w)
    logits = (jnp.dot(a_hi, w_hi, preferred_element_type=F32)
              + jnp.dot(a_hi, w_lo, preferred_element_type=F32)
              + jnp.dot(a_lo, w_hi, preferred_element_type=F32)) + rb_ref[...]
    tt = logits.shape[0]
    lane = lax.broadcasted_iota(jnp.int32, (tt, LANES), 1)
    lane_f = lane.astype(F32)
    work = jnp.where(lane < N_EXPERTS, logits, -jnp.inf)
    e_out = jnp.zeros((tt, LANES), F32)
    v_out = jnp.full((tt, LANES), -jnp.inf, F32)
    hits = []
    for kk in range(TOP_K):
        m = jnp.max(work, axis=1, keepdims=True)
        idx = jnp.min(jnp.where(work == m, lane_f, float(LANES)), axis=1, keepdims=True)
        e_out = jnp.where(lane == kk, idx, e_out)
        v_out = jnp.where(lane == kk, m, v_out)
        hits.append(lane_f == idx)
        work = jnp.where(hits[-1], -jnp.inf, work)
    e_out = e_out.astype(jnp.int32)
    vmax = jnp.max(v_out, axis=1, keepdims=True)
    pe = jnp.exp(v_out - vmax)
    p_ref[...] = pe / jnp.sum(pe, axis=1, keepdims=True)
    e_ref[...] = e_out

    @pl.when(pl.program_id(0) == 0)
    def _():
        run_ref[...] = jnp.zeros_like(run_ref)

    onehot = jnp.zeros((tt, LANES), F32)
    for hit in hits:
        onehot = jnp.where(hit, 1.0, onehot)
    earlier = (lax.broadcasted_iota(jnp.int32, (tt, tt), 1)
               < lax.broadcasted_iota(jnp.int32, (tt, tt), 0))
    prefix = jnp.dot(jnp.where(earlier, 1.0, 0.0).astype(BF16), onehot.astype(BF16),
                     preferred_element_type=F32)
    base = run_ref[...] + prefix
    rank = jnp.zeros((tt, LANES), F32)
    for kk, hit in enumerate(hits):
        rk = jnp.sum(jnp.where(hit, base, 0.0), axis=1, keepdims=True)
        rank = jnp.where(lane == kk, rk, rank)
    rank_ref[...] = rank.astype(jnp.int32)
    total = run_ref[...] + jnp.sum(onehot, axis=0, keepdims=True)
    run_ref[...] = total
    cnt_ref[...] = total.astype(jnp.int32)


def norm_mod(x2d, gain, sc, sh, rows_per_batch, *, out_dtype=BF16, router=None, tt=256):
    N, D = x2d.shape
    tt = _tile(rows_per_batch, tt)
    bpb = rows_per_batch // tt
    ins = [x2d, gain.reshape(1, D), sc, sh]
    specs = [pl.BlockSpec((tt, D), lambda i: (i, 0)),
             pl.BlockSpec((1, D), lambda i: (0, 0)),
             pl.BlockSpec((None, 1, D), lambda i: (i // bpb, 0, 0)),
             pl.BlockSpec((None, 1, D), lambda i: (i // bpb, 0, 0))]
    if router is None:
        out_shape = [jax.ShapeDtypeStruct((N, D), out_dtype)]
        out_specs = [pl.BlockSpec((tt, D), lambda i: (i, 0))]
        scratch = []
    else:
        rw, rb = router
        E = rw.shape[1]
        rw_p = jnp.pad(rw, ((0, 0), (0, LANES - E)))
        rb_p = jnp.pad(rb, (0, LANES - E)).reshape(1, LANES)
        ins += [rw_p, rb_p]
        specs += [pl.BlockSpec((D, LANES), lambda i: (0, 0)),
                  pl.BlockSpec((1, LANES), lambda i: (0, 0))]
        out_shape = [jax.ShapeDtypeStruct((N, D // 2), jnp.int32),
                     jax.ShapeDtypeStruct((N, LANES), jnp.int32),
                     jax.ShapeDtypeStruct((N, LANES), F32),
                     jax.ShapeDtypeStruct((N, LANES), jnp.int32),
                     jax.ShapeDtypeStruct((1, LANES), jnp.int32)]
        row = pl.BlockSpec((tt, LANES), lambda i: (i, 0))
        out_specs = [pl.BlockSpec((tt, D // 2), lambda i: (i, 0)), row, row, row,
                     pl.BlockSpec((1, LANES), lambda i: (0, 0))]
        scratch = [pltpu.VMEM((1, LANES), F32)]
    res = pl.pallas_call(
        functools.partial(_norm_mod_kernel, route=router is not None),
        out_shape=out_shape, grid=(N // tt,), in_specs=specs, out_specs=out_specs,
        scratch_shapes=scratch,
        compiler_params=pltpu.CompilerParams(
            dimension_semantics=("parallel",) if router is None else ("arbitrary",)),
        name="norm_mod_route" if router is not None else "norm_mod",
    )(*ins)
    return res if router is not None else res[0]


def _compress_kernel(h_ref, pe_ref, w1_ref, w2_ref, g_ref, o_ref, xf_ref, *, real_d, norm, nh):
    n_tiles = xf_ref.shape[0]
    for j in range(n_tiles):
        xf_ref[j] = h_ref[:, j * LANES:(j + 1) * LANES].astype(F32)
    half = CMP_BLOCK // 2
    top = bot = None
    for l in range(half):
        parts = [xf_ref[j, pl.ds(l, nh, stride=CMP_STRIDE), :] for j in range(n_tiles)]
        xl = jnp.concatenate(parts, axis=1) if n_tiles > 1 else parts[0]
        t = jnp.dot((xl + pe_ref[l:l + 1, :]).astype(BF16), w1_ref[l], preferred_element_type=F32)
        b = jnp.dot((xl + pe_ref[half + l:half + l + 1, :]).astype(BF16), w1_ref[half + l],
                    preferred_element_type=F32)
        top = t if top is None else top + t
        bot = b if bot is None else bot + b
    hid = top + pltpu.roll(bot, nh - 1, 0)
    hid = hid * jax.nn.sigmoid(hid)
    y = jnp.dot(hid.astype(BF16), w2_ref[...], preferred_element_type=F32)
    if norm:
        r = lax.rsqrt(jnp.sum(y * y, axis=1, keepdims=True) * (1.0 / real_d) + EPS)
        y = y * r * g_ref[...]
    o_ref[...] = y.astype(o_ref.dtype)


def compress(h, col0, B, T, pe, w1, w2, gain, d, dp):
    G = NSA_KV_HEADS
    nh = T // CMP_STRIDE
    cb = col0 // dp
    pe_p = jnp.pad(pe, ((0, 0), (0, dp - d)))
    w1_p = jnp.pad(w1.reshape(CMP_BLOCK, d, d), ((0, 0), (0, dp - d), (0, dp - d))).astype(BF16)
    w2_p = jnp.pad(w2, ((0, dp - d), (0, dp - d))).astype(BF16)
    g_p = (jnp.ones((dp,), F32) if gain is None else jnp.pad(gain, (0, dp - d))).reshape(1, dp)
    full = lambda shape: pl.BlockSpec(shape, lambda b, g: (0,) * len(shape))
    return pl.pallas_call(
        functools.partial(_compress_kernel, real_d=d, norm=gain is not None, nh=nh),
        out_shape=jax.ShapeDtypeStruct((B * G, nh, dp), BF16),
        grid=(B, G),
        in_specs=[pl.BlockSpec((T, dp), lambda b, g: (b, cb + g)),
                  full((CMP_BLOCK, dp)), full((CMP_BLOCK, dp, dp)), full((dp, dp)), full((1, dp))],
        out_specs=pl.BlockSpec((None, nh, dp), lambda b, g: (b * G + g, 0, 0)),
        scratch_shapes=[pltpu.VMEM((dp // LANES, T, LANES), F32)],
        compiler_params=pltpu.CompilerParams(dimension_semantics=("parallel", "parallel")),
        name="nsa_compress",
    )(h, pe_p, w1_p, w2_p, g_p)


def _stack_heads(q_ref, n, w):
    return jnp.concatenate([q_ref[:, r * w:(r + 1) * w] for r in range(n)], axis=0)


def _row_scalars(vals, tq):
    rows = len(vals) * tq
    rid = lax.broadcasted_iota(jnp.int32, (rows, 1), 0) // tq
    col = jnp.full((rows, 1), vals[-1], F32)
    for r in range(len(vals) - 2, -1, -1):
        col = jnp.where(rid == r, vals[r], col)
    return col


def _nt_dot(a, b):
    return lax.dot_general(a, b, (((1,), (1,)), ((), ())), preferred_element_type=F32)


def _cmp_attn_kernel(slope_ref, q_ref, kc_ref, vc_ref, ovl_ref, o_ref, bits_ref, *, tq, R, n_slc):
    g = pl.program_id(1)
    t0 = pl.program_id(2) * tq
    rows = R * tq
    q = _stack_heads(q_ref, R, HEAD_PAD)
    s = _nt_dot(q, kc_ref[...])
    ncol = s.shape[1]
    row_t = t0 + lax.broadcasted_iota(jnp.int32, (rows, ncol), 0) % tq
    n_id = lax.broadcasted_iota(jnp.int32, (rows, ncol), 1)
    dist = row_t - (n_id * CMP_STRIDE + (CMP_BLOCK - 1))
    slope = _row_scalars([slope_ref[g * R + r] for r in range(R)], tq)
    valid = dist >= 0
    s = jnp.where(valid, s - slope * dist.astype(F32), NEG)
    m = jnp.max(s, axis=1, keepdims=True)
    p = jnp.where(valid, jnp.exp(s - m), 0.0)
    p = p / jnp.maximum(jnp.sum(p, axis=1, keepdims=True), 1e-30)
    o = jnp.dot(p.astype(BF16), vc_ref[...], preferred_element_type=F32)
    for r in range(R):
        o_ref[:, r * NSA_V_DIM:(r + 1) * NSA_V_DIM] = o[r * tq:(r + 1) * tq].astype(o_ref.dtype)

    psum = p[0:tq]
    for r in range(1, R):
        psum = psum + p[r * tq:(r + 1) * tq]
    p_hi, p_lo = _split_bf16(psum)
    imp = (jnp.dot(p_hi, ovl_ref[...], preferred_element_type=F32)
           + jnp.dot(p_lo, ovl_ref[...], preferred_element_type=F32))
    lane = lax.broadcasted_iota(jnp.int32, (tq, LANES), 1)
    cur = (t0 + lax.broadcasted_iota(jnp.int32, (tq, LANES), 0)) // SLC_BLOCK
    forced = (lane == 0) | (lane == cur) | (lane == cur - 1)
    work = jnp.where(forced, jnp.inf, jnp.where(lane > cur, -jnp.inf, imp))
    removed = -3.0e38
    work = jnp.where(lane < n_slc, jnp.where(work == -jnp.inf, -2.0e38, work), removed)
    sel = jnp.zeros((tq, LANES), jnp.bool_)
    lane_f = lane.astype(F32)
    for _ in range(min(SLC_TOPN, n_slc)):
        mx = jnp.max(work, axis=1, keepdims=True)
        idx = jnp.min(jnp.where(work == mx, lane_f, float(LANES)), axis=1, keepdims=True)
        hit = lane_f == idx
        sel = sel | hit
        work = jnp.where(hit, removed, work)
    half = 16
    w_lo = jnp.where(sel & (lane < half), jnp.left_shift(1, jnp.minimum(lane, half - 1)), 0)
    w_hi = jnp.where(sel & (lane >= half), jnp.left_shift(1, jnp.clip(lane - half, 0, half - 1)), 0)
    b_lo = jnp.sum(w_lo.astype(F32), axis=1, keepdims=True).astype(jnp.int32)
    b_hi = jnp.sum(w_hi.astype(F32), axis=1, keepdims=True).astype(jnp.int32)
    bits = b_lo | jnp.left_shift(b_hi, half)
    bits_ref[...] = jnp.broadcast_to(bits, (tq, LANES))


def cmp_attention(h, kc, vc, B, T, tq=256):
    G, R = NSA_KV_HEADS, NSA_HEADS // NSA_KV_HEADS
    tq = _tile(T, tq)
    nq = T // tq
    n_slc = T // SLC_BLOCK
    n_cmp = kc.shape[1]
    assert n_slc <= 32 and n_cmp <= LANES and n_cmp % 8 == 0
    cmp_start = np.arange(n_cmp) * CMP_STRIDE
    slc_start = np.arange(LANES) * SLC_BLOCK
    ovl = ((cmp_start[:, None] < slc_start[None, :] + SLC_BLOCK)
           & (cmp_start[:, None] + CMP_BLOCK > slc_start[None, :])
           & (np.arange(LANES)[None, :] < n_slc) & (np.arange(n_cmp)[:, None] < n_cmp - 1))
    ovl = jnp.asarray(ovl, BF16)
    qw = R * HEAD_PAD
    return pl.pallas_call(
        functools.partial(_cmp_attn_kernel, tq=tq, R=R, n_slc=n_slc),
        out_shape=[jax.ShapeDtypeStruct((B * T, NSA_HEADS * NSA_V_DIM), BF16),
                   jax.ShapeDtypeStruct((B, G, T, LANES), jnp.int32)],
        grid=(B, G, nq),
        in_specs=[pl.BlockSpec(memory_space=pltpu.SMEM),
                  pl.BlockSpec((tq, qw), lambda b, g, i: (b * nq + i, EV_Q // qw + g)),
                  pl.BlockSpec((None, n_cmp, HEAD_PAD), lambda b, g, i: (b * G + g, 0, 0)),
                  pl.BlockSpec((None, n_cmp, NSA_V_DIM), lambda b, g, i: (b * G + g, 0, 0)),
                  pl.BlockSpec((n_cmp, LANES), lambda b, g, i: (0, 0))],
        out_specs=[pl.BlockSpec((tq, R * NSA_V_DIM), lambda b, g, i: (b * nq + i, g)),
                   pl.BlockSpec((None, None, tq, LANES), lambda b, g, i: (b, g, i, 0))],
        compiler_params=pltpu.CompilerParams(dimension_semantics=("parallel", "parallel", "parallel")),
        name="nsa_cmp_attn",
    )(_alibi(NSA_HEADS), h, kc, vc, ovl)


ALIBI_LANE0 = 64
SEL_LANE0 = 70


def _key_table(T):
    s = np.arange(T)
    tab = np.zeros((T, LANES), np.float32)
    tab[:, ALIBI_LANE0:ALIBI_LANE0 + 3] = (s // 256 * 256)[:, None]
    tab[:, ALIBI_LANE0 + 3:ALIBI_LANE0 + 6] = (s % 256)[:, None]
    tab[s, SEL_LANE0 + s // SLC_BLOCK] = 1.0
    return jnp.asarray(tab, BF16)


def _slope_parts(n_heads):
    s = _alibi_np(n_heads)
    h1 = s.astype(BF16).astype(np.float32)
    h2 = (s - h1).astype(BF16).astype(np.float32)
    h3 = (s - h1 - h2).astype(BF16).astype(np.float32)
    return jnp.asarray(np.stack([h1, h2, h3], axis=1).reshape(-1))


def _flash_kernel(*refs, tq, tk, R, HP, v_off, v_step, window, alibi, use_bits):
    it = iter(refs)
    slope_ref = next(it) if alibi else None
    q_ref = next(it)
    k_ref = next(it)
    v_ref = next(it)
    ktab_ref = next(it) if alibi else None
    bits_ref = next(it) if use_bits else None
    o_ref = next(it)
    m_ref = next(it)
    acc_ref = next(it)

    g = pl.program_id(1)
    t0 = pl.program_id(2) * tq
    rows = R * tq
    vd = NSA_V_DIM
    lane = lax.broadcasted_iota(jnp.int32, (tq, LANES), 1)
    sel_add = None
    if use_bits:
        j = jnp.clip(lane - SEL_LANE0, 0, 31)
        in_sel = (lane >= SEL_LANE0) & (lane < SEL_LANE0 + 32)
        picked = (jnp.right_shift(bits_ref[...], j) & 1) != 0
        sel_add = jnp.where(in_sel & jnp.logical_not(picked), NEG, 0.0)

    qs = []
    for hp in range(HP):
        pieces = []
        for r in range(R):
            c0 = (hp * R + r) * HEAD_PAD
            q_lo = q_ref[:, c0:c0 + LANES]
            q_hi = q_ref[:, c0 + LANES:c0 + HEAD_PAD]
            if alibi:
                hd = (g * HP + hp) * R + r
                add = jnp.zeros((tq, LANES), F32) if sel_add is None else sel_add
                for part in range(3):
                    sp = slope_ref[3 * hd + part]
                    add = jnp.where((lane == ALIBI_LANE0 + part) | (lane == ALIBI_LANE0 + 3 + part), sp, add)
                q_hi = (q_hi.astype(F32) + add).astype(BF16)
            pieces.append(jnp.concatenate([q_lo, q_hi], axis=1))
        qs.append(jnp.concatenate(pieces, axis=0) if R > 1 else pieces[0])

    rel0 = (lax.broadcasted_iota(jnp.int32, (rows, tk), 0) % tq
            - lax.broadcasted_iota(jnp.int32, (rows, tk), 1))
    ones = jnp.ones((tk, LANES), BF16)
    m_ref[...] = jnp.full(m_ref.shape, M_FLOOR, F32)
    acc_ref[...] = jnp.zeros(acc_ref.shape, F32)

    def step(c, masked):
        s0 = pl.multiple_of(c * tk, tk)
        off = t0 - s0
        if masked:
            valid = rel0 >= -off
            if window is not None:
                valid = valid & (rel0 < window - off)
        for hp in range(HP):
            kc = k_ref[pl.ds(s0, tk), hp * HEAD_PAD:(hp + 1) * HEAD_PAD]
            if alibi:
                kc = jnp.concatenate([kc[:, :LANES], kc[:, LANES:] + ktab_ref[pl.ds(s0, tk), :]], axis=1)
            vc = v_ref[pl.ds(s0, tk), v_off + hp * v_step:v_off + hp * v_step + vd]
            s = _nt_dot(qs[hp], kc)
            if masked:
                s = jnp.where(valid, s, NEG)
            tiles = [s[:, j * LANES:(j + 1) * LANES] for j in range(tk // LANES)]
            mx = tiles[0]
            for t in tiles[1:]:
                mx = jnp.maximum(mx, t)
            m_old = m_ref[hp]
            m_new = jnp.maximum(m_old, jnp.max(mx, axis=1, keepdims=True))
            alpha = jnp.exp(m_old - m_new)
            p = jnp.concatenate([jnp.exp(t - m_new).astype(BF16) for t in tiles], axis=1)
            pv = jnp.dot(p, jnp.concatenate([vc, ones], axis=1), preferred_element_type=F32)
            acc_ref[hp] = jnp.concatenate([alpha, alpha], axis=1) * acc_ref[hp] + pv
            m_ref[hp] = m_new

    def body(c, carry):
        s0 = c * tk
        full = s0 + tk - 1 <= t0
        if window is not None:
            full = full & (s0 >= t0 + tq - window)
        lax.cond(full, lambda: step(c, False), lambda: step(c, True))
        return carry

    c_hi = (t0 + tq - 1) // tk
    c_lo = 0 if window is None else jnp.maximum((t0 - window + 1) // tk, 0)
    lax.fori_loop(c_lo, c_hi + 1, body, 0)
    for hp in range(HP):
        acc = acc_ref[hp]
        o = acc[:, :vd] / jnp.maximum(acc[:, vd:], 1e-30)
        for r in range(R):
            c0 = (hp * R + r) * vd
            o_ref[:, c0:c0 + vd] = o[r * tq:(r + 1) * tq].astype(o_ref.dtype)


def flash_attention(q_arr, q_col0, k_arr, k_col0, v_arr, v_col0, *, B, T, G, R, HP=1, v_width=NSA_V_DIM,
                    v_off=0, v_step=0, out_cols, out_col0=0, window=None, n_alibi_heads=0, bits=None,
                    tq=256, tk=256, name="flash"):
    vd = NSA_V_DIM
    tq = _tile(T, tq)
    tk = _tile(T, tk)
    nq = T // tq
    qw, kw, ow = HP * R * HEAD_PAD, HP * HEAD_PAD, HP * R * vd
    assert q_col0 % qw == 0 and k_col0 % kw == 0 and v_col0 % v_width == 0 and out_col0 % ow == 0
    assert bits is None or n_alibi_heads
    qb, kb, vb, ob = q_col0 // qw, k_col0 // kw, v_col0 // v_width, out_col0 // ow
    alibi = n_alibi_heads > 0
    ins, specs = [], []
    if alibi:
        ins.append(_slope_parts(n_alibi_heads))
        specs.append(pl.BlockSpec(memory_space=pltpu.SMEM))
    ins += [q_arr, k_arr, v_arr]
    specs += [pl.BlockSpec((tq, qw), lambda b, g, i: (b * nq + i, qb + g)),
              pl.BlockSpec((T, kw), lambda b, g, i: (b, kb + g)),
              pl.BlockSpec((T, v_width), lambda b, g, i: (b, vb + g))]
    if alibi:
        ins.append(_key_table(T))
        specs.append(pl.BlockSpec((T, LANES), lambda b, g, i: (0, 0)))
    if bits is not None:
        ins.append(bits)
        specs.append(pl.BlockSpec((None, None, tq, LANES), lambda b, g, i: (b, g, i, 0)))
    rows = R * tq
    return pl.pallas_call(
        functools.partial(_flash_kernel, tq=tq, tk=tk, R=R, HP=HP, v_off=v_off, v_step=v_step,
                          window=window, alibi=alibi, use_bits=bits is not None),
        out_shape=jax.ShapeDtypeStruct((B * T, out_cols), BF16),
        grid=(B, G // HP, nq),
        in_specs=specs,
        out_specs=pl.BlockSpec((tq, ow), lambda b, g, i: (b * nq + i, ob + g)),
        scratch_shapes=[pltpu.VMEM((HP, rows, LANES), F32), pltpu.VMEM((HP, rows, 2 * vd), F32)],
        compiler_params=pltpu.CompilerParams(dimension_semantics=("parallel", "parallel", "parallel")),
        name=name,
    )(*ins)


def _nsa_combine_kernel(gate_ref, oc_ref, os_ref, ow_ref, o_ref):
    gates = jax.nn.sigmoid(gate_ref[...].astype(F32))
    for h in range(NSA_HEADS):
        sl = slice(h * NSA_V_DIM, (h + 1) * NSA_V_DIM)
        acc = None
        for br, ref in enumerate((oc_ref, os_ref, ow_ref)):
            gcol = gates[:, 3 * h + br:3 * h + br + 1]
            term = gcol * ref[:, sl].astype(F32)
            acc = term if acc is None else acc + term
        o_ref[:, sl] = acc.astype(o_ref.dtype)


def nsa_combine(h, o_cmp, o_slc, o_win, tt=256):
    N, W = o_cmp.shape
    tt = _tile(N, tt)
    gb = EV_GATE // LANES
    blk = pl.BlockSpec((tt, W), lambda i: (i, 0))
    return pl.pallas_call(
        _nsa_combine_kernel,
        out_shape=jax.ShapeDtypeStruct((N, W), BF16),
        grid=(N // tt,),
        in_specs=[pl.BlockSpec((tt, LANES), lambda i: (i, gb)), blk, blk, blk],
        out_specs=blk,
        compiler_params=pltpu.CompilerParams(dimension_semantics=("parallel",)),
        name="nsa_combine",
    )(h, o_cmp, o_slc, o_win)


def _mla_prep_kernel(*refs, shared_rope):
    if shared_rope:
        x_ref, hi_ref, alo_ref, ahi_ref, bhi_ref, o_ref = refs
    else:
        x_ref, alo_ref, ahi_ref, bhi_ref, o_ref = refs
    tt = x_ref.shape[0]
    is_rope = lax.broadcasted_iota(jnp.int32, (tt, LANES), 1) < MLA_ROPE_DIM
    a_hi, b_hi, a_lo = ahi_ref[...], bhi_ref[...], alo_ref[...]

    def rope_part(x_hi):
        ss = jnp.sum(jnp.where(is_rope, x_hi * x_hi, 0.0), axis=1, keepdims=True)
        return x_hi * a_hi + pltpu.roll(x_hi * b_hi, MLA_ROPE_DIM, 1), ss

    if shared_rope:
        y_hi, ss_hi = rope_part(hi_ref[...].astype(F32))
    for h in range(MLA_HEADS):
        c = h * HEAD_PAD
        x_lo = x_ref[:, c:c + LANES].astype(F32)
        if not shared_rope:
            y_hi, ss_hi = rope_part(x_ref[:, c + LANES:c + HEAD_PAD].astype(F32))
        ss = jnp.sum(x_lo * x_lo, axis=1, keepdims=True) + ss_hi
        r = lax.rsqrt(ss * (1.0 / (MLA_NOPE_DIM + MLA_ROPE_DIM)) + EPS)
        o_ref[:, c:c + LANES] = (x_lo * r * a_lo).astype(o_ref.dtype)
        o_ref[:, c + LANES:c + HEAD_PAD] = (y_hi * r).astype(o_ref.dtype)


def mla_prep(x_arr, rope_arr, rope_col0, a_lo, a_hi, b_hi, T, tt=256):
    N = x_arr.shape[0]
    W = MLA_HEADS * HEAD_PAD
    tt = _tile(T, tt)
    nt = T // tt
    shared = rope_arr is not None
    ins = [x_arr]
    specs = [pl.BlockSpec((tt, W), lambda i: (i, 0))]
    if shared:
        rb = rope_col0 // LANES
        ins.append(rope_arr)
        specs.append(pl.BlockSpec((tt, LANES), lambda i: (i, rb)))
    ins += [a_lo, a_hi, b_hi]
    specs += [pl.BlockSpec((1, LANES), lambda i: (0, 0)),
              pl.BlockSpec((tt, LANES), lambda i: (i % nt, 0)),
              pl.BlockSpec((tt, LANES), lambda i: (i % nt, 0))]
    return pl.pallas_call(
        functools.partial(_mla_prep_kernel, shared_rope=shared),
        out_shape=jax.ShapeDtypeStruct((N, W), BF16),
        grid=(N // tt,),
        in_specs=specs,
        out_specs=pl.BlockSpec((tt, W), lambda i: (i, 0)),
        compiler_params=pltpu.CompilerParams(dimension_semantics=("parallel",)),
        name="mla_prep",
    )(*ins)


def _rope_tables(gain, T, scale):
    half = MLA_ROPE_DIM // 2
    inv_freq = ROPE_THETA ** (-jnp.arange(half, dtype=F32) / half)
    ang = jnp.arange(T, dtype=F32)[:, None] * inv_freq[None, :]
    cos2 = jnp.concatenate([jnp.cos(ang), jnp.cos(ang)], axis=1)
    sin_s = jnp.concatenate([-jnp.sin(ang), jnp.sin(ang)], axis=1)
    g_nope, g_rope = gain[:MLA_NOPE_DIM], gain[MLA_NOPE_DIM:]
    g_perm = jnp.concatenate([g_rope[half:], g_rope[:half]])
    zeros = jnp.zeros((T, MLA_ROPE_DIM), F32)
    a_lo = (g_nope * scale).reshape(1, LANES)
    a_hi = jnp.concatenate([g_rope[None, :] * cos2 * scale, zeros], axis=1)
    b_hi = jnp.concatenate([zeros, g_perm[None, :] * sin_s * scale], axis=1)
    return a_lo, a_hi, b_hi


def _swa_kernel(qadd_ref, slope_ref, sink_ref, q_ref, k_ref, v_ref, ktab_ref, o_ref, mask_ref,
                *, tq, tk, R):
    t0 = pl.program_id(2) * tq
    npair = R // 2
    rows = npair * tq
    qp = _stack_heads(q_ref, npair, LANES)
    lane = lax.broadcasted_iota(jnp.int32, (rows, LANES), 1)
    first = lane < SWA_HEAD_DIM
    start = pl.multiple_of(jnp.maximum(t0 - SWA_WINDOW, 0), LANES)
    kc = jnp.concatenate([k_ref[pl.ds(start, tk), :], ktab_ref[pl.ds(start, tk), :]], axis=1)
    v_aug = jnp.concatenate([v_ref[pl.ds(start, tk), :], jnp.ones((tk, LANES), BF16)], axis=1)
    @pl.when(pl.program_id(2) <= 1)
    def _():
        row_in = lax.broadcasted_iota(jnp.int32, (2 * rows, tk), 0) % tq
        rel = t0 - start + row_in - lax.broadcasted_iota(jnp.int32, (2 * rows, tk), 1)
        mask_ref[...] = jnp.where((rel >= 0) & (rel < SWA_WINDOW), 0.0, NEG)

    t_row = (t0 + lax.broadcasted_iota(jnp.int32, (2 * rows, LANES), 0) % tq).astype(F32)
    zero = jnp.zeros_like(qp)
    qu = jnp.concatenate([jnp.where(first, qp, zero), jnp.where(first, zero, qp)], axis=0)
    qa = jnp.concatenate([qu, qadd_ref[...].reshape(2 * rows, LANES)], axis=1)
    sink_t = (sink_ref[...].reshape(2 * rows, LANES)
              + slope_ref[...].reshape(2 * rows, LANES) * t_row)
    s = _nt_dot(qa, kc) + mask_ref[...]
    tiles = [s[:, j * LANES:(j + 1) * LANES] for j in range(tk // LANES)]
    mx = tiles[0]
    for t in tiles[1:]:
        mx = jnp.maximum(mx, t)
    m = jnp.maximum(jnp.max(mx, axis=1, keepdims=True), sink_t)
    p = jnp.concatenate([jnp.exp(t - m).astype(BF16) for t in tiles], axis=1)
    pv = jnp.dot(p, v_aug, preferred_element_type=F32)
    denom = pv[:, LANES:] + jnp.exp(sink_t - m)
    both = pv[:, :LANES] / jnp.maximum(denom, 1e-30)
    o = jnp.where(first, both[:rows], both[rows:])
    for p_ in range(npair):
        o_ref[:, p_ * LANES:(p_ + 1) * LANES] = o[p_ * tq:(p_ + 1) * tq].astype(o_ref.dtype)


def swa_attention(h, sinks, B, T, k_col0, v_col0, tq=128):
    G, R = SWA_KV_HEADS, SWA_HEADS // SWA_KV_HEADS
    tq = _tile(T, tq)
    tk = min(T, tq + SWA_WINDOW)
    assert tk % LANES == 0 and tq % LANES == 0
    nq = T // tq
    qw = R * SWA_HEAD_DIM
    kb, vb = k_col0 // LANES, v_col0 // LANES
    npair = R // 2
    rows = npair * tq
    per_head = lambda v: jnp.repeat(v.reshape(G, npair, 2).transpose(0, 2, 1), tq, axis=2)
    lanes = lambda v: jnp.broadcast_to(v[..., None], (G, 2, rows, LANES))
    parts = _slope_parts(SWA_HEADS).reshape(SWA_HEADS, 3)
    lane_id = np.arange(LANES)
    qadd = jnp.zeros((G, 2, rows, LANES), F32)
    for part in range(3):
        at_lane = (lane_id == ALIBI_LANE0 + part) | (lane_id == ALIBI_LANE0 + 3 + part)
        qadd = jnp.where(at_lane[None, None, None, :], lanes(per_head(parts[:, part])), qadd)
    slope_rows = lanes(per_head(_alibi(SWA_HEADS)))
    sink_rows = lanes(per_head(sinks.astype(F32)))
    const_spec = pl.BlockSpec((None, 2, rows, LANES), lambda b, g, i: (g, 0, 0, 0))
    return pl.pallas_call(
        functools.partial(_swa_kernel, tq=tq, tk=tk, R=R),
        out_shape=jax.ShapeDtypeStruct((B * T, SWA_HEADS * SWA_HEAD_DIM), BF16),
        grid=(B, G, nq),
        in_specs=[const_spec, const_spec, const_spec,
                  pl.BlockSpec((tq, qw), lambda b, g, i: (b * nq + i, g)),
                  pl.BlockSpec((T, LANES), lambda b, g, i: (b, kb + g)),
                  pl.BlockSpec((T, LANES), lambda b, g, i: (b, vb + g)),
                  pl.BlockSpec((T, LANES), lambda b, g, i: (0, 0))],
        out_specs=pl.BlockSpec((tq, qw), lambda b, g, i: (b * nq + i, g)),
        scratch_shapes=[pltpu.VMEM((2 * rows, tk), F32)],
        compiler_params=pltpu.CompilerParams(dimension_semantics=("parallel", "parallel", "arbitrary")),
        name="swa_attn",
    )(qadd.astype(BF16), slope_rows, sink_rows, h, h, h, _key_table(T))


def _dispatch_kernel(zfrom_ref, zto_ref, pos_ref, src_ref, dst_hbm, zrow, sem, zsem, *, td, n_seg):
    @pl.when(pl.program_id(0) == 0)
    def _():
        zrow[...] = jnp.zeros_like(zrow)

        def zero_copy(r):
            return pltpu.make_async_copy(zrow, dst_hbm.at[pl.ds(r, 1)], zsem)

        def seg(s, c):
            lax.fori_loop(zfrom_ref[s], zto_ref[s], lambda r, cc: (zero_copy(r).start(), cc)[1], 0)
            return c

        def seg_wait(s, c):
            lax.fori_loop(zfrom_ref[s], zto_ref[s], lambda r, cc: (zero_copy(r).wait(), cc)[1], 0)
            return c

        lax.fori_loop(0, n_seg, seg, 0)
        lax.fori_loop(0, n_seg, seg_wait, 0)

    def row_copy(t, k):
        return pltpu.make_async_copy(src_ref.at[pl.ds(t, 1)],
                                     dst_hbm.at[pl.ds(pos_ref[0, t * TOP_K + k], 1)], sem)

    def issue(t, c):
        for k in range(TOP_K):
            row_copy(t, k).start()
        return c

    def drain(t, c):
        for k in range(TOP_K):
            row_copy(t, k).wait()
        return c

    lax.fori_loop(0, td, issue, 0)
    lax.fori_loop(0, td, drain, 0)


def moe_dispatch(xm_packed, pos, zfrom, zto, cap, td=128):
    N, W = xm_packed.shape
    td = _tile(N, td)
    pos3 = pos.reshape(N // td, 1, td * TOP_K)
    grid_spec = pltpu.PrefetchScalarGridSpec(
        num_scalar_prefetch=2,
        grid=(N // td,),
        in_specs=[pl.BlockSpec((None, 1, td * TOP_K), lambda i, zf, zt: (i, 0, 0), memory_space=pltpu.SMEM),
                  pl.BlockSpec((td, W), lambda i, zf, zt: (i, 0))],
        out_specs=pl.BlockSpec(memory_space=pl.ANY),
        scratch_shapes=[pltpu.VMEM((1, W), jnp.int32), pltpu.SemaphoreType.DMA(()),
                        pltpu.SemaphoreType.DMA(())],
    )
    return pl.pallas_call(
        functools.partial(_dispatch_kernel, td=td, n_seg=zfrom.shape[0]),
        out_shape=jax.ShapeDtypeStruct((cap, W), jnp.int32),
        grid_spec=grid_spec,
        compiler_params=pltpu.CompilerParams(dimension_semantics=("arbitrary",), has_side_effects=True),
        name="moe_dispatch",
    )(zfrom, zto, pos3, xm_packed)


def _expert_kernel(be_ref, first_ref, nu_ref, xs_ref, wg_hbm, bg_ref, wu_hbm, bu_ref, wd_hbm, bd_ref,
                   o_ref, wg_s, wu_s, wd_s, stg_a, stg_d, sem, *, ca, cd, layer):
    blk = pl.program_id(0)
    e = be_ref[blk]
    D, F = wg_s.shape

    chunks = []
    for src, dst in ((wg_hbm, wg_s), (wu_hbm, wu_s)):
        chunks += [(src, dst, c * ca, ca, stg_a, 0) for c in range(D // ca)]
    chunks += [(wd_hbm, wd_s, c * cd, cd, stg_d, 2) for c in range(F // cd)]

    def chunk_copy(i):
        src, _, r0, n, stg, s0 = chunks[i]
        slot = i % 2
        return pltpu.make_async_copy(src.at[layer, e, pl.ds(r0, n), :], stg.at[slot], sem.at[s0 + slot])

    @pl.when((blk < nu_ref[0]) & (first_ref[blk] == 1))
    def _():
        chunk_copy(0).start()
        for i, (_, dst, r0, n, stg, _) in enumerate(chunks):
            if i + 1 < len(chunks):
                chunk_copy(i + 1).start()
            chunk_copy(i).wait()
            dst[pl.ds(r0, n), :] = stg[i % 2].astype(BF16)

    @pl.when(blk < nu_ref[0])
    def _():
        x = _unpack_bf16_pairs(xs_ref[...])
        gg = jnp.dot(x, wg_s[...], preferred_element_type=F32) + bg_ref[...]
        uu = jnp.dot(x, wu_s[...], preferred_element_type=F32) + bu_ref[...]
        gg = jnp.minimum(gg, SWIGLU_LIMIT)
        uu = jnp.clip(uu, -SWIGLU_LIMIT, SWIGLU_LIMIT)
        act = gg * jax.nn.sigmoid(SWIGLU_ALPHA * gg) * (uu + 1.0)
        y = jnp.dot(act.astype(BF16), wd_s[...], preferred_element_type=F32) + bd_ref[...]
        o_ref[...] = _pack_bf16_pairs(y)

    @pl.when(blk >= nu_ref[0])
    def _():
        o_ref[...] = jnp.zeros_like(o_ref)


def moe_experts(xs, blk_e, first, n_used, layer, wg, bg, wu, bu, wd, bd, rb=MOE_ROW_BLOCK):
    cap, W = xs.shape
    depth, E, D, F = wg.shape
    n_blk = cap // rb
    ca, cd = _tile(D, 512), _tile(F, 128)
    by_expert = lambda shape: pl.BlockSpec((None,) + shape,
                                           lambda i, be, fi, nu: (layer * E + be[i], 0, 0))
    grid_spec = pltpu.PrefetchScalarGridSpec(
        num_scalar_prefetch=3,
        grid=(n_blk,),
        in_specs=[pl.BlockSpec((rb, W), lambda i, be, fi, nu: (i, 0)),
                  pl.BlockSpec(memory_space=pl.ANY), by_expert((1, F)),
                  pl.BlockSpec(memory_space=pl.ANY), by_expert((1, F)),
                  pl.BlockSpec(memory_space=pl.ANY), by_expert((1, D))],
        out_specs=pl.BlockSpec((rb, W), lambda i, be, fi, nu: (i, 0)),
        scratch_shapes=[pltpu.VMEM((D, F), BF16), pltpu.VMEM((D, F), BF16), pltpu.VMEM((F, D), BF16),
                        pltpu.VMEM((2, ca, F), F32), pltpu.VMEM((2, cd, D), F32),
                        pltpu.SemaphoreType.DMA((4,))],
    )
    return pl.pallas_call(
        functools.partial(_expert_kernel, ca=ca, cd=cd, layer=layer),
        out_shape=jax.ShapeDtypeStruct((cap, W), jnp.int32),
        grid_spec=grid_spec,
        compiler_params=pltpu.CompilerParams(dimension_semantics=("arbitrary",)),
        name="moe_experts",
    )(blk_e, first, n_used, xs, wg, bg.reshape(depth * E, 1, F), wu, bu.reshape(depth * E, 1, F),
      wd, bd.reshape(depth * E, 1, D))


def _moe_combine_kernel(pos_ref, pos_next_ref, y_hbm, x_ref, w_ref, gate_ref, o_ref, ybuf, sem,
                        *, tt, n_steps):
    i = pl.program_id(0)
    slot = i % 2

    def row_copy(p_ref, t, k, sl):
        return pltpu.make_async_copy(y_hbm.at[pl.ds(p_ref[0, t * TOP_K + k], 1)],
                                     ybuf.at[sl, k, pl.ds(t, 1)], sem.at[sl])

    def issue(p_ref, sl):
        def body(t, c):
            for k in range(TOP_K):
                row_copy(p_ref, t, k, sl).start()
            return c
        lax.fori_loop(0, tt, body, 0)

    @pl.when(i == 0)
    def _():
        issue(pos_ref, 0)

    @pl.when(i + 1 < n_steps)
    def _():
        issue(pos_next_ref, 1 - slot)

    def drain(t, c):
        for k in range(TOP_K):
            row_copy(pos_ref, t, k, slot).wait()
        return c

    lax.fori_loop(0, tt, drain, 0)
    w = w_ref[...]
    half = x_ref.shape[1] // 2
    y_lo = y_hi = None
    for k in range(TOP_K):
        u = ybuf[slot, k]
        wk = w[:, k:k + 1]
        lo = wk * lax.bitcast_convert_type(lax.shift_left(u, 16), F32)
        hi = wk * lax.bitcast_convert_type(u & jnp.int32(-65536), F32)
        y_lo = lo if y_lo is None else y_lo + lo
        y_hi = hi if y_hi is None else y_hi + hi
    o_ref[:, :half] = x_ref[:, :half] + gate_ref[:, :half] * y_lo
    o_ref[:, half:] = x_ref[:, half:] + gate_ref[:, half:] * y_hi


def moe_combine(yb, pos, top_w, x2d, gate, rows_per_batch, tt=64):
    N, D = x2d.shape
    tt = _tile(rows_per_batch, tt)
    bpb = rows_per_batch // tt
    n_steps = N // tt
    pos3 = pos.reshape(n_steps, 1, tt * TOP_K)
    pos_spec = lambda f: pl.BlockSpec((None, 1, tt * TOP_K), f, memory_space=pltpu.SMEM)
    return pl.pallas_call(
        functools.partial(_moe_combine_kernel, tt=tt, n_steps=n_steps),
        out_shape=jax.ShapeDtypeStruct((N, D), F32),
        grid=(n_steps,),
        in_specs=[pos_spec(lambda i: (i, 0, 0)),
                  pos_spec(lambda i: (jnp.minimum(i + 1, n_steps - 1), 0, 0)),
                  pl.BlockSpec(memory_space=pl.ANY),
                  pl.BlockSpec((tt, D), lambda i: (i, 0)),
                  pl.BlockSpec((tt, LANES), lambda i: (i, 0)),
                  pl.BlockSpec((None, 1, D), lambda i: (i // bpb, 0, 0))],
        out_specs=pl.BlockSpec((tt, D), lambda i: (i, 0)),
        scratch_shapes=[pltpu.VMEM((2, TOP_K, tt, D // 2), jnp.int32), pltpu.SemaphoreType.DMA((2,))],
        compiler_params=pltpu.CompilerParams(dimension_semantics=("arbitrary",)),
        name="moe_combine",
    )(pos3, pos3, yb, x2d, top_w, gate)


def moe_layer(x2d, gain, sc, sh, gate, rows_per_batch, router_w, router_b, layer, wg, bg, wu, bu, wd, bd):
    N, D = x2d.shape
    RB = MOE_ROW_BLOCK
    E = N_EXPERTS
    xm_packed, top_e, top_w, rank, counts = norm_mod(x2d, gain, sc, sh, rows_per_batch,
                                                     router=(router_w, router_b))
    nk = N * TOP_K
    counts = counts[0, :E]
    padded = (counts + RB - 1) // RB * RB
    pad_end = jnp.cumsum(padded)
    pad_start = pad_end - padded
    e4 = top_e[:, :TOP_K]
    start_of = jnp.sum(jnp.where(e4[:, :, None] == jnp.arange(E)[None, None, :],
                                 pad_start[None, None, :], 0), axis=2)
    pos = (start_of + rank[:, :TOP_K]).astype(jnp.int32)
    cap = (-(-nk // RB)) * RB + E * RB
    n_blk = cap // RB
    blk_start = jnp.arange(n_blk, dtype=jnp.int32) * RB
    blk_e = jnp.minimum(jnp.sum(pad_end[None, :] <= blk_start[:, None], axis=1), E - 1).astype(jnp.int32)
    first = jnp.concatenate([jnp.ones((1,), jnp.int32), (blk_e[1:] != blk_e[:-1]).astype(jnp.int32)])
    n_used = (pad_end[-1] // RB).astype(jnp.int32).reshape(1)
    zfrom = jnp.concatenate([pad_start + counts, pad_end[-1:]]).astype(jnp.int32)
    zto = jnp.concatenate([pad_end, jnp.full((1,), cap)]).astype(jnp.int32)
    xs = moe_dispatch(xm_packed, pos, zfrom, zto, cap)
    yb = moe_experts(xs, blk_e, first, n_used, layer, wg, bg, wu, bu, wd, bd)
    return moe_combine(yb, pos, top_w, x2d, gate, rows_per_batch)


def _pad_heads(w, n_heads, d, dp):
    lead = w.shape[:-1]
    w = w.reshape(lead + (n_heads, d))
    w = jnp.pad(w, [(0, 0)] * len(lead) + [(0, 0), (0, dp - d)])
    return w.reshape(lead + (n_heads * dp,))


def _even_in_weights(w_in, q_gain, k_gain):
    D = w_in.shape[0]
    cuts = np.cumsum(EVEN_IN_SPLITS)[:-1].tolist()
    q, kc, vc, ks, vs, kw, vw, gates, cq, kva = jnp.split(w_in, cuts, axis=1)
    half = MLA_ROPE_DIM // 2
    kr = kva[:, MLA_KV_RANK:]
    kr_perm = jnp.concatenate([kr[:, half:], kr[:, :half]], axis=1)
    G = NSA_KV_HEADS
    cols = [_pad_heads(q, NSA_HEADS, NSA_QK_DIM, HEAD_PAD),
            _pad_heads(kc, G, NSA_QK_DIM, HEAD_PAD), _pad_heads(ks, G, NSA_QK_DIM, HEAD_PAD),
            _pad_heads(kw, G, NSA_QK_DIM, HEAD_PAD), cq, kva[:, :MLA_KV_RANK], kr, kr_perm,
            vc, vs, vw, gates]
    w = jnp.concatenate(cols, axis=1)
    w = jnp.pad(w, ((0, 0), (0, EV_WIDTH - w.shape[1]))).astype(BF16)
    scale = NSA_QK_DIM ** -0.5
    pad_g = lambda g: jnp.pad(g, (0, HEAD_PAD - NSA_QK_DIM))
    gain = jnp.ones((EV_WIDTH,), F32)
    gain = gain.at[EV_Q:EV_Q + _NQ].set(jnp.tile(pad_g(q_gain * scale), NSA_HEADS))
    gain = gain.at[EV_KS:EV_KS + _NK].set(jnp.tile(pad_g(k_gain[1]), G))
    gain = gain.at[EV_KW:EV_KW + _NK].set(jnp.tile(pad_g(k_gain[2]), G))
    col = np.arange(EV_WIDTH)
    flag = ((col < EV_Q + _NQ) | ((col >= EV_KS) & (col < EV_KW + _NK))).astype(np.float32)
    return w, gain, jnp.asarray(flag)


def _even_mixer(xm, x2d, g_a, B, T, w_in, w_out, q_gain, k_gain, pe_k, pe_v, w_ck1, w_ck2, w_cv1, w_cv2,
                g_cq, g_ckv, w_uq, w_ukv, mq_gain, mk_gain):
    N, D = x2d.shape
    G, R = NSA_KV_HEADS, NSA_HEADS // NSA_KV_HEADS
    w_p, gain, flag = _even_in_weights(w_in, q_gain, k_gain)
    h = matmul(xm, w_p, head_norm=(HEAD_PAD, NSA_QK_DIM, gain, flag), name="in_proj_even")

    kc = compress(h, EV_KC, B, T, pe_k, w_ck1, w_ck2, k_gain[0], NSA_QK_DIM, HEAD_PAD)
    vc = compress(h, EV_VC, B, T, pe_v, w_cv1, w_cv2, None, NSA_V_DIM, NSA_V_DIM)

    o_cmp, bits = cmp_attention(h, kc, vc, B, T)
    HV = NSA_HEADS * NSA_V_DIM
    o_slc = flash_attention(h, EV_Q, h, EV_KS, h, EV_VS, B=B, T=T, G=G, R=R, out_cols=HV,
                            n_alibi_heads=NSA_HEADS, bits=bits, tk=512, name="nsa_slc_attn")
    o_win = flash_attention(h, EV_Q, h, EV_KW, h, EV_VW, B=B, T=T, G=G, R=R, out_cols=HV,
                            window=NSA_WINDOW, n_alibi_heads=NSA_HEADS, name="nsa_win_attn")
    o_a = nsa_combine(h, o_cmp, o_slc, o_win)

    H = MLA_HEADS
    dqk = MLA_NOPE_DIM + MLA_ROPE_DIM
    half = MLA_ROPE_DIM // 2
    wq = w_uq.reshape(MLA_Q_RANK, H, dqk)
    wq_rope = wq[:, :, MLA_NOPE_DIM:]
    wq_p = jnp.concatenate([wq, wq_rope[:, :, half:], wq_rope[:, :, :half]], axis=2)
    wq_p = wq_p.reshape(MLA_Q_RANK, H * HEAD_PAD).astype(BF16)
    q_raw = matmul(h, wq_p, a_col0=EV_CQ, a_pro="rms", a_gain=g_cq, name="mla_q_up")
    kv_raw = matmul(h, w_ukv.astype(BF16), a_col0=EV_CKV, a_pro="rms", a_gain=g_ckv, name="mla_kv_up")
    qa_lo, qa_hi, qb_hi = _rope_tables(mq_gain, T, dqk ** -0.5)
    ka_lo, ka_hi, kb_hi = _rope_tables(mk_gain, T, 1.0)
    q_m = mla_prep(q_raw, None, 0, qa_lo, qa_hi, qb_hi, T)
    k_m = mla_prep(kv_raw, h, EV_KR, ka_lo, ka_hi, kb_hi, T)
    hp = 4
    kv_w = MLA_NOPE_DIM + MLA_V_DIM
    o_b = flash_attention(q_m, 0, k_m, 0, kv_raw, 0, B=B, T=T, G=H, R=1, HP=hp, v_width=hp * kv_w,
                          v_off=MLA_NOPE_DIM, v_step=kv_w, out_cols=H * MLA_V_DIM, tk=512,
                          name="mla_attn")
    return matmul(o_a, w_out.astype(BF16), a2=o_b, resid=(x2d, g_a, T), out_dtype=F32, name="out_proj_even")


def _odd_mixer(xm, x2d, g_a, B, T, w_in, b_in, w_out, b_out, q_gain, k_gain, sinks):
    D = w_in.shape[0]
    G, hd = SWA_KV_HEADS, SWA_HEAD_DIM
    nq = SWA_HEADS * hd

    def dup(t):
        lead = t.shape[:-1]
        t = t.reshape(lead + (G, 1, hd))
        return jnp.broadcast_to(t, lead + (G, 2, hd)).reshape(lead + (G * 2 * hd,))

    kw = G * hd
    w_p = jnp.concatenate([w_in[:, :nq], dup(w_in[:, nq:nq + kw]), dup(w_in[:, nq + kw:])], axis=1)
    b_p = jnp.concatenate([b_in[:nq], dup(b_in[nq:nq + kw]), dup(b_in[nq + kw:])])
    gain = jnp.concatenate([jnp.tile(q_gain * hd ** -0.5, SWA_HEADS), jnp.tile(k_gain, 2 * G),
                            jnp.ones((2 * kw,), F32)])
    flag = jnp.concatenate([jnp.ones((nq + 2 * kw,), F32), jnp.zeros((2 * kw,), F32)])
    h = matmul(xm, w_p.astype(BF16), bias=b_p, head_norm=(hd, hd, gain, flag), name="in_proj_odd")
    o_c = swa_attention(h, sinks, B, T, nq, nq + 2 * kw)
    return matmul(o_c, w_out.astype(BF16), bias=b_out, resid=(x2d, g_a, T), out_dtype=F32,
                  name="out_proj_odd")


def kernel(x, c, w_mod, mod_table, norm_attn, norm_ffn, w_in_even, w_out_even, nsa_q_gain, nsa_k_gain,
           nsa_pe_k, nsa_pe_v, nsa_w_ck1, nsa_w_ck2, nsa_w_cv1, nsa_w_cv2, mla_g_cq, mla_g_ckv, mla_w_uq,
           mla_w_ukv, mla_q_gain, mla_k_gain, w_in_odd, b_in_odd, w_out_odd, b_out_odd, swa_q_gain,
           swa_k_gain, swa_sinks, router_w, router_b, moe_w_gate, moe_b_gate, moe_w_up, moe_b_up,
           moe_w_down, moe_b_down):
    B, T, D = x.shape
    N = B * T
    depth = mod_table.shape[0]
    c_pad = jnp.pad(c, ((0, 8 - B % 8 if B % 8 else 0), (0, 0)))
    cond = matmul(c_pad, w_mod, a_pro="silu", out_dtype=F32, tn=1024, tk=1024, name="adaln_proj")[:B]
    x2d = x.reshape(N, D)
    for layer in range(depth):
        mod = (cond + mod_table[layer]).reshape(B, 6, 1, D)
        sh_a, sc_a, g_a, sh_f, sc_f, g_f = (mod[:, j] for j in range(6))
        xm = norm_mod(x2d, norm_attn[layer], sc_a, sh_a, T)
        i = layer // 2
        if layer % 2 == 0:
            x2d = _even_mixer(xm, x2d, g_a, B, T, w_in_even[i], w_out_even[i], nsa_q_gain[i], nsa_k_gain[i],
                              nsa_pe_k[i], nsa_pe_v[i], nsa_w_ck1[i], nsa_w_ck2[i], nsa_w_cv1[i],
                              nsa_w_cv2[i], mla_g_cq[i], mla_g_ckv[i], mla_w_uq[i], mla_w_ukv[i],
                              mla_q_gain[i], mla_k_gain[i])
        else:
            x2d = _odd_mixer(xm, x2d, g_a, B, T, w_in_odd[i], b_in_odd[i], w_out_odd[i], b_out_odd[i],
                             swa_q_gain[i], swa_k_gain[i], swa_sinks[i])
        x2d = moe_layer(x2d, norm_ffn[layer], sc_f, sh_f, g_f, T, router_w[layer], router_b[layer],
                        layer, moe_w_gate, moe_b_gate, moe_w_up, moe_b_up, moe_w_down, moe_b_down)
    return x2d.reshape(B, T, D)
```

```python
import functools
import math

import numpy as np
import jax
import jax.numpy as jnp
from jax import lax
from jax.experimental import pallas as pl
from jax.experimental.pallas import tpu as pltpu

BF16 = jnp.bfloat16
F32 = jnp.float32

NSA_HEADS = 16
NSA_KV_HEADS = 4
NSA_QK_DIM = 192
NSA_V_DIM = 128
CMP_BLOCK = 32
CMP_STRIDE = 16
SLC_BLOCK = 64
SLC_TOPN = 8
NSA_WINDOW = 512
MLA_HEADS = 16
MLA_Q_RANK = 1024
MLA_KV_RANK = 512
MLA_NOPE_DIM = 128
MLA_ROPE_DIM = 64
MLA_V_DIM = 128
ROPE_THETA = 10000.0
SWA_HEADS = 64
SWA_KV_HEADS = 8
SWA_HEAD_DIM = 64
SWA_WINDOW = 128
N_EXPERTS = 32
TOP_K = 4
SWIGLU_ALPHA = 1.702
SWIGLU_LIMIT = 7.0
MOE_ROW_BLOCK = 128
EPS = 1e-6

LANES = 128
HEAD_PAD = 256
NEG = -1e30
M_FLOOR = -1e29

EVEN_IN_SPLITS = (NSA_HEADS * NSA_QK_DIM,
                  NSA_KV_HEADS * NSA_QK_DIM, NSA_KV_HEADS * NSA_V_DIM,
                  NSA_KV_HEADS * NSA_QK_DIM, NSA_KV_HEADS * NSA_V_DIM,
                  NSA_KV_HEADS * NSA_QK_DIM, NSA_KV_HEADS * NSA_V_DIM,
                  3 * NSA_HEADS, MLA_Q_RANK, MLA_KV_RANK + MLA_ROPE_DIM)

_NQ = NSA_HEADS * HEAD_PAD
_NK = NSA_KV_HEADS * HEAD_PAD
_NV = NSA_KV_HEADS * NSA_V_DIM
EV_Q = 0
EV_KC = EV_Q + _NQ
EV_KS = EV_KC + _NK
EV_KW = EV_KS + _NK
EV_CQ = EV_KW + _NK
EV_CKV = EV_CQ + MLA_Q_RANK
EV_KR = EV_CKV + MLA_KV_RANK
EV_VC = EV_KR + 2 * MLA_ROPE_DIM
EV_VS = EV_VC + _NV
EV_VW = EV_VS + _NV
EV_GATE = EV_VW + _NV
EV_END = EV_GATE + LANES
EV_WIDTH = -(-EV_END // 512) * 512


def _tile(dim, want):
    t = min(dim, want)
    while dim % t:
        t //= 2
    return t


def _alibi_np(n):
    return np.exp2(-8.0 * np.arange(1, n + 1, dtype=np.float32) / n).astype(np.float32)


def _alibi(n):
    return jnp.asarray(_alibi_np(n))


def _head_norm(res, hd, real_d):
    tm, tn = res.shape
    x2 = res * res
    pieces = []
    if hd == HEAD_PAD:
        for s in range(tn // hd):
            ss = jnp.sum(x2[:, s * hd:s * hd + LANES] + x2[:, s * hd + LANES:(s + 1) * hd],
                         axis=1, keepdims=True)
            r = lax.rsqrt(ss * (1.0 / real_d) + EPS)
            pieces.append(jnp.broadcast_to(r, (tm, hd)))
    else:
        lo = lax.broadcasted_iota(jnp.int32, (tm, LANES), 1) < hd
        for s in range(tn // LANES):
            c = x2[:, s * LANES:(s + 1) * LANES]
            ss_lo = jnp.sum(jnp.where(lo, c, 0.0), axis=1, keepdims=True)
            ss_hi = jnp.sum(jnp.where(lo, 0.0, c), axis=1, keepdims=True)
            r_lo = lax.rsqrt(ss_lo * (1.0 / real_d) + EPS)
            r_hi = lax.rsqrt(ss_hi * (1.0 / real_d) + EPS)
            pieces.append(jnp.where(lo, r_lo, r_hi))
    return jnp.concatenate(pieces, axis=1) if len(pieces) > 1 else pieces[0]


def _mm_kernel(*refs, nk, a_pro, has_a2, has_bias, hd, real_d, has_resid):
    it = iter(refs)
    a_ref = next(it)
    a2_ref = next(it) if has_a2 else None
    w_ref = next(it)
    again_ref = next(it) if a_pro == "rms" else None
    b_ref = next(it) if has_bias else None
    gain_ref = flag_ref = None
    if hd:
        gain_ref = next(it)
        flag_ref = next(it)
    x_ref = gate_ref = None
    if has_resid:
        x_ref = next(it)
        gate_ref = next(it)
    o_ref = next(it)
    acc_ref = next(it) if nk > 1 else None

    a = a_ref[...]
    if a_pro == "silu":
        af = a.astype(F32)
        a = af * jax.nn.sigmoid(af)
    elif a_pro == "rms":
        af = a.astype(F32)
        r = lax.rsqrt(jnp.mean(af * af, axis=1, keepdims=True) + EPS)
        a = af * r * again_ref[...]
    if has_a2:
        k1 = a.shape[1]
        part = (jnp.dot(a.astype(BF16), w_ref[:k1, :].astype(BF16), preferred_element_type=F32)
                + jnp.dot(a2_ref[...].astype(BF16), w_ref[k1:, :].astype(BF16), preferred_element_type=F32))
    else:
        part = jnp.dot(a.astype(BF16), w_ref[...].astype(BF16), preferred_element_type=F32)

    def finish(res):
        if has_bias:
            res = res + b_ref[...]
        if hd:
            r = _head_norm(res, hd, real_d)
            res = res * jnp.where(flag_ref[...] > 0.0, r, 1.0) * gain_ref[...]
        if has_resid:
            res = x_ref[...] + gate_ref[...] * res
        o_ref[...] = res.astype(o_ref.dtype)

    if nk == 1:
        finish(part)
    else:
        k = pl.program_id(2)

        @pl.when(k == 0)
        def _():
            acc_ref[...] = part

        @pl.when(k > 0)
        def _():
            acc_ref[...] += part

        @pl.when(k == nk - 1)
        def _():
            finish(acc_ref[...])


def matmul(a, w, *, a2=None, a_col0=0, a_pro=None, a_gain=None, bias=None,
           head_norm=None, resid=None, out_dtype=BF16, tm=1024, tn=512, tk=4096, name="mm"):
    M = a.shape[0]
    K, N = w.shape
    tm, tn, tk = _tile(M, tm), _tile(N, tn), _tile(K, tk)
    if a_pro == "rms" or a2 is not None:
        tk = K
    assert a_col0 % tk == 0 and M % tm == 0 and N % tn == 0 and K % tk == 0
    nk = K // tk
    koff = a_col0 // tk
    hd = head_norm[0] if head_norm else 0
    real_d = head_norm[1] if head_norm else 0
    if hd:
        assert tn % max(hd, LANES) == 0

    if a2 is None:
        ins = [a, w]
        specs = [pl.BlockSpec((tm, tk), lambda i, j, k: (i, koff + k))]
    else:
        assert a_col0 == 0 and a_pro is None and a.shape[1] + a2.shape[1] == K
        ins = [a, a2, w]
        specs = [pl.BlockSpec((tm, a.shape[1]), lambda i, j, k: (i, 0)),
                 pl.BlockSpec((tm, a2.shape[1]), lambda i, j, k: (i, 0))]
    specs.append(pl.BlockSpec((tk, tn), lambda i, j, k: (k, j)))
    if a_pro == "rms":
        ins.append(a_gain.reshape(1, K).astype(F32))
        specs.append(pl.BlockSpec((1, tk), lambda i, j, k: (0, k)))
    if bias is not None:
        ins.append(bias.reshape(1, N).astype(F32))
        specs.append(pl.BlockSpec((1, tn), lambda i, j, k: (0, j)))
    if hd:
        ins += [head_norm[2].reshape(1, N).astype(F32), head_norm[3].reshape(1, N).astype(F32)]
        specs += [pl.BlockSpec((1, tn), lambda i, j, k: (0, j))] * 2
    if resid is not None:
        x, gate, rows_per_batch = resid
        assert rows_per_batch % tm == 0
        bpb = rows_per_batch // tm
        ins += [x, gate]
        specs += [pl.BlockSpec((tm, tn), lambda i, j, k: (i, j)),
                  pl.BlockSpec((None, 1, tn), lambda i, j, k: (i // bpb, 0, j))]
    kern = functools.partial(_mm_kernel, nk=nk, a_pro=a_pro, has_a2=a2 is not None, has_bias=bias is not None,
                             hd=hd, real_d=real_d, has_resid=resid is not None)
    return pl.pallas_call(
        kern,
        out_shape=jax.ShapeDtypeStruct((M, N), out_dtype),
        grid=(M // tm, N // tn, nk),
        in_specs=specs,
        out_specs=pl.BlockSpec((tm, tn), lambda i, j, k: (i, j)),
        scratch_shapes=[pltpu.VMEM((tm, tn), F32)] if nk > 1 else [],
        compiler_params=pltpu.CompilerParams(
            dimension_semantics=("parallel", "parallel", "arbitrary")),
        name=name,
    )(*ins)


def _split_bf16(v):
    hi = v.astype(BF16)
    lo = (v - hi.astype(F32)).astype(BF16)
    return hi, lo


def _pack_bf16_pairs(v):
    half = v.shape[1] // 2
    vb = v.astype(BF16).astype(F32)
    lo = lax.shift_right_logical(lax.bitcast_convert_type(vb[:, :half], jnp.int32), 16)
    hi = lax.bitcast_convert_type(vb[:, half:], jnp.int32) & jnp.int32(-65536)
    return hi | lo


def _unpack_bf16_pairs(u):
    lo = lax.bitcast_convert_type(lax.shift_left(u, 16), F32)
    hi = lax.bitcast_convert_type(u & jnp.int32(-65536), F32)
    return jnp.concatenate([lo, hi], axis=1).astype(BF16)


def _norm_mod_kernel(x_ref, g_ref, sc_ref, sh_ref, *rest, route):
    x = x_ref[...]
    r = lax.rsqrt(jnp.mean(x * x, axis=1, keepdims=True) + EPS)
    xm = (x * r * g_ref[...]) * (1.0 + sc_ref[...]) + sh_ref[...]
    if not route:
        (o_ref,) = rest
        o_ref[...] = xm.astype(o_ref.dtype)
        return
    rw_ref, rb_ref, o_ref, e_ref, p_ref, rank_ref, cnt_ref, run_ref = rest
    o_ref[...] = _pack_bf16_pairs(xm)
    a_hi, a_lo = _split_bf16(xm)
    w = rw_ref[...]
    w_hi, w_lo = _split_bf16(w)
    logits = (jnp.dot(a_hi, w_hi, preferred_element_type=F32)
              + jnp.dot(a_hi, w_lo, preferred_element_type=F32)
              + jnp.dot(a_lo, w_hi, preferred_element_type=F32)) + rb_ref[...]
    tt = logits.shape[0]
    lane = lax.broadcasted_iota(jnp.int32, (tt, LANES), 1)
    lane_f = lane.astype(F32)
    work = jnp.where(lane < N_EXPERTS, logits, -jnp.inf)
    e_out = jnp.zeros((tt, LANES), F32)
    v_out = jnp.full((tt, LANES), -jnp.inf, F32)
    hits = []
    for kk in range(TOP_K):
        m = jnp.max(work, axis=1, keepdims=True)
        idx = jnp.min(jnp.where(work == m, lane_f, float(LANES)), axis=1, keepdims=True)
        e_out = jnp.where(lane == kk, idx, e_out)
        v_out = jnp.where(lane == kk, m, v_out)
        hits.append(lane_f == idx)
        work = jnp.where(hits[-1], -jnp.inf, work)
    e_out = e_out.astype(jnp.int32)
    vmax = jnp.max(v_out, axis=1, keepdims=True)
    pe = jnp.exp(v_out - vmax)
    p_ref[...] = pe / jnp.sum(pe, axis=1, keepdims=True)
    e_ref[...] = e_out

    @pl.when(pl.program_id(0) == 0)
    def _():
        run_ref[...] = jnp.zeros_like(run_ref)

    onehot = jnp.zeros((tt, LANES), F32)
    for hit in hits:
        onehot = jnp.where(hit, 1.0, onehot)
    earlier = (lax.broadcasted_iota(jnp.int32, (tt, tt), 1)
               < lax.broadcasted_iota(jnp.int32, (tt, tt), 0))
    prefix = jnp.dot(jnp.where(earlier, 1.0, 0.0).astype(BF16), onehot.astype(BF16),
                     preferred_element_type=F32)
    base = run_ref[...] + prefix
    rank = jnp.zeros((tt, LANES), F32)
    for kk, hit in enumerate(hits):
        rk = jnp.sum(jnp.where(hit, base, 0.0), axis=1, keepdims=True)
        rank = jnp.where(lane == kk, rk, rank)
    rank_ref[...] = rank.astype(jnp.int32)
    total = run_ref[...] + jnp.sum(onehot, axis=0, keepdims=True)
    run_ref[...] = total
    cnt_ref[...] = total.astype(jnp.int32)


def norm_mod(x2d, gain, sc, sh, rows_per_batch, *, out_dtype=BF16, router=None, tt=256):
    N, D = x2d.shape
    tt = _tile(rows_per_batch, tt)
    bpb = rows_per_batch // tt
    ins = [x2d, gain.reshape(1, D), sc, sh]
    specs = [pl.BlockSpec((tt, D), lambda i: (i, 0)),
             pl.BlockSpec((1, D), lambda i: (0, 0)),
             pl.BlockSpec((None, 1, D), lambda i: (i // bpb, 0, 0)),
             pl.BlockSpec((None, 1, D), lambda i: (i // bpb, 0, 0))]
    if router is None:
        out_shape = [jax.ShapeDtypeStruct((N, D), out_dtype)]
        out_specs = [pl.BlockSpec((tt, D), lambda i: (i, 0))]
        scratch = []
    else:
        rw, rb = router
        E = rw.shape[1]
        rw_p = jnp.pad(rw, ((0, 0), (0, LANES - E)))
        rb_p = jnp.pad(rb, (0, LANES - E)).reshape(1, LANES)
        ins += [rw_p, rb_p]
        specs += [pl.BlockSpec((D, LANES), lambda i: (0, 0)),
                  pl.BlockSpec((1, LANES), lambda i: (0, 0))]
        out_shape = [jax.ShapeDtypeStruct((N, D // 2), jnp.int32),
                     jax.ShapeDtypeStruct((N, LANES), jnp.int32),
                     jax.ShapeDtypeStruct((N, LANES), F32),
                     jax.ShapeDtypeStruct((N, LANES), jnp.int32),
                     jax.ShapeDtypeStruct((1, LANES), jnp.int32)]
        row = pl.BlockSpec((tt, LANES), lambda i: (i, 0))
        out_specs = [pl.BlockSpec((tt, D // 2), lambda i: (i, 0)), row, row, row,
                     pl.BlockSpec((1, LANES), lambda i: (0, 0))]
        scratch = [pltpu.VMEM((1, LANES), F32)]
    res = pl.pallas_call(
        functools.partial(_norm_mod_kernel, route=router is not None),
        out_shape=out_shape, grid=(N // tt,), in_specs=specs, out_specs=out_specs,
        scratch_shapes=scratch,
        compiler_params=pltpu.CompilerParams(
            dimension_semantics=("parallel",) if router is None else ("arbitrary",)),
        name="norm_mod_route" if router is not None else "norm_mod",
    )(*ins)
    return res if router is not None else res[0]


def _compress_kernel(h_ref, pe_ref, w1_ref, w2_ref, g_ref, o_ref, xf_ref, *, real_d, norm, nh):
    n_tiles = xf_ref.shape[0]
    for j in range(n_tiles):
        xf_ref[j] = h_ref[:, j * LANES:(j + 1) * LANES].astype(F32)
    half = CMP_BLOCK // 2
    top = bot = None
    for l in range(half):
        parts = [xf_ref[j, pl.ds(l, nh, stride=CMP_STRIDE), :] for j in range(n_tiles)]
        xl = jnp.concatenate(parts, axis=1) if n_tiles > 1 else parts[0]
        t = jnp.dot((xl + pe_ref[l:l + 1, :]).astype(BF16), w1_ref[l], preferred_element_type=F32)
        b = jnp.dot((xl + pe_ref[half + l:half + l + 1, :]).astype(BF16), w1_ref[half + l],
                    preferred_element_type=F32)
        top = t if top is None else top + t
        bot = b if bot is None else bot + b
    hid = top + pltpu.roll(bot, nh - 1, 0)
    hid = hid * jax.nn.sigmoid(hid)
    y = jnp.dot(hid.astype(BF16), w2_ref[...], preferred_element_type=F32)
    if norm:
        r = lax.rsqrt(jnp.sum(y * y, axis=1, keepdims=True) * (1.0 / real_d) + EPS)
        y = y * r * g_ref[...]
    o_ref[...] = y.astype(o_ref.dtype)


def compress(h, col0, B, T, pe, w1, w2, gain, d, dp):
    G = NSA_KV_HEADS
    nh = T // CMP_STRIDE
    cb = col0 // dp
    pe_p = jnp.pad(pe, ((0, 0), (0, dp - d)))
    w1_p = jnp.pad(w1.reshape(CMP_BLOCK, d, d), ((0, 0), (0, dp - d), (0, dp - d))).astype(BF16)
    w2_p = jnp.pad(w2, ((0, dp - d), (0, dp - d))).astype(BF16)
    g_p = (jnp.ones((dp,), F32) if gain is None else jnp.pad(gain, (0, dp - d))).reshape(1, dp)
    full = lambda shape: pl.BlockSpec(shape, lambda b, g: (0,) * len(shape))
    return pl.pallas_call(
        functools.partial(_compress_kernel, real_d=d, norm=gain is not None, nh=nh),
        out_shape=jax.ShapeDtypeStruct((B * G, nh, dp), BF16),
        grid=(B, G),
        in_specs=[pl.BlockSpec((T, dp), lambda b, g: (b, cb + g)),
                  full((CMP_BLOCK, dp)), full((CMP_BLOCK, dp, dp)), full((dp, dp)), full((1, dp))],
        out_specs=pl.BlockSpec((None, nh, dp), lambda b, g: (b * G + g, 0, 0)),
        scratch_shapes=[pltpu.VMEM((dp // LANES, T, LANES), F32)],
        compiler_params=pltpu.CompilerParams(dimension_semantics=("parallel", "parallel")),
        name="nsa_compress",
    )(h, pe_p, w1_p, w2_p, g_p)


def _stack_heads(q_ref, n, w):
    return jnp.concatenate([q_ref[:, r * w:(r + 1) * w] for r in range(n)], axis=0)


def _row_scalars(vals, tq):
    rows = len(vals) * tq
    rid = lax.broadcasted_iota(jnp.int32, (rows, 1), 0) // tq
    col = jnp.full((rows, 1), vals[-1], F32)
    for r in range(len(vals) - 2, -1, -1):
        col = jnp.where(rid == r, vals[r], col)
    return col


def _nt_dot(a, b):
    return lax.dot_general(a, b, (((1,), (1,)), ((), ())), preferred_element_type=F32)


def _cmp_attn_kernel(slope_ref, q_ref, kc_ref, vc_ref, ovl_ref, o_ref, bits_ref, *, tq, R, n_slc):
    g = pl.program_id(1)
    t0 = pl.program_id(2) * tq
    rows = R * tq
    q = _stack_heads(q_ref, R, HEAD_PAD)
    s = _nt_dot(q, kc_ref[...])
    ncol = s.shape[1]
    row_t = t0 + lax.broadcasted_iota(jnp.int32, (rows, ncol), 0) % tq
    n_id = lax.broadcasted_iota(jnp.int32, (rows, ncol), 1)
    dist = row_t - (n_id * CMP_STRIDE + (CMP_BLOCK - 1))
    slope = _row_scalars([slope_ref[g * R + r] for r in range(R)], tq)
    valid = dist >= 0
    s = jnp.where(valid, s - slope * dist.astype(F32), NEG)
    m = jnp.max(s, axis=1, keepdims=True)
    p = jnp.where(valid, jnp.exp(s - m), 0.0)
    p = p / jnp.maximum(jnp.sum(p, axis=1, keepdims=True), 1e-30)
    o = jnp.dot(p.astype(BF16), vc_ref[...], preferred_element_type=F32)
    for r in range(R):
        o_ref[:, r * NSA_V_DIM:(r + 1) * NSA_V_DIM] = o[r * tq:(r + 1) * tq].astype(o_ref.dtype)

    psum = p[0:tq]
    for r in range(1, R):
        psum = psum + p[r * tq:(r + 1) * tq]
    p_hi, p_lo = _split_bf16(psum)
    imp = (jnp.dot(p_hi, ovl_ref[...], preferred_element_type=F32)
           + jnp.dot(p_lo, ovl_ref[...], preferred_element_type=F32))
    lane = lax.broadcasted_iota(jnp.int32, (tq, LANES), 1)
    cur = (t0 + lax.broadcasted_iota(jnp.int32, (tq, LANES), 0)) // SLC_BLOCK
    forced = (lane == 0) | (lane == cur) | (lane == cur - 1)
    work = jnp.where(forced, jnp.inf, jnp.where(lane > cur, -jnp.inf, imp))
    removed = -3.0e38
    work = jnp.where(lane < n_slc, jnp.where(work == -jnp.inf, -2.0e38, work), removed)
    sel = jnp.zeros((tq, LANES), jnp.bool_)
    lane_f = lane.astype(F32)
    for _ in range(min(SLC_TOPN, n_slc)):
        mx = jnp.max(work, axis=1, keepdims=True)
        idx = jnp.min(jnp.where(work == mx, lane_f, float(LANES)), axis=1, keepdims=True)
        hit = lane_f == idx
        sel = sel | hit
        work = jnp.where(hit, removed, work)
    half = 16
    w_lo = jnp.where(sel & (lane < half), jnp.left_shift(1, jnp.minimum(lane, half - 1)), 0)
    w_hi = jnp.where(sel & (lane >= half), jnp.left_shift(1, jnp.clip(lane - half, 0, half - 1)), 0)
    b_lo = jnp.sum(w_lo.astype(F32), axis=1, keepdims=True).astype(jnp.int32)
    b_hi = jnp.sum(w_hi.astype(F32), axis=1, keepdims=True).astype(jnp.int32)
    bits = b_lo | jnp.left_shift(b_hi, half)
    bits_ref[...] = jnp.broadcast_to(bits, (tq, LANES))


def cmp_attention(h, kc, vc, B, T, tq=256):
    G, R = NSA_KV_HEADS, NSA_HEADS // NSA_KV_HEADS
    tq = _tile(T, tq)
    nq = T // tq
    n_slc = T // SLC_BLOCK
    n_cmp = kc.shape[1]
    assert n_slc <= 32 and n_cmp <= LANES and n_cmp % 8 == 0
    cmp_start = np.arange(n_cmp) * CMP_STRIDE
    slc_start = np.arange(LANES) * SLC_BLOCK
    ovl = ((cmp_start[:, None] < slc_start[None, :] + SLC_BLOCK)
           & (cmp_start[:, None] + CMP_BLOCK > slc_start[None, :])
           & (np.arange(LANES)[None, :] < n_slc) & (np.arange(n_cmp)[:, None] < n_cmp - 1))
    ovl = jnp.asarray(ovl, BF16)
    qw = R * HEAD_PAD
    return pl.pallas_call(
        functools.partial(_cmp_attn_kernel, tq=tq, R=R, n_slc=n_slc),
        out_shape=[jax.ShapeDtypeStruct((B * T, NSA_HEADS * NSA_V_DIM), BF16),
                   jax.ShapeDtypeStruct((B, G, T, LANES), jnp.int32)],
        grid=(B, G, nq),
        in_specs=[pl.BlockSpec(memory_space=pltpu.SMEM),
                  pl.BlockSpec((tq, qw), lambda b, g, i: (b * nq + i, EV_Q // qw + g)),
                  pl.BlockSpec((None, n_cmp, HEAD_PAD), lambda b, g, i: (b * G + g, 0, 0)),
                  pl.BlockSpec((None, n_cmp, NSA_V_DIM), lambda b, g, i: (b * G + g, 0, 0)),
                  pl.BlockSpec((n_cmp, LANES), lambda b, g, i: (0, 0))],
        out_specs=[pl.BlockSpec((tq, R * NSA_V_DIM), lambda b, g, i: (b * nq + i, g)),
                   pl.BlockSpec((None, None, tq, LANES), lambda b, g, i: (b, g, i, 0))],
        compiler_params=pltpu.CompilerParams(dimension_semantics=("parallel", "parallel", "parallel")),
        name="nsa_cmp_attn",
    )(_alibi(NSA_HEADS), h, kc, vc, ovl)


ALIBI_LANE0 = 64
SEL_LANE0 = 70


def _key_table(T):
    s = np.arange(T)
    tab = np.zeros((T, LANES), np.float32)
    tab[:, ALIBI_LANE0:ALIBI_LANE0 + 3] = (s // 256 * 256)[:, None]
    tab[:, ALIBI_LANE0 + 3:ALIBI_LANE0 + 6] = (s % 256)[:, None]
    tab[s, SEL_LANE0 + s // SLC_BLOCK] = 1.0
    return jnp.asarray(tab, BF16)


def _slope_parts(n_heads):
    s = _alibi_np(n_heads)
    h1 = s.astype(BF16).astype(np.float32)
    h2 = (s - h1).astype(BF16).astype(np.float32)
    h3 = (s - h1 - h2).astype(BF16).astype(np.float32)
    return jnp.asarray(np.stack([h1, h2, h3], axis=1).reshape(-1))


def _flash_kernel(*refs, tq, tk, R, HP, v_off, v_step, window, alibi, use_bits):
    it = iter(refs)
    slope_ref = next(it) if alibi else None
    q_ref = next(it)
    k_ref = next(it)
    v_ref = next(it)
    ktab_ref = next(it) if alibi else None
    bits_ref = next(it) if use_bits else None
    o_ref = next(it)
    m_ref = next(it)
    acc_ref = next(it)

    g = pl.program_id(1)
    t0 = pl.program_id(2) * tq
    rows = R * tq
    vd = NSA_V_DIM
    lane = lax.broadcasted_iota(jnp.int32, (tq, LANES), 1)
    sel_add = None
    if use_bits:
        j = jnp.clip(lane - SEL_LANE0, 0, 31)
        in_sel = (lane >= SEL_LANE0) & (lane < SEL_LANE0 + 32)
        picked = (jnp.right_shift(bits_ref[...], j) & 1) != 0
        sel_add = jnp.where(in_sel & jnp.logical_not(picked), NEG, 0.0)

    qs = []
    for hp in range(HP):
        pieces = []
        for r in range(R):
            c0 = (hp * R + r) * HEAD_PAD
            q_lo = q_ref[:, c0:c0 + LANES]
            q_hi = q_ref[:, c0 + LANES:c0 + HEAD_PAD]
            if alibi:
                hd = (g * HP + hp) * R + r
                add = jnp.zeros((tq, LANES), F32) if sel_add is None else sel_add
                for part in range(3):
                    sp = slope_ref[3 * hd + part]
                    add = jnp.where((lane == ALIBI_LANE0 + part) | (lane == ALIBI_LANE0 + 3 + part), sp, add)
                q_hi = (q_hi.astype(F32) + add).astype(BF16)
            pieces.append(jnp.concatenate([q_lo, q_hi], axis=1))
        qs.append(jnp.concatenate(pieces, axis=0) if R > 1 else pieces[0])

    rel0 = (lax.broadcasted_iota(jnp.int32, (rows, tk), 0) % tq
            - lax.broadcasted_iota(jnp.int32, (rows, tk), 1))
    ones = jnp.ones((tk, LANES), BF16)
    m_ref[...] = jnp.full(m_ref.shape, M_FLOOR, F32)
    acc_ref[...] = jnp.zeros(acc_ref.shape, F32)

    def step(c, masked):
        s0 = pl.multiple_of(c * tk, tk)
        off = t0 - s0
        if masked:
            valid = rel0 >= -off
            if window is not None:
                valid = valid & (rel0 < window - off)
        for hp in range(HP):
            kc = k_ref[pl.ds(s0, tk), hp * HEAD_PAD:(hp + 1) * HEAD_PAD]
            if alibi:
                kc = jnp.concatenate([kc[:, :LANES], kc[:, LANES:] + ktab_ref[pl.ds(s0, tk), :]], axis=1)
            vc = v_ref[pl.ds(s0, tk), v_off + hp * v_step:v_off + hp * v_step + vd]
            s = _nt_dot(qs[hp], kc)
            if masked:
                s = jnp.where(valid, s, NEG)
            tiles = [s[:, j * LANES:(j + 1) * LANES] for j in range(tk // LANES)]
            mx = tiles[0]
            for t in tiles[1:]:
                mx = jnp.maximum(mx, t)
            m_old = m_ref[hp]
            m_new = jnp.maximum(m_old, jnp.max(mx, axis=1, keepdims=True))
            alpha = jnp.exp(m_old - m_new)
            p = jnp.concatenate([jnp.exp(t - m_new).astype(BF16) for t in tiles], axis=1)
            pv = jnp.dot(p, jnp.concatenate([vc, ones], axis=1), preferred_element_type=F32)
            acc_ref[hp] = jnp.concatenate([alpha, alpha], axis=1) * acc_ref[hp] + pv
            m_ref[hp] = m_new

    def body(c, carry):
        s0 = c * tk
        full = s0 + tk - 1 <= t0
        if window is not None:
            full = full & (s0 >= t0 + tq - window)
        lax.cond(full, lambda: step(c, False), lambda: step(c, True))
        return carry

    c_hi = (t0 + tq - 1) // tk
    c_lo = 0 if window is None else jnp.maximum((t0 - window + 1) // tk, 0)
    lax.fori_loop(c_lo, c_hi + 1, body, 0)
    for hp in range(HP):
        acc = acc_ref[hp]
        o = acc[:, :vd] / jnp.maximum(acc[:, vd:], 1e-30)
        for r in range(R):
            c0 = (hp * R + r) * vd
            o_ref[:, c0:c0 + vd] = o[r * tq:(r + 1) * tq].astype(o_ref.dtype)


def flash_attention(q_arr, q_col0, k_arr, k_col0, v_arr, v_col0, *, B, T, G, R, HP=1, v_width=NSA_V_DIM,
                    v_off=0, v_step=0, out_cols, out_col0=0, window=None, n_alibi_heads=0, bits=None,
                    tq=256, tk=256, name="flash"):
    vd = NSA_V_DIM
    tq = _tile(T, tq)
    tk = _tile(T, tk)
    nq = T // tq
    qw, kw, ow = HP * R * HEAD_PAD, HP * HEAD_PAD, HP * R * vd
    assert q_col0 % qw == 0 and k_col0 % kw == 0 and v_col0 % v_width == 0 and out_col0 % ow == 0
    assert bits is None or n_alibi_heads
    qb, kb, vb, ob = q_col0 // qw, k_col0 // kw, v_col0 // v_width, out_col0 // ow
    alibi = n_alibi_heads > 0
    ins, specs = [], []
    if alibi:
        ins.append(_slope_parts(n_alibi_heads))
        specs.append(pl.BlockSpec(memory_space=pltpu.SMEM))
    ins += [q_arr, k_arr, v_arr]
    specs += [pl.BlockSpec((tq, qw), lambda b, g, i: (b * nq + i, qb + g)),
              pl.BlockSpec((T, kw), lambda b, g, i: (b, kb + g)),
              pl.BlockSpec((T, v_width), lambda b, g, i: (b, vb + g))]
    if alibi:
        ins.append(_key_table(T))
        specs.append(pl.BlockSpec((T, LANES), lambda b, g, i: (0, 0)))
    if bits is not None:
        ins.append(bits)
        specs.append(pl.BlockSpec((None, None, tq, LANES), lambda b, g, i: (b, g, i, 0)))
    rows = R * tq
    return pl.pallas_call(
        functools.partial(_flash_kernel, tq=tq, tk=tk, R=R, HP=HP, v_off=v_off, v_step=v_step,
                          window=window, alibi=alibi, use_bits=bits is not None),
        out_shape=jax.ShapeDtypeStruct((B * T, out_cols), BF16),
        grid=(B, G // HP, nq),
        in_specs=specs,
        out_specs=pl.BlockSpec((tq, ow), lambda b, g, i: (b * nq + i, ob + g)),
        scratch_shapes=[pltpu.VMEM((HP, rows, LANES), F32), pltpu.VMEM((HP, rows, 2 * vd), F32)],
        compiler_params=pltpu.CompilerParams(dimension_semantics=("parallel", "parallel", "parallel")),
        name=name,
    )(*ins)


def _nsa_combine_kernel(gate_ref, oc_ref, os_ref, ow_ref, o_ref):
    gates = jax.nn.sigmoid(gate_ref[...].astype(F32))
    for h in range(NSA_HEADS):
        sl = slice(h * NSA_V_DIM, (h + 1) * NSA_V_DIM)
        acc = None
        for br, ref in enumerate((oc_ref, os_ref, ow_ref)):
            gcol = gates[:, 3 * h + br:3 * h + br + 1]
            term = gcol * ref[:, sl].astype(F32)
            acc = term if acc is None else acc + term
        o_ref[:, sl] = acc.astype(o_ref.dtype)


def nsa_combine(h, o_cmp, o_slc, o_win, tt=256):
    N, W = o_cmp.shape
    tt = _tile(N, tt)
    gb = EV_GATE // LANES
    blk = pl.BlockSpec((tt, W), lambda i: (i, 0))
    return pl.pallas_call(
        _nsa_combine_kernel,
        out_shape=jax.ShapeDtypeStruct((N, W), BF16),
        grid=(N // tt,),
        in_specs=[pl.BlockSpec((tt, LANES), lambda i: (i, gb)), blk, blk, blk],
        out_specs=blk,
        compiler_params=pltpu.CompilerParams(dimension_semantics=("parallel",)),
        name="nsa_combine",
    )(h, o_cmp, o_slc, o_win)


def _mla_prep_kernel(*refs, shared_rope):
    if shared_rope:
        x_ref, hi_ref, alo_ref, ahi_ref, bhi_ref, o_ref = refs
    else:
        x_ref, alo_ref, ahi_ref, bhi_ref, o_ref = refs
    tt = x_ref.shape[0]
    is_rope = lax.broadcasted_iota(jnp.int32, (tt, LANES), 1) < MLA_ROPE_DIM
    a_hi, b_hi, a_lo = ahi_ref[...], bhi_ref[...], alo_ref[...]

    def rope_part(x_hi):
        ss = jnp.sum(jnp.where(is_rope, x_hi * x_hi, 0.0), axis=1, keepdims=True)
        return x_hi * a_hi + pltpu.roll(x_hi * b_hi, MLA_ROPE_DIM, 1), ss

    if shared_rope:
        y_hi, ss_hi = rope_part(hi_ref[...].astype(F32))
    for h in range(MLA_HEADS):
        c = h * HEAD_PAD
        x_lo = x_ref[:, c:c + LANES].astype(F32)
        if not shared_rope:
            y_hi, ss_hi = rope_part(x_ref[:, c + LANES:c + HEAD_PAD].astype(F32))
        ss = jnp.sum(x_lo * x_lo, axis=1, keepdims=True) + ss_hi
        r = lax.rsqrt(ss * (1.0 / (MLA_NOPE_DIM + MLA_ROPE_DIM)) + EPS)
        o_ref[:, c:c + LANES] = (x_lo * r * a_lo).astype(o_ref.dtype)
        o_ref[:, c + LANES:c + HEAD_PAD] = (y_hi * r).astype(o_ref.dtype)


def mla_prep(x_arr, rope_arr, rope_col0, a_lo, a_hi, b_hi, T, tt=256):
    N = x_arr.shape[0]
    W = MLA_HEADS * HEAD_PAD
    tt = _tile(T, tt)
    nt = T // tt
    shared = rope_arr is not None
    ins = [x_arr]
    specs = [pl.BlockSpec((tt, W), lambda i: (i, 0))]
    if shared:
        rb = rope_col0 // LANES
        ins.append(rope_arr)
        specs.append(pl.BlockSpec((tt, LANES), lambda i: (i, rb)))
    ins += [a_lo, a_hi, b_hi]
    specs += [pl.BlockSpec((1, LANES), lambda i: (0, 0)),
              pl.BlockSpec((tt, LANES), lambda i: (i % nt, 0)),
              pl.BlockSpec((tt, LANES), lambda i: (i % nt, 0))]
    return pl.pallas_call(
        functools.partial(_mla_prep_kernel, shared_rope=shared),
        out_shape=jax.ShapeDtypeStruct((N, W), BF16),
        grid=(N // tt,),
        in_specs=specs,
        out_specs=pl.BlockSpec((tt, W), lambda i: (i, 0)),
        compiler_params=pltpu.CompilerParams(dimension_semantics=("parallel",)),
        name="mla_prep",
    )(*ins)


def _rope_tables(gain, T, scale):
    half = MLA_ROPE_DIM // 2
    inv_freq = ROPE_THETA ** (-jnp.arange(half, dtype=F32) / half)
    ang = jnp.arange(T, dtype=F32)[:, None] * inv_freq[None, :]
    cos2 = jnp.concatenate([jnp.cos(ang), jnp.cos(ang)], axis=1)
    sin_s = jnp.concatenate([-jnp.sin(ang), jnp.sin(ang)], axis=1)
    g_nope, g_rope = gain[:MLA_NOPE_DIM], gain[MLA_NOPE_DIM:]
    g_perm = jnp.concatenate([g_rope[half:], g_rope[:half]])
    zeros = jnp.zeros((T, MLA_ROPE_DIM), F32)
    a_lo = (g_nope * scale).reshape(1, LANES)
    a_hi = jnp.concatenate([g_rope[None, :] * cos2 * scale, zeros], axis=1)
    b_hi = jnp.concatenate([zeros, g_perm[None, :] * sin_s * scale], axis=1)
    return a_lo, a_hi, b_hi


def _swa_kernel(qadd_ref, slope_ref, sink_ref, q_ref, k_ref, v_ref, ktab_ref, o_ref, mask_ref,
                *, tq, tk, R):
    t0 = pl.program_id(2) * tq
    npair = R // 2
    rows = npair * tq
    qp = _stack_heads(q_ref, npair, LANES)
    lane = lax.broadcasted_iota(jnp.int32, (rows, LANES), 1)
    first = lane < SWA_HEAD_DIM
    start = pl.multiple_of(jnp.maximum(t0 - SWA_WINDOW, 0), LANES)
    kc = jnp.concatenate([k_ref[pl.ds(start, tk), :], ktab_ref[pl.ds(start, tk), :]], axis=1)
    v_aug = jnp.concatenate([v_ref[pl.ds(start, tk), :], jnp.ones((tk, LANES), BF16)], axis=1)
    @pl.when(pl.program_id(2) <= 1)
    def _():
        row_in = lax.broadcasted_iota(jnp.int32, (2 * rows, tk), 0) % tq
        rel = t0 - start + row_in - lax.broadcasted_iota(jnp.int32, (2 * rows, tk), 1)
        mask_ref[...] = jnp.where((rel >= 0) & (rel < SWA_WINDOW), 0.0, NEG)

    t_row = (t0 + lax.broadcasted_iota(jnp.int32, (2 * rows, LANES), 0) % tq).astype(F32)
    zero = jnp.zeros_like(qp)
    qu = jnp.concatenate([jnp.where(first, qp, zero), jnp.where(first, zero, qp)], axis=0)
    qa = jnp.concatenate([qu, qadd_ref[...].reshape(2 * rows, LANES)], axis=1)
    sink_t = (sink_ref[...].reshape(2 * rows, LANES)
              + slope_ref[...].reshape(2 * rows, LANES) * t_row)
    s = _nt_dot(qa, kc) + mask_ref[...]
    tiles = [s[:, j * LANES:(j + 1) * LANES] for j in range(tk // LANES)]
    mx = tiles[0]
    for t in tiles[1:]:
        mx = jnp.maximum(mx, t)
    m = jnp.maximum(jnp.max(mx, axis=1, keepdims=True), sink_t)
    p = jnp.concatenate([jnp.exp(t - m).astype(BF16) for t in tiles], axis=1)
    pv = jnp.dot(p, v_aug, preferred_element_type=F32)
    denom = pv[:, LANES:] + jnp.exp(sink_t - m)
    both = pv[:, :LANES] / jnp.maximum(denom, 1e-30)
    o = jnp.where(first, both[:rows], both[rows:])
    for p_ in range(npair):
        o_ref[:, p_ * LANES:(p_ + 1) * LANES] = o[p_ * tq:(p_ + 1) * tq].astype(o_ref.dtype)


def swa_attention(h, sinks, B, T, k_col0, v_col0, tq=128):
    G, R = SWA_KV_HEADS, SWA_HEADS // SWA_KV_HEADS
    tq = _tile(T, tq)
    tk = min(T, tq + SWA_WINDOW)
    assert tk % LANES == 0 and tq % LANES == 0
    nq = T // tq
    qw = R * SWA_HEAD_DIM
    kb, vb = k_col0 // LANES, v_col0 // LANES
    npair = R // 2
    rows = npair * tq
    per_head = lambda v: jnp.repeat(v.reshape(G, npair, 2).transpose(0, 2, 1), tq, axis=2)
    lanes = lambda v: jnp.broadcast_to(v[..., None], (G, 2, rows, LANES))
    parts = _slope_parts(SWA_HEADS).reshape(SWA_HEADS, 3)
    lane_id = np.arange(LANES)
    qadd = jnp.zeros((G, 2, rows, LANES), F32)
    for part in range(3):
        at_lane = (lane_id == ALIBI_LANE0 + part) | (lane_id == ALIBI_LANE0 + 3 + part)
        qadd = jnp.where(at_lane[None, None, None, :], lanes(per_head(parts[:, part])), qadd)
    slope_rows = lanes(per_head(_alibi(SWA_HEADS)))
    sink_rows = lanes(per_head(sinks.astype(F32)))
    const_spec = pl.BlockSpec((None, 2, rows, LANES), lambda b, g, i: (g, 0, 0, 0))
    return pl.pallas_call(
        functools.partial(_swa_kernel, tq=tq, tk=tk, R=R),
        out_shape=jax.ShapeDtypeStruct((B * T, SWA_HEADS * SWA_HEAD_DIM), BF16),
        grid=(B, G, nq),
        in_specs=[const_spec, const_spec, const_spec,
                  pl.BlockSpec((tq, qw), lambda b, g, i: (b * nq + i, g)),
                  pl.BlockSpec((T, LANES), lambda b, g, i: (b, kb + g)),
                  pl.BlockSpec((T, LANES), lambda b, g, i: (b, vb + g)),
                  pl.BlockSpec((T, LANES), lambda b, g, i: (0, 0))],
        out_specs=pl.BlockSpec((tq, qw), lambda b, g, i: (b * nq + i, g)),
        scratch_shapes=[pltpu.VMEM((2 * rows, tk), F32)],
        compiler_params=pltpu.CompilerParams(dimension_semantics=("parallel", "parallel", "arbitrary")),
        name="swa_attn",
    )(qadd.astype(BF16), slope_rows, sink_rows, h, h, h, _key_table(T))


def _dispatch_kernel(zfrom_ref, zto_ref, pos_ref, src_ref, dst_hbm, zrow, sem, zsem, *, td, n_seg):
    @pl.when(pl.program_id(0) == 0)
    def _():
        zrow[...] = jnp.zeros_like(zrow)

        def zero_copy(r):
            return pltpu.make_async_copy(zrow, dst_hbm.at[pl.ds(r, 1)], zsem)

        def seg(s, c):
            lax.fori_loop(zfrom_ref[s], zto_ref[s], lambda r, cc: (zero_copy(r).start(), cc)[1], 0)
            return c

        def seg_wait(s, c):
            lax.fori_loop(zfrom_ref[s], zto_ref[s], lambda r, cc: (zero_copy(r).wait(), cc)[1], 0)
            return c

        lax.fori_loop(0, n_seg, seg, 0)
        lax.fori_loop(0, n_seg, seg_wait, 0)

    def row_copy(t, k):
        return pltpu.make_async_copy(src_ref.at[pl.ds(t, 1)],
                                     dst_hbm.at[pl.ds(pos_ref[0, t * TOP_K + k], 1)], sem)

    def issue(t, c):
        for k in range(TOP_K):
            row_copy(t, k).start()
        return c

    def drain(t, c):
        for k in range(TOP_K):
            row_copy(t, k).wait()
        return c

    lax.fori_loop(0, td, issue, 0)
    lax.fori_loop(0, td, drain, 0)


def moe_dispatch(xm_packed, pos, zfrom, zto, cap, td=128):
    N, W = xm_packed.shape
    td = _tile(N, td)
    pos3 = pos.reshape(N // td, 1, td * TOP_K)
    grid_spec = pltpu.PrefetchScalarGridSpec(
        num_scalar_prefetch=2,
        grid=(N // td,),
        in_specs=[pl.BlockSpec((None, 1, td * TOP_K), lambda i, zf, zt: (i, 0, 0), memory_space=pltpu.SMEM),
                  pl.BlockSpec((td, W), lambda i, zf, zt: (i, 0))],
        out_specs=pl.BlockSpec(memory_space=pl.ANY),
        scratch_shapes=[pltpu.VMEM((1, W), jnp.int32), pltpu.SemaphoreType.DMA(()),
                        pltpu.SemaphoreType.DMA(())],
    )
    return pl.pallas_call(
        functools.partial(_dispatch_kernel, td=td, n_seg=zfrom.shape[0]),
        out_shape=jax.ShapeDtypeStruct((cap, W), jnp.int32),
        grid_spec=grid_spec,
        compiler_params=pltpu.CompilerParams(dimension_semantics=("arbitrary",), has_side_effects=True),
        name="moe_dispatch",
    )(zfrom, zto, pos3, xm_packed)


def _expert_kernel(be_ref, slot_ref, nxt_ref, c0_ref, c1_ref, nu_ref, xs_ref, wg_hbm, bg_ref, wu_hbm, bu_ref,
                   wd_hbm, bd_ref, o_ref, wg_s, wu_s, wd_s, stg_a, stg_d, sem, *, ca, cd, ring, layer):
    blk = pl.program_id(0)
    e = be_ref[blk]
    s = slot_ref[blk]
    used = blk < nu_ref[0]
    _, D, F = wg_s.shape
    na, nd = D // ca, F // cd
    n_chunks = 2 * na + nd

    def on_chunk(c, expert, fn):
        k = c % ring

        def go(src, dst, r0, n, stg, sem0):
            copy = pltpu.make_async_copy(src.at[layer, expert, pl.ds(r0, n), :], stg.at[k], sem.at[sem0 + k])
            fn(copy, dst, r0, n, stg.at[k])

        @pl.when(c < na)
        def _():
            go(wg_hbm, wg_s, pl.multiple_of(c * ca, ca), ca, stg_a, 0)

        @pl.when((c >= na) & (c < 2 * na))
        def _():
            go(wu_hbm, wu_s, pl.multiple_of((c - na) * ca, ca), ca, stg_a, 0)

        @pl.when(c >= 2 * na)
        def _():
            go(wd_hbm, wd_s, pl.multiple_of((c - 2 * na) * cd, cd), cd, stg_d, ring)

    def start(c, expert):
        on_chunk(c, expert, lambda copy, dst, r0, n, stg: copy.start())

    def finish(c, expert, dst_slot):
        def fn(copy, dst, r0, n, stg):
            copy.wait()
            dst[dst_slot, pl.ds(r0, n), :] = stg[...].astype(BF16)
        on_chunk(c, expert, fn)

    def stream(lo, hi, expert, dst_slot, prestarted):
        if not prestarted:
            for i in range(ring):
                @pl.when(lo + i < hi)
                def _():
                    start(lo + i, expert)

        def body(c, carry):
            finish(c, expert, dst_slot)

            @pl.when(c + ring < hi)
            def _():
                start(c + ring, expert)
            return carry

        lax.fori_loop(lo, hi, body, 0)

    @pl.when(used & (blk == 0))
    def _():
        stream(0, n_chunks, e, s, False)

    c0, c1, nxt = c0_ref[blk], c1_ref[blk], nxt_ref[blk]
    for i in range(ring):
        @pl.when(used & (c0 + i < c1))
        def _():
            start(c0 + i, nxt)

    @pl.when(used)
    def _():
        x = _unpack_bf16_pairs(xs_ref[...])
        gg = jnp.dot(x, wg_s[s], preferred_element_type=F32) + bg_ref[...]
        uu = jnp.dot(x, wu_s[s], preferred_element_type=F32) + bu_ref[...]
        gg = jnp.minimum(gg, SWIGLU_LIMIT)
        uu = jnp.clip(uu, -SWIGLU_LIMIT, SWIGLU_LIMIT)
        act = gg * jax.nn.sigmoid(SWIGLU_ALPHA * gg) * (uu + 1.0)
        y = jnp.dot(act.astype(BF16), wd_s[s], preferred_element_type=F32) + bd_ref[...]
        o_ref[...] = _pack_bf16_pairs(y)
        stream(c0, c1, nxt, 1 - s, True)

    @pl.when(jnp.logical_not(used))
    def _():
        o_ref[...] = jnp.zeros_like(o_ref)


EXPERT_STAGE_RING = 3
VMEM_LIMIT_EXPERTS = 60 * 1024 * 1024


def _expert_schedule(blk_e, n_used, n_chunks):
    n_blk = blk_e.shape[0]
    idx = jnp.arange(n_blk, dtype=jnp.int32)
    first = jnp.concatenate([jnp.ones((1,), jnp.int32), (blk_e[1:] != blk_e[:-1]).astype(jnp.int32)])
    slot = (jnp.cumsum(first) - 1) % 2
    run_start = lax.cummax(jnp.where(first == 1, idx, 0))
    starts_after = jnp.concatenate([jnp.where(first == 1, idx, n_blk)[1:], jnp.full((1,), n_blk, jnp.int32)])
    next_start = lax.cummin(starts_after, reverse=True)
    has_next = next_start < n_used[0]
    nxt = blk_e[jnp.minimum(next_start, n_blk - 1)]
    run_len = jnp.maximum(jnp.minimum(next_start, n_used[0]) - run_start, 1)
    j = idx - run_start
    c0 = jnp.where(has_next, j * n_chunks // run_len, 0)
    c1 = jnp.where(has_next, (j + 1) * n_chunks // run_len, 0)
    as_i32 = lambda v: v.astype(jnp.int32)
    return as_i32(slot), as_i32(nxt), as_i32(c0), as_i32(c1)


def moe_experts(xs, blk_e, n_used, layer, wg, bg, wu, bu, wd, bd, rb=MOE_ROW_BLOCK):
    cap, W = xs.shape
    depth, E, D, F = wg.shape
    n_blk = cap // rb
    ca, cd = _tile(D, 256), _tile(F, 64)
    ring = EXPERT_STAGE_RING
    slot, nxt, c0, c1 = _expert_schedule(blk_e, n_used, 2 * (D // ca) + F // cd)
    by_expert = lambda shape: pl.BlockSpec((None,) + shape,
                                           lambda i, be, *_: (layer * E + be[i], 0, 0))
    grid_spec = pltpu.PrefetchScalarGridSpec(
        num_scalar_prefetch=6,
        grid=(n_blk,),
        in_specs=[pl.BlockSpec((rb, W), lambda i, *_: (i, 0)),
                  pl.BlockSpec(memory_space=pl.ANY), by_expert((1, F)),
                  pl.BlockSpec(memory_space=pl.ANY), by_expert((1, F)),
                  pl.BlockSpec(memory_space=pl.ANY), by_expert((1, D))],
        out_specs=pl.BlockSpec((rb, W), lambda i, *_: (i, 0)),
        scratch_shapes=[pltpu.VMEM((2, D, F), BF16), pltpu.VMEM((2, D, F), BF16), pltpu.VMEM((2, F, D), BF16),
                        pltpu.VMEM((ring, ca, F), F32), pltpu.VMEM((ring, cd, D), F32),
                        pltpu.SemaphoreType.DMA((2 * ring,))],
    )
    return pl.pallas_call(
        functools.partial(_expert_kernel, ca=ca, cd=cd, ring=ring, layer=layer),
        out_shape=jax.ShapeDtypeStruct((cap, W), jnp.int32),
        grid_spec=grid_spec,
        compiler_params=pltpu.CompilerParams(dimension_semantics=("arbitrary",),
                                             vmem_limit_bytes=VMEM_LIMIT_EXPERTS),
        name="moe_experts",
    )(blk_e, slot, nxt, c0, c1, n_used, xs, wg, bg.reshape(depth * E, 1, F), wu, bu.reshape(depth * E, 1, F),
      wd, bd.reshape(depth * E, 1, D))


def _moe_combine_kernel(pos_ref, pos_next_ref, y_hbm, x_ref, w_ref, gate_ref, o_ref, ybuf, sem,
                        *, tt, n_steps):
    i = pl.program_id(0)
    slot = i % 2

    def row_copy(p_ref, t, k, sl):
        return pltpu.make_async_copy(y_hbm.at[pl.ds(p_ref[0, t * TOP_K + k], 1)],
                                     ybuf.at[sl, k, pl.ds(t, 1)], sem.at[sl])

    def issue(p_ref, sl):
        def body(t, c):
            for k in range(TOP_K):
                row_copy(p_ref, t, k, sl).start()
            return c
        lax.fori_loop(0, tt, body, 0)

    @pl.when(i == 0)
    def _():
        issue(pos_ref, 0)

    @pl.when(i + 1 < n_steps)
    def _():
        issue(pos_next_ref, 1 - slot)

    def drain(t, c):
        for k in range(TOP_K):
            row_copy(pos_ref, t, k, slot).wait()
        return c

    lax.fori_loop(0, tt, drain, 0)
    w = w_ref[...]
    half = x_ref.shape[1] // 2
    y_lo = y_hi = None
    for k in range(TOP_K):
        u = ybuf[slot, k]
        wk = w[:, k:k + 1]
        lo = wk * lax.bitcast_convert_type(lax.shift_left(u, 16), F32)
        hi = wk * lax.bitcast_convert_type(u & jnp.int32(-65536), F32)
        y_lo = lo if y_lo is None else y_lo + lo
        y_hi = hi if y_hi is None else y_hi + hi
    o_ref[:, :half] = x_ref[:, :half] + gate_ref[:, :half] * y_lo
    o_ref[:, half:] = x_ref[:, half:] + gate_ref[:, half:] * y_hi


def moe_combine(yb, pos, top_w, x2d, gate, rows_per_batch, tt=64):
    N, D = x2d.shape
    tt = _tile(rows_per_batch, tt)
    bpb = rows_per_batch // tt
    n_steps = N // tt
    pos3 = pos.reshape(n_steps, 1, tt * TOP_K)
    pos_spec = lambda f: pl.BlockSpec((None, 1, tt * TOP_K), f, memory_space=pltpu.SMEM)
    return pl.pallas_call(
        functools.partial(_moe_combine_kernel, tt=tt, n_steps=n_steps),
        out_shape=jax.ShapeDtypeStruct((N, D), F32),
        grid=(n_steps,),
        in_specs=[pos_spec(lambda i: (i, 0, 0)),
                  pos_spec(lambda i: (jnp.minimum(i + 1, n_steps - 1), 0, 0)),
                  pl.BlockSpec(memory_space=pl.ANY),
                  pl.BlockSpec((tt, D), lambda i: (i, 0)),
                  pl.BlockSpec((tt, LANES), lambda i: (i, 0)),
                  pl.BlockSpec((None, 1, D), lambda i: (i // bpb, 0, 0))],
        out_specs=pl.BlockSpec((tt, D), lambda i: (i, 0)),
        scratch_shapes=[pltpu.VMEM((2, TOP_K, tt, D // 2), jnp.int32), pltpu.SemaphoreType.DMA((2,))],
        compiler_params=pltpu.CompilerParams(dimension_semantics=("arbitrary",)),
        name="moe_combine",
    )(pos3, pos3, yb, x2d, top_w, gate)


def moe_layer(x2d, gain, sc, sh, gate, rows_per_batch, router_w, router_b, layer, wg, bg, wu, bu, wd, bd):
    N, D = x2d.shape
    RB = MOE_ROW_BLOCK
    E = N_EXPERTS
    xm_packed, top_e, top_w, rank, counts = norm_mod(x2d, gain, sc, sh, rows_per_batch,
                                                     router=(router_w, router_b))
    nk = N * TOP_K
    counts = counts[0, :E]
    padded = (counts + RB - 1) // RB * RB
    pad_end = jnp.cumsum(padded)
    pad_start = pad_end - padded
    e4 = top_e[:, :TOP_K]
    start_of = jnp.sum(jnp.where(e4[:, :, None] == jnp.arange(E)[None, None, :],
                                 pad_start[None, None, :], 0), axis=2)
    pos = (start_of + rank[:, :TOP_K]).astype(jnp.int32)
    cap = (-(-nk // RB)) * RB + E * RB
    n_blk = cap // RB
    blk_start = jnp.arange(n_blk, dtype=jnp.int32) * RB
    blk_e = jnp.minimum(jnp.sum(pad_end[None, :] <= blk_start[:, None], axis=1), E - 1).astype(jnp.int32)
    n_used = (pad_end[-1] // RB).astype(jnp.int32).reshape(1)
    zfrom = jnp.concatenate([pad_start + counts, pad_end[-1:]]).astype(jnp.int32)
    zto = jnp.concatenate([pad_end, jnp.full((1,), cap)]).astype(jnp.int32)
    xs = moe_dispatch(xm_packed, pos, zfrom, zto, cap)
    yb = moe_experts(xs, blk_e, n_used, layer, wg, bg, wu, bu, wd, bd)
    return moe_combine(yb, pos, top_w, x2d, gate, rows_per_batch)


def _pad_heads(w, n_heads, d, dp):
    lead = w.shape[:-1]
    w = w.reshape(lead + (n_heads, d))
    w = jnp.pad(w, [(0, 0)] * len(lead) + [(0, 0), (0, dp - d)])
    return w.reshape(lead + (n_heads * dp,))


def _even_in_weights(w_in, q_gain, k_gain):
    D = w_in.shape[0]
    cuts = np.cumsum(EVEN_IN_SPLITS)[:-1].tolist()
    q, kc, vc, ks, vs, kw, vw, gates, cq, kva = jnp.split(w_in, cuts, axis=1)
    half = MLA_ROPE_DIM // 2
    kr = kva[:, MLA_KV_RANK:]
    kr_perm = jnp.concatenate([kr[:, half:], kr[:, :half]], axis=1)
    G = NSA_KV_HEADS
    cols = [_pad_heads(q, NSA_HEADS, NSA_QK_DIM, HEAD_PAD),
            _pad_heads(kc, G, NSA_QK_DIM, HEAD_PAD), _pad_heads(ks, G, NSA_QK_DIM, HEAD_PAD),
            _pad_heads(kw, G, NSA_QK_DIM, HEAD_PAD), cq, kva[:, :MLA_KV_RANK], kr, kr_perm,
            vc, vs, vw, gates]
    w = jnp.concatenate(cols, axis=1)
    w = jnp.pad(w, ((0, 0), (0, EV_WIDTH - w.shape[1]))).astype(BF16)
    scale = NSA_QK_DIM ** -0.5
    pad_g = lambda g: jnp.pad(g, (0, HEAD_PAD - NSA_QK_DIM))
    gain = jnp.ones((EV_WIDTH,), F32)
    gain = gain.at[EV_Q:EV_Q + _NQ].set(jnp.tile(pad_g(q_gain * scale), NSA_HEADS))
    gain = gain.at[EV_KS:EV_KS + _NK].set(jnp.tile(pad_g(k_gain[1]), G))
    gain = gain.at[EV_KW:EV_KW + _NK].set(jnp.tile(pad_g(k_gain[2]), G))
    col = np.arange(EV_WIDTH)
    flag = ((col < EV_Q + _NQ) | ((col >= EV_KS) & (col < EV_KW + _NK))).astype(np.float32)
    return w, gain, jnp.asarray(flag)


def _even_mixer(xm, x2d, g_a, B, T, w_in, w_out, q_gain, k_gain, pe_k, pe_v, w_ck1, w_ck2, w_cv1, w_cv2,
                g_cq, g_ckv, w_uq, w_ukv, mq_gain, mk_gain):
    N, D = x2d.shape
    G, R = NSA_KV_HEADS, NSA_HEADS // NSA_KV_HEADS
    w_p, gain, flag = _even_in_weights(w_in, q_gain, k_gain)
    h = matmul(xm, w_p, head_norm=(HEAD_PAD, NSA_QK_DIM, gain, flag), name="in_proj_even")

    kc = compress(h, EV_KC, B, T, pe_k, w_ck1, w_ck2, k_gain[0], NSA_QK_DIM, HEAD_PAD)
    vc = compress(h, EV_VC, B, T, pe_v, w_cv1, w_cv2, None, NSA_V_DIM, NSA_V_DIM)

    o_cmp, bits = cmp_attention(h, kc, vc, B, T)
    HV = NSA_HEADS * NSA_V_DIM
    o_slc = flash_attention(h, EV_Q, h, EV_KS, h, EV_VS, B=B, T=T, G=G, R=R, out_cols=HV,
                            n_alibi_heads=NSA_HEADS, bits=bits, tk=512, name="nsa_slc_attn")
    o_win = flash_attention(h, EV_Q, h, EV_KW, h, EV_VW, B=B, T=T, G=G, R=R, out_cols=HV,
                            window=NSA_WINDOW, n_alibi_heads=NSA_HEADS, name="nsa_win_attn")
    o_a = nsa_combine(h, o_cmp, o_slc, o_win)

    H = MLA_HEADS
    dqk = MLA_NOPE_DIM + MLA_ROPE_DIM
    half = MLA_ROPE_DIM // 2
    wq = w_uq.reshape(MLA_Q_RANK, H, dqk)
    wq_rope = wq[:, :, MLA_NOPE_DIM:]
    wq_p = jnp.concatenate([wq, wq_rope[:, :, half:], wq_rope[:, :, :half]], axis=2)
    wq_p = wq_p.reshape(MLA_Q_RANK, H * HEAD_PAD).astype(BF16)
    q_raw = matmul(h, wq_p, a_col0=EV_CQ, a_pro="rms", a_gain=g_cq, name="mla_q_up")
    kv_raw = matmul(h, w_ukv.astype(BF16), a_col0=EV_CKV, a_pro="rms", a_gain=g_ckv, name="mla_kv_up")
    qa_lo, qa_hi, qb_hi = _rope_tables(mq_gain, T, dqk ** -0.5)
    ka_lo, ka_hi, kb_hi = _rope_tables(mk_gain, T, 1.0)
    q_m = mla_prep(q_raw, None, 0, qa_lo, qa_hi, qb_hi, T)
    k_m = mla_prep(kv_raw, h, EV_KR, ka_lo, ka_hi, kb_hi, T)
    hp = 4
    kv_w = MLA_NOPE_DIM + MLA_V_DIM
    o_b = flash_attention(q_m, 0, k_m, 0, kv_raw, 0, B=B, T=T, G=H, R=1, HP=hp, v_width=hp * kv_w,
                          v_off=MLA_NOPE_DIM, v_step=kv_w, out_cols=H * MLA_V_DIM, tk=512,
                          name="mla_attn")
    return matmul(o_a, w_out.astype(BF16), a2=o_b, resid=(x2d, g_a, T), out_dtype=F32, name="out_proj_even")


def _odd_mixer(xm, x2d, g_a, B, T, w_in, b_in, w_out, b_out, q_gain, k_gain, sinks):
    D = w_in.shape[0]
    G, hd = SWA_KV_HEADS, SWA_HEAD_DIM
    nq = SWA_HEADS * hd

    def dup(t):
        lead = t.shape[:-1]
        t = t.reshape(lead + (G, 1, hd))
        return jnp.broadcast_to(t, lead + (G, 2, hd)).reshape(lead + (G * 2 * hd,))

    kw = G * hd
    w_p = jnp.concatenate([w_in[:, :nq], dup(w_in[:, nq:nq + kw]), dup(w_in[:, nq + kw:])], axis=1)
    b_p = jnp.concatenate([b_in[:nq], dup(b_in[nq:nq + kw]), dup(b_in[nq + kw:])])
    gain = jnp.concatenate([jnp.tile(q_gain * hd ** -0.5, SWA_HEADS), jnp.tile(k_gain, 2 * G),
                            jnp.ones((2 * kw,), F32)])
    flag = jnp.concatenate([jnp.ones((nq + 2 * kw,), F32), jnp.zeros((2 * kw,), F32)])
    h = matmul(xm, w_p.astype(BF16), bias=b_p, head_norm=(hd, hd, gain, flag), name="in_proj_odd")
    o_c = swa_attention(h, sinks, B, T, nq, nq + 2 * kw)
    return matmul(o_c, w_out.astype(BF16), bias=b_out, resid=(x2d, g_a, T), out_dtype=F32,
                  name="out_proj_odd")


def kernel(x, c, w_mod, mod_table, norm_attn, norm_ffn, w_in_even, w_out_even, nsa_q_gain, nsa_k_gain,
           nsa_pe_k, nsa_pe_v, nsa_w_ck1, nsa_w_ck2, nsa_w_cv1, nsa_w_cv2, mla_g_cq, mla_g_ckv, mla_w_uq,
           mla_w_ukv, mla_q_gain, mla_k_gain, w_in_odd, b_in_odd, w_out_odd, b_out_odd, swa_q_gain,
           swa_k_gain, swa_sinks, router_w, router_b, moe_w_gate, moe_b_gate, moe_w_up, moe_b_up,
           moe_w_down, moe_b_down):
    B, T, D = x.shape
    N = B * T
    depth = mod_table.shape[0]
    c_pad = jnp.pad(c, ((0, 8 - B % 8 if B % 8 else 0), (0, 0)))
    cond = matmul(c_pad, w_mod, a_pro="silu", out_dtype=F32, tn=1024, tk=1024, name="adaln_proj")[:B]
    x2d = x.reshape(N, D)
    for layer in range(depth):
        mod = (cond + mod_table[layer]).reshape(B, 6, 1, D)
        sh_a, sc_a, g_a, sh_f, sc_f, g_f = (mod[:, j] for j in range(6))
        xm = norm_mod(x2d, norm_attn[layer], sc_a, sh_a, T)
        i = layer // 2
        if layer % 2 == 0:
            x2d = _even_mixer(xm, x2d, g_a, B, T, w_in_even[i], w_out_even[i], nsa_q_gain[i], nsa_k_gain[i],
                              nsa_pe_k[i], nsa_pe_v[i], nsa_w_ck1[i], nsa_w_ck2[i], nsa_w_cv1[i],
                              nsa_w_cv2[i], mla_g_cq[i], mla_g_ckv[i], mla_w_uq[i], mla_w_ukv[i],
                              mla_q_gain[i], mla_k_gain[i])
        else:
            x2d = _odd_mixer(xm, x2d, g_a, B, T, w_in_odd[i], b_in_odd[i], w_out_odd[i], b_out_odd[i],
                             swa_q_gain[i], swa_k_gain[i], swa_sinks[i])
        x2d = moe_layer(x2d, norm_ffn[layer], sc_f, sh_f, g_f, T, router_w[layer], router_b[layer],
                        layer, moe_w_gate, moe_b_gate, moe_w_up, moe_b_up, moe_w_down, moe_b_down)
    return x2d.reshape(B, T, D)
```

```python
import functools
import math

import numpy as np
import jax
import jax.numpy as jnp
from jax import lax
from jax.experimental import pallas as pl
from jax.experimental.pallas import tpu as pltpu

BF16 = jnp.bfloat16
F32 = jnp.float32

NSA_HEADS = 16
NSA_KV_HEADS = 4
NSA_QK_DIM = 192
NSA_V_DIM = 128
CMP_BLOCK = 32
CMP_STRIDE = 16
SLC_BLOCK = 64
SLC_TOPN = 8
NSA_WINDOW = 512
MLA_HEADS = 16
MLA_Q_RANK = 1024
MLA_KV_RANK = 512
MLA_NOPE_DIM = 128
MLA_ROPE_DIM = 64
MLA_V_DIM = 128
ROPE_THETA = 10000.0
SWA_HEADS = 64
SWA_KV_HEADS = 8
SWA_HEAD_DIM = 64
SWA_WINDOW = 128
N_EXPERTS = 32
TOP_K = 4
SWIGLU_ALPHA = 1.702
SWIGLU_LIMIT = 7.0
MOE_ROW_BLOCK = 256
EPS = 1e-6

LANES = 128
HEAD_PAD = 256
NEG = -1e30
M_FLOOR = -1e29

EVEN_IN_SPLITS = (NSA_HEADS * NSA_QK_DIM,
                  NSA_KV_HEADS * NSA_QK_DIM, NSA_KV_HEADS * NSA_V_DIM,
                  NSA_KV_HEADS * NSA_QK_DIM, NSA_KV_HEADS * NSA_V_DIM,
                  NSA_KV_HEADS * NSA_QK_DIM, NSA_KV_HEADS * NSA_V_DIM,
                  3 * NSA_HEADS, MLA_Q_RANK, MLA_KV_RANK + MLA_ROPE_DIM)

_NQ = NSA_HEADS * HEAD_PAD
_NK = NSA_KV_HEADS * HEAD_PAD
_NV = NSA_KV_HEADS * NSA_V_DIM
EV_Q = 0
EV_KC = EV_Q + _NQ
EV_KS = EV_KC + _NK
EV_KW = EV_KS + _NK
EV_CQ = EV_KW + _NK
EV_CKV = EV_CQ + MLA_Q_RANK
EV_KR = EV_CKV + MLA_KV_RANK
EV_VC = EV_KR + 2 * MLA_ROPE_DIM
EV_VS = EV_VC + _NV
EV_VW = EV_VS + _NV
EV_GATE = EV_VW + _NV
EV_END = EV_GATE + LANES
EV_WIDTH = -(-EV_END // 512) * 512


def _tile(dim, want):
    t = min(dim, want)
    while dim % t:
        t //= 2
    return t


def _alibi_np(n):
    return np.exp2(-8.0 * np.arange(1, n + 1, dtype=np.float32) / n).astype(np.float32)


def _alibi(n):
    return jnp.asarray(_alibi_np(n))


def _head_norm(res, hd, real_d):
    tm, tn = res.shape
    x2 = res * res
    pieces = []
    if hd == HEAD_PAD:
        for s in range(tn // hd):
            ss = jnp.sum(x2[:, s * hd:s * hd + LANES] + x2[:, s * hd + LANES:(s + 1) * hd],
                         axis=1, keepdims=True)
            r = lax.rsqrt(ss * (1.0 / real_d) + EPS)
            pieces.append(jnp.broadcast_to(r, (tm, hd)))
    else:
        lo = lax.broadcasted_iota(jnp.int32, (tm, LANES), 1) < hd
        for s in range(tn // LANES):
            c = x2[:, s * LANES:(s + 1) * LANES]
            ss_lo = jnp.sum(jnp.where(lo, c, 0.0), axis=1, keepdims=True)
            ss_hi = jnp.sum(jnp.where(lo, 0.0, c), axis=1, keepdims=True)
            r_lo = lax.rsqrt(ss_lo * (1.0 / real_d) + EPS)
            r_hi = lax.rsqrt(ss_hi * (1.0 / real_d) + EPS)
            pieces.append(jnp.where(lo, r_lo, r_hi))
    return jnp.concatenate(pieces, axis=1) if len(pieces) > 1 else pieces[0]


def _mm_kernel(*refs, nk, a_pro, has_a2, has_bias, hd, real_d, has_resid):
    it = iter(refs)
    a_ref = next(it)
    a2_ref = next(it) if has_a2 else None
    w_ref = next(it)
    again_ref = next(it) if a_pro == "rms" else None
    b_ref = next(it) if has_bias else None
    gain_ref = flag_ref = None
    if hd:
        gain_ref = next(it)
        flag_ref = next(it)
    x_ref = gate_ref = None
    if has_resid:
        x_ref = next(it)
        gate_ref = next(it)
    o_ref = next(it)
    acc_ref = next(it) if nk > 1 else None

    a = a_ref[...]
    if a_pro == "silu":
        af = a.astype(F32)
        a = af * jax.nn.sigmoid(af)
    elif a_pro == "rms":
        af = a.astype(F32)
        r = lax.rsqrt(jnp.mean(af * af, axis=1, keepdims=True) + EPS)
        a = af * r * again_ref[...]
    if has_a2:
        k1 = a.shape[1]
        part = (jnp.dot(a.astype(BF16), w_ref[:k1, :].astype(BF16), preferred_element_type=F32)
                + jnp.dot(a2_ref[...].astype(BF16), w_ref[k1:, :].astype(BF16), preferred_element_type=F32))
    else:
        part = jnp.dot(a.astype(BF16), w_ref[...].astype(BF16), preferred_element_type=F32)

    def finish(res):
        if has_bias:
            res = res + b_ref[...]
        if hd:
            r = _head_norm(res, hd, real_d)
            res = res * jnp.where(flag_ref[...] > 0.0, r, 1.0) * gain_ref[...]
        if has_resid:
            res = x_ref[...] + gate_ref[...] * res
        o_ref[...] = res.astype(o_ref.dtype)

    if nk == 1:
        finish(part)
    else:
        k = pl.program_id(2)

        @pl.when(k == 0)
        def _():
            acc_ref[...] = part

        @pl.when(k > 0)
        def _():
            acc_ref[...] += part

        @pl.when(k == nk - 1)
        def _():
            finish(acc_ref[...])


def matmul(a, w, *, a2=None, a_col0=0, a_pro=None, a_gain=None, bias=None,
           head_norm=None, resid=None, out_dtype=BF16, tm=1024, tn=512, tk=4096, name="mm"):
    M = a.shape[0]
    K, N = w.shape
    tm, tn, tk = _tile(M, tm), _tile(N, tn), _tile(K, tk)
    if a_pro == "rms" or a2 is not None:
        tk = K
    assert a_col0 % tk == 0 and M % tm == 0 and N % tn == 0 and K % tk == 0
    nk = K // tk
    koff = a_col0 // tk
    hd = head_norm[0] if head_norm else 0
    real_d = head_norm[1] if head_norm else 0
    if hd:
        assert tn % max(hd, LANES) == 0

    if a2 is None:
        ins = [a, w]
        specs = [pl.BlockSpec((tm, tk), lambda i, j, k: (i, koff + k))]
    else:
        assert a_col0 == 0 and a_pro is None and a.shape[1] + a2.shape[1] == K
        ins = [a, a2, w]
        specs = [pl.BlockSpec((tm, a.shape[1]), lambda i, j, k: (i, 0)),
                 pl.BlockSpec((tm, a2.shape[1]), lambda i, j, k: (i, 0))]
    specs.append(pl.BlockSpec((tk, tn), lambda i, j, k: (k, j)))
    if a_pro == "rms":
        ins.append(a_gain.reshape(1, K).astype(F32))
        specs.append(pl.BlockSpec((1, tk), lambda i, j, k: (0, k)))
    if bias is not None:
        ins.append(bias.reshape(1, N).astype(F32))
        specs.append(pl.BlockSpec((1, tn), lambda i, j, k: (0, j)))
    if hd:
        ins += [head_norm[2].reshape(1, N).astype(F32), head_norm[3].reshape(1, N).astype(F32)]
        specs += [pl.BlockSpec((1, tn), lambda i, j, k: (0, j))] * 2
    if resid is not None:
        x, gate, rows_per_batch = resid
        assert rows_per_batch % tm == 0
        bpb = rows_per_batch // tm
        ins += [x, gate]
        specs += [pl.BlockSpec((tm, tn), lambda i, j, k: (i, j)),
                  pl.BlockSpec((None, 1, tn), lambda i, j, k: (i // bpb, 0, j))]
    kern = functools.partial(_mm_kernel, nk=nk, a_pro=a_pro, has_a2=a2 is not None, has_bias=bias is not None,
                             hd=hd, real_d=real_d, has_resid=resid is not None)
    return pl.pallas_call(
        kern,
        out_shape=jax.ShapeDtypeStruct((M, N), out_dtype),
        grid=(M // tm, N // tn, nk),
        in_specs=specs,
        out_specs=pl.BlockSpec((tm, tn), lambda i, j, k: (i, j)),
        scratch_shapes=[pltpu.VMEM((tm, tn), F32)] if nk > 1 else [],
        compiler_params=pltpu.CompilerParams(
            dimension_semantics=("parallel", "parallel", "arbitrary")),
        name=name,
    )(*ins)


def _split_bf16(v):
    hi = v.astype(BF16)
    lo = (v - hi.astype(F32)).astype(BF16)
    return hi, lo


def _pack_bf16_pairs(v):
    half = v.shape[1] // 2
    vb = v.astype(BF16).astype(F32)
    lo = lax.shift_right_logical(lax.bitcast_convert_type(vb[:, :half], jnp.int32), 16)
    hi = lax.bitcast_convert_type(vb[:, half:], jnp.int32) & jnp.int32(-65536)
    return hi | lo


def _unpack_bf16_pairs(u):
    lo = lax.bitcast_convert_type(lax.shift_left(u, 16), F32)
    hi = lax.bitcast_convert_type(u & jnp.int32(-65536), F32)
    return jnp.concatenate([lo, hi], axis=1).astype(BF16)


def _norm_mod_kernel(x_ref, g_ref, sc_ref, sh_ref, *rest, route):
    x = x_ref[...]
    r = lax.rsqrt(jnp.mean(x * x, axis=1, keepdims=True) + EPS)
    xm = (x * r * g_ref[...]) * (1.0 + sc_ref[...]) + sh_ref[...]
    if not route:
        (o_ref,) = rest
        o_ref[...] = xm.astype(o_ref.dtype)
        return
    rw_ref, rb_ref, o_ref, e_ref, p_ref, rank_ref, cnt_ref, run_ref = rest
    o_ref[...] = _pack_bf16_pairs(xm)
    a_hi, a_lo = _split_bf16(xm)
    w = rw_ref[...]
    w_hi, w_lo = _split_bf16(w)
    logits = (jnp.dot(a_hi, w_hi, preferred_element_type=F32)
              + jnp.dot(a_hi, w_lo, preferred_element_type=F32)
              + jnp.dot(a_lo, w_hi, preferred_element_type=F32)) + rb_ref[...]
    tt = logits.shape[0]
    lane = lax.broadcasted_iota(jnp.int32, (tt, LANES), 1)
    lane_f = lane.astype(F32)
    work = jnp.where(lane < N_EXPERTS, logits, -jnp.inf)
    e_out = jnp.zeros((tt, LANES), F32)
    v_out = jnp.full((tt, LANES), -jnp.inf, F32)
    hits = []
    for kk in range(TOP_K):
        m = jnp.max(work, axis=1, keepdims=True)
        idx = jnp.min(jnp.where(work == m, lane_f, float(LANES)), axis=1, keepdims=True)
        e_out = jnp.where(lane == kk, idx, e_out)
        v_out = jnp.where(lane == kk, m, v_out)
        hits.append(lane_f == idx)
        work = jnp.where(hits[-1], -jnp.inf, work)
    e_out = e_out.astype(jnp.int32)
    vmax = jnp.max(v_out, axis=1, keepdims=True)
    pe = jnp.exp(v_out - vmax)
    p_ref[...] = pe / jnp.sum(pe, axis=1, keepdims=True)
    e_ref[...] = e_out

    @pl.when(pl.program_id(0) == 0)
    def _():
        run_ref[...] = jnp.zeros_like(run_ref)

    onehot = jnp.zeros((tt, LANES), F32)
    for hit in hits:
        onehot = jnp.where(hit, 1.0, onehot)
    earlier = (lax.broadcasted_iota(jnp.int32, (tt, tt), 1)
               < lax.broadcasted_iota(jnp.int32, (tt, tt), 0))
    prefix = jnp.dot(jnp.where(earlier, 1.0, 0.0).astype(BF16), onehot.astype(BF16),
                     preferred_element_type=F32)
    base = run_ref[...] + prefix
    rank = jnp.zeros((tt, LANES), F32)
    for kk, hit in enumerate(hits):
        rk = jnp.sum(jnp.where(hit, base, 0.0), axis=1, keepdims=True)
        rank = jnp.where(lane == kk, rk, rank)
    rank_ref[...] = rank.astype(jnp.int32)
    total = run_ref[...] + jnp.sum(onehot, axis=0, keepdims=True)
    run_ref[...] = total
    cnt_ref[...] = total.astype(jnp.int32)


def norm_mod(x2d, gain, sc, sh, rows_per_batch, *, out_dtype=BF16, router=None, tt=256):
    N, D = x2d.shape
    tt = _tile(rows_per_batch, tt)
    bpb = rows_per_batch // tt
    ins = [x2d, gain.reshape(1, D), sc, sh]
    specs = [pl.BlockSpec((tt, D), lambda i: (i, 0)),
             pl.BlockSpec((1, D), lambda i: (0, 0)),
             pl.BlockSpec((None, 1, D), lambda i: (i // bpb, 0, 0)),
             pl.BlockSpec((None, 1, D), lambda i: (i // bpb, 0, 0))]
    if router is None:
        out_shape = [jax.ShapeDtypeStruct((N, D), out_dtype)]
        out_specs = [pl.BlockSpec((tt, D), lambda i: (i, 0))]
        scratch = []
    else:
        rw, rb = router
        E = rw.shape[1]
        rw_p = jnp.pad(rw, ((0, 0), (0, LANES - E)))
        rb_p = jnp.pad(rb, (0, LANES - E)).reshape(1, LANES)
        ins += [rw_p, rb_p]
        specs += [pl.BlockSpec((D, LANES), lambda i: (0, 0)),
                  pl.BlockSpec((1, LANES), lambda i: (0, 0))]
        out_shape = [jax.ShapeDtypeStruct((N, D // 2), jnp.int32),
                     jax.ShapeDtypeStruct((N, LANES), jnp.int32),
                     jax.ShapeDtypeStruct((N, LANES), F32),
                     jax.ShapeDtypeStruct((N, LANES), jnp.int32),
                     jax.ShapeDtypeStruct((1, LANES), jnp.int32)]
        row = pl.BlockSpec((tt, LANES), lambda i: (i, 0))
        out_specs = [pl.BlockSpec((tt, D // 2), lambda i: (i, 0)), row, row, row,
                     pl.BlockSpec((1, LANES), lambda i: (0, 0))]
        scratch = [pltpu.VMEM((1, LANES), F32)]
    res = pl.pallas_call(
        functools.partial(_norm_mod_kernel, route=router is not None),
        out_shape=out_shape, grid=(N // tt,), in_specs=specs, out_specs=out_specs,
        scratch_shapes=scratch,
        compiler_params=pltpu.CompilerParams(
            dimension_semantics=("parallel",) if router is None else ("arbitrary",)),
        name="norm_mod_route" if router is not None else "norm_mod",
    )(*ins)
    return res if router is not None else res[0]


def _compress_kernel(h_ref, pe_ref, w1_ref, w2_ref, g_ref, o_ref, xf_ref, *, real_d, norm, nh):
    n_tiles = xf_ref.shape[0]
    for j in range(n_tiles):
        xf_ref[j] = h_ref[:, j * LANES:(j + 1) * LANES].astype(F32)
    half = CMP_BLOCK // 2
    top = bot = None
    for l in range(half):
        parts = [xf_ref[j, pl.ds(l, nh, stride=CMP_STRIDE), :] for j in range(n_tiles)]
        xl = jnp.concatenate(parts, axis=1) if n_tiles > 1 else parts[0]
        t = jnp.dot((xl + pe_ref[l:l + 1, :]).astype(BF16), w1_ref[l], preferred_element_type=F32)
        b = jnp.dot((xl + pe_ref[half + l:half + l + 1, :]).astype(BF16), w1_ref[half + l],
                    preferred_element_type=F32)
        top = t if top is None else top + t
        bot = b if bot is None else bot + b
    hid = top + pltpu.roll(bot, nh - 1, 0)
    hid = hid * jax.nn.sigmoid(hid)
    y = jnp.dot(hid.astype(BF16), w2_ref[...], preferred_element_type=F32)
    if norm:
        r = lax.rsqrt(jnp.sum(y * y, axis=1, keepdims=True) * (1.0 / real_d) + EPS)
        y = y * r * g_ref[...]
    o_ref[...] = y.astype(o_ref.dtype)


def compress(h, col0, B, T, pe, w1, w2, gain, d, dp):
    G = NSA_KV_HEADS
    nh = T // CMP_STRIDE
    cb = col0 // dp
    pe_p = jnp.pad(pe, ((0, 0), (0, dp - d)))
    w1_p = jnp.pad(w1.reshape(CMP_BLOCK, d, d), ((0, 0), (0, dp - d), (0, dp - d))).astype(BF16)
    w2_p = jnp.pad(w2, ((0, dp - d), (0, dp - d))).astype(BF16)
    g_p = (jnp.ones((dp,), F32) if gain is None else jnp.pad(gain, (0, dp - d))).reshape(1, dp)
    full = lambda shape: pl.BlockSpec(shape, lambda b, g: (0,) * len(shape))
    return pl.pallas_call(
        functools.partial(_compress_kernel, real_d=d, norm=gain is not None, nh=nh),
        out_shape=jax.ShapeDtypeStruct((B * G, nh, dp), BF16),
        grid=(B, G),
        in_specs=[pl.BlockSpec((T, dp), lambda b, g: (b, cb + g)),
                  full((CMP_BLOCK, dp)), full((CMP_BLOCK, dp, dp)), full((dp, dp)), full((1, dp))],
        out_specs=pl.BlockSpec((None, nh, dp), lambda b, g: (b * G + g, 0, 0)),
        scratch_shapes=[pltpu.VMEM((dp // LANES, T, LANES), F32)],
        compiler_params=pltpu.CompilerParams(dimension_semantics=("parallel", "parallel")),
        name="nsa_compress",
    )(h, pe_p, w1_p, w2_p, g_p)


def _stack_heads(q_ref, n, w):
    return jnp.concatenate([q_ref[:, r * w:(r + 1) * w] for r in range(n)], axis=0)


def _row_scalars(vals, tq):
    rows = len(vals) * tq
    rid = lax.broadcasted_iota(jnp.int32, (rows, 1), 0) // tq
    col = jnp.full((rows, 1), vals[-1], F32)
    for r in range(len(vals) - 2, -1, -1):
        col = jnp.where(rid == r, vals[r], col)
    return col


def _nt_dot(a, b):
    return lax.dot_general(a, b, (((1,), (1,)), ((), ())), preferred_element_type=F32)


def _cmp_attn_kernel(slope_ref, q_ref, kc_ref, vc_ref, ovl_ref, o_ref, bits_ref, *, tq, R, n_slc):
    g = pl.program_id(1)
    t0 = pl.program_id(2) * tq
    rows = R * tq
    q = _stack_heads(q_ref, R, HEAD_PAD)
    s = _nt_dot(q, kc_ref[...])
    ncol = s.shape[1]
    row_t = t0 + lax.broadcasted_iota(jnp.int32, (rows, ncol), 0) % tq
    n_id = lax.broadcasted_iota(jnp.int32, (rows, ncol), 1)
    dist = row_t - (n_id * CMP_STRIDE + (CMP_BLOCK - 1))
    slope = _row_scalars([slope_ref[g * R + r] for r in range(R)], tq)
    valid = dist >= 0
    s = jnp.where(valid, s - slope * dist.astype(F32), NEG)
    m = jnp.max(s, axis=1, keepdims=True)
    p = jnp.where(valid, jnp.exp(s - m), 0.0)
    p = p / jnp.maximum(jnp.sum(p, axis=1, keepdims=True), 1e-30)
    o = jnp.dot(p.astype(BF16), vc_ref[...], preferred_element_type=F32)
    for r in range(R):
        o_ref[:, r * NSA_V_DIM:(r + 1) * NSA_V_DIM] = o[r * tq:(r + 1) * tq].astype(o_ref.dtype)

    psum = p[0:tq]
    for r in range(1, R):
        psum = psum + p[r * tq:(r + 1) * tq]
    p_hi, p_lo = _split_bf16(psum)
    imp = (jnp.dot(p_hi, ovl_ref[...], preferred_element_type=F32)
           + jnp.dot(p_lo, ovl_ref[...], preferred_element_type=F32))
    lane = lax.broadcasted_iota(jnp.int32, (tq, LANES), 1)
    cur = (t0 + lax.broadcasted_iota(jnp.int32, (tq, LANES), 0)) // SLC_BLOCK
    forced = (lane == 0) | (lane == cur) | (lane == cur - 1)
    work = jnp.where(forced, jnp.inf, jnp.where(lane > cur, -jnp.inf, imp))
    removed = -3.0e38
    work = jnp.where(lane < n_slc, jnp.where(work == -jnp.inf, -2.0e38, work), removed)
    sel = jnp.zeros((tq, LANES), jnp.bool_)
    lane_f = lane.astype(F32)
    for _ in range(min(SLC_TOPN, n_slc)):
        mx = jnp.max(work, axis=1, keepdims=True)
        idx = jnp.min(jnp.where(work == mx, lane_f, float(LANES)), axis=1, keepdims=True)
        hit = lane_f == idx
        sel = sel | hit
        work = jnp.where(hit, removed, work)
    half = 16
    w_lo = jnp.where(sel & (lane < half), jnp.left_shift(1, jnp.minimum(lane, half - 1)), 0)
    w_hi = jnp.where(sel & (lane >= half), jnp.left_shift(1, jnp.clip(lane - half, 0, half - 1)), 0)
    b_lo = jnp.sum(w_lo.astype(F32), axis=1, keepdims=True).astype(jnp.int32)
    b_hi = jnp.sum(w_hi.astype(F32), axis=1, keepdims=True).astype(jnp.int32)
    bits = b_lo | jnp.left_shift(b_hi, half)
    bits_ref[...] = jnp.broadcast_to(bits, (tq, LANES))


def cmp_attention(h, kc, vc, B, T, tq=256):
    G, R = NSA_KV_HEADS, NSA_HEADS // NSA_KV_HEADS
    tq = _tile(T, tq)
    nq = T // tq
    n_slc = T // SLC_BLOCK
    n_cmp = kc.shape[1]
    assert n_slc <= 32 and n_cmp <= LANES and n_cmp % 8 == 0
    cmp_start = np.arange(n_cmp) * CMP_STRIDE
    slc_start = np.arange(LANES) * SLC_BLOCK
    ovl = ((cmp_start[:, None] < slc_start[None, :] + SLC_BLOCK)
           & (cmp_start[:, None] + CMP_BLOCK > slc_start[None, :])
           & (np.arange(LANES)[None, :] < n_slc) & (np.arange(n_cmp)[:, None] < n_cmp - 1))
    ovl = jnp.asarray(ovl, BF16)
    qw = R * HEAD_PAD
    return pl.pallas_call(
        functools.partial(_cmp_attn_kernel, tq=tq, R=R, n_slc=n_slc),
        out_shape=[jax.ShapeDtypeStruct((B * T, NSA_HEADS * NSA_V_DIM), BF16),
                   jax.ShapeDtypeStruct((B, G, T, LANES), jnp.int32)],
        grid=(B, G, nq),
        in_specs=[pl.BlockSpec(memory_space=pltpu.SMEM),
                  pl.BlockSpec((tq, qw), lambda b, g, i: (b * nq + i, EV_Q // qw + g)),
                  pl.BlockSpec((None, n_cmp, HEAD_PAD), lambda b, g, i: (b * G + g, 0, 0)),
                  pl.BlockSpec((None, n_cmp, NSA_V_DIM), lambda b, g, i: (b * G + g, 0, 0)),
                  pl.BlockSpec((n_cmp, LANES), lambda b, g, i: (0, 0))],
        out_specs=[pl.BlockSpec((tq, R * NSA_V_DIM), lambda b, g, i: (b * nq + i, g)),
                   pl.BlockSpec((None, None, tq, LANES), lambda b, g, i: (b, g, i, 0))],
        compiler_params=pltpu.CompilerParams(dimension_semantics=("parallel", "parallel", "parallel")),
        name="nsa_cmp_attn",
    )(_alibi(NSA_HEADS), h, kc, vc, ovl)


ALIBI_LANE0 = 64
SEL_LANE0 = 70


def _key_table(T):
    s = np.arange(T)
    tab = np.zeros((T, LANES), np.float32)
    tab[:, ALIBI_LANE0:ALIBI_LANE0 + 3] = (s // 256 * 256)[:, None]
    tab[:, ALIBI_LANE0 + 3:ALIBI_LANE0 + 6] = (s % 256)[:, None]
    tab[s, SEL_LANE0 + s // SLC_BLOCK] = 1.0
    return jnp.asarray(tab, BF16)


def _slope_parts(n_heads):
    s = _alibi_np(n_heads)
    h1 = s.astype(BF16).astype(np.float32)
    h2 = (s - h1).astype(BF16).astype(np.float32)
    h3 = (s - h1 - h2).astype(BF16).astype(np.float32)
    return jnp.asarray(np.stack([h1, h2, h3], axis=1).reshape(-1))


def _flash_kernel(*refs, tq, tk, R, HP, v_off, v_step, window, alibi, use_bits):
    it = iter(refs)
    slope_ref = next(it) if alibi else None
    q_ref = next(it)
    k_ref = next(it)
    v_ref = next(it)
    ktab_ref = next(it) if alibi else None
    bits_ref = next(it) if use_bits else None
    o_ref = next(it)
    m_ref = next(it)
    acc_ref = next(it)

    g = pl.program_id(1)
    t0 = pl.program_id(2) * tq
    rows = R * tq
    vd = NSA_V_DIM
    lane = lax.broadcasted_iota(jnp.int32, (tq, LANES), 1)
    sel_add = None
    if use_bits:
        j = jnp.clip(lane - SEL_LANE0, 0, 31)
        in_sel = (lane >= SEL_LANE0) & (lane < SEL_LANE0 + 32)
        picked = (jnp.right_shift(bits_ref[...], j) & 1) != 0
        sel_add = jnp.where(in_sel & jnp.logical_not(picked), NEG, 0.0)

    qs = []
    for hp in range(HP):
        pieces = []
        for r in range(R):
            c0 = (hp * R + r) * HEAD_PAD
            q_lo = q_ref[:, c0:c0 + LANES]
            q_hi = q_ref[:, c0 + LANES:c0 + HEAD_PAD]
            if alibi:
                hd = (g * HP + hp) * R + r
                add = jnp.zeros((tq, LANES), F32) if sel_add is None else sel_add
                for part in range(3):
                    sp = slope_ref[3 * hd + part]
                    add = jnp.where((lane == ALIBI_LANE0 + part) | (lane == ALIBI_LANE0 + 3 + part), sp, add)
                q_hi = (q_hi.astype(F32) + add).astype(BF16)
            pieces.append(jnp.concatenate([q_lo, q_hi], axis=1))
        qs.append(jnp.concatenate(pieces, axis=0) if R > 1 else pieces[0])

    rel0 = (lax.broadcasted_iota(jnp.int32, (rows, tk), 0) % tq
            - lax.broadcasted_iota(jnp.int32, (rows, tk), 1))
    ones = jnp.ones((tk, LANES), BF16)
    m_ref[...] = jnp.full(m_ref.shape, M_FLOOR, F32)
    acc_ref[...] = jnp.zeros(acc_ref.shape, F32)

    def step(c, masked):
        s0 = pl.multiple_of(c * tk, tk)
        off = t0 - s0
        if masked:
            valid = rel0 >= -off
            if window is not None:
                valid = valid & (rel0 < window - off)
        for hp in range(HP):
            kc = k_ref[pl.ds(s0, tk), hp * HEAD_PAD:(hp + 1) * HEAD_PAD]
            if alibi:
                kc = jnp.concatenate([kc[:, :LANES], kc[:, LANES:] + ktab_ref[pl.ds(s0, tk), :]], axis=1)
            vc = v_ref[pl.ds(s0, tk), v_off + hp * v_step:v_off + hp * v_step + vd]
            s = _nt_dot(qs[hp], kc)
            if masked:
                s = jnp.where(valid, s, NEG)
            tiles = [s[:, j * LANES:(j + 1) * LANES] for j in range(tk // LANES)]
            mx = tiles[0]
            for t in tiles[1:]:
                mx = jnp.maximum(mx, t)
            m_old = m_ref[hp]
            m_new = jnp.maximum(m_old, jnp.max(mx, axis=1, keepdims=True))
            alpha = jnp.exp(m_old - m_new)
            p = jnp.concatenate([jnp.exp(t - m_new).astype(BF16) for t in tiles], axis=1)
            pv = jnp.dot(p, jnp.concatenate([vc, ones], axis=1), preferred_element_type=F32)
            acc_ref[hp] = jnp.concatenate([alpha, alpha], axis=1) * acc_ref[hp] + pv
            m_ref[hp] = m_new

    def body(c, carry):
        s0 = c * tk
        full = s0 + tk - 1 <= t0
        if window is not None:
            full = full & (s0 >= t0 + tq - window)
        lax.cond(full, lambda: step(c, False), lambda: step(c, True))
        return carry

    c_hi = (t0 + tq - 1) // tk
    c_lo = 0 if window is None else jnp.maximum((t0 - window + 1) // tk, 0)
    lax.fori_loop(c_lo, c_hi + 1, body, 0)
    for hp in range(HP):
        acc = acc_ref[hp]
        o = acc[:, :vd] / jnp.maximum(acc[:, vd:], 1e-30)
        for r in range(R):
            c0 = (hp * R + r) * vd
            o_ref[:, c0:c0 + vd] = o[r * tq:(r + 1) * tq].astype(o_ref.dtype)


def flash_attention(q_arr, q_col0, k_arr, k_col0, v_arr, v_col0, *, B, T, G, R, HP=1, v_width=NSA_V_DIM,
                    v_off=0, v_step=0, out_cols, out_col0=0, window=None, n_alibi_heads=0, bits=None,
                    tq=256, tk=256, name="flash"):
    vd = NSA_V_DIM
    tq = _tile(T, tq)
    tk = _tile(T, tk)
    nq = T // tq
    qw, kw, ow = HP * R * HEAD_PAD, HP * HEAD_PAD, HP * R * vd
    assert q_col0 % qw == 0 and k_col0 % kw == 0 and v_col0 % v_width == 0 and out_col0 % ow == 0
    assert bits is None or n_alibi_heads
    qb, kb, vb, ob = q_col0 // qw, k_col0 // kw, v_col0 // v_width, out_col0 // ow
    alibi = n_alibi_heads > 0
    ins, specs = [], []
    if alibi:
        ins.append(_slope_parts(n_alibi_heads))
        specs.append(pl.BlockSpec(memory_space=pltpu.SMEM))
    ins += [q_arr, k_arr, v_arr]
    specs += [pl.BlockSpec((tq, qw), lambda b, g, i: (b * nq + i, qb + g)),
              pl.BlockSpec((T, kw), lambda b, g, i: (b, kb + g)),
              pl.BlockSpec((T, v_width), lambda b, g, i: (b, vb + g))]
    if alibi:
        ins.append(_key_table(T))
        specs.append(pl.BlockSpec((T, LANES), lambda b, g, i: (0, 0)))
    if bits is not None:
        ins.append(bits)
        specs.append(pl.BlockSpec((None, None, tq, LANES), lambda b, g, i: (b, g, i, 0)))
    rows = R * tq
    return pl.pallas_call(
        functools.partial(_flash_kernel, tq=tq, tk=tk, R=R, HP=HP, v_off=v_off, v_step=v_step,
                          window=window, alibi=alibi, use_bits=bits is not None),
        out_shape=jax.ShapeDtypeStruct((B * T, out_cols), BF16),
        grid=(B, G // HP, nq),
        in_specs=specs,
        out_specs=pl.BlockSpec((tq, ow), lambda b, g, i: (b * nq + i, ob + g)),
        scratch_shapes=[pltpu.VMEM((HP, rows, LANES), F32), pltpu.VMEM((HP, rows, 2 * vd), F32)],
        compiler_params=pltpu.CompilerParams(dimension_semantics=("parallel", "parallel", "parallel")),
        name=name,
    )(*ins)


def _nsa_combine_kernel(gate_ref, oc_ref, os_ref, ow_ref, o_ref):
    gates = jax.nn.sigmoid(gate_ref[...].astype(F32))
    for h in range(NSA_HEADS):
        sl = slice(h * NSA_V_DIM, (h + 1) * NSA_V_DIM)
        acc = None
        for br, ref in enumerate((oc_ref, os_ref, ow_ref)):
            gcol = gates[:, 3 * h + br:3 * h + br + 1]
            term = gcol * ref[:, sl].astype(F32)
            acc = term if acc is None else acc + term
        o_ref[:, sl] = acc.astype(o_ref.dtype)


def nsa_combine(h, o_cmp, o_slc, o_win, tt=256):
    N, W = o_cmp.shape
    tt = _tile(N, tt)
    gb = EV_GATE // LANES
    blk = pl.BlockSpec((tt, W), lambda i: (i, 0))
    return pl.pallas_call(
        _nsa_combine_kernel,
        out_shape=jax.ShapeDtypeStruct((N, W), BF16),
        grid=(N // tt,),
        in_specs=[pl.BlockSpec((tt, LANES), lambda i: (i, gb)), blk, blk, blk],
        out_specs=blk,
        compiler_params=pltpu.CompilerParams(dimension_semantics=("parallel",)),
        name="nsa_combine",
    )(h, o_cmp, o_slc, o_win)


def _mla_prep_kernel(*refs, shared_rope):
    if shared_rope:
        x_ref, hi_ref, alo_ref, ahi_ref, bhi_ref, o_ref = refs
    else:
        x_ref, alo_ref, ahi_ref, bhi_ref, o_ref = refs
    tt = x_ref.shape[0]
    is_rope = lax.broadcasted_iota(jnp.int32, (tt, LANES), 1) < MLA_ROPE_DIM
    a_hi, b_hi, a_lo = ahi_ref[...], bhi_ref[...], alo_ref[...]

    def rope_part(x_hi):
        ss = jnp.sum(jnp.where(is_rope, x_hi * x_hi, 0.0), axis=1, keepdims=True)
        return x_hi * a_hi + pltpu.roll(x_hi * b_hi, MLA_ROPE_DIM, 1), ss

    if shared_rope:
        y_hi, ss_hi = rope_part(hi_ref[...].astype(F32))
    for h in range(MLA_HEADS):
        c = h * HEAD_PAD
        x_lo = x_ref[:, c:c + LANES].astype(F32)
        if not shared_rope:
            y_hi, ss_hi = rope_part(x_ref[:, c + LANES:c + HEAD_PAD].astype(F32))
        ss = jnp.sum(x_lo * x_lo, axis=1, keepdims=True) + ss_hi
        r = lax.rsqrt(ss * (1.0 / (MLA_NOPE_DIM + MLA_ROPE_DIM)) + EPS)
        o_ref[:, c:c + LANES] = (x_lo * r * a_lo).astype(o_ref.dtype)
        o_ref[:, c + LANES:c + HEAD_PAD] = (y_hi * r).astype(o_ref.dtype)


def mla_prep(x_arr, rope_arr, rope_col0, a_lo, a_hi, b_hi, T, tt=256):
    N = x_arr.shape[0]
    W = MLA_HEADS * HEAD_PAD
    tt = _tile(T, tt)
    nt = T // tt
    shared = rope_arr is not None
    ins = [x_arr]
    specs = [pl.BlockSpec((tt, W), lambda i: (i, 0))]
    if shared:
        rb = rope_col0 // LANES
        ins.append(rope_arr)
        specs.append(pl.BlockSpec((tt, LANES), lambda i: (i, rb)))
    ins += [a_lo, a_hi, b_hi]
    specs += [pl.BlockSpec((1, LANES), lambda i: (0, 0)),
              pl.BlockSpec((tt, LANES), lambda i: (i % nt, 0)),
              pl.BlockSpec((tt, LANES), lambda i: (i % nt, 0))]
    return pl.pallas_call(
        functools.partial(_mla_prep_kernel, shared_rope=shared),
        out_shape=jax.ShapeDtypeStruct((N, W), BF16),
        grid=(N // tt,),
        in_specs=specs,
        out_specs=pl.BlockSpec((tt, W), lambda i: (i, 0)),
        compiler_params=pltpu.CompilerParams(dimension_semantics=("parallel",)),
        name="mla_prep",
    )(*ins)


def _rope_tables(gain, T, scale):
    half = MLA_ROPE_DIM // 2
    inv_freq = ROPE_THETA ** (-jnp.arange(half, dtype=F32) / half)
    ang = jnp.arange(T, dtype=F32)[:, None] * inv_freq[None, :]
    cos2 = jnp.concatenate([jnp.cos(ang), jnp.cos(ang)], axis=1)
    sin_s = jnp.concatenate([-jnp.sin(ang), jnp.sin(ang)], axis=1)
    g_nope, g_rope = gain[:MLA_NOPE_DIM], gain[MLA_NOPE_DIM:]
    g_perm = jnp.concatenate([g_rope[half:], g_rope[:half]])
    zeros = jnp.zeros((T, MLA_ROPE_DIM), F32)
    a_lo = (g_nope * scale).reshape(1, LANES)
    a_hi = jnp.concatenate([g_rope[None, :] * cos2 * scale, zeros], axis=1)
    b_hi = jnp.concatenate([zeros, g_perm[None, :] * sin_s * scale], axis=1)
    return a_lo, a_hi, b_hi


def _swa_kernel(qadd_ref, slope_ref, sink_ref, q_ref, k_ref, v_ref, ktab_ref, o_ref, mask_ref,
                *, tq, tk, R):
    t0 = pl.program_id(2) * tq
    npair = R // 2
    rows = npair * tq
    qp = _stack_heads(q_ref, npair, LANES)
    lane = lax.broadcasted_iota(jnp.int32, (rows, LANES), 1)
    first = lane < SWA_HEAD_DIM
    start = pl.multiple_of(jnp.maximum(t0 - SWA_WINDOW, 0), LANES)
    kc = jnp.concatenate([k_ref[pl.ds(start, tk), :], ktab_ref[pl.ds(start, tk), :]], axis=1)
    v_aug = jnp.concatenate([v_ref[pl.ds(start, tk), :], jnp.ones((tk, LANES), BF16)], axis=1)
    @pl.when(pl.program_id(2) <= 1)
    def _():
        row_in = lax.broadcasted_iota(jnp.int32, (2 * rows, tk), 0) % tq
        rel = t0 - start + row_in - lax.broadcasted_iota(jnp.int32, (2 * rows, tk), 1)
        mask_ref[...] = jnp.where((rel >= 0) & (rel < SWA_WINDOW), 0.0, NEG)

    t_row = (t0 + lax.broadcasted_iota(jnp.int32, (2 * rows, LANES), 0) % tq).astype(F32)
    zero = jnp.zeros_like(qp)
    qu = jnp.concatenate([jnp.where(first, qp, zero), jnp.where(first, zero, qp)], axis=0)
    qa = jnp.concatenate([qu, qadd_ref[...].reshape(2 * rows, LANES)], axis=1)
    sink_t = (sink_ref[...].reshape(2 * rows, LANES)
              + slope_ref[...].reshape(2 * rows, LANES) * t_row)
    s = _nt_dot(qa, kc) + mask_ref[...]
    tiles = [s[:, j * LANES:(j + 1) * LANES] for j in range(tk // LANES)]
    mx = tiles[0]
    for t in tiles[1:]:
        mx = jnp.maximum(mx, t)
    m = jnp.maximum(jnp.max(mx, axis=1, keepdims=True), sink_t)
    p = jnp.concatenate([jnp.exp(t - m).astype(BF16) for t in tiles], axis=1)
    pv = jnp.dot(p, v_aug, preferred_element_type=F32)
    denom = pv[:, LANES:] + jnp.exp(sink_t - m)
    both = pv[:, :LANES] / jnp.maximum(denom, 1e-30)
    o = jnp.where(first, both[:rows], both[rows:])
    for p_ in range(npair):
        o_ref[:, p_ * LANES:(p_ + 1) * LANES] = o[p_ * tq:(p_ + 1) * tq].astype(o_ref.dtype)


def swa_attention(h, sinks, B, T, k_col0, v_col0, tq=128):
    G, R = SWA_KV_HEADS, SWA_HEADS // SWA_KV_HEADS
    tq = _tile(T, tq)
    tk = min(T, tq + SWA_WINDOW)
    assert tk % LANES == 0 and tq % LANES == 0
    nq = T // tq
    qw = R * SWA_HEAD_DIM
    kb, vb = k_col0 // LANES, v_col0 // LANES
    npair = R // 2
    rows = npair * tq
    per_head = lambda v: jnp.repeat(v.reshape(G, npair, 2).transpose(0, 2, 1), tq, axis=2)
    lanes = lambda v: jnp.broadcast_to(v[..., None], (G, 2, rows, LANES))
    parts = _slope_parts(SWA_HEADS).reshape(SWA_HEADS, 3)
    lane_id = np.arange(LANES)
    qadd = jnp.zeros((G, 2, rows, LANES), F32)
    for part in range(3):
        at_lane = (lane_id == ALIBI_LANE0 + part) | (lane_id == ALIBI_LANE0 + 3 + part)
        qadd = jnp.where(at_lane[None, None, None, :], lanes(per_head(parts[:, part])), qadd)
    slope_rows = lanes(per_head(_alibi(SWA_HEADS)))
    sink_rows = lanes(per_head(sinks.astype(F32)))
    const_spec = pl.BlockSpec((None, 2, rows, LANES), lambda b, g, i: (g, 0, 0, 0))
    return pl.pallas_call(
        functools.partial(_swa_kernel, tq=tq, tk=tk, R=R),
        out_shape=jax.ShapeDtypeStruct((B * T, SWA_HEADS * SWA_HEAD_DIM), BF16),
        grid=(B, G, nq),
        in_specs=[const_spec, const_spec, const_spec,
                  pl.BlockSpec((tq, qw), lambda b, g, i: (b * nq + i, g)),
                  pl.BlockSpec((T, LANES), lambda b, g, i: (b, kb + g)),
                  pl.BlockSpec((T, LANES), lambda b, g, i: (b, vb + g)),
                  pl.BlockSpec((T, LANES), lambda b, g, i: (0, 0))],
        out_specs=pl.BlockSpec((tq, qw), lambda b, g, i: (b * nq + i, g)),
        scratch_shapes=[pltpu.VMEM((2 * rows, tk), F32)],
        compiler_params=pltpu.CompilerParams(dimension_semantics=("parallel", "parallel", "arbitrary")),
        name="swa_attn",
    )(qadd.astype(BF16), slope_rows, sink_rows, h, h, h, _key_table(T))


def _dispatch_kernel(zfrom_ref, zto_ref, pos_ref, src_ref, dst_hbm, zrow, sem, zsem, *, td, n_seg):
    @pl.when(pl.program_id(0) == 0)
    def _():
        zrow[...] = jnp.zeros_like(zrow)

        def zero_copy(r):
            return pltpu.make_async_copy(zrow, dst_hbm.at[pl.ds(r, 1)], zsem)

        def seg(s, c):
            lax.fori_loop(zfrom_ref[s], zto_ref[s], lambda r, cc: (zero_copy(r).start(), cc)[1], 0)
            return c

        def seg_wait(s, c):
            lax.fori_loop(zfrom_ref[s], zto_ref[s], lambda r, cc: (zero_copy(r).wait(), cc)[1], 0)
            return c

        lax.fori_loop(0, n_seg, seg, 0)
        lax.fori_loop(0, n_seg, seg_wait, 0)

    def row_copy(t, k):
        return pltpu.make_async_copy(src_ref.at[pl.ds(t, 1)],
                                     dst_hbm.at[pl.ds(pos_ref[0, t * TOP_K + k], 1)], sem)

    def issue(t, c):
        for k in range(TOP_K):
            row_copy(t, k).start()
        return c

    def drain(t, c):
        for k in range(TOP_K):
            row_copy(t, k).wait()
        return c

    lax.fori_loop(0, td, issue, 0)
    lax.fori_loop(0, td, drain, 0)


def moe_dispatch(xm_packed, pos, zfrom, zto, cap, td=256):
    N, W = xm_packed.shape
    td = _tile(N, td)
    pos3 = pos.reshape(N // td, 1, td * TOP_K)
    grid_spec = pltpu.PrefetchScalarGridSpec(
        num_scalar_prefetch=2,
        grid=(N // td,),
        in_specs=[pl.BlockSpec((None, 1, td * TOP_K), lambda i, zf, zt: (i, 0, 0), memory_space=pltpu.SMEM),
                  pl.BlockSpec((td, W), lambda i, zf, zt: (i, 0))],
        out_specs=pl.BlockSpec(memory_space=pl.ANY),
        scratch_shapes=[pltpu.VMEM((1, W), jnp.int32), pltpu.SemaphoreType.DMA(()),
                        pltpu.SemaphoreType.DMA(())],
    )
    return pl.pallas_call(
        functools.partial(_dispatch_kernel, td=td, n_seg=zfrom.shape[0]),
        out_shape=jax.ShapeDtypeStruct((cap, W), jnp.int32),
        grid_spec=grid_spec,
        compiler_params=pltpu.CompilerParams(dimension_semantics=("arbitrary",), has_side_effects=True),
        name="moe_dispatch",
    )(zfrom, zto, pos3, xm_packed)


def _expert_kernel(be_ref, slot_ref, nxt_ref, c0_ref, c1_ref, nu_ref, xs_ref, wg_hbm, bg_ref, wu_hbm, bu_ref,
                   wd_hbm, bd_ref, o_ref, wg_s, wu_s, wd_s, stg_a, stg_d, sem, *, ca, cd, ring, layer):
    blk = pl.program_id(0)
    e = be_ref[blk]
    s = slot_ref[blk]
    used = blk < nu_ref[0]
    _, D, F = wg_s.shape
    na, nd = D // ca, F // cd
    n_chunks = 2 * na + nd

    def on_chunk(c, expert, fn):
        k = c % ring

        def go(src, dst, r0, n, stg, sem0):
            copy = pltpu.make_async_copy(src.at[layer, expert, pl.ds(r0, n), :], stg.at[k], sem.at[sem0 + k])
            fn(copy, dst, r0, n, stg.at[k])

        @pl.when(c < na)
        def _():
            go(wg_hbm, wg_s, pl.multiple_of(c * ca, ca), ca, stg_a, 0)

        @pl.when((c >= na) & (c < 2 * na))
        def _():
            go(wu_hbm, wu_s, pl.multiple_of((c - na) * ca, ca), ca, stg_a, 0)

        @pl.when(c >= 2 * na)
        def _():
            go(wd_hbm, wd_s, pl.multiple_of((c - 2 * na) * cd, cd), cd, stg_d, ring)

    def start(c, expert):
        on_chunk(c, expert, lambda copy, dst, r0, n, stg: copy.start())

    def finish(c, expert, dst_slot):
        def fn(copy, dst, r0, n, stg):
            copy.wait()
            dst[dst_slot, pl.ds(r0, n), :] = stg[...].astype(BF16)
        on_chunk(c, expert, fn)

    def stream(lo, hi, expert, dst_slot, prestarted):
        if not prestarted:
            for i in range(ring):
                @pl.when(lo + i < hi)
                def _():
                    start(lo + i, expert)

        def body(c, carry):
            finish(c, expert, dst_slot)

            @pl.when(c + ring < hi)
            def _():
                start(c + ring, expert)
            return carry

        lax.fori_loop(lo, hi, body, 0)

    @pl.when(used & (blk == 0))
    def _():
        stream(0, n_chunks, e, s, False)

    c0, c1, nxt = c0_ref[blk], c1_ref[blk], nxt_ref[blk]
    for i in range(ring):
        @pl.when(used & (c0 + i < c1))
        def _():
            start(c0 + i, nxt)

    @pl.when(used)
    def _():
        x = _unpack_bf16_pairs(xs_ref[...])
        gg = jnp.dot(x, wg_s[s], preferred_element_type=F32) + bg_ref[...]
        uu = jnp.dot(x, wu_s[s], preferred_element_type=F32) + bu_ref[...]
        gg = jnp.minimum(gg, SWIGLU_LIMIT)
        uu = jnp.clip(uu, -SWIGLU_LIMIT, SWIGLU_LIMIT)
        act = gg * jax.nn.sigmoid(SWIGLU_ALPHA * gg) * (uu + 1.0)
        y = jnp.dot(act.astype(BF16), wd_s[s], preferred_element_type=F32) + bd_ref[...]
        o_ref[...] = _pack_bf16_pairs(y)
        stream(c0, c1, nxt, 1 - s, True)

    @pl.when(jnp.logical_not(used))
    def _():
        o_ref[...] = jnp.zeros_like(o_ref)


EXPERT_STAGE_RING = 4
VMEM_LIMIT_EXPERTS = 60 * 1024 * 1024


def _expert_schedule(blk_e, n_used, n_chunks):
    n_blk = blk_e.shape[0]
    idx = jnp.arange(n_blk, dtype=jnp.int32)
    first = jnp.concatenate([jnp.ones((1,), jnp.int32), (blk_e[1:] != blk_e[:-1]).astype(jnp.int32)])
    slot = (jnp.cumsum(first) - 1) % 2
    run_start = lax.cummax(jnp.where(first == 1, idx, 0))
    starts_after = jnp.concatenate([jnp.where(first == 1, idx, n_blk)[1:], jnp.full((1,), n_blk, jnp.int32)])
    next_start = lax.cummin(starts_after, reverse=True)
    has_next = next_start < n_used[0]
    nxt = blk_e[jnp.minimum(next_start, n_blk - 1)]
    run_len = jnp.maximum(jnp.minimum(next_start, n_used[0]) - run_start, 1)
    j = idx - run_start
    c0 = jnp.where(has_next, j * n_chunks // run_len, 0)
    c1 = jnp.where(has_next, (j + 1) * n_chunks // run_len, 0)
    as_i32 = lambda v: v.astype(jnp.int32)
    return as_i32(slot), as_i32(nxt), as_i32(c0), as_i32(c1)


def moe_experts(xs, blk_e, n_used, layer, wg, bg, wu, bu, wd, bd, rb=MOE_ROW_BLOCK):
    cap, W = xs.shape
    depth, E, D, F = wg.shape
    n_blk = cap // rb
    ca, cd = _tile(D, 256), _tile(F, 64)
    ring = EXPERT_STAGE_RING
    slot, nxt, c0, c1 = _expert_schedule(blk_e, n_used, 2 * (D // ca) + F // cd)
    by_expert = lambda shape: pl.BlockSpec((None,) + shape,
                                           lambda i, be, *_: (layer * E + be[i], 0, 0))
    grid_spec = pltpu.PrefetchScalarGridSpec(
        num_scalar_prefetch=6,
        grid=(n_blk,),
        in_specs=[pl.BlockSpec((rb, W), lambda i, *_: (i, 0)),
                  pl.BlockSpec(memory_space=pl.ANY), by_expert((1, F)),
                  pl.BlockSpec(memory_space=pl.ANY), by_expert((1, F)),
                  pl.BlockSpec(memory_space=pl.ANY), by_expert((1, D))],
        out_specs=pl.BlockSpec((rb, W), lambda i, *_: (i, 0)),
        scratch_shapes=[pltpu.VMEM((2, D, F), BF16), pltpu.VMEM((2, D, F), BF16), pltpu.VMEM((2, F, D), BF16),
                        pltpu.VMEM((ring, ca, F), F32), pltpu.VMEM((ring, cd, D), F32),
                        pltpu.SemaphoreType.DMA((2 * ring,))],
    )
    return pl.pallas_call(
        functools.partial(_expert_kernel, ca=ca, cd=cd, ring=ring, layer=layer),
        out_shape=jax.ShapeDtypeStruct((cap, W), jnp.int32),
        grid_spec=grid_spec,
        compiler_params=pltpu.CompilerParams(dimension_semantics=("arbitrary",),
                                             vmem_limit_bytes=VMEM_LIMIT_EXPERTS),
        name="moe_experts",
    )(blk_e, slot, nxt, c0, c1, n_used, xs, wg, bg.reshape(depth * E, 1, F), wu, bu.reshape(depth * E, 1, F),
      wd, bd.reshape(depth * E, 1, D))


def _moe_combine_kernel(pos_ref, pos_next_ref, y_hbm, x_ref, w_ref, gate_ref, o_ref, ybuf, sem,
                        *, tt, n_steps):
    i = pl.program_id(0)
    slot = i % 2

    def row_copy(p_ref, t, k, sl):
        return pltpu.make_async_copy(y_hbm.at[pl.ds(p_ref[0, t * TOP_K + k], 1)],
                                     ybuf.at[sl, k, pl.ds(t, 1)], sem.at[sl])

    def issue(p_ref, sl):
        def body(t, c):
            for k in range(TOP_K):
                row_copy(p_ref, t, k, sl).start()
            return c
        lax.fori_loop(0, tt, body, 0)

    @pl.when(i == 0)
    def _():
        issue(pos_ref, 0)

    @pl.when(i + 1 < n_steps)
    def _():
        issue(pos_next_ref, 1 - slot)

    def drain(t, c):
        for k in range(TOP_K):
            row_copy(pos_ref, t, k, slot).wait()
        return c

    lax.fori_loop(0, tt, drain, 0)
    w = w_ref[...]
    half = x_ref.shape[1] // 2
    y_lo = y_hi = None
    for k in range(TOP_K):
        u = ybuf[slot, k]
        wk = w[:, k:k + 1]
        lo = wk * lax.bitcast_convert_type(lax.shift_left(u, 16), F32)
        hi = wk * lax.bitcast_convert_type(u & jnp.int32(-65536), F32)
        y_lo = lo if y_lo is None else y_lo + lo
        y_hi = hi if y_hi is None else y_hi + hi
    o_ref[:, :half] = x_ref[:, :half] + gate_ref[:, :half] * y_lo
    o_ref[:, half:] = x_ref[:, half:] + gate_ref[:, half:] * y_hi


def moe_combine(yb, pos, top_w, x2d, gate, rows_per_batch, tt=128):
    N, D = x2d.shape
    tt = _tile(rows_per_batch, tt)
    bpb = rows_per_batch // tt
    n_steps = N // tt
    pos3 = pos.reshape(n_steps, 1, tt * TOP_K)
    pos_spec = lambda f: pl.BlockSpec((None, 1, tt * TOP_K), f, memory_space=pltpu.SMEM)
    return pl.pallas_call(
        functools.partial(_moe_combine_kernel, tt=tt, n_steps=n_steps),
        out_shape=jax.ShapeDtypeStruct((N, D), F32),
        grid=(n_steps,),
        in_specs=[pos_spec(lambda i: (i, 0, 0)),
                  pos_spec(lambda i: (jnp.minimum(i + 1, n_steps - 1), 0, 0)),
                  pl.BlockSpec(memory_space=pl.ANY),
                  pl.BlockSpec((tt, D), lambda i: (i, 0)),
                  pl.BlockSpec((tt, LANES), lambda i: (i, 0)),
                  pl.BlockSpec((None, 1, D), lambda i: (i // bpb, 0, 0))],
        out_specs=pl.BlockSpec((tt, D), lambda i: (i, 0)),
        scratch_shapes=[pltpu.VMEM((2, TOP_K, tt, D // 2), jnp.int32), pltpu.SemaphoreType.DMA((2,))],
        compiler_params=pltpu.CompilerParams(dimension_semantics=("arbitrary",)),
        name="moe_combine",
    )(pos3, pos3, yb, x2d, top_w, gate)


def moe_layer(x2d, gain, sc, sh, gate, rows_per_batch, router_w, router_b, layer, wg, bg, wu, bu, wd, bd):
    N, D = x2d.shape
    RB = MOE_ROW_BLOCK
    E = N_EXPERTS
    xm_packed, top_e, top_w, rank, counts = norm_mod(x2d, gain, sc, sh, rows_per_batch,
                                                     router=(router_w, router_b))
    nk = N * TOP_K
    counts = counts[0, :E]
    padded = (counts + RB - 1) // RB * RB
    pad_end = jnp.cumsum(padded)
    pad_start = pad_end - padded
    e4 = top_e[:, :TOP_K]
    start_of = jnp.sum(jnp.where(e4[:, :, None] == jnp.arange(E)[None, None, :],
                                 pad_start[None, None, :], 0), axis=2)
    pos = (start_of + rank[:, :TOP_K]).astype(jnp.int32)
    cap = (-(-nk // RB)) * RB + E * RB
    n_blk = cap // RB
    blk_start = jnp.arange(n_blk, dtype=jnp.int32) * RB
    blk_e = jnp.minimum(jnp.sum(pad_end[None, :] <= blk_start[:, None], axis=1), E - 1).astype(jnp.int32)
    n_used = (pad_end[-1] // RB).astype(jnp.int32).reshape(1)
    zfrom = jnp.concatenate([pad_start + counts, pad_end[-1:]]).astype(jnp.int32)
    zto = jnp.concatenate([pad_end, jnp.full((1,), cap)]).astype(jnp.int32)
    xs = moe_dispatch(xm_packed, pos, zfrom, zto, cap)
    yb = moe_experts(xs, blk_e, n_used, layer, wg, bg, wu, bu, wd, bd)
    return moe_combine(yb, pos, top_w, x2d, gate, rows_per_batch)


def _pad_heads(w, n_heads, d, dp):
    lead = w.shape[:-1]
    w = w.reshape(lead + (n_heads, d))
    w = jnp.pad(w, [(0, 0)] * len(lead) + [(0, 0), (0, dp - d)])
    return w.reshape(lead + (n_heads * dp,))


def _even_in_weights(w_in, q_gain, k_gain):
    D = w_in.shape[0]
    cuts = np.cumsum(EVEN_IN_SPLITS)[:-1].tolist()
    q, kc, vc, ks, vs, kw, vw, gates, cq, kva = jnp.split(w_in.astype(BF16), cuts, axis=1)
    half = MLA_ROPE_DIM // 2
    kr = kva[:, MLA_KV_RANK:]
    kr_perm = jnp.concatenate([kr[:, half:], kr[:, :half]], axis=1)
    G = NSA_KV_HEADS
    cols = [_pad_heads(q, NSA_HEADS, NSA_QK_DIM, HEAD_PAD),
            _pad_heads(kc, G, NSA_QK_DIM, HEAD_PAD), _pad_heads(ks, G, NSA_QK_DIM, HEAD_PAD),
            _pad_heads(kw, G, NSA_QK_DIM, HEAD_PAD), cq, kva[:, :MLA_KV_RANK], kr, kr_perm,
            vc, vs, vw, gates]
    cols.append(jnp.zeros((D, EV_WIDTH - EV_END + LANES - gates.shape[1]), BF16))
    w = jnp.concatenate(cols, axis=1)
    assert w.shape[1] == EV_WIDTH
    scale = NSA_QK_DIM ** -0.5
    pad_g = lambda g: jnp.pad(g, (0, HEAD_PAD - NSA_QK_DIM))
    gain = jnp.ones((EV_WIDTH,), F32)
    gain = gain.at[EV_Q:EV_Q + _NQ].set(jnp.tile(pad_g(q_gain * scale), NSA_HEADS))
    gain = gain.at[EV_KS:EV_KS + _NK].set(jnp.tile(pad_g(k_gain[1]), G))
    gain = gain.at[EV_KW:EV_KW + _NK].set(jnp.tile(pad_g(k_gain[2]), G))
    col = np.arange(EV_WIDTH)
    flag = ((col < EV_Q + _NQ) | ((col >= EV_KS) & (col < EV_KW + _NK))).astype(np.float32)
    return w, gain, jnp.asarray(flag)


def _even_mixer(xm, x2d, g_a, B, T, w_in, w_out, q_gain, k_gain, pe_k, pe_v, w_ck1, w_ck2, w_cv1, w_cv2,
                g_cq, g_ckv, w_uq, w_ukv, mq_gain, mk_gain):
    N, D = x2d.shape
    G, R = NSA_KV_HEADS, NSA_HEADS // NSA_KV_HEADS
    w_p, gain, flag = _even_in_weights(w_in, q_gain, k_gain)
    h = matmul(xm, w_p, head_norm=(HEAD_PAD, NSA_QK_DIM, gain, flag), name="in_proj_even")

    kc = compress(h, EV_KC, B, T, pe_k, w_ck1, w_ck2, k_gain[0], NSA_QK_DIM, HEAD_PAD)
    vc = compress(h, EV_VC, B, T, pe_v, w_cv1, w_cv2, None, NSA_V_DIM, NSA_V_DIM)

    o_cmp, bits = cmp_attention(h, kc, vc, B, T)
    HV = NSA_HEADS * NSA_V_DIM
    o_slc = flash_attention(h, EV_Q, h, EV_KS, h, EV_VS, B=B, T=T, G=G, R=R, out_cols=HV,
                            n_alibi_heads=NSA_HEADS, bits=bits, tq=512, tk=512, name="nsa_slc_attn")
    o_win = flash_attention(h, EV_Q, h, EV_KW, h, EV_VW, B=B, T=T, G=G, R=R, out_cols=HV,
                            window=NSA_WINDOW, n_alibi_heads=NSA_HEADS, name="nsa_win_attn")
    o_a = nsa_combine(h, o_cmp, o_slc, o_win)

    H = MLA_HEADS
    dqk = MLA_NOPE_DIM + MLA_ROPE_DIM
    half = MLA_ROPE_DIM // 2
    wq = w_uq.reshape(MLA_Q_RANK, H, dqk)
    wq_rope = wq[:, :, MLA_NOPE_DIM:]
    wq_p = jnp.concatenate([wq, wq_rope[:, :, half:], wq_rope[:, :, :half]], axis=2)
    wq_p = wq_p.reshape(MLA_Q_RANK, H * HEAD_PAD).astype(BF16)
    q_raw = matmul(h, wq_p, a_col0=EV_CQ, a_pro="rms", a_gain=g_cq, name="mla_q_up")
    kv_raw = matmul(h, w_ukv.astype(BF16), a_col0=EV_CKV, a_pro="rms", a_gain=g_ckv, name="mla_kv_up")
    qa_lo, qa_hi, qb_hi = _rope_tables(mq_gain, T, dqk ** -0.5)
    ka_lo, ka_hi, kb_hi = _rope_tables(mk_gain, T, 1.0)
    q_m = mla_prep(q_raw, None, 0, qa_lo, qa_hi, qb_hi, T)
    k_m = mla_prep(kv_raw, h, EV_KR, ka_lo, ka_hi, kb_hi, T)
    hp = 4
    kv_w = MLA_NOPE_DIM + MLA_V_DIM
    o_b = flash_attention(q_m, 0, k_m, 0, kv_raw, 0, B=B, T=T, G=H, R=1, HP=hp, v_width=hp * kv_w,
                          v_off=MLA_NOPE_DIM, v_step=kv_w, out_cols=H * MLA_V_DIM, tq=512, tk=512,
                          name="mla_attn")
    return matmul(o_a, w_out.astype(BF16), a2=o_b, resid=(x2d, g_a, T), out_dtype=F32, name="out_proj_even")


def _odd_mixer(xm, x2d, g_a, B, T, w_in, b_in, w_out, b_out, q_gain, k_gain, sinks):
    D = w_in.shape[0]
    G, hd = SWA_KV_HEADS, SWA_HEAD_DIM
    nq = SWA_HEADS * hd

    def dup(t):
        lead = t.shape[:-1]
        t = t.reshape(lead + (G, 1, hd))
        return jnp.broadcast_to(t, lead + (G, 2, hd)).reshape(lead + (G * 2 * hd,))

    kw = G * hd
    w_in = w_in.astype(BF16)
    w_p = jnp.concatenate([w_in[:, :nq], dup(w_in[:, nq:nq + kw]), dup(w_in[:, nq + kw:])], axis=1)
    b_p = jnp.concatenate([b_in[:nq], dup(b_in[nq:nq + kw]), dup(b_in[nq + kw:])])
    gain = jnp.concatenate([jnp.tile(q_gain * hd ** -0.5, SWA_HEADS), jnp.tile(k_gain, 2 * G),
                            jnp.ones((2 * kw,), F32)])
    flag = jnp.concatenate([jnp.ones((nq + 2 * kw,), F32), jnp.zeros((2 * kw,), F32)])
    h = matmul(xm, w_p.astype(BF16), bias=b_p, head_norm=(hd, hd, gain, flag), name="in_proj_odd")
    o_c = swa_attention(h, sinks, B, T, nq, nq + 2 * kw)
    return matmul(o_c, w_out.astype(BF16), bias=b_out, resid=(x2d, g_a, T), out_dtype=F32,
                  name="out_proj_odd")


def kernel(x, c, w_mod, mod_table, norm_attn, norm_ffn, w_in_even, w_out_even, nsa_q_gain, nsa_k_gain,
           nsa_pe_k, nsa_pe_v, nsa_w_ck1, nsa_w_ck2, nsa_w_cv1, nsa_w_cv2, mla_g_cq, mla_g_ckv, mla_w_uq,
           mla_w_ukv, mla_q_gain, mla_k_gain, w_in_odd, b_in_odd, w_out_odd, b_out_odd, swa_q_gain,
           swa_k_gain, swa_sinks, router_w, router_b, moe_w_gate, moe_b_gate, moe_w_up, moe_b_up,
           moe_w_down, moe_b_down):
    B, T, D = x.shape
    N = B * T
    depth = mod_table.shape[0]
    c_pad = jnp.pad(c, ((0, 8 - B % 8 if B % 8 else 0), (0, 0)))
    cond = matmul(c_pad, w_mod, a_pro="silu", out_dtype=F32, tn=1024, tk=1024, name="adaln_proj")[:B]
    x2d = x.reshape(N, D)
    for layer in range(depth):
        mod = (cond + mod_table[layer]).reshape(B, 6, 1, D)
        sh_a, sc_a, g_a, sh_f, sc_f, g_f = (mod[:, j] for j in range(6))
        xm = norm_mod(x2d, norm_attn[layer], sc_a, sh_a, T)
        i = layer // 2
        if layer % 2 == 0:
            x2d = _even_mixer(xm, x2d, g_a, B, T, w_in_even[i], w_out_even[i], nsa_q_gain[i], nsa_k_gain[i],
                              nsa_pe_k[i], nsa_pe_v[i], nsa_w_ck1[i], nsa_w_ck2[i], nsa_w_cv1[i],
                              nsa_w_cv2[i], mla_g_cq[i], mla_g_ckv[i], mla_w_uq[i], mla_w_ukv[i],
                              mla_q_gain[i], mla_k_gain[i])
        else:
            x2d = _odd_mixer(xm, x2d, g_a, B, T, w_in_odd[i], b_in_odd[i], w_out_odd[i], b_out_odd[i],
                             swa_q_gain[i], swa_k_gain[i], swa_sinks[i])
        x2d = moe_layer(x2d, norm_ffn[layer], sc_f, sh_f, g_f, T, router_w[layer], router_b[layer],
                        layer, moe_w_gate, moe_b_gate, moe_w_up, moe_b_up, moe_w_down, moe_b_down)
    return x2d.reshape(B, T, D)
```

```python
import functools
import math

import numpy as np
import jax
import jax.numpy as jnp
from jax import lax
from jax.experimental import pallas as pl
from jax.experimental.pallas import tpu as pltpu

BF16 = jnp.bfloat16
F32 = jnp.float32

NSA_HEADS = 16
NSA_KV_HEADS = 4
NSA_QK_DIM = 192
NSA_V_DIM = 128
CMP_BLOCK = 32
CMP_STRIDE = 16
SLC_BLOCK = 64
SLC_TOPN = 8
NSA_WINDOW = 512
MLA_HEADS = 16
MLA_Q_RANK = 1024
MLA_KV_RANK = 512
MLA_NOPE_DIM = 128
MLA_ROPE_DIM = 64
MLA_V_DIM = 128
ROPE_THETA = 10000.0
SWA_HEADS = 64
SWA_KV_HEADS = 8
SWA_HEAD_DIM = 64
SWA_WINDOW = 128
N_EXPERTS = 32
TOP_K = 4
SWIGLU_ALPHA = 1.702
SWIGLU_LIMIT = 7.0
MOE_ROW_BLOCK = 256
EPS = 1e-6

LANES = 128
HEAD_PAD = 256
NEG = -1e30
M_FLOOR = -1e29

EVEN_IN_SPLITS = (NSA_HEADS * NSA_QK_DIM,
                  NSA_KV_HEADS * NSA_QK_DIM, NSA_KV_HEADS * NSA_V_DIM,
                  NSA_KV_HEADS * NSA_QK_DIM, NSA_KV_HEADS * NSA_V_DIM,
                  NSA_KV_HEADS * NSA_QK_DIM, NSA_KV_HEADS * NSA_V_DIM,
                  3 * NSA_HEADS, MLA_Q_RANK, MLA_KV_RANK + MLA_ROPE_DIM)

_NQ = NSA_HEADS * HEAD_PAD
_NK = NSA_KV_HEADS * HEAD_PAD
_NV = NSA_KV_HEADS * NSA_V_DIM
EV_Q = 0
EV_KC = EV_Q + _NQ
EV_KS = EV_KC + _NK
EV_KW = EV_KS + _NK
EV_CQ = EV_KW + _NK
EV_CKV = EV_CQ + MLA_Q_RANK
EV_KR = EV_CKV + MLA_KV_RANK
EV_VC = EV_KR + 2 * MLA_ROPE_DIM
EV_VS = EV_VC + _NV
EV_VW = EV_VS + _NV
EV_GATE = EV_VW + _NV
EV_END = EV_GATE + LANES
EV_WIDTH = -(-EV_END // 512) * 512


def _tile(dim, want):
    t = min(dim, want)
    while dim % t:
        t //= 2
    return t


def _alibi_np(n):
    return np.exp2(-8.0 * np.arange(1, n + 1, dtype=np.float32) / n).astype(np.float32)


def _alibi(n):
    return jnp.asarray(_alibi_np(n))


def _head_norm(res, hd, real_d):
    tm, tn = res.shape
    x2 = res * res
    pieces = []
    if hd == HEAD_PAD:
        for s in range(tn // hd):
            ss = jnp.sum(x2[:, s * hd:s * hd + LANES] + x2[:, s * hd + LANES:(s + 1) * hd],
                         axis=1, keepdims=True)
            r = lax.rsqrt(ss * (1.0 / real_d) + EPS)
            pieces.append(jnp.broadcast_to(r, (tm, hd)))
    else:
        lo = lax.broadcasted_iota(jnp.int32, (tm, LANES), 1) < hd
        for s in range(tn // LANES):
            c = x2[:, s * LANES:(s + 1) * LANES]
            ss_lo = jnp.sum(jnp.where(lo, c, 0.0), axis=1, keepdims=True)
            ss_hi = jnp.sum(jnp.where(lo, 0.0, c), axis=1, keepdims=True)
            r_lo = lax.rsqrt(ss_lo * (1.0 / real_d) + EPS)
            r_hi = lax.rsqrt(ss_hi * (1.0 / real_d) + EPS)
            pieces.append(jnp.where(lo, r_lo, r_hi))
    return jnp.concatenate(pieces, axis=1) if len(pieces) > 1 else pieces[0]


def _mm_kernel(*refs, nk, a_pro, has_a2, has_bias, hd, real_d, has_resid):
    it = iter(refs)
    a_ref = next(it)
    a2_ref = next(it) if has_a2 else None
    w_ref = next(it)
    again_ref = next(it) if a_pro == "rms" else None
    b_ref = next(it) if has_bias else None
    gain_ref = flag_ref = None
    if hd:
        gain_ref = next(it)
        flag_ref = next(it)
    x_ref = gate_ref = None
    if has_resid:
        x_ref = next(it)
        gate_ref = next(it)
    o_ref = next(it)
    acc_ref = next(it) if nk > 1 else None

    a = a_ref[...]
    if a_pro == "silu":
        af = a.astype(F32)
        a = af * jax.nn.sigmoid(af)
    elif a_pro == "rms":
        af = a.astype(F32)
        r = lax.rsqrt(jnp.mean(af * af, axis=1, keepdims=True) + EPS)
        a = af * r * again_ref[...]
    if has_a2:
        k1 = a.shape[1]
        part = (jnp.dot(a.astype(BF16), w_ref[:k1, :].astype(BF16), preferred_element_type=F32)
                + jnp.dot(a2_ref[...].astype(BF16), w_ref[k1:, :].astype(BF16), preferred_element_type=F32))
    else:
        part = jnp.dot(a.astype(BF16), w_ref[...].astype(BF16), preferred_element_type=F32)

    def finish(res):
        if has_bias:
            res = res + b_ref[...]
        if hd:
            r = _head_norm(res, hd, real_d)
            res = res * jnp.where(flag_ref[...] > 0.0, r, 1.0) * gain_ref[...]
        if has_resid:
            res = x_ref[...] + gate_ref[...] * res
        o_ref[...] = res.astype(o_ref.dtype)

    if nk == 1:
        finish(part)
    else:
        k = pl.program_id(2)

        @pl.when(k == 0)
        def _():
            acc_ref[...] = part

        @pl.when(k > 0)
        def _():
            acc_ref[...] += part

        @pl.when(k == nk - 1)
        def _():
            finish(acc_ref[...])


def matmul(a, w, *, a2=None, a_col0=0, a_pro=None, a_gain=None, bias=None,
           head_norm=None, resid=None, out_dtype=BF16, tm=1024, tn=512, tk=4096, name="mm"):
    M = a.shape[0]
    K, N = w.shape
    tm, tn, tk = _tile(M, tm), _tile(N, tn), _tile(K, tk)
    if a_pro == "rms" or a2 is not None:
        tk = K
    assert a_col0 % tk == 0 and M % tm == 0 and N % tn == 0 and K % tk == 0
    nk = K // tk
    koff = a_col0 // tk
    hd = head_norm[0] if head_norm else 0
    real_d = head_norm[1] if head_norm else 0
    if hd:
        assert tn % max(hd, LANES) == 0

    if a2 is None:
        ins = [a, w]
        specs = [pl.BlockSpec((tm, tk), lambda i, j, k: (i, koff + k))]
    else:
        assert a_col0 == 0 and a_pro is None and a.shape[1] + a2.shape[1] == K
        ins = [a, a2, w]
        specs = [pl.BlockSpec((tm, a.shape[1]), lambda i, j, k: (i, 0)),
                 pl.BlockSpec((tm, a2.shape[1]), lambda i, j, k: (i, 0))]
    specs.append(pl.BlockSpec((tk, tn), lambda i, j, k: (k, j)))
    if a_pro == "rms":
        ins.append(a_gain.reshape(1, K).astype(F32))
        specs.append(pl.BlockSpec((1, tk), lambda i, j, k: (0, k)))
    if bias is not None:
        ins.append(bias.reshape(1, N).astype(F32))
        specs.append(pl.BlockSpec((1, tn), lambda i, j, k: (0, j)))
    if hd:
        ins += [head_norm[2].reshape(1, N).astype(F32), head_norm[3].reshape(1, N).astype(F32)]
        specs += [pl.BlockSpec((1, tn), lambda i, j, k: (0, j))] * 2
    if resid is not None:
        x, gate, rows_per_batch = resid
        assert rows_per_batch % tm == 0
        bpb = rows_per_batch // tm
        ins += [x, gate]
        specs += [pl.BlockSpec((tm, tn), lambda i, j, k: (i, j)),
                  pl.BlockSpec((None, 1, tn), lambda i, j, k: (i // bpb, 0, j))]
    kern = functools.partial(_mm_kernel, nk=nk, a_pro=a_pro, has_a2=a2 is not None, has_bias=bias is not None,
                             hd=hd, real_d=real_d, has_resid=resid is not None)
    return pl.pallas_call(
        kern,
        out_shape=jax.ShapeDtypeStruct((M, N), out_dtype),
        grid=(M // tm, N // tn, nk),
        in_specs=specs,
        out_specs=pl.BlockSpec((tm, tn), lambda i, j, k: (i, j)),
        scratch_shapes=[pltpu.VMEM((tm, tn), F32)] if nk > 1 else [],
        compiler_params=pltpu.CompilerParams(
            dimension_semantics=("parallel", "parallel", "arbitrary")),
        name=name,
    )(*ins)


def _split_bf16(v):
    hi = v.astype(BF16)
    lo = (v - hi.astype(F32)).astype(BF16)
    return hi, lo


def _pack_bf16_pairs(v):
    half = v.shape[1] // 2
    vb = v.astype(BF16).astype(F32)
    lo = lax.shift_right_logical(lax.bitcast_convert_type(vb[:, :half], jnp.int32), 16)
    hi = lax.bitcast_convert_type(vb[:, half:], jnp.int32) & jnp.int32(-65536)
    return hi | lo


def _unpack_bf16_pairs(u):
    lo = lax.bitcast_convert_type(lax.shift_left(u, 16), F32)
    hi = lax.bitcast_convert_type(u & jnp.int32(-65536), F32)
    return jnp.concatenate([lo, hi], axis=1).astype(BF16)


def _norm_mod_kernel(x_ref, g_ref, sc_ref, sh_ref, *rest, route):
    x = x_ref[...]
    r = lax.rsqrt(jnp.mean(x * x, axis=1, keepdims=True) + EPS)
    xm = (x * r * g_ref[...]) * (1.0 + sc_ref[...]) + sh_ref[...]
    if not route:
        (o_ref,) = rest
        o_ref[...] = xm.astype(o_ref.dtype)
        return
    rw_ref, rb_ref, o_ref, e_ref, p_ref, rank_ref, cnt_ref, run_ref = rest
    o_ref[...] = _pack_bf16_pairs(xm)
    a_hi, a_lo = _split_bf16(xm)
    w = rw_ref[...]
    w_hi, w_lo = _split_bf16(w)
    logits = (jnp.dot(a_hi, w_hi, preferred_element_type=F32)
              + jnp.dot(a_hi, w_lo, preferred_element_type=F32)
              + jnp.dot(a_lo, w_hi, preferred_element_type=F32)) + rb_ref[...]
    tt = logits.shape[0]
    lane = lax.broadcasted_iota(jnp.int32, (tt, LANES), 1)
    lane_f = lane.astype(F32)
    work = jnp.where(lane < N_EXPERTS, logits, -jnp.inf)
    e_out = jnp.zeros((tt, LANES), F32)
    v_out = jnp.full((tt, LANES), -jnp.inf, F32)
    hits = []
    for kk in range(TOP_K):
        m = jnp.max(work, axis=1, keepdims=True)
        idx = jnp.min(jnp.where(work == m, lane_f, float(LANES)), axis=1, keepdims=True)
        e_out = jnp.where(lane == kk, idx, e_out)
        v_out = jnp.where(lane == kk, m, v_out)
        hits.append(lane_f == idx)
        work = jnp.where(hits[-1], -jnp.inf, work)
    e_out = e_out.astype(jnp.int32)
    vmax = jnp.max(v_out, axis=1, keepdims=True)
    pe = jnp.exp(v_out - vmax)
    p_ref[...] = pe / jnp.sum(pe, axis=1, keepdims=True)
    e_ref[...] = e_out

    @pl.when(pl.program_id(0) == 0)
    def _():
        run_ref[...] = jnp.zeros_like(run_ref)

    onehot = jnp.zeros((tt, LANES), F32)
    for hit in hits:
        onehot = jnp.where(hit, 1.0, onehot)
    earlier = (lax.broadcasted_iota(jnp.int32, (tt, tt), 1)
               < lax.broadcasted_iota(jnp.int32, (tt, tt), 0))
    prefix = jnp.dot(jnp.where(earlier, 1.0, 0.0).astype(BF16), onehot.astype(BF16),
                     preferred_element_type=F32)
    base = run_ref[...] + prefix
    rank = jnp.zeros((tt, LANES), F32)
    for kk, hit in enumerate(hits):
        rk = jnp.sum(jnp.where(hit, base, 0.0), axis=1, keepdims=True)
        rank = jnp.where(lane == kk, rk, rank)
    rank_ref[...] = rank.astype(jnp.int32)
    total = run_ref[...] + jnp.sum(onehot, axis=0, keepdims=True)
    run_ref[...] = total
    cnt_ref[...] = total.astype(jnp.int32)


def norm_mod(x2d, gain, sc, sh, rows_per_batch, *, out_dtype=BF16, router=None, tt=256):
    N, D = x2d.shape
    tt = _tile(rows_per_batch, tt)
    bpb = rows_per_batch // tt
    ins = [x2d, gain.reshape(1, D), sc, sh]
    specs = [pl.BlockSpec((tt, D), lambda i: (i, 0)),
             pl.BlockSpec((1, D), lambda i: (0, 0)),
             pl.BlockSpec((None, 1, D), lambda i: (i // bpb, 0, 0)),
             pl.BlockSpec((None, 1, D), lambda i: (i // bpb, 0, 0))]
    if router is None:
        out_shape = [jax.ShapeDtypeStruct((N, D), out_dtype)]
        out_specs = [pl.BlockSpec((tt, D), lambda i: (i, 0))]
        scratch = []
    else:
        rw, rb = router
        E = rw.shape[1]
        rw_p = jnp.pad(rw, ((0, 0), (0, LANES - E)))
        rb_p = jnp.pad(rb, (0, LANES - E)).reshape(1, LANES)
        ins += [rw_p, rb_p]
        specs += [pl.BlockSpec((D, LANES), lambda i: (0, 0)),
                  pl.BlockSpec((1, LANES), lambda i: (0, 0))]
        out_shape = [jax.ShapeDtypeStruct((N, D // 2), jnp.int32),
                     jax.ShapeDtypeStruct((N, LANES), jnp.int32),
                     jax.ShapeDtypeStruct((N, LANES), F32),
                     jax.ShapeDtypeStruct((N, LANES), jnp.int32),
                     jax.ShapeDtypeStruct((1, LANES), jnp.int32)]
        row = pl.BlockSpec((tt, LANES), lambda i: (i, 0))
        out_specs = [pl.BlockSpec((tt, D // 2), lambda i: (i, 0)), row, row, row,
                     pl.BlockSpec((1, LANES), lambda i: (0, 0))]
        scratch = [pltpu.VMEM((1, LANES), F32)]
    res = pl.pallas_call(
        functools.partial(_norm_mod_kernel, route=router is not None),
        out_shape=out_shape, grid=(N // tt,), in_specs=specs, out_specs=out_specs,
        scratch_shapes=scratch,
        compiler_params=pltpu.CompilerParams(
            dimension_semantics=("parallel",) if router is None else ("arbitrary",)),
        name="norm_mod_route" if router is not None else "norm_mod",
    )(*ins)
    return res if router is not None else res[0]


def _compress_kernel(h_ref, pe_ref, w1_ref, w2_ref, g_ref, o_ref, xf_ref, *, real_d, norm, nh):
    n_tiles = xf_ref.shape[0]
    for j in range(n_tiles):
        xf_ref[j] = h_ref[:, j * LANES:(j + 1) * LANES].astype(F32)
    half = CMP_BLOCK // 2
    top = bot = None
    for l in range(half):
        parts = [xf_ref[j, pl.ds(l, nh, stride=CMP_STRIDE), :] for j in range(n_tiles)]
        xl = jnp.concatenate(parts, axis=1) if n_tiles > 1 else parts[0]
        t = jnp.dot((xl + pe_ref[l:l + 1, :]).astype(BF16), w1_ref[l], preferred_element_type=F32)
        b = jnp.dot((xl + pe_ref[half + l:half + l + 1, :]).astype(BF16), w1_ref[half + l],
                    preferred_element_type=F32)
        top = t if top is None else top + t
        bot = b if bot is None else bot + b
    hid = top + pltpu.roll(bot, nh - 1, 0)
    hid = hid * jax.nn.sigmoid(hid)
    y = jnp.dot(hid.astype(BF16), w2_ref[...], preferred_element_type=F32)
    if norm:
        r = lax.rsqrt(jnp.sum(y * y, axis=1, keepdims=True) * (1.0 / real_d) + EPS)
        y = y * r * g_ref[...]
    o_ref[...] = y.astype(o_ref.dtype)


def compress(h, col0, B, T, pe, w1, w2, gain, d, dp):
    G = NSA_KV_HEADS
    nh = T // CMP_STRIDE
    cb = col0 // dp
    pe_p = jnp.pad(pe, ((0, 0), (0, dp - d)))
    w1_p = jnp.pad(w1.reshape(CMP_BLOCK, d, d), ((0, 0), (0, dp - d), (0, dp - d))).astype(BF16)
    w2_p = jnp.pad(w2, ((0, dp - d), (0, dp - d))).astype(BF16)
    g_p = (jnp.ones((dp,), F32) if gain is None else jnp.pad(gain, (0, dp - d))).reshape(1, dp)
    full = lambda shape: pl.BlockSpec(shape, lambda b, g: (0,) * len(shape))
    return pl.pallas_call(
        functools.partial(_compress_kernel, real_d=d, norm=gain is not None, nh=nh),
        out_shape=jax.ShapeDtypeStruct((B * G, nh, dp), BF16),
        grid=(B, G),
        in_specs=[pl.BlockSpec((T, dp), lambda b, g: (b, cb + g)),
                  full((CMP_BLOCK, dp)), full((CMP_BLOCK, dp, dp)), full((dp, dp)), full((1, dp))],
        out_specs=pl.BlockSpec((None, nh, dp), lambda b, g: (b * G + g, 0, 0)),
        scratch_shapes=[pltpu.VMEM((dp // LANES, T, LANES), F32)],
        compiler_params=pltpu.CompilerParams(dimension_semantics=("parallel", "parallel")),
        name="nsa_compress",
    )(h, pe_p, w1_p, w2_p, g_p)


def _stack_heads(q_ref, n, w):
    return jnp.concatenate([q_ref[:, r * w:(r + 1) * w] for r in range(n)], axis=0)


def _row_scalars(vals, tq):
    rows = len(vals) * tq
    rid = lax.broadcasted_iota(jnp.int32, (rows, 1), 0) // tq
    col = jnp.full((rows, 1), vals[-1], F32)
    for r in range(len(vals) - 2, -1, -1):
        col = jnp.where(rid == r, vals[r], col)
    return col


def _nt_dot(a, b):
    return lax.dot_general(a, b, (((1,), (1,)), ((), ())), preferred_element_type=F32)


def _cmp_attn_kernel(slope_ref, q_ref, kc_ref, vc_ref, ovl_ref, o_ref, bits_ref, *, tq, R, n_slc):
    g = pl.program_id(1)
    t0 = pl.program_id(2) * tq
    rows = R * tq
    q = _stack_heads(q_ref, R, HEAD_PAD)
    s = _nt_dot(q, kc_ref[...])
    ncol = s.shape[1]
    row_t = t0 + lax.broadcasted_iota(jnp.int32, (rows, ncol), 0) % tq
    n_id = lax.broadcasted_iota(jnp.int32, (rows, ncol), 1)
    dist = row_t - (n_id * CMP_STRIDE + (CMP_BLOCK - 1))
    slope = _row_scalars([slope_ref[g * R + r] for r in range(R)], tq)
    valid = dist >= 0
    s = jnp.where(valid, s - slope * dist.astype(F32), NEG)
    m = jnp.max(s, axis=1, keepdims=True)
    p = jnp.where(valid, jnp.exp(s - m), 0.0)
    p = p / jnp.maximum(jnp.sum(p, axis=1, keepdims=True), 1e-30)
    o = jnp.dot(p.astype(BF16), vc_ref[...], preferred_element_type=F32)
    for r in range(R):
        o_ref[:, r * NSA_V_DIM:(r + 1) * NSA_V_DIM] = o[r * tq:(r + 1) * tq].astype(o_ref.dtype)

    psum = p[0:tq]
    for r in range(1, R):
        psum = psum + p[r * tq:(r + 1) * tq]
    p_hi, p_lo = _split_bf16(psum)
    imp = (jnp.dot(p_hi, ovl_ref[...], preferred_element_type=F32)
           + jnp.dot(p_lo, ovl_ref[...], preferred_element_type=F32))
    lane = lax.broadcasted_iota(jnp.int32, (tq, LANES), 1)
    cur = (t0 + lax.broadcasted_iota(jnp.int32, (tq, LANES), 0)) // SLC_BLOCK
    forced = (lane == 0) | (lane == cur) | (lane == cur - 1)
    work = jnp.where(forced, jnp.inf, jnp.where(lane > cur, -jnp.inf, imp))
    removed = -3.0e38
    work = jnp.where(lane < n_slc, jnp.where(work == -jnp.inf, -2.0e38, work), removed)
    sel = jnp.zeros((tq, LANES), jnp.bool_)
    lane_f = lane.astype(F32)
    for _ in range(min(SLC_TOPN, n_slc)):
        mx = jnp.max(work, axis=1, keepdims=True)
        idx = jnp.min(jnp.where(work == mx, lane_f, float(LANES)), axis=1, keepdims=True)
        hit = lane_f == idx
        sel = sel | hit
        work = jnp.where(hit, removed, work)
    half = 16
    w_lo = jnp.where(sel & (lane < half), jnp.left_shift(1, jnp.minimum(lane, half - 1)), 0)
    w_hi = jnp.where(sel & (lane >= half), jnp.left_shift(1, jnp.clip(lane - half, 0, half - 1)), 0)
    b_lo = jnp.sum(w_lo.astype(F32), axis=1, keepdims=True).astype(jnp.int32)
    b_hi = jnp.sum(w_hi.astype(F32), axis=1, keepdims=True).astype(jnp.int32)
    bits = b_lo | jnp.left_shift(b_hi, half)
    bits_ref[...] = jnp.broadcast_to(bits, (tq, LANES))


def cmp_attention(h, kc, vc, B, T, tq=256):
    G, R = NSA_KV_HEADS, NSA_HEADS // NSA_KV_HEADS
    tq = _tile(T, tq)
    nq = T // tq
    n_slc = T // SLC_BLOCK
    n_cmp = kc.shape[1]
    assert n_slc <= 32 and n_cmp <= LANES and n_cmp % 8 == 0
    cmp_start = np.arange(n_cmp) * CMP_STRIDE
    slc_start = np.arange(LANES) * SLC_BLOCK
    ovl = ((cmp_start[:, None] < slc_start[None, :] + SLC_BLOCK)
           & (cmp_start[:, None] + CMP_BLOCK > slc_start[None, :])
           & (np.arange(LANES)[None, :] < n_slc) & (np.arange(n_cmp)[:, None] < n_cmp - 1))
    ovl = jnp.asarray(ovl, BF16)
    qw = R * HEAD_PAD
    return pl.pallas_call(
        functools.partial(_cmp_attn_kernel, tq=tq, R=R, n_slc=n_slc),
        out_shape=[jax.ShapeDtypeStruct((B * T, NSA_HEADS * NSA_V_DIM), BF16),
                   jax.ShapeDtypeStruct((B, G, T, LANES), jnp.int32)],
        grid=(B, G, nq),
        in_specs=[pl.BlockSpec(memory_space=pltpu.SMEM),
                  pl.BlockSpec((tq, qw), lambda b, g, i: (b * nq + i, EV_Q // qw + g)),
                  pl.BlockSpec((None, n_cmp, HEAD_PAD), lambda b, g, i: (b * G + g, 0, 0)),
                  pl.BlockSpec((None, n_cmp, NSA_V_DIM), lambda b, g, i: (b * G + g, 0, 0)),
                  pl.BlockSpec((n_cmp, LANES), lambda b, g, i: (0, 0))],
        out_specs=[pl.BlockSpec((tq, R * NSA_V_DIM), lambda b, g, i: (b * nq + i, g)),
                   pl.BlockSpec((None, None, tq, LANES), lambda b, g, i: (b, g, i, 0))],
        compiler_params=pltpu.CompilerParams(dimension_semantics=("parallel", "parallel", "parallel")),
        name="nsa_cmp_attn",
    )(_alibi(NSA_HEADS), h, kc, vc, ovl)


ALIBI_LANE0 = 64
SEL_LANE0 = 70


def _key_table(T):
    s = np.arange(T)
    tab = np.zeros((T, LANES), np.float32)
    tab[:, ALIBI_LANE0:ALIBI_LANE0 + 3] = (s // 256 * 256)[:, None]
    tab[:, ALIBI_LANE0 + 3:ALIBI_LANE0 + 6] = (s % 256)[:, None]
    tab[s, SEL_LANE0 + s // SLC_BLOCK] = 1.0
    return jnp.asarray(tab, BF16)


def _slope_parts(n_heads):
    s = _alibi_np(n_heads)
    h1 = s.astype(BF16).astype(np.float32)
    h2 = (s - h1).astype(BF16).astype(np.float32)
    h3 = (s - h1 - h2).astype(BF16).astype(np.float32)
    return jnp.asarray(np.stack([h1, h2, h3], axis=1).reshape(-1))


def _flash_kernel(*refs, tq, tk, R, HP, v_off, v_step, window, alibi, use_bits):
    it = iter(refs)
    slope_ref = next(it) if alibi else None
    q_ref = next(it)
    k_ref = next(it)
    v_ref = next(it)
    ktab_ref = next(it) if alibi else None
    bits_ref = next(it) if use_bits else None
    o_ref = next(it)
    m_ref = next(it)
    acc_ref = next(it)

    g = pl.program_id(1)
    t0 = pl.program_id(2) * tq
    rows = R * tq
    vd = NSA_V_DIM
    lane = lax.broadcasted_iota(jnp.int32, (tq, LANES), 1)
    sel_add = None
    if use_bits:
        j = jnp.clip(lane - SEL_LANE0, 0, 31)
        in_sel = (lane >= SEL_LANE0) & (lane < SEL_LANE0 + 32)
        picked = (jnp.right_shift(bits_ref[...], j) & 1) != 0
        sel_add = jnp.where(in_sel & jnp.logical_not(picked), NEG, 0.0)

    qs = []
    for hp in range(HP):
        pieces = []
        for r in range(R):
            c0 = (hp * R + r) * HEAD_PAD
            q_lo = q_ref[:, c0:c0 + LANES]
            q_hi = q_ref[:, c0 + LANES:c0 + HEAD_PAD]
            if alibi:
                hd = (g * HP + hp) * R + r
                add = jnp.zeros((tq, LANES), F32) if sel_add is None else sel_add
                for part in range(3):
                    sp = slope_ref[3 * hd + part]
                    add = jnp.where((lane == ALIBI_LANE0 + part) | (lane == ALIBI_LANE0 + 3 + part), sp, add)
                q_hi = (q_hi.astype(F32) + add).astype(BF16)
            pieces.append(jnp.concatenate([q_lo, q_hi], axis=1))
        qs.append(jnp.concatenate(pieces, axis=0) if R > 1 else pieces[0])

    rel0 = (lax.broadcasted_iota(jnp.int32, (rows, tk), 0) % tq
            - lax.broadcasted_iota(jnp.int32, (rows, tk), 1))
    ones = jnp.ones((tk, LANES), BF16)
    m_ref[...] = jnp.full(m_ref.shape, M_FLOOR, F32)
    acc_ref[...] = jnp.zeros(acc_ref.shape, F32)

    def step(c, masked):
        s0 = pl.multiple_of(c * tk, tk)
        off = t0 - s0
        if masked:
            valid = rel0 >= -off
            if window is not None:
                valid = valid & (rel0 < window - off)
        for hp in range(HP):
            kc = k_ref[pl.ds(s0, tk), hp * HEAD_PAD:(hp + 1) * HEAD_PAD]
            if alibi:
                kc = jnp.concatenate([kc[:, :LANES], kc[:, LANES:] + ktab_ref[pl.ds(s0, tk), :]], axis=1)
            vc = v_ref[pl.ds(s0, tk), v_off + hp * v_step:v_off + hp * v_step + vd]
            s = _nt_dot(qs[hp], kc)
            if masked:
                s = jnp.where(valid, s, NEG)
            tiles = [s[:, j * LANES:(j + 1) * LANES] for j in range(tk // LANES)]
            mx = tiles[0]
            for t in tiles[1:]:
                mx = jnp.maximum(mx, t)
            m_old = m_ref[hp]
            m_new = jnp.maximum(m_old, jnp.max(mx, axis=1, keepdims=True))
            alpha = jnp.exp(m_old - m_new)
            p = jnp.concatenate([jnp.exp(t - m_new).astype(BF16) for t in tiles], axis=1)
            pv = jnp.dot(p, jnp.concatenate([vc, ones], axis=1), preferred_element_type=F32)
            acc_ref[hp] = jnp.concatenate([alpha, alpha], axis=1) * acc_ref[hp] + pv
            m_ref[hp] = m_new

    def body(c, carry):
        s0 = c * tk
        full = s0 + tk - 1 <= t0
        if window is not None:
            full = full & (s0 >= t0 + tq - window)
        lax.cond(full, lambda: step(c, False), lambda: step(c, True))
        return carry

    c_hi = (t0 + tq - 1) // tk
    c_lo = 0 if window is None else jnp.maximum((t0 - window + 1) // tk, 0)
    lax.fori_loop(c_lo, c_hi + 1, body, 0)
    for hp in range(HP):
        acc = acc_ref[hp]
        o = acc[:, :vd] / jnp.maximum(acc[:, vd:], 1e-30)
        for r in range(R):
            c0 = (hp * R + r) * vd
            o_ref[:, c0:c0 + vd] = o[r * tq:(r + 1) * tq].astype(o_ref.dtype)


def flash_attention(q_arr, q_col0, k_arr, k_col0, v_arr, v_col0, *, B, T, G, R, HP=1, v_width=NSA_V_DIM,
                    v_off=0, v_step=0, out_cols, out_col0=0, window=None, n_alibi_heads=0, bits=None,
                    tq=256, tk=256, name="flash"):
    vd = NSA_V_DIM
    tq = _tile(T, tq)
    tk = _tile(T, tk)
    nq = T // tq
    qw, kw, ow = HP * R * HEAD_PAD, HP * HEAD_PAD, HP * R * vd
    assert q_col0 % qw == 0 and k_col0 % kw == 0 and v_col0 % v_width == 0 and out_col0 % ow == 0
    assert bits is None or n_alibi_heads
    qb, kb, vb, ob = q_col0 // qw, k_col0 // kw, v_col0 // v_width, out_col0 // ow
    alibi = n_alibi_heads > 0
    ins, specs = [], []
    if alibi:
        ins.append(_slope_parts(n_alibi_heads))
        specs.append(pl.BlockSpec(memory_space=pltpu.SMEM))
    ins += [q_arr, k_arr, v_arr]
    specs += [pl.BlockSpec((tq, qw), lambda b, g, i: (b * nq + i, qb + g)),
              pl.BlockSpec((T, kw), lambda b, g, i: (b, kb + g)),
              pl.BlockSpec((T, v_width), lambda b, g, i: (b, vb + g))]
    if alibi:
        ins.append(_key_table(T))
        specs.append(pl.BlockSpec((T, LANES), lambda b, g, i: (0, 0)))
    if bits is not None:
        ins.append(bits)
        specs.append(pl.BlockSpec((None, None, tq, LANES), lambda b, g, i: (b, g, i, 0)))
    rows = R * tq
    return pl.pallas_call(
        functools.partial(_flash_kernel, tq=tq, tk=tk, R=R, HP=HP, v_off=v_off, v_step=v_step,
                          window=window, alibi=alibi, use_bits=bits is not None),
        out_shape=jax.ShapeDtypeStruct((B * T, out_cols), BF16),
        grid=(B, G // HP, nq),
        in_specs=specs,
        out_specs=pl.BlockSpec((tq, ow), lambda b, g, i: (b * nq + i, ob + g)),
        scratch_shapes=[pltpu.VMEM((HP, rows, LANES), F32), pltpu.VMEM((HP, rows, 2 * vd), F32)],
        compiler_params=pltpu.CompilerParams(dimension_semantics=("parallel", "parallel", "parallel")),
        name=name,
    )(*ins)


def _nsa_combine_kernel(gate_ref, oc_ref, os_ref, ow_ref, o_ref):
    gates = jax.nn.sigmoid(gate_ref[...].astype(F32))
    for h in range(NSA_HEADS):
        sl = slice(h * NSA_V_DIM, (h + 1) * NSA_V_DIM)
        acc = None
        for br, ref in enumerate((oc_ref, os_ref, ow_ref)):
            gcol = gates[:, 3 * h + br:3 * h + br + 1]
            term = gcol * ref[:, sl].astype(F32)
            acc = term if acc is None else acc + term
        o_ref[:, sl] = acc.astype(o_ref.dtype)


def nsa_combine(h, o_cmp, o_slc, o_win, tt=256):
    N, W = o_cmp.shape
    tt = _tile(N, tt)
    gb = EV_GATE // LANES
    blk = pl.BlockSpec((tt, W), lambda i: (i, 0))
    return pl.pallas_call(
        _nsa_combine_kernel,
        out_shape=jax.ShapeDtypeStruct((N, W), BF16),
        grid=(N // tt,),
        in_specs=[pl.BlockSpec((tt, LANES), lambda i: (i, gb)), blk, blk, blk],
        out_specs=blk,
        compiler_params=pltpu.CompilerParams(dimension_semantics=("parallel",)),
        name="nsa_combine",
    )(h, o_cmp, o_slc, o_win)


def _mla_prep_kernel(*refs, shared_rope):
    if shared_rope:
        x_ref, hi_ref, alo_ref, ahi_ref, bhi_ref, o_ref = refs
    else:
        x_ref, alo_ref, ahi_ref, bhi_ref, o_ref = refs
    tt = x_ref.shape[0]
    is_rope = lax.broadcasted_iota(jnp.int32, (tt, LANES), 1) < MLA_ROPE_DIM
    a_hi, b_hi, a_lo = ahi_ref[...], bhi_ref[...], alo_ref[...]

    def rope_part(x_hi):
        ss = jnp.sum(jnp.where(is_rope, x_hi * x_hi, 0.0), axis=1, keepdims=True)
        return x_hi * a_hi + pltpu.roll(x_hi * b_hi, MLA_ROPE_DIM, 1), ss

    if shared_rope:
        y_hi, ss_hi = rope_part(hi_ref[...].astype(F32))
    for h in range(MLA_HEADS):
        c = h * HEAD_PAD
        x_lo = x_ref[:, c:c + LANES].astype(F32)
        if not shared_rope:
            y_hi, ss_hi = rope_part(x_ref[:, c + LANES:c + HEAD_PAD].astype(F32))
        ss = jnp.sum(x_lo * x_lo, axis=1, keepdims=True) + ss_hi
        r = lax.rsqrt(ss * (1.0 / (MLA_NOPE_DIM + MLA_ROPE_DIM)) + EPS)
        o_ref[:, c:c + LANES] = (x_lo * r * a_lo).astype(o_ref.dtype)
        o_ref[:, c + LANES:c + HEAD_PAD] = (y_hi * r).astype(o_ref.dtype)


def mla_prep(x_arr, rope_arr, rope_col0, a_lo, a_hi, b_hi, T, tt=256):
    N = x_arr.shape[0]
    W = MLA_HEADS * HEAD_PAD
    tt = _tile(T, tt)
    nt = T // tt
    shared = rope_arr is not None
    ins = [x_arr]
    specs = [pl.BlockSpec((tt, W), lambda i: (i, 0))]
    if shared:
        rb = rope_col0 // LANES
        ins.append(rope_arr)
        specs.append(pl.BlockSpec((tt, LANES), lambda i: (i, rb)))
    ins += [a_lo, a_hi, b_hi]
    specs += [pl.BlockSpec((1, LANES), lambda i: (0, 0)),
              pl.BlockSpec((tt, LANES), lambda i: (i % nt, 0)),
              pl.BlockSpec((tt, LANES), lambda i: (i % nt, 0))]
    return pl.pallas_call(
        functools.partial(_mla_prep_kernel, shared_rope=shared),
        out_shape=jax.ShapeDtypeStruct((N, W), BF16),
        grid=(N // tt,),
        in_specs=specs,
        out_specs=pl.BlockSpec((tt, W), lambda i: (i, 0)),
        compiler_params=pltpu.CompilerParams(dimension_semantics=("parallel",)),
        name="mla_prep",
    )(*ins)


def _rope_tables(gain, T, scale):
    half = MLA_ROPE_DIM // 2
    inv_freq = ROPE_THETA ** (-jnp.arange(half, dtype=F32) / half)
    ang = jnp.arange(T, dtype=F32)[:, None] * inv_freq[None, :]
    cos2 = jnp.concatenate([jnp.cos(ang), jnp.cos(ang)], axis=1)
    sin_s = jnp.concatenate([-jnp.sin(ang), jnp.sin(ang)], axis=1)
    g_nope, g_rope = gain[:MLA_NOPE_DIM], gain[MLA_NOPE_DIM:]
    g_perm = jnp.concatenate([g_rope[half:], g_rope[:half]])
    zeros = jnp.zeros((T, MLA_ROPE_DIM), F32)
    a_lo = (g_nope * scale).reshape(1, LANES)
    a_hi = jnp.concatenate([g_rope[None, :] * cos2 * scale, zeros], axis=1)
    b_hi = jnp.concatenate([zeros, g_perm[None, :] * sin_s * scale], axis=1)
    return a_lo, a_hi, b_hi


def _swa_kernel(qadd_ref, slope_ref, sink_ref, q_ref, k_ref, v_ref, ktab_ref, o_ref, mask_ref,
                *, tq, tk, R):
    t0 = pl.program_id(2) * tq
    npair = R // 2
    rows = npair * tq
    qp = _stack_heads(q_ref, npair, LANES)
    lane = lax.broadcasted_iota(jnp.int32, (rows, LANES), 1)
    first = lane < SWA_HEAD_DIM
    start = pl.multiple_of(jnp.maximum(t0 - SWA_WINDOW, 0), LANES)
    own_half = lax.broadcasted_iota(jnp.int32, (tk, LANES), 1) // SWA_HEAD_DIM == pl.program_id(1) % 2

    def both_halves(ref):
        pair = ref[pl.ds(start, tk), :].astype(F32)
        return jnp.where(own_half, pair, pltpu.roll(pair, SWA_HEAD_DIM, 1)).astype(BF16)

    kc = jnp.concatenate([both_halves(k_ref), ktab_ref[pl.ds(start, tk), :]], axis=1)
    v_aug = jnp.concatenate([both_halves(v_ref), jnp.ones((tk, LANES), BF16)], axis=1)
    @pl.when(pl.program_id(2) <= 1)
    def _():
        row_in = lax.broadcasted_iota(jnp.int32, (2 * rows, tk), 0) % tq
        rel = t0 - start + row_in - lax.broadcasted_iota(jnp.int32, (2 * rows, tk), 1)
        mask_ref[...] = jnp.where((rel >= 0) & (rel < SWA_WINDOW), 0.0, NEG)

    t_row = (t0 + lax.broadcasted_iota(jnp.int32, (2 * rows, LANES), 0) % tq).astype(F32)
    zero = jnp.zeros_like(qp)
    qu = jnp.concatenate([jnp.where(first, qp, zero), jnp.where(first, zero, qp)], axis=0)
    qa = jnp.concatenate([qu, qadd_ref[...].reshape(2 * rows, LANES)], axis=1)
    sink_t = (sink_ref[...].reshape(2 * rows, LANES)
              + slope_ref[...].reshape(2 * rows, LANES) * t_row)
    s = _nt_dot(qa, kc) + mask_ref[...]
    tiles = [s[:, j * LANES:(j + 1) * LANES] for j in range(tk // LANES)]
    mx = tiles[0]
    for t in tiles[1:]:
        mx = jnp.maximum(mx, t)
    m = jnp.maximum(jnp.max(mx, axis=1, keepdims=True), sink_t)
    p = jnp.concatenate([jnp.exp(t - m).astype(BF16) for t in tiles], axis=1)
    pv = jnp.dot(p, v_aug, preferred_element_type=F32)
    denom = pv[:, LANES:] + jnp.exp(sink_t - m)
    both = pv[:, :LANES] / jnp.maximum(denom, 1e-30)
    o = jnp.where(first, both[:rows], both[rows:])
    for p_ in range(npair):
        o_ref[:, p_ * LANES:(p_ + 1) * LANES] = o[p_ * tq:(p_ + 1) * tq].astype(o_ref.dtype)


def swa_attention(h, sinks, B, T, k_col0, v_col0, tq=128):
    G, R = SWA_KV_HEADS, SWA_HEADS // SWA_KV_HEADS
    tq = _tile(T, tq)
    tk = min(T, tq + SWA_WINDOW)
    assert tk % LANES == 0 and tq % LANES == 0
    nq = T // tq
    qw = R * SWA_HEAD_DIM
    kb, vb = k_col0 // LANES, v_col0 // LANES
    npair = R // 2
    rows = npair * tq
    per_head = lambda v: jnp.repeat(v.reshape(G, npair, 2).transpose(0, 2, 1), tq, axis=2)
    lanes = lambda v: jnp.broadcast_to(v[..., None], (G, 2, rows, LANES))
    parts = _slope_parts(SWA_HEADS).reshape(SWA_HEADS, 3)
    lane_id = np.arange(LANES)
    qadd = jnp.zeros((G, 2, rows, LANES), F32)
    for part in range(3):
        at_lane = (lane_id == ALIBI_LANE0 + part) | (lane_id == ALIBI_LANE0 + 3 + part)
        qadd = jnp.where(at_lane[None, None, None, :], lanes(per_head(parts[:, part])), qadd)
    slope_rows = lanes(per_head(_alibi(SWA_HEADS)))
    sink_rows = lanes(per_head(sinks.astype(F32)))
    const_spec = pl.BlockSpec((None, 2, rows, LANES), lambda b, g, i: (g, 0, 0, 0))
    return pl.pallas_call(
        functools.partial(_swa_kernel, tq=tq, tk=tk, R=R),
        out_shape=jax.ShapeDtypeStruct((B * T, SWA_HEADS * SWA_HEAD_DIM), BF16),
        grid=(B, G, nq),
        in_specs=[const_spec, const_spec, const_spec,
                  pl.BlockSpec((tq, qw), lambda b, g, i: (b * nq + i, g)),
                  pl.BlockSpec((T, LANES), lambda b, g, i: (b, kb + g // 2)),
                  pl.BlockSpec((T, LANES), lambda b, g, i: (b, vb + g // 2)),
                  pl.BlockSpec((T, LANES), lambda b, g, i: (0, 0))],
        out_specs=pl.BlockSpec((tq, qw), lambda b, g, i: (b * nq + i, g)),
        scratch_shapes=[pltpu.VMEM((2 * rows, tk), F32)],
        compiler_params=pltpu.CompilerParams(dimension_semantics=("parallel", "parallel", "arbitrary")),
        name="swa_attn",
    )(qadd.astype(BF16), slope_rows, sink_rows, h, h, h, _key_table(T))


def _dispatch_kernel(zfrom_ref, zto_ref, pos_ref, src_ref, dst_hbm, zrow, sem, zsem, *, td, n_seg):
    @pl.when(pl.program_id(0) == 0)
    def _():
        zrow[...] = jnp.zeros_like(zrow)

        def zero_copy(r):
            return pltpu.make_async_copy(zrow, dst_hbm.at[pl.ds(r, 1)], zsem)

        def seg(s, c):
            lax.fori_loop(zfrom_ref[s], zto_ref[s], lambda r, cc: (zero_copy(r).start(), cc)[1], 0)
            return c

        def seg_wait(s, c):
            lax.fori_loop(zfrom_ref[s], zto_ref[s], lambda r, cc: (zero_copy(r).wait(), cc)[1], 0)
            return c

        lax.fori_loop(0, n_seg, seg, 0)
        lax.fori_loop(0, n_seg, seg_wait, 0)

    def row_copy(t, k):
        return pltpu.make_async_copy(src_ref.at[pl.ds(t, 1)],
                                     dst_hbm.at[pl.ds(pos_ref[0, t * TOP_K + k], 1)], sem)

    def issue(t, c):
        for k in range(TOP_K):
            row_copy(t, k).start()
        return c

    def drain(t, c):
        for k in range(TOP_K):
            row_copy(t, k).wait()
        return c

    lax.fori_loop(0, td, issue, 0)
    lax.fori_loop(0, td, drain, 0)


def moe_dispatch(xm_packed, pos, zfrom, zto, cap, td=128):
    N, W = xm_packed.shape
    td = _tile(N, td)
    pos3 = pos.reshape(N // td, 1, td * TOP_K)
    grid_spec = pltpu.PrefetchScalarGridSpec(
        num_scalar_prefetch=2,
        grid=(N // td,),
        in_specs=[pl.BlockSpec((None, 1, td * TOP_K), lambda i, zf, zt: (i, 0, 0), memory_space=pltpu.SMEM),
                  pl.BlockSpec((td, W), lambda i, zf, zt: (i, 0))],
        out_specs=pl.BlockSpec(memory_space=pl.ANY),
        scratch_shapes=[pltpu.VMEM((1, W), jnp.int32), pltpu.SemaphoreType.DMA(()),
                        pltpu.SemaphoreType.DMA(())],
    )
    return pl.pallas_call(
        functools.partial(_dispatch_kernel, td=td, n_seg=zfrom.shape[0]),
        out_shape=jax.ShapeDtypeStruct((cap, W), jnp.int32),
        grid_spec=grid_spec,
        compiler_params=pltpu.CompilerParams(dimension_semantics=("arbitrary",), has_side_effects=True),
        name="moe_dispatch",
    )(zfrom, zto, pos3, xm_packed)


def _expert_kernel(be_ref, slot_ref, nxt_ref, c0_ref, c1_ref, nu_ref, xs_ref, wg_hbm, bg_ref, wu_hbm, bu_ref,
                   wd_hbm, bd_ref, o_ref, wg_s, wu_s, wd_s, stg_a, stg_d, sem, *, ca, cd, ring, layer):
    blk = pl.program_id(0)
    e = be_ref[blk]
    s = slot_ref[blk]
    used = blk < nu_ref[0]
    _, D, F = wg_s.shape
    na, nd = D // ca, F // cd
    n_chunks = 2 * na + nd

    def on_chunk(c, expert, fn):
        k = c % ring

        def go(src, dst, r0, n, stg, sem0):
            copy = pltpu.make_async_copy(src.at[layer, expert, pl.ds(r0, n), :], stg.at[k], sem.at[sem0 + k])
            fn(copy, dst, r0, n, stg.at[k])

        @pl.when(c < na)
        def _():
            go(wg_hbm, wg_s, pl.multiple_of(c * ca, ca), ca, stg_a, 0)

        @pl.when((c >= na) & (c < 2 * na))
        def _():
            go(wu_hbm, wu_s, pl.multiple_of((c - na) * ca, ca), ca, stg_a, 0)

        @pl.when(c >= 2 * na)
        def _():
            go(wd_hbm, wd_s, pl.multiple_of((c - 2 * na) * cd, cd), cd, stg_d, ring)

    def start(c, expert):
        on_chunk(c, expert, lambda copy, dst, r0, n, stg: copy.start())

    def finish(c, expert, dst_slot):
        def fn(copy, dst, r0, n, stg):
            copy.wait()
            dst[dst_slot, pl.ds(r0, n), :] = stg[...].astype(BF16)
        on_chunk(c, expert, fn)

    def prestart(lo, hi, expert):
        for i in range(ring):
            @pl.when(lo + i < hi)
            def _():
                start(lo + i, expert)

    def drain(lo, hi, end, expert, dst_slot):
        def body(c, carry):
            finish(c, expert, dst_slot)

            @pl.when(c + ring < end)
            def _():
                start(c + ring, expert)
            return carry

        lax.fori_loop(lo, hi, body, 0)

    @pl.when(used & (blk == 0))
    def _():
        prestart(0, n_chunks, e)
        drain(0, n_chunks, n_chunks, e, s)

    c0, c1, nxt = c0_ref[blk], c1_ref[blk], nxt_ref[blk]

    @pl.when(used)
    def _():
        prestart(c0, c1, nxt)
        mid1 = jnp.minimum(c0 + ring, c1)
        mid2 = jnp.minimum(c0 + 2 * ring, c1)
        x = _unpack_bf16_pairs(xs_ref[...])
        gg = jnp.dot(x, wg_s[s], preferred_element_type=F32) + bg_ref[...]
        drain(c0, mid1, c1, nxt, 1 - s)
        uu = jnp.dot(x, wu_s[s], preferred_element_type=F32) + bu_ref[...]
        drain(mid1, mid2, c1, nxt, 1 - s)
        gg = jnp.minimum(gg, SWIGLU_LIMIT)
        uu = jnp.clip(uu, -SWIGLU_LIMIT, SWIGLU_LIMIT)
        act = gg * jax.nn.sigmoid(SWIGLU_ALPHA * gg) * (uu + 1.0)
        y = jnp.dot(act.astype(BF16), wd_s[s], preferred_element_type=F32) + bd_ref[...]
        o_ref[...] = _pack_bf16_pairs(y)
        drain(mid2, c1, c1, nxt, 1 - s)

    @pl.when(jnp.logical_not(used))
    def _():
        o_ref[...] = jnp.zeros_like(o_ref)


EXPERT_STAGE_RING = 4
VMEM_LIMIT_EXPERTS = 60 * 1024 * 1024


def _expert_schedule(blk_e, n_used, n_chunks):
    n_blk = blk_e.shape[0]
    idx = jnp.arange(n_blk, dtype=jnp.int32)
    first = jnp.concatenate([jnp.ones((1,), jnp.int32), (blk_e[1:] != blk_e[:-1]).astype(jnp.int32)])
    slot = (jnp.cumsum(first) - 1) % 2
    run_start = lax.cummax(jnp.where(first == 1, idx, 0))
    starts_after = jnp.concatenate([jnp.where(first == 1, idx, n_blk)[1:], jnp.full((1,), n_blk, jnp.int32)])
    next_start = lax.cummin(starts_after, reverse=True)
    has_next = next_start < n_used[0]
    nxt = blk_e[jnp.minimum(next_start, n_blk - 1)]
    run_len = jnp.maximum(jnp.minimum(next_start, n_used[0]) - run_start, 1)
    j = idx - run_start
    c0 = jnp.where(has_next, j * n_chunks // run_len, 0)
    c1 = jnp.where(has_next, (j + 1) * n_chunks // run_len, 0)
    as_i32 = lambda v: v.astype(jnp.int32)
    return as_i32(slot), as_i32(nxt), as_i32(c0), as_i32(c1)


def moe_experts(xs, blk_e, n_used, layer, wg, bg, wu, bu, wd, bd, rb=MOE_ROW_BLOCK):
    cap, W = xs.shape
    depth, E, D, F = wg.shape
    n_blk = cap // rb
    ca, cd = _tile(D, 256), _tile(F, 64)
    ring = EXPERT_STAGE_RING
    slot, nxt, c0, c1 = _expert_schedule(blk_e, n_used, 2 * (D // ca) + F // cd)
    by_expert = lambda shape: pl.BlockSpec((None,) + shape,
                                           lambda i, be, *_: (layer * E + be[i], 0, 0))
    grid_spec = pltpu.PrefetchScalarGridSpec(
        num_scalar_prefetch=6,
        grid=(n_blk,),
        in_specs=[pl.BlockSpec((rb, W), lambda i, *_: (i, 0)),
                  pl.BlockSpec(memory_space=pl.ANY), by_expert((1, F)),
                  pl.BlockSpec(memory_space=pl.ANY), by_expert((1, F)),
                  pl.BlockSpec(memory_space=pl.ANY), by_expert((1, D))],
        out_specs=pl.BlockSpec((rb, W), lambda i, *_: (i, 0)),
        scratch_shapes=[pltpu.VMEM((2, D, F), BF16), pltpu.VMEM((2, D, F), BF16), pltpu.VMEM((2, F, D), BF16),
                        pltpu.VMEM((ring, ca, F), F32), pltpu.VMEM((ring, cd, D), F32),
                        pltpu.SemaphoreType.DMA((2 * ring,))],
    )
    return pl.pallas_call(
        functools.partial(_expert_kernel, ca=ca, cd=cd, ring=ring, layer=layer),
        out_shape=jax.ShapeDtypeStruct((cap, W), jnp.int32),
        grid_spec=grid_spec,
        compiler_params=pltpu.CompilerParams(dimension_semantics=("arbitrary",),
                                             vmem_limit_bytes=VMEM_LIMIT_EXPERTS),
        name="moe_experts",
    )(blk_e, slot, nxt, c0, c1, n_used, xs, wg, bg.reshape(depth * E, 1, F), wu, bu.reshape(depth * E, 1, F),
      wd, bd.reshape(depth * E, 1, D))


def _moe_combine_kernel(pos_ref, pos_next_ref, y_hbm, x_ref, w_ref, gate_ref, o_ref, ybuf, sem,
                        *, tt, n_steps):
    i = pl.program_id(0)
    slot = i % 2

    def row_copy(p_ref, t, k, sl):
        return pltpu.make_async_copy(y_hbm.at[pl.ds(p_ref[0, t * TOP_K + k], 1)],
                                     ybuf.at[sl, k, pl.ds(t, 1)], sem.at[sl])

    def issue(p_ref, sl):
        def body(t, c):
            for k in range(TOP_K):
                row_copy(p_ref, t, k, sl).start()
            return c
        lax.fori_loop(0, tt, body, 0)

    @pl.when(i == 0)
    def _():
        issue(pos_ref, 0)

    @pl.when(i + 1 < n_steps)
    def _():
        issue(pos_next_ref, 1 - slot)

    def drain(t, c):
        for k in range(TOP_K):
            row_copy(pos_ref, t, k, slot).wait()
        return c

    lax.fori_loop(0, tt, drain, 0)
    w = w_ref[...]
    half = x_ref.shape[1] // 2
    y_lo = y_hi = None
    for k in range(TOP_K):
        u = ybuf[slot, k]
        wk = w[:, k:k + 1]
        lo = wk * lax.bitcast_convert_type(lax.shift_left(u, 16), F32)
        hi = wk * lax.bitcast_convert_type(u & jnp.int32(-65536), F32)
        y_lo = lo if y_lo is None else y_lo + lo
        y_hi = hi if y_hi is None else y_hi + hi
    o_ref[:, :half] = x_ref[:, :half] + gate_ref[:, :half] * y_lo
    o_ref[:, half:] = x_ref[:, half:] + gate_ref[:, half:] * y_hi


def moe_combine(yb, pos, top_w, x2d, gate, rows_per_batch, tt=64):
    N, D = x2d.shape
    tt = _tile(rows_per_batch, tt)
    bpb = rows_per_batch // tt
    n_steps = N // tt
    pos3 = pos.reshape(n_steps, 1, tt * TOP_K)
    pos_spec = lambda f: pl.BlockSpec((None, 1, tt * TOP_K), f, memory_space=pltpu.SMEM)
    return pl.pallas_call(
        functools.partial(_moe_combine_kernel, tt=tt, n_steps=n_steps),
        out_shape=jax.ShapeDtypeStruct((N, D), F32),
        grid=(n_steps,),
        in_specs=[pos_spec(lambda i: (i, 0, 0)),
                  pos_spec(lambda i: (jnp.minimum(i + 1, n_steps - 1), 0, 0)),
                  pl.BlockSpec(memory_space=pl.ANY),
                  pl.BlockSpec((tt, D), lambda i: (i, 0)),
                  pl.BlockSpec((tt, LANES), lambda i: (i, 0)),
                  pl.BlockSpec((None, 1, D), lambda i: (i // bpb, 0, 0))],
        out_specs=pl.BlockSpec((tt, D), lambda i: (i, 0)),
        scratch_shapes=[pltpu.VMEM((2, TOP_K, tt, D // 2), jnp.int32), pltpu.SemaphoreType.DMA((2,))],
        compiler_params=pltpu.CompilerParams(dimension_semantics=("arbitrary",)),
        name="moe_combine",
    )(pos3, pos3, yb, x2d, top_w, gate)


def moe_layer(x2d, gain, sc, sh, gate, rows_per_batch, router_w, router_b, layer, wg, bg, wu, bu, wd, bd):
    N, D = x2d.shape
    RB = MOE_ROW_BLOCK
    E = N_EXPERTS
    xm_packed, top_e, top_w, rank, counts = norm_mod(x2d, gain, sc, sh, rows_per_batch,
                                                     router=(router_w, router_b))
    nk = N * TOP_K
    counts = counts[0, :E]
    padded = (counts + RB - 1) // RB * RB
    pad_end = jnp.cumsum(padded)
    pad_start = pad_end - padded
    e4 = top_e[:, :TOP_K]
    start_of = jnp.sum(jnp.where(e4[:, :, None] == jnp.arange(E)[None, None, :],
                                 pad_start[None, None, :], 0), axis=2)
    pos = (start_of + rank[:, :TOP_K]).astype(jnp.int32)
    cap = (-(-nk // RB)) * RB + E * RB
    n_blk = cap // RB
    blk_start = jnp.arange(n_blk, dtype=jnp.int32) * RB
    blk_e = jnp.minimum(jnp.sum(pad_end[None, :] <= blk_start[:, None], axis=1), E - 1).astype(jnp.int32)
    n_used = (pad_end[-1] // RB).astype(jnp.int32).reshape(1)
    zfrom = jnp.concatenate([pad_start + counts, pad_end[-1:]]).astype(jnp.int32)
    zto = jnp.concatenate([pad_end, jnp.full((1,), cap)]).astype(jnp.int32)
    xs = moe_dispatch(xm_packed, pos, zfrom, zto, cap)
    yb = moe_experts(xs, blk_e, n_used, layer, wg, bg, wu, bu, wd, bd)
    return moe_combine(yb, pos, top_w, x2d, gate, rows_per_batch)


def _pad_heads(w, n_heads, d, dp):
    lead = w.shape[:-1]
    w = w.reshape(lead + (n_heads, d))
    w = jnp.pad(w, [(0, 0)] * len(lead) + [(0, 0), (0, dp - d)])
    return w.reshape(lead + (n_heads * dp,))


def _even_in_weights(w_in, q_gain, k_gain):
    D = w_in.shape[0]
    cuts = np.cumsum(EVEN_IN_SPLITS)[:-1].tolist()
    q, kc, vc, ks, vs, kw, vw, gates, cq, kva = jnp.split(w_in.astype(BF16), cuts, axis=1)
    half = MLA_ROPE_DIM // 2
    kr = kva[:, MLA_KV_RANK:]
    kr_perm = jnp.concatenate([kr[:, half:], kr[:, :half]], axis=1)
    G = NSA_KV_HEADS
    cols = [_pad_heads(q, NSA_HEADS, NSA_QK_DIM, HEAD_PAD),
            _pad_heads(kc, G, NSA_QK_DIM, HEAD_PAD), _pad_heads(ks, G, NSA_QK_DIM, HEAD_PAD),
            _pad_heads(kw, G, NSA_QK_DIM, HEAD_PAD), cq, kva[:, :MLA_KV_RANK], kr, kr_perm,
            vc, vs, vw, gates]
    cols.append(jnp.zeros((D, EV_WIDTH - EV_END + LANES - gates.shape[1]), BF16))
    w = jnp.concatenate(cols, axis=1)
    assert w.shape[1] == EV_WIDTH
    scale = NSA_QK_DIM ** -0.5
    pad_g = lambda g: jnp.pad(g, (0, HEAD_PAD - NSA_QK_DIM))
    gain = jnp.ones((EV_WIDTH,), F32)
    gain = gain.at[EV_Q:EV_Q + _NQ].set(jnp.tile(pad_g(q_gain * scale), NSA_HEADS))
    gain = gain.at[EV_KS:EV_KS + _NK].set(jnp.tile(pad_g(k_gain[1]), G))
    gain = gain.at[EV_KW:EV_KW + _NK].set(jnp.tile(pad_g(k_gain[2]), G))
    col = np.arange(EV_WIDTH)
    flag = ((col < EV_Q + _NQ) | ((col >= EV_KS) & (col < EV_KW + _NK))).astype(np.float32)
    return w, gain, jnp.asarray(flag)


def _even_mixer(xm, x2d, g_a, B, T, w_in, w_out, q_gain, k_gain, pe_k, pe_v, w_ck1, w_ck2, w_cv1, w_cv2,
                g_cq, g_ckv, w_uq, w_ukv, mq_gain, mk_gain):
    N, D = x2d.shape
    G, R = NSA_KV_HEADS, NSA_HEADS // NSA_KV_HEADS
    w_p, gain, flag = _even_in_weights(w_in, q_gain, k_gain)
    h = matmul(xm, w_p, head_norm=(HEAD_PAD, NSA_QK_DIM, gain, flag), name="in_proj_even")

    kc = compress(h, EV_KC, B, T, pe_k, w_ck1, w_ck2, k_gain[0], NSA_QK_DIM, HEAD_PAD)
    vc = compress(h, EV_VC, B, T, pe_v, w_cv1, w_cv2, None, NSA_V_DIM, NSA_V_DIM)

    o_cmp, bits = cmp_attention(h, kc, vc, B, T)
    HV = NSA_HEADS * NSA_V_DIM
    o_slc = flash_attention(h, EV_Q, h, EV_KS, h, EV_VS, B=B, T=T, G=G, R=R, out_cols=HV,
                            n_alibi_heads=NSA_HEADS, bits=bits, tq=512, tk=512, name="nsa_slc_attn")
    o_win = flash_attention(h, EV_Q, h, EV_KW, h, EV_VW, B=B, T=T, G=G, R=R, out_cols=HV,
                            window=NSA_WINDOW, n_alibi_heads=NSA_HEADS, name="nsa_win_attn")
    o_a = nsa_combine(h, o_cmp, o_slc, o_win)

    H = MLA_HEADS
    dqk = MLA_NOPE_DIM + MLA_ROPE_DIM
    half = MLA_ROPE_DIM // 2
    wq = w_uq.reshape(MLA_Q_RANK, H, dqk)
    wq_rope = wq[:, :, MLA_NOPE_DIM:]
    wq_p = jnp.concatenate([wq, wq_rope[:, :, half:], wq_rope[:, :, :half]], axis=2)
    wq_p = wq_p.reshape(MLA_Q_RANK, H * HEAD_PAD).astype(BF16)
    q_raw = matmul(h, wq_p, a_col0=EV_CQ, a_pro="rms", a_gain=g_cq, name="mla_q_up")
    kv_raw = matmul(h, w_ukv.astype(BF16), a_col0=EV_CKV, a_pro="rms", a_gain=g_ckv, name="mla_kv_up")
    qa_lo, qa_hi, qb_hi = _rope_tables(mq_gain, T, dqk ** -0.5)
    ka_lo, ka_hi, kb_hi = _rope_tables(mk_gain, T, 1.0)
    q_m = mla_prep(q_raw, None, 0, qa_lo, qa_hi, qb_hi, T)
    k_m = mla_prep(kv_raw, h, EV_KR, ka_lo, ka_hi, kb_hi, T)
    hp = 4
    kv_w = MLA_NOPE_DIM + MLA_V_DIM
    o_b = flash_attention(q_m, 0, k_m, 0, kv_raw, 0, B=B, T=T, G=H, R=1, HP=hp, v_width=hp * kv_w,
                          v_off=MLA_NOPE_DIM, v_step=kv_w, out_cols=H * MLA_V_DIM, tq=512, tk=512,
                          name="mla_attn")
    return matmul(o_a, w_out.astype(BF16), a2=o_b, resid=(x2d, g_a, T), out_dtype=F32, name="out_proj_even")


def _odd_mixer(xm, x2d, g_a, B, T, w_in, b_in, w_out, b_out, q_gain, k_gain, sinks):
    G, hd = SWA_KV_HEADS, SWA_HEAD_DIM
    nq = SWA_HEADS * hd
    kw = G * hd
    gain = jnp.concatenate([jnp.tile(q_gain * hd ** -0.5, SWA_HEADS), jnp.tile(k_gain, G),
                            jnp.ones((kw,), F32)])
    flag = jnp.concatenate([jnp.ones((nq + kw,), F32), jnp.zeros((kw,), F32)])
    h = matmul(xm, w_in.astype(BF16), bias=b_in, head_norm=(hd, hd, gain, flag), name="in_proj_odd")
    o_c = swa_attention(h, sinks, B, T, nq, nq + kw)
    return matmul(o_c, w_out.astype(BF16), bias=b_out, resid=(x2d, g_a, T), out_dtype=F32,
                  name="out_proj_odd")


def kernel(x, c, w_mod, mod_table, norm_attn, norm_ffn, w_in_even, w_out_even, nsa_q_gain, nsa_k_gain,
           nsa_pe_k, nsa_pe_v, nsa_w_ck1, nsa_w_ck2, nsa_w_cv1, nsa_w_cv2, mla_g_cq, mla_g_ckv, mla_w_uq,
           mla_w_ukv, mla_q_gain, mla_k_gain, w_in_odd, b_in_odd, w_out_odd, b_out_odd, swa_q_gain,
           swa_k_gain, swa_sinks, router_w, router_b, moe_w_gate, moe_b_gate, moe_w_up, moe_b_up,
           moe_w_down, moe_b_down):
    B, T, D = x.shape
    N = B * T
    depth = mod_table.shape[0]
    c_pad = jnp.pad(c, ((0, 8 - B % 8 if B % 8 else 0), (0, 0)))
    cond = matmul(c_pad, w_mod, a_pro="silu", out_dtype=F32, tn=1024, tk=1024, name="adaln_proj")[:B]
    x2d = x.reshape(N, D)
    for layer in range(depth):
        mod = (cond + mod_table[layer]).reshape(B, 6, 1, D)
        sh_a, sc_a, g_a, sh_f, sc_f, g_f = (mod[:, j] for j in range(6))
        xm = norm_mod(x2d, norm_attn[layer], sc_a, sh_a, T)
        i = layer // 2
        if layer % 2 == 0:
            x2d = _even_mixer(xm, x2d, g_a, B, T, w_in_even[i], w_out_even[i], nsa_q_gain[i], nsa_k_gain[i],
                              nsa_pe_k[i], nsa_pe_v[i], nsa_w_ck1[i], nsa_w_ck2[i], nsa_w_cv1[i],
                              nsa_w_cv2[i], mla_g_cq[i], mla_g_ckv[i], mla_w_uq[i], mla_w_ukv[i],
                              mla_q_gain[i], mla_k_gain[i])
        else:
            x2d = _odd_mixer(xm, x2d, g_a, B, T, w_in_odd[i], b_in_odd[i], w_out_odd[i], b_out_odd[i],
                             swa_q_gain[i], swa_k_gain[i], swa_sinks[i])
        x2d = moe_layer(x2d, norm_ffn[layer], sc_f, sh_f, g_f, T, router_w[layer], router_b[layer],
                        layer, moe_w_gate, moe_b_gate, moe_w_up, moe_b_up, moe_w_down, moe_b_down)
    return x2d.reshape(B, T, D)
```

```python
import functools
import math

import numpy as np
import jax
import jax.numpy as jnp
from jax import lax
from jax.experimental import pallas as pl
from jax.experimental.pallas import tpu as pltpu

BF16 = jnp.bfloat16
F32 = jnp.float32

NSA_HEADS = 16
NSA_KV_HEADS = 4
NSA_QK_DIM = 192
NSA_V_DIM = 128
CMP_BLOCK = 32
CMP_STRIDE = 16
SLC_BLOCK = 64
SLC_TOPN = 8
NSA_WINDOW = 512
MLA_HEADS = 16
MLA_Q_RANK = 1024
MLA_KV_RANK = 512
MLA_NOPE_DIM = 128
MLA_ROPE_DIM = 64
MLA_V_DIM = 128
ROPE_THETA = 10000.0
SWA_HEADS = 64
SWA_KV_HEADS = 8
SWA_HEAD_DIM = 64
SWA_WINDOW = 128
N_EXPERTS = 32
TOP_K = 4
SWIGLU_ALPHA = 1.702
SWIGLU_LIMIT = 7.0
MOE_ROW_BLOCK = 256
EPS = 1e-6

LANES = 128
HEAD_PAD = 256
NEG = -1e30
M_FLOOR = -1e29

EVEN_IN_SPLITS = (NSA_HEADS * NSA_QK_DIM,
                  NSA_KV_HEADS * NSA_QK_DIM, NSA_KV_HEADS * NSA_V_DIM,
                  NSA_KV_HEADS * NSA_QK_DIM, NSA_KV_HEADS * NSA_V_DIM,
                  NSA_KV_HEADS * NSA_QK_DIM, NSA_KV_HEADS * NSA_V_DIM,
                  3 * NSA_HEADS, MLA_Q_RANK, MLA_KV_RANK + MLA_ROPE_DIM)

_NQ = NSA_HEADS * HEAD_PAD
_NK = NSA_KV_HEADS * HEAD_PAD
_NV = NSA_KV_HEADS * NSA_V_DIM
EV_Q = 0
EV_KC = EV_Q + _NQ
EV_KS = EV_KC + _NK
EV_KW = EV_KS + _NK
EV_CQ = EV_KW + _NK
EV_CKV = EV_CQ + MLA_Q_RANK
EV_KR = EV_CKV + MLA_KV_RANK
EV_VC = EV_KR + 2 * MLA_ROPE_DIM
EV_VS = EV_VC + _NV
EV_VW = EV_VS + _NV
EV_GATE = EV_VW + _NV
EV_END = EV_GATE + LANES
EV_WIDTH = -(-EV_END // 512) * 512


def _tile(dim, want):
    t = min(dim, want)
    while dim % t:
        t //= 2
    return t


def _alibi_np(n):
    return np.exp2(-8.0 * np.arange(1, n + 1, dtype=np.float32) / n).astype(np.float32)


def _alibi(n):
    return jnp.asarray(_alibi_np(n))


def _head_norm(res, hd, real_d):
    tm, tn = res.shape
    x2 = res * res
    pieces = []
    if hd == HEAD_PAD:
        for s in range(tn // hd):
            ss = jnp.sum(x2[:, s * hd:s * hd + LANES] + x2[:, s * hd + LANES:(s + 1) * hd],
                         axis=1, keepdims=True)
            r = lax.rsqrt(ss * (1.0 / real_d) + EPS)
            pieces.append(jnp.broadcast_to(r, (tm, hd)))
    else:
        lo = lax.broadcasted_iota(jnp.int32, (tm, LANES), 1) < hd
        for s in range(tn // LANES):
            c = x2[:, s * LANES:(s + 1) * LANES]
            ss_lo = jnp.sum(jnp.where(lo, c, 0.0), axis=1, keepdims=True)
            ss_hi = jnp.sum(jnp.where(lo, 0.0, c), axis=1, keepdims=True)
            r_lo = lax.rsqrt(ss_lo * (1.0 / real_d) + EPS)
            r_hi = lax.rsqrt(ss_hi * (1.0 / real_d) + EPS)
            pieces.append(jnp.where(lo, r_lo, r_hi))
    return jnp.concatenate(pieces, axis=1) if len(pieces) > 1 else pieces[0]


def _mm_kernel(*refs, nk, a_pro, has_a2, has_bias, hd, real_d, has_resid):
    it = iter(refs)
    a_ref = next(it)
    a2_ref = next(it) if has_a2 else None
    w_ref = next(it)
    again_ref = next(it) if a_pro == "rms" else None
    b_ref = next(it) if has_bias else None
    gain_ref = flag_ref = None
    if hd:
        gain_ref = next(it)
        flag_ref = next(it)
    x_ref = gate_ref = None
    if has_resid:
        x_ref = next(it)
        gate_ref = next(it)
    o_ref = next(it)
    acc_ref = next(it) if nk > 1 else None

    a = a_ref[...]
    if a_pro == "silu":
        af = a.astype(F32)
        a = af * jax.nn.sigmoid(af)
    elif a_pro == "rms":
        af = a.astype(F32)
        r = lax.rsqrt(jnp.mean(af * af, axis=1, keepdims=True) + EPS)
        a = af * r * again_ref[...]
    if has_a2:
        k1 = a.shape[1]
        part = (jnp.dot(a.astype(BF16), w_ref[:k1, :].astype(BF16), preferred_element_type=F32)
                + jnp.dot(a2_ref[...].astype(BF16), w_ref[k1:, :].astype(BF16), preferred_element_type=F32))
    else:
        part = jnp.dot(a.astype(BF16), w_ref[...].astype(BF16), preferred_element_type=F32)

    def finish(res):
        if has_bias:
            res = res + b_ref[...]
        if hd:
            r = _head_norm(res, hd, real_d)
            res = res * jnp.where(flag_ref[...] > 0.0, r, 1.0) * gain_ref[...]
        if has_resid:
            res = x_ref[...] + gate_ref[...] * res
        o_ref[...] = res.astype(o_ref.dtype)

    if nk == 1:
        finish(part)
    else:
        k = pl.program_id(2)

        @pl.when(k == 0)
        def _():
            acc_ref[...] = part

        @pl.when(k > 0)
        def _():
            acc_ref[...] += part

        @pl.when(k == nk - 1)
        def _():
            finish(acc_ref[...])


def matmul(a, w, *, a2=None, a_col0=0, a_pro=None, a_gain=None, bias=None,
           head_norm=None, resid=None, out_dtype=BF16, tm=1024, tn=512, tk=4096, name="mm"):
    M = a.shape[0]
    K, N = w.shape
    tm, tn, tk = _tile(M, tm), _tile(N, tn), _tile(K, tk)
    if a_pro == "rms" or a2 is not None:
        tk = K
    assert a_col0 % tk == 0 and M % tm == 0 and N % tn == 0 and K % tk == 0
    nk = K // tk
    koff = a_col0 // tk
    hd = head_norm[0] if head_norm else 0
    real_d = head_norm[1] if head_norm else 0
    if hd:
        assert tn % max(hd, LANES) == 0

    if a2 is None:
        ins = [a, w]
        specs = [pl.BlockSpec((tm, tk), lambda i, j, k: (i, koff + k))]
    else:
        assert a_col0 == 0 and a_pro is None and a.shape[1] + a2.shape[1] == K
        ins = [a, a2, w]
        specs = [pl.BlockSpec((tm, a.shape[1]), lambda i, j, k: (i, 0)),
                 pl.BlockSpec((tm, a2.shape[1]), lambda i, j, k: (i, 0))]
    specs.append(pl.BlockSpec((tk, tn), lambda i, j, k: (k, j)))
    if a_pro == "rms":
        ins.append(a_gain.reshape(1, K).astype(F32))
        specs.append(pl.BlockSpec((1, tk), lambda i, j, k: (0, k)))
    if bias is not None:
        ins.append(bias.reshape(1, N).astype(F32))
        specs.append(pl.BlockSpec((1, tn), lambda i, j, k: (0, j)))
    if hd:
        ins += [head_norm[2].reshape(1, N).astype(F32), head_norm[3].reshape(1, N).astype(F32)]
        specs += [pl.BlockSpec((1, tn), lambda i, j, k: (0, j))] * 2
    if resid is not None:
        x, gate, rows_per_batch = resid
        assert rows_per_batch % tm == 0
        bpb = rows_per_batch // tm
        ins += [x, gate]
        specs += [pl.BlockSpec((tm, tn), lambda i, j, k: (i, j)),
                  pl.BlockSpec((None, 1, tn), lambda i, j, k: (i // bpb, 0, j))]
    kern = functools.partial(_mm_kernel, nk=nk, a_pro=a_pro, has_a2=a2 is not None, has_bias=bias is not None,
                             hd=hd, real_d=real_d, has_resid=resid is not None)
    return pl.pallas_call(
        kern,
        out_shape=jax.ShapeDtypeStruct((M, N), out_dtype),
        grid=(M // tm, N // tn, nk),
        in_specs=specs,
        out_specs=pl.BlockSpec((tm, tn), lambda i, j, k: (i, j)),
        scratch_shapes=[pltpu.VMEM((tm, tn), F32)] if nk > 1 else [],
        compiler_params=pltpu.CompilerParams(
            dimension_semantics=("parallel", "parallel", "arbitrary")),
        name=name,
    )(*ins)


def _split_bf16(v):
    hi = v.astype(BF16)
    lo = (v - hi.astype(F32)).astype(BF16)
    return hi, lo


def _pack_bf16_pairs(v):
    half = v.shape[1] // 2
    vb = v.astype(BF16).astype(F32)
    lo = lax.shift_right_logical(lax.bitcast_convert_type(vb[:, :half], jnp.int32), 16)
    hi = lax.bitcast_convert_type(vb[:, half:], jnp.int32) & jnp.int32(-65536)
    return hi | lo


def _unpack_bf16_pairs(u):
    lo = lax.bitcast_convert_type(lax.shift_left(u, 16), F32)
    hi = lax.bitcast_convert_type(u & jnp.int32(-65536), F32)
    return jnp.concatenate([lo, hi], axis=1).astype(BF16)


def _norm_mod_kernel(x_ref, g_ref, sc_ref, sh_ref, *rest, route):
    x = x_ref[...]
    r = lax.rsqrt(jnp.mean(x * x, axis=1, keepdims=True) + EPS)
    xm = (x * r * g_ref[...]) * (1.0 + sc_ref[...]) + sh_ref[...]
    if not route:
        (o_ref,) = rest
        o_ref[...] = xm.astype(o_ref.dtype)
        return
    rw_ref, rb_ref, o_ref, e_ref, p_ref, rank_ref, cnt_ref, run_ref = rest
    o_ref[...] = _pack_bf16_pairs(xm)
    a_hi, a_lo = _split_bf16(xm)
    w = rw_ref[...]
    w_hi, w_lo = _split_bf16(w)
    logits = (jnp.dot(a_hi, w_hi, preferred_element_type=F32)
              + jnp.dot(a_hi, w_lo, preferred_element_type=F32)
              + jnp.dot(a_lo, w_hi, preferred_element_type=F32)) + rb_ref[...]
    tt = logits.shape[0]
    lane = lax.broadcasted_iota(jnp.int32, (tt, LANES), 1)
    lane_f = lane.astype(F32)
    work = jnp.where(lane < N_EXPERTS, logits, -jnp.inf)
    e_out = jnp.zeros((tt, LANES), F32)
    v_out = jnp.full((tt, LANES), -jnp.inf, F32)
    hits = []
    for kk in range(TOP_K):
        m = jnp.max(work, axis=1, keepdims=True)
        idx = jnp.min(jnp.where(work == m, lane_f, float(LANES)), axis=1, keepdims=True)
        e_out = jnp.where(lane == kk, idx, e_out)
        v_out = jnp.where(lane == kk, m, v_out)
        hits.append(lane_f == idx)
        work = jnp.where(hits[-1], -jnp.inf, work)
    e_out = e_out.astype(jnp.int32)
    vmax = jnp.max(v_out, axis=1, keepdims=True)
    pe = jnp.exp(v_out - vmax)
    p_ref[...] = pe / jnp.sum(pe, axis=1, keepdims=True)
    e_ref[...] = e_out

    @pl.when(pl.program_id(0) == 0)
    def _():
        run_ref[...] = jnp.zeros_like(run_ref)

    onehot = jnp.zeros((tt, LANES), F32)
    for hit in hits:
        onehot = jnp.where(hit, 1.0, onehot)
    earlier = (lax.broadcasted_iota(jnp.int32, (tt, tt), 1)
               < lax.broadcasted_iota(jnp.int32, (tt, tt), 0))
    prefix = jnp.dot(jnp.where(earlier, 1.0, 0.0).astype(BF16), onehot.astype(BF16),
                     preferred_element_type=F32)
    base = run_ref[...] + prefix
    rank = jnp.zeros((tt, LANES), F32)
    for kk, hit in enumerate(hits):
        rk = jnp.sum(jnp.where(hit, base, 0.0), axis=1, keepdims=True)
        rank = jnp.where(lane == kk, rk, rank)
    rank_ref[...] = rank.astype(jnp.int32)
    total = run_ref[...] + jnp.sum(onehot, axis=0, keepdims=True)
    run_ref[...] = total
    cnt_ref[...] = total.astype(jnp.int32)


def norm_mod(x2d, gain, sc, sh, rows_per_batch, *, out_dtype=BF16, router=None, tt=256):
    N, D = x2d.shape
    tt = _tile(rows_per_batch, tt)
    bpb = rows_per_batch // tt
    ins = [x2d, gain.reshape(1, D), sc, sh]
    specs = [pl.BlockSpec((tt, D), lambda i: (i, 0)),
             pl.BlockSpec((1, D), lambda i: (0, 0)),
             pl.BlockSpec((None, 1, D), lambda i: (i // bpb, 0, 0)),
             pl.BlockSpec((None, 1, D), lambda i: (i // bpb, 0, 0))]
    if router is None:
        out_shape = [jax.ShapeDtypeStruct((N, D), out_dtype)]
        out_specs = [pl.BlockSpec((tt, D), lambda i: (i, 0))]
        scratch = []
    else:
        rw, rb = router
        E = rw.shape[1]
        rw_p = jnp.pad(rw, ((0, 0), (0, LANES - E)))
        rb_p = jnp.pad(rb, (0, LANES - E)).reshape(1, LANES)
        ins += [rw_p, rb_p]
        specs += [pl.BlockSpec((D, LANES), lambda i: (0, 0)),
                  pl.BlockSpec((1, LANES), lambda i: (0, 0))]
        out_shape = [jax.ShapeDtypeStruct((N, D // 2), jnp.int32),
                     jax.ShapeDtypeStruct((N, LANES), jnp.int32),
                     jax.ShapeDtypeStruct((N, LANES), F32),
                     jax.ShapeDtypeStruct((N, LANES), jnp.int32),
                     jax.ShapeDtypeStruct((1, LANES), jnp.int32)]
        row = pl.BlockSpec((tt, LANES), lambda i: (i, 0))
        out_specs = [pl.BlockSpec((tt, D // 2), lambda i: (i, 0)), row, row, row,
                     pl.BlockSpec((1, LANES), lambda i: (0, 0))]
        scratch = [pltpu.VMEM((1, LANES), F32)]
    res = pl.pallas_call(
        functools.partial(_norm_mod_kernel, route=router is not None),
        out_shape=out_shape, grid=(N // tt,), in_specs=specs, out_specs=out_specs,
        scratch_shapes=scratch,
        compiler_params=pltpu.CompilerParams(
            dimension_semantics=("parallel",) if router is None else ("arbitrary",)),
        name="norm_mod_route" if router is not None else "norm_mod",
    )(*ins)
    return res if router is not None else res[0]


def _compress_kernel(h_ref, pe_ref, w1_ref, w2_ref, g_ref, o_ref, xf_ref, *, real_d, norm, nh):
    n_tiles = xf_ref.shape[0]
    for j in range(n_tiles):
        xf_ref[j] = h_ref[:, j * LANES:(j + 1) * LANES].astype(F32)
    half = CMP_BLOCK // 2
    top = bot = None
    for l in range(half):
        parts = [xf_ref[j, pl.ds(l, nh, stride=CMP_STRIDE), :] for j in range(n_tiles)]
        xl = jnp.concatenate(parts, axis=1) if n_tiles > 1 else parts[0]
        t = jnp.dot((xl + pe_ref[l:l + 1, :]).astype(BF16), w1_ref[l], preferred_element_type=F32)
        b = jnp.dot((xl + pe_ref[half + l:half + l + 1, :]).astype(BF16), w1_ref[half + l],
                    preferred_element_type=F32)
        top = t if top is None else top + t
        bot = b if bot is None else bot + b
    hid = top + pltpu.roll(bot, nh - 1, 0)
    hid = hid * jax.nn.sigmoid(hid)
    y = jnp.dot(hid.astype(BF16), w2_ref[...], preferred_element_type=F32)
    if norm:
        r = lax.rsqrt(jnp.sum(y * y, axis=1, keepdims=True) * (1.0 / real_d) + EPS)
        y = y * r * g_ref[...]
    o_ref[...] = y.astype(o_ref.dtype)


def compress(h, col0, B, T, pe, w1, w2, gain, d, dp):
    G = NSA_KV_HEADS
    nh = T // CMP_STRIDE
    cb = col0 // dp
    pe_p = jnp.pad(pe, ((0, 0), (0, dp - d)))
    w1_p = jnp.pad(w1.reshape(CMP_BLOCK, d, d), ((0, 0), (0, dp - d), (0, dp - d))).astype(BF16)
    w2_p = jnp.pad(w2, ((0, dp - d), (0, dp - d))).astype(BF16)
    g_p = (jnp.ones((dp,), F32) if gain is None else jnp.pad(gain, (0, dp - d))).reshape(1, dp)
    full = lambda shape: pl.BlockSpec(shape, lambda b, g: (0,) * len(shape))
    return pl.pallas_call(
        functools.partial(_compress_kernel, real_d=d, norm=gain is not None, nh=nh),
        out_shape=jax.ShapeDtypeStruct((B * G, nh, dp), BF16),
        grid=(B, G),
        in_specs=[pl.BlockSpec((T, dp), lambda b, g: (b, cb + g)),
                  full((CMP_BLOCK, dp)), full((CMP_BLOCK, dp, dp)), full((dp, dp)), full((1, dp))],
        out_specs=pl.BlockSpec((None, nh, dp), lambda b, g: (b * G + g, 0, 0)),
        scratch_shapes=[pltpu.VMEM((dp // LANES, T, LANES), F32)],
        compiler_params=pltpu.CompilerParams(dimension_semantics=("parallel", "parallel")),
        name="nsa_compress",
    )(h, pe_p, w1_p, w2_p, g_p)


def _stack_heads(q_ref, n, w):
    return jnp.concatenate([q_ref[:, r * w:(r + 1) * w] for r in range(n)], axis=0)


def _row_scalars(vals, tq):
    rows = len(vals) * tq
    rid = lax.broadcasted_iota(jnp.int32, (rows, 1), 0) // tq
    col = jnp.full((rows, 1), vals[-1], F32)
    for r in range(len(vals) - 2, -1, -1):
        col = jnp.where(rid == r, vals[r], col)
    return col


def _nt_dot(a, b):
    return lax.dot_general(a, b, (((1,), (1,)), ((), ())), preferred_element_type=F32)


def _cmp_attn_kernel(slope_ref, q_ref, kc_ref, vc_ref, ovl_ref, o_ref, bits_ref, *, tq, R, n_slc):
    g = pl.program_id(1)
    t0 = pl.program_id(2) * tq
    rows = R * tq
    q = _stack_heads(q_ref, R, HEAD_PAD)
    s = _nt_dot(q, kc_ref[...])
    ncol = s.shape[1]
    row_t = t0 + lax.broadcasted_iota(jnp.int32, (rows, ncol), 0) % tq
    n_id = lax.broadcasted_iota(jnp.int32, (rows, ncol), 1)
    dist = row_t - (n_id * CMP_STRIDE + (CMP_BLOCK - 1))
    slope = _row_scalars([slope_ref[g * R + r] for r in range(R)], tq)
    valid = dist >= 0
    s = jnp.where(valid, s - slope * dist.astype(F32), NEG)
    m = jnp.max(s, axis=1, keepdims=True)
    p = jnp.where(valid, jnp.exp(s - m), 0.0)
    p = p / jnp.maximum(jnp.sum(p, axis=1, keepdims=True), 1e-30)
    o = jnp.dot(p.astype(BF16), vc_ref[...], preferred_element_type=F32)
    for r in range(R):
        o_ref[:, r * NSA_V_DIM:(r + 1) * NSA_V_DIM] = o[r * tq:(r + 1) * tq].astype(o_ref.dtype)

    psum = p[0:tq]
    for r in range(1, R):
        psum = psum + p[r * tq:(r + 1) * tq]
    p_hi, p_lo = _split_bf16(psum)
    imp = (jnp.dot(p_hi, ovl_ref[...], preferred_element_type=F32)
           + jnp.dot(p_lo, ovl_ref[...], preferred_element_type=F32))
    lane = lax.broadcasted_iota(jnp.int32, (tq, LANES), 1)
    cur = (t0 + lax.broadcasted_iota(jnp.int32, (tq, LANES), 0)) // SLC_BLOCK
    forced = (lane == 0) | (lane == cur) | (lane == cur - 1)
    work = jnp.where(forced, jnp.inf, jnp.where(lane > cur, -jnp.inf, imp))
    removed = -3.0e38
    work = jnp.where(lane < n_slc, jnp.where(work == -jnp.inf, -2.0e38, work), removed)
    sel = jnp.zeros((tq, LANES), jnp.bool_)
    lane_f = lane.astype(F32)
    for _ in range(min(SLC_TOPN, n_slc)):
        mx = jnp.max(work, axis=1, keepdims=True)
        idx = jnp.min(jnp.where(work == mx, lane_f, float(LANES)), axis=1, keepdims=True)
        hit = lane_f == idx
        sel = sel | hit
        work = jnp.where(hit, removed, work)
    half = 16
    w_lo = jnp.where(sel & (lane < half), jnp.left_shift(1, jnp.minimum(lane, half - 1)), 0)
    w_hi = jnp.where(sel & (lane >= half), jnp.left_shift(1, jnp.clip(lane - half, 0, half - 1)), 0)
    b_lo = jnp.sum(w_lo.astype(F32), axis=1, keepdims=True).astype(jnp.int32)
    b_hi = jnp.sum(w_hi.astype(F32), axis=1, keepdims=True).astype(jnp.int32)
    bits = b_lo | jnp.left_shift(b_hi, half)
    bits_ref[...] = jnp.broadcast_to(bits, (tq, LANES))


def cmp_attention(h, kc, vc, B, T, tq=256):
    G, R = NSA_KV_HEADS, NSA_HEADS // NSA_KV_HEADS
    tq = _tile(T, tq)
    nq = T // tq
    n_slc = T // SLC_BLOCK
    n_cmp = kc.shape[1]
    assert n_slc <= 32 and n_cmp <= LANES and n_cmp % 8 == 0
    cmp_start = np.arange(n_cmp) * CMP_STRIDE
    slc_start = np.arange(LANES) * SLC_BLOCK
    ovl = ((cmp_start[:, None] < slc_start[None, :] + SLC_BLOCK)
           & (cmp_start[:, None] + CMP_BLOCK > slc_start[None, :])
           & (np.arange(LANES)[None, :] < n_slc) & (np.arange(n_cmp)[:, None] < n_cmp - 1))
    ovl = jnp.asarray(ovl, BF16)
    qw = R * HEAD_PAD
    return pl.pallas_call(
        functools.partial(_cmp_attn_kernel, tq=tq, R=R, n_slc=n_slc),
        out_shape=[jax.ShapeDtypeStruct((B * T, NSA_HEADS * NSA_V_DIM), BF16),
                   jax.ShapeDtypeStruct((B, G, T, LANES), jnp.int32)],
        grid=(B, G, nq),
        in_specs=[pl.BlockSpec(memory_space=pltpu.SMEM),
                  pl.BlockSpec((tq, qw), lambda b, g, i: (b * nq + i, EV_Q // qw + g)),
                  pl.BlockSpec((None, n_cmp, HEAD_PAD), lambda b, g, i: (b * G + g, 0, 0)),
                  pl.BlockSpec((None, n_cmp, NSA_V_DIM), lambda b, g, i: (b * G + g, 0, 0)),
                  pl.BlockSpec((n_cmp, LANES), lambda b, g, i: (0, 0))],
        out_specs=[pl.BlockSpec((tq, R * NSA_V_DIM), lambda b, g, i: (b * nq + i, g)),
                   pl.BlockSpec((None, None, tq, LANES), lambda b, g, i: (b, g, i, 0))],
        compiler_params=pltpu.CompilerParams(dimension_semantics=("parallel", "parallel", "parallel")),
        name="nsa_cmp_attn",
    )(_alibi(NSA_HEADS), h, kc, vc, ovl)


ALIBI_LANE0 = 64
SEL_LANE0 = 70


def _key_table(T):
    s = np.arange(T)
    tab = np.zeros((T, LANES), np.float32)
    tab[:, ALIBI_LANE0:ALIBI_LANE0 + 3] = (s // 256 * 256)[:, None]
    tab[:, ALIBI_LANE0 + 3:ALIBI_LANE0 + 6] = (s % 256)[:, None]
    tab[s, SEL_LANE0 + s // SLC_BLOCK] = 1.0
    return jnp.asarray(tab, BF16)


def _slope_parts(n_heads):
    s = _alibi_np(n_heads)
    h1 = s.astype(BF16).astype(np.float32)
    h2 = (s - h1).astype(BF16).astype(np.float32)
    h3 = (s - h1 - h2).astype(BF16).astype(np.float32)
    return jnp.asarray(np.stack([h1, h2, h3], axis=1).reshape(-1))


def _flash_kernel(*refs, tq, tk, R, HP, v_off, v_step, window, alibi, use_bits):
    it = iter(refs)
    slope_ref = next(it) if alibi else None
    q_ref = next(it)
    k_ref = next(it)
    v_ref = next(it)
    ktab_ref = next(it) if alibi else None
    bits_ref = next(it) if use_bits else None
    o_ref = next(it)
    m_ref = next(it)
    acc_ref = next(it)

    g = pl.program_id(1)
    t0 = pl.program_id(2) * tq
    rows = R * tq
    vd = NSA_V_DIM
    lane = lax.broadcasted_iota(jnp.int32, (tq, LANES), 1)
    sel_add = None
    if use_bits:
        j = jnp.clip(lane - SEL_LANE0, 0, 31)
        in_sel = (lane >= SEL_LANE0) & (lane < SEL_LANE0 + 32)
        picked = (jnp.right_shift(bits_ref[...], j) & 1) != 0
        sel_add = jnp.where(in_sel & jnp.logical_not(picked), NEG, 0.0)

    qs = []
    for hp in range(HP):
        pieces = []
        for r in range(R):
            c0 = (hp * R + r) * HEAD_PAD
            q_lo = q_ref[:, c0:c0 + LANES]
            q_hi = q_ref[:, c0 + LANES:c0 + HEAD_PAD]
            if alibi:
                hd = (g * HP + hp) * R + r
                add = jnp.zeros((tq, LANES), F32) if sel_add is None else sel_add
                for part in range(3):
                    sp = slope_ref[3 * hd + part]
                    add = jnp.where((lane == ALIBI_LANE0 + part) | (lane == ALIBI_LANE0 + 3 + part), sp, add)
                q_hi = (q_hi.astype(F32) + add).astype(BF16)
            pieces.append(jnp.concatenate([q_lo, q_hi], axis=1))
        qs.append(jnp.concatenate(pieces, axis=0) if R > 1 else pieces[0])

    rel0 = (lax.broadcasted_iota(jnp.int32, (rows, tk), 0) % tq
            - lax.broadcasted_iota(jnp.int32, (rows, tk), 1))
    ones = jnp.ones((tk, LANES), BF16)
    m_ref[...] = jnp.full(m_ref.shape, M_FLOOR, F32)
    acc_ref[...] = jnp.zeros(acc_ref.shape, F32)

    def step(c, masked):
        s0 = pl.multiple_of(c * tk, tk)
        off = t0 - s0
        if masked:
            valid = rel0 >= -off
            if window is not None:
                valid = valid & (rel0 < window - off)
        for hp in range(HP):
            kc = k_ref[pl.ds(s0, tk), hp * HEAD_PAD:(hp + 1) * HEAD_PAD]
            if alibi:
                kc = jnp.concatenate([kc[:, :LANES], kc[:, LANES:] + ktab_ref[pl.ds(s0, tk), :]], axis=1)
            vc = v_ref[pl.ds(s0, tk), v_off + hp * v_step:v_off + hp * v_step + vd]
            s = _nt_dot(qs[hp], kc)
            if masked:
                s = jnp.where(valid, s, NEG)
            tiles = [s[:, j * LANES:(j + 1) * LANES] for j in range(tk // LANES)]
            mx = tiles[0]
            for t in tiles[1:]:
                mx = jnp.maximum(mx, t)
            m_old = m_ref[hp]
            m_new = jnp.maximum(m_old, jnp.max(mx, axis=1, keepdims=True))
            alpha = jnp.exp(m_old - m_new)
            p = jnp.concatenate([jnp.exp(t - m_new).astype(BF16) for t in tiles], axis=1)
            pv = jnp.dot(p, jnp.concatenate([vc, ones], axis=1), preferred_element_type=F32)
            acc_ref[hp] = jnp.concatenate([alpha, alpha], axis=1) * acc_ref[hp] + pv
            m_ref[hp] = m_new

    def body(c, carry):
        s0 = c * tk
        full = s0 + tk - 1 <= t0
        if window is not None:
            full = full & (s0 >= t0 + tq - window)
        lax.cond(full, lambda: step(c, False), lambda: step(c, True))
        return carry

    c_hi = (t0 + tq - 1) // tk
    c_lo = 0 if window is None else jnp.maximum((t0 - window + 1) // tk, 0)
    lax.fori_loop(c_lo, c_hi + 1, body, 0)
    for hp in range(HP):
        acc = acc_ref[hp]
        o = acc[:, :vd] / jnp.maximum(acc[:, vd:], 1e-30)
        for r in range(R):
            c0 = (hp * R + r) * vd
            o_ref[:, c0:c0 + vd] = o[r * tq:(r + 1) * tq].astype(o_ref.dtype)


def flash_attention(q_arr, q_col0, k_arr, k_col0, v_arr, v_col0, *, B, T, G, R, HP=1, v_width=NSA_V_DIM,
                    v_off=0, v_step=0, out_cols, out_col0=0, window=None, n_alibi_heads=0, bits=None,
                    tq=256, tk=256, name="flash"):
    vd = NSA_V_DIM
    tq = _tile(T, tq)
    tk = _tile(T, tk)
    nq = T // tq
    qw, kw, ow = HP * R * HEAD_PAD, HP * HEAD_PAD, HP * R * vd
    assert q_col0 % qw == 0 and k_col0 % kw == 0 and v_col0 % v_width == 0 and out_col0 % ow == 0
    assert bits is None or n_alibi_heads
    qb, kb, vb, ob = q_col0 // qw, k_col0 // kw, v_col0 // v_width, out_col0 // ow
    alibi = n_alibi_heads > 0
    ins, specs = [], []
    if alibi:
        ins.append(_slope_parts(n_alibi_heads))
        specs.append(pl.BlockSpec(memory_space=pltpu.SMEM))
    ins += [q_arr, k_arr, v_arr]
    specs += [pl.BlockSpec((tq, qw), lambda b, g, i: (b * nq + i, qb + g)),
              pl.BlockSpec((T, kw), lambda b, g, i: (b, kb + g)),
              pl.BlockSpec((T, v_width), lambda b, g, i: (b, vb + g))]
    if alibi:
        ins.append(_key_table(T))
        specs.append(pl.BlockSpec((T, LANES), lambda b, g, i: (0, 0)))
    if bits is not None:
        ins.append(bits)
        specs.append(pl.BlockSpec((None, None, tq, LANES), lambda b, g, i: (b, g, i, 0)))
    rows = R * tq
    return pl.pallas_call(
        functools.partial(_flash_kernel, tq=tq, tk=tk, R=R, HP=HP, v_off=v_off, v_step=v_step,
                          window=window, alibi=alibi, use_bits=bits is not None),
        out_shape=jax.ShapeDtypeStruct((B * T, out_cols), BF16),
        grid=(B, G // HP, nq),
        in_specs=specs,
        out_specs=pl.BlockSpec((tq, ow), lambda b, g, i: (b * nq + i, ob + g)),
        scratch_shapes=[pltpu.VMEM((HP, rows, LANES), F32), pltpu.VMEM((HP, rows, 2 * vd), F32)],
        compiler_params=pltpu.CompilerParams(dimension_semantics=("parallel", "parallel", "parallel")),
        name=name,
    )(*ins)


def _nsa_combine_kernel(gate_ref, oc_ref, os_ref, ow_ref, o_ref):
    gates = jax.nn.sigmoid(gate_ref[...].astype(F32))
    for h in range(NSA_HEADS):
        sl = slice(h * NSA_V_DIM, (h + 1) * NSA_V_DIM)
        acc = None
        for br, ref in enumerate((oc_ref, os_ref, ow_ref)):
            gcol = gates[:, 3 * h + br:3 * h + br + 1]
            term = gcol * ref[:, sl].astype(F32)
            acc = term if acc is None else acc + term
        o_ref[:, sl] = acc.astype(o_ref.dtype)


def nsa_combine(h, o_cmp, o_slc, o_win, tt=256):
    N, W = o_cmp.shape
    tt = _tile(N, tt)
    gb = EV_GATE // LANES
    blk = pl.BlockSpec((tt, W), lambda i: (i, 0))
    return pl.pallas_call(
        _nsa_combine_kernel,
        out_shape=jax.ShapeDtypeStruct((N, W), BF16),
        grid=(N // tt,),
        in_specs=[pl.BlockSpec((tt, LANES), lambda i: (i, gb)), blk, blk, blk],
        out_specs=blk,
        compiler_params=pltpu.CompilerParams(dimension_semantics=("parallel",)),
        name="nsa_combine",
    )(h, o_cmp, o_slc, o_win)


def _mla_prep_kernel(*refs, shared_rope):
    if shared_rope:
        x_ref, hi_ref, alo_ref, ahi_ref, bhi_ref, o_ref = refs
    else:
        x_ref, alo_ref, ahi_ref, bhi_ref, o_ref = refs
    tt = x_ref.shape[0]
    is_rope = lax.broadcasted_iota(jnp.int32, (tt, LANES), 1) < MLA_ROPE_DIM
    a_hi, b_hi, a_lo = ahi_ref[...], bhi_ref[...], alo_ref[...]

    def rope_part(x_hi):
        ss = jnp.sum(jnp.where(is_rope, x_hi * x_hi, 0.0), axis=1, keepdims=True)
        return x_hi * a_hi + pltpu.roll(x_hi * b_hi, MLA_ROPE_DIM, 1), ss

    if shared_rope:
        y_hi, ss_hi = rope_part(hi_ref[...].astype(F32))
    for h in range(MLA_HEADS):
        c = h * HEAD_PAD
        x_lo = x_ref[:, c:c + LANES].astype(F32)
        if not shared_rope:
            y_hi, ss_hi = rope_part(x_ref[:, c + LANES:c + HEAD_PAD].astype(F32))
        ss = jnp.sum(x_lo * x_lo, axis=1, keepdims=True) + ss_hi
        r = lax.rsqrt(ss * (1.0 / (MLA_NOPE_DIM + MLA_ROPE_DIM)) + EPS)
        o_ref[:, c:c + LANES] = (x_lo * r * a_lo).astype(o_ref.dtype)
        o_ref[:, c + LANES:c + HEAD_PAD] = (y_hi * r).astype(o_ref.dtype)


def mla_prep(x_arr, rope_arr, rope_col0, a_lo, a_hi, b_hi, T, tt=256):
    N = x_arr.shape[0]
    W = MLA_HEADS * HEAD_PAD
    tt = _tile(T, tt)
    nt = T // tt
    shared = rope_arr is not None
    ins = [x_arr]
    specs = [pl.BlockSpec((tt, W), lambda i: (i, 0))]
    if shared:
        rb = rope_col0 // LANES
        ins.append(rope_arr)
        specs.append(pl.BlockSpec((tt, LANES), lambda i: (i, rb)))
    ins += [a_lo, a_hi, b_hi]
    specs += [pl.BlockSpec((1, LANES), lambda i: (0, 0)),
              pl.BlockSpec((tt, LANES), lambda i: (i % nt, 0)),
              pl.BlockSpec((tt, LANES), lambda i: (i % nt, 0))]
    return pl.pallas_call(
        functools.partial(_mla_prep_kernel, shared_rope=shared),
        out_shape=jax.ShapeDtypeStruct((N, W), BF16),
        grid=(N // tt,),
        in_specs=specs,
        out_specs=pl.BlockSpec((tt, W), lambda i: (i, 0)),
        compiler_params=pltpu.CompilerParams(dimension_semantics=("parallel",)),
        name="mla_prep",
    )(*ins)


def _rope_tables(gain, T, scale):
    half = MLA_ROPE_DIM // 2
    inv_freq = ROPE_THETA ** (-jnp.arange(half, dtype=F32) / half)
    ang = jnp.arange(T, dtype=F32)[:, None] * inv_freq[None, :]
    cos2 = jnp.concatenate([jnp.cos(ang), jnp.cos(ang)], axis=1)
    sin_s = jnp.concatenate([-jnp.sin(ang), jnp.sin(ang)], axis=1)
    g_nope, g_rope = gain[:MLA_NOPE_DIM], gain[MLA_NOPE_DIM:]
    g_perm = jnp.concatenate([g_rope[half:], g_rope[:half]])
    zeros = jnp.zeros((T, MLA_ROPE_DIM), F32)
    a_lo = (g_nope * scale).reshape(1, LANES)
    a_hi = jnp.concatenate([g_rope[None, :] * cos2 * scale, zeros], axis=1)
    b_hi = jnp.concatenate([zeros, g_perm[None, :] * sin_s * scale], axis=1)
    return a_lo, a_hi, b_hi


def _swa_kernel(qadd_ref, slope_ref, sink_ref, q_ref, k_ref, v_ref, ktab_ref, o_ref, mask_ref,
                *, tq, tk, R):
    t0 = pl.program_id(2) * tq
    npair = R // 2
    rows = npair * tq
    qp = _stack_heads(q_ref, npair, LANES)
    lane = lax.broadcasted_iota(jnp.int32, (rows, LANES), 1)
    first = lane < SWA_HEAD_DIM
    start = pl.multiple_of(jnp.maximum(t0 - SWA_WINDOW, 0), LANES)
    own_half = lax.broadcasted_iota(jnp.int32, (tk, LANES), 1) // SWA_HEAD_DIM == pl.program_id(1) % 2

    def both_halves(ref):
        pair = ref[pl.ds(start, tk), :].astype(F32)
        return jnp.where(own_half, pair, pltpu.roll(pair, SWA_HEAD_DIM, 1)).astype(BF16)

    kc = jnp.concatenate([both_halves(k_ref), ktab_ref[pl.ds(start, tk), :]], axis=1)
    v_aug = jnp.concatenate([both_halves(v_ref), jnp.ones((tk, LANES), BF16)], axis=1)
    @pl.when(pl.program_id(2) <= 1)
    def _():
        row_in = lax.broadcasted_iota(jnp.int32, (2 * rows, tk), 0) % tq
        rel = t0 - start + row_in - lax.broadcasted_iota(jnp.int32, (2 * rows, tk), 1)
        mask_ref[...] = jnp.where((rel >= 0) & (rel < SWA_WINDOW), 0.0, NEG)

    t_row = (t0 + lax.broadcasted_iota(jnp.int32, (2 * rows, LANES), 0) % tq).astype(F32)
    zero = jnp.zeros_like(qp)
    qu = jnp.concatenate([jnp.where(first, qp, zero), jnp.where(first, zero, qp)], axis=0)
    qa = jnp.concatenate([qu, qadd_ref[...].reshape(2 * rows, LANES)], axis=1)
    sink_t = (sink_ref[...].reshape(2 * rows, LANES)
              + slope_ref[...].reshape(2 * rows, LANES) * t_row)
    s = _nt_dot(qa, kc) + mask_ref[...]
    tiles = [s[:, j * LANES:(j + 1) * LANES] for j in range(tk // LANES)]
    mx = tiles[0]
    for t in tiles[1:]:
        mx = jnp.maximum(mx, t)
    m = jnp.maximum(jnp.max(mx, axis=1, keepdims=True), sink_t)
    p = jnp.concatenate([jnp.exp(t - m).astype(BF16) for t in tiles], axis=1)
    pv = jnp.dot(p, v_aug, preferred_element_type=F32)
    denom = pv[:, LANES:] + jnp.exp(sink_t - m)
    both = pv[:, :LANES] / jnp.maximum(denom, 1e-30)
    o = jnp.where(first, both[:rows], both[rows:])
    for p_ in range(npair):
        o_ref[:, p_ * LANES:(p_ + 1) * LANES] = o[p_ * tq:(p_ + 1) * tq].astype(o_ref.dtype)


def swa_attention(h, sinks, B, T, k_col0, v_col0, tq=128):
    G, R = SWA_KV_HEADS, SWA_HEADS // SWA_KV_HEADS
    tq = _tile(T, tq)
    tk = min(T, tq + SWA_WINDOW)
    assert tk % LANES == 0 and tq % LANES == 0
    nq = T // tq
    qw = R * SWA_HEAD_DIM
    kb, vb = k_col0 // LANES, v_col0 // LANES
    npair = R // 2
    rows = npair * tq
    per_head = lambda v: jnp.repeat(v.reshape(G, npair, 2).transpose(0, 2, 1), tq, axis=2)
    lanes = lambda v: jnp.broadcast_to(v[..., None], (G, 2, rows, LANES))
    parts = _slope_parts(SWA_HEADS).reshape(SWA_HEADS, 3)
    lane_id = np.arange(LANES)
    qadd = jnp.zeros((G, 2, rows, LANES), F32)
    for part in range(3):
        at_lane = (lane_id == ALIBI_LANE0 + part) | (lane_id == ALIBI_LANE0 + 3 + part)
        qadd = jnp.where(at_lane[None, None, None, :], lanes(per_head(parts[:, part])), qadd)
    slope_rows = lanes(per_head(_alibi(SWA_HEADS)))
    sink_rows = lanes(per_head(sinks.astype(F32)))
    const_spec = pl.BlockSpec((None, 2, rows, LANES), lambda b, g, i: (g, 0, 0, 0))
    return pl.pallas_call(
        functools.partial(_swa_kernel, tq=tq, tk=tk, R=R),
        out_shape=jax.ShapeDtypeStruct((B * T, SWA_HEADS * SWA_HEAD_DIM), BF16),
        grid=(B, G, nq),
        in_specs=[const_spec, const_spec, const_spec,
                  pl.BlockSpec((tq, qw), lambda b, g, i: (b * nq + i, g)),
                  pl.BlockSpec((T, LANES), lambda b, g, i: (b, kb + g // 2)),
                  pl.BlockSpec((T, LANES), lambda b, g, i: (b, vb + g // 2)),
                  pl.BlockSpec((T, LANES), lambda b, g, i: (0, 0))],
        out_specs=pl.BlockSpec((tq, qw), lambda b, g, i: (b * nq + i, g)),
        scratch_shapes=[pltpu.VMEM((2 * rows, tk), F32)],
        compiler_params=pltpu.CompilerParams(dimension_semantics=("parallel", "parallel", "arbitrary")),
        name="swa_attn",
    )(qadd.astype(BF16), slope_rows, sink_rows, h, h, h, _key_table(T))


def _dispatch_kernel(zfrom_ref, zto_ref, nu_ref, pos_ref, src_ref, dst_hbm, zbuf, sem, zsem,
                     *, td, n_seg, rb, n_blk):
    @pl.when(pl.program_id(0) == 0)
    def _():
        zbuf[...] = jnp.zeros_like(zbuf)

        def zero_row(r):
            return pltpu.make_async_copy(zbuf.at[pl.ds(0, 1)], dst_hbm.at[pl.ds(r, 1)], zsem)

        def zero_block(b):
            return pltpu.make_async_copy(zbuf, dst_hbm.at[pl.ds(pl.multiple_of(b * rb, rb), rb)], zsem)

        def seg(s, c):
            lax.fori_loop(zfrom_ref[s], zto_ref[s], lambda r, cc: (zero_row(r).start(), cc)[1], 0)
            return c

        def seg_wait(s, c):
            lax.fori_loop(zfrom_ref[s], zto_ref[s], lambda r, cc: (zero_row(r).wait(), cc)[1], 0)
            return c

        lax.fori_loop(0, n_seg, seg, 0)
        lax.fori_loop(nu_ref[0], n_blk, lambda b, cc: (zero_block(b).start(), cc)[1], 0)
        lax.fori_loop(0, n_seg, seg_wait, 0)
        lax.fori_loop(nu_ref[0], n_blk, lambda b, cc: (zero_block(b).wait(), cc)[1], 0)

    def row_copy(t, k):
        return pltpu.make_async_copy(src_ref.at[pl.ds(t, 1)],
                                     dst_hbm.at[pl.ds(pos_ref[0, t * TOP_K + k], 1)], sem)

    def issue(t, c):
        for k in range(TOP_K):
            row_copy(t, k).start()
        return c

    lax.fori_loop(0, td, issue, 0)
    for k in range(TOP_K):
        pltpu.make_async_copy(src_ref, dst_hbm.at[pl.ds(0, td)], sem).wait()


def moe_dispatch(xm_packed, pos, zfrom, zto, n_used, cap, td=128, rb=MOE_ROW_BLOCK):
    N, W = xm_packed.shape
    td = _tile(N, td)
    pos3 = pos.reshape(N // td, 1, td * TOP_K)
    grid_spec = pltpu.PrefetchScalarGridSpec(
        num_scalar_prefetch=3,
        grid=(N // td,),
        in_specs=[pl.BlockSpec((None, 1, td * TOP_K), lambda i, *_: (i, 0, 0), memory_space=pltpu.SMEM),
                  pl.BlockSpec((td, W), lambda i, *_: (i, 0))],
        out_specs=pl.BlockSpec(memory_space=pl.ANY),
        scratch_shapes=[pltpu.VMEM((rb, W), jnp.int32), pltpu.SemaphoreType.DMA(()),
                        pltpu.SemaphoreType.DMA(())],
    )
    return pl.pallas_call(
        functools.partial(_dispatch_kernel, td=td, n_seg=zfrom.shape[0], rb=rb, n_blk=cap // rb),
        out_shape=jax.ShapeDtypeStruct((cap, W), jnp.int32),
        grid_spec=grid_spec,
        compiler_params=pltpu.CompilerParams(dimension_semantics=("arbitrary",), has_side_effects=True),
        name="moe_dispatch",
    )(zfrom, zto, n_used, pos3, xm_packed)


def _expert_kernel(be_ref, slot_ref, nxt_ref, c0_ref, c1_ref, nu_ref, xs_ref, wg_hbm, bg_ref, wu_hbm, bu_ref,
                   wd_hbm, bd_ref, o_ref, wg_s, wu_s, wd_s, stg_a, stg_d, sem, *, ca, cd, ring, layer):
    blk = pl.program_id(0)
    e = be_ref[blk]
    s = slot_ref[blk]
    used = blk < nu_ref[0]
    _, D, F = wg_s.shape
    na, nd = D // ca, F // cd
    n_chunks = 2 * na + nd

    def on_chunk(c, expert, fn):
        k = c % ring

        def go(src, dst, r0, n, stg, sem0):
            copy = pltpu.make_async_copy(src.at[layer, expert, pl.ds(r0, n), :], stg.at[k], sem.at[sem0 + k])
            fn(copy, dst, r0, n, stg.at[k])

        @pl.when(c < na)
        def _():
            go(wg_hbm, wg_s, pl.multiple_of(c * ca, ca), ca, stg_a, 0)

        @pl.when((c >= na) & (c < 2 * na))
        def _():
            go(wu_hbm, wu_s, pl.multiple_of((c - na) * ca, ca), ca, stg_a, 0)

        @pl.when(c >= 2 * na)
        def _():
            go(wd_hbm, wd_s, pl.multiple_of((c - 2 * na) * cd, cd), cd, stg_d, ring)

    def start(c, expert):
        on_chunk(c, expert, lambda copy, dst, r0, n, stg: copy.start())

    def finish(c, expert, dst_slot):
        def fn(copy, dst, r0, n, stg):
            copy.wait()
            dst[dst_slot, pl.ds(r0, n), :] = stg[...].astype(BF16)
        on_chunk(c, expert, fn)

    def prestart(lo, hi, expert):
        for i in range(ring):
            @pl.when(lo + i < hi)
            def _():
                start(lo + i, expert)

    def drain(lo, hi, end, expert, dst_slot):
        def body(c, carry):
            finish(c, expert, dst_slot)

            @pl.when(c + ring < end)
            def _():
                start(c + ring, expert)
            return carry

        lax.fori_loop(lo, hi, body, 0)

    @pl.when(used & (blk == 0))
    def _():
        prestart(0, n_chunks, e)
        drain(0, n_chunks, n_chunks, e, s)

    c0, c1, nxt = c0_ref[blk], c1_ref[blk], nxt_ref[blk]

    @pl.when(used)
    def _():
        prestart(c0, c1, nxt)
        x = _unpack_bf16_pairs(xs_ref[...])
        gg = jnp.dot(x, wg_s[s], preferred_element_type=F32) + bg_ref[...]
        uu = jnp.dot(x, wu_s[s], preferred_element_type=F32) + bu_ref[...]
        gg = jnp.minimum(gg, SWIGLU_LIMIT)
        uu = jnp.clip(uu, -SWIGLU_LIMIT, SWIGLU_LIMIT)
        act = gg * jax.nn.sigmoid(SWIGLU_ALPHA * gg) * (uu + 1.0)
        y = jnp.dot(act.astype(BF16), wd_s[s], preferred_element_type=F32) + bd_ref[...]
        o_ref[...] = _pack_bf16_pairs(y)
        drain(c0, c1, c1, nxt, 1 - s)

    @pl.when(jnp.logical_not(used))
    def _():
        o_ref[...] = jnp.zeros_like(o_ref)


EXPERT_STAGE_RING = 4
VMEM_LIMIT_EXPERTS = 60 * 1024 * 1024


def _expert_schedule(blk_e, n_used, n_chunks):
    n_blk = blk_e.shape[0]
    idx = jnp.arange(n_blk, dtype=jnp.int32)
    first = jnp.concatenate([jnp.ones((1,), jnp.int32), (blk_e[1:] != blk_e[:-1]).astype(jnp.int32)])
    slot = (jnp.cumsum(first) - 1) % 2
    run_start = lax.cummax(jnp.where(first == 1, idx, 0))
    starts_after = jnp.concatenate([jnp.where(first == 1, idx, n_blk)[1:], jnp.full((1,), n_blk, jnp.int32)])
    next_start = lax.cummin(starts_after, reverse=True)
    has_next = next_start < n_used[0]
    nxt = blk_e[jnp.minimum(next_start, n_blk - 1)]
    run_len = jnp.maximum(jnp.minimum(next_start, n_used[0]) - run_start, 1)
    j = idx - run_start
    c0 = jnp.where(has_next, j * n_chunks // run_len, 0)
    c1 = jnp.where(has_next, (j + 1) * n_chunks // run_len, 0)
    as_i32 = lambda v: v.astype(jnp.int32)
    return as_i32(slot), as_i32(nxt), as_i32(c0), as_i32(c1)


def moe_experts(xs, blk_e, n_used, layer, wg, bg, wu, bu, wd, bd, rb=MOE_ROW_BLOCK):
    cap, W = xs.shape
    depth, E, D, F = wg.shape
    n_blk = cap // rb
    ca, cd = _tile(D, 256), _tile(F, 64)
    ring = EXPERT_STAGE_RING
    slot, nxt, c0, c1 = _expert_schedule(blk_e, n_used, 2 * (D // ca) + F // cd)
    by_expert = lambda shape: pl.BlockSpec((None,) + shape,
                                           lambda i, be, *_: (layer * E + be[i], 0, 0))
    grid_spec = pltpu.PrefetchScalarGridSpec(
        num_scalar_prefetch=6,
        grid=(n_blk,),
        in_specs=[pl.BlockSpec((rb, W), lambda i, *sp: (jnp.minimum(i, sp[-1][0] - 1), 0)),
                  pl.BlockSpec(memory_space=pl.ANY), by_expert((1, F)),
                  pl.BlockSpec(memory_space=pl.ANY), by_expert((1, F)),
                  pl.BlockSpec(memory_space=pl.ANY), by_expert((1, D))],
        out_specs=pl.BlockSpec((rb, W), lambda i, *_: (i, 0)),
        scratch_shapes=[pltpu.VMEM((2, D, F), BF16), pltpu.VMEM((2, D, F), BF16), pltpu.VMEM((2, F, D), BF16),
                        pltpu.VMEM((ring, ca, F), F32), pltpu.VMEM((ring, cd, D), F32),
                        pltpu.SemaphoreType.DMA((2 * ring,))],
    )
    return pl.pallas_call(
        functools.partial(_expert_kernel, ca=ca, cd=cd, ring=ring, layer=layer),
        out_shape=jax.ShapeDtypeStruct((cap, W), jnp.int32),
        grid_spec=grid_spec,
        compiler_params=pltpu.CompilerParams(dimension_semantics=("arbitrary",),
                                             vmem_limit_bytes=VMEM_LIMIT_EXPERTS),
        name="moe_experts",
    )(blk_e, slot, nxt, c0, c1, n_used, xs, wg, bg.reshape(depth * E, 1, F), wu, bu.reshape(depth * E, 1, F),
      wd, bd.reshape(depth * E, 1, D))


def _moe_combine_kernel(pos_ref, pos_next_ref, y_hbm, x_ref, w_ref, gate_ref, o_ref, ybuf, sem,
                        *, tt, n_steps):
    i = pl.program_id(0)
    slot = i % 2

    def row_copy(p_ref, t, k, sl):
        return pltpu.make_async_copy(y_hbm.at[pl.ds(p_ref[0, t * TOP_K + k], 1)],
                                     ybuf.at[sl, k, pl.ds(t, 1)], sem.at[sl])

    def issue(p_ref, sl):
        def body(t, c):
            for k in range(TOP_K):
                row_copy(p_ref, t, k, sl).start()
            return c
        lax.fori_loop(0, tt, body, 0)

    @pl.when(i == 0)
    def _():
        issue(pos_ref, 0)

    @pl.when(i + 1 < n_steps)
    def _():
        issue(pos_next_ref, 1 - slot)

    for k in range(TOP_K):
        pltpu.make_async_copy(y_hbm.at[pl.ds(0, tt)], ybuf.at[slot, k], sem.at[slot]).wait()
    w = w_ref[...]
    half = x_ref.shape[1] // 2
    y_lo = y_hi = None
    for k in range(TOP_K):
        u = ybuf[slot, k]
        wk = w[:, k:k + 1]
        lo = wk * lax.bitcast_convert_type(lax.shift_left(u, 16), F32)
        hi = wk * lax.bitcast_convert_type(u & jnp.int32(-65536), F32)
        y_lo = lo if y_lo is None else y_lo + lo
        y_hi = hi if y_hi is None else y_hi + hi
    o_ref[:, :half] = x_ref[:, :half] + gate_ref[:, :half] * y_lo
    o_ref[:, half:] = x_ref[:, half:] + gate_ref[:, half:] * y_hi


def moe_combine(yb, pos, top_w, x2d, gate, rows_per_batch, tt=64):
    N, D = x2d.shape
    tt = _tile(rows_per_batch, tt)
    bpb = rows_per_batch // tt
    n_steps = N // tt
    pos3 = pos.reshape(n_steps, 1, tt * TOP_K)
    pos_spec = lambda f: pl.BlockSpec((None, 1, tt * TOP_K), f, memory_space=pltpu.SMEM)
    return pl.pallas_call(
        functools.partial(_moe_combine_kernel, tt=tt, n_steps=n_steps),
        out_shape=jax.ShapeDtypeStruct((N, D), F32),
        grid=(n_steps,),
        in_specs=[pos_spec(lambda i: (i, 0, 0)),
                  pos_spec(lambda i: (jnp.minimum(i + 1, n_steps - 1), 0, 0)),
                  pl.BlockSpec(memory_space=pl.ANY),
                  pl.BlockSpec((tt, D), lambda i: (i, 0)),
                  pl.BlockSpec((tt, LANES), lambda i: (i, 0)),
                  pl.BlockSpec((None, 1, D), lambda i: (i // bpb, 0, 0))],
        out_specs=pl.BlockSpec((tt, D), lambda i: (i, 0)),
        scratch_shapes=[pltpu.VMEM((2, TOP_K, tt, D // 2), jnp.int32), pltpu.SemaphoreType.DMA((2,))],
        compiler_params=pltpu.CompilerParams(dimension_semantics=("arbitrary",)),
        name="moe_combine",
    )(pos3, pos3, yb, x2d, top_w, gate)


def moe_layer(x2d, gain, sc, sh, gate, rows_per_batch, router_w, router_b, layer, wg, bg, wu, bu, wd, bd):
    N, D = x2d.shape
    RB = MOE_ROW_BLOCK
    E = N_EXPERTS
    xm_packed, top_e, top_w, rank, counts = norm_mod(x2d, gain, sc, sh, rows_per_batch,
                                                     router=(router_w, router_b))
    nk = N * TOP_K
    counts = counts[0, :E]
    padded = (counts + RB - 1) // RB * RB
    pad_end = jnp.cumsum(padded)
    pad_start = pad_end - padded
    e4 = top_e[:, :TOP_K]
    start_of = jnp.sum(jnp.where(e4[:, :, None] == jnp.arange(E)[None, None, :],
                                 pad_start[None, None, :], 0), axis=2)
    pos = (start_of + rank[:, :TOP_K]).astype(jnp.int32)
    cap = (-(-nk // RB)) * RB + E * RB
    n_blk = cap // RB
    blk_start = jnp.arange(n_blk, dtype=jnp.int32) * RB
    blk_e = jnp.minimum(jnp.sum(pad_end[None, :] <= blk_start[:, None], axis=1), E - 1).astype(jnp.int32)
    n_used = (pad_end[-1] // RB).astype(jnp.int32).reshape(1)
    zfrom = (pad_start + counts).astype(jnp.int32)
    zto = pad_end.astype(jnp.int32)
    xs = moe_dispatch(xm_packed, pos, zfrom, zto, n_used, cap)
    yb = moe_experts(xs, blk_e, n_used, layer, wg, bg, wu, bu, wd, bd)
    return moe_combine(yb, pos, top_w, x2d, gate, rows_per_batch)


def _pad_heads(w, n_heads, d, dp):
    lead = w.shape[:-1]
    w = w.reshape(lead + (n_heads, d))
    w = jnp.pad(w, [(0, 0)] * len(lead) + [(0, 0), (0, dp - d)])
    return w.reshape(lead + (n_heads * dp,))


def _even_in_weights(w_in, q_gain, k_gain):
    D = w_in.shape[0]
    cuts = np.cumsum(EVEN_IN_SPLITS)[:-1].tolist()
    q, kc, vc, ks, vs, kw, vw, gates, cq, kva = jnp.split(w_in.astype(BF16), cuts, axis=1)
    half = MLA_ROPE_DIM // 2
    kr = kva[:, MLA_KV_RANK:]
    kr_perm = jnp.concatenate([kr[:, half:], kr[:, :half]], axis=1)
    G = NSA_KV_HEADS
    cols = [_pad_heads(q, NSA_HEADS, NSA_QK_DIM, HEAD_PAD),
            _pad_heads(kc, G, NSA_QK_DIM, HEAD_PAD), _pad_heads(ks, G, NSA_QK_DIM, HEAD_PAD),
            _pad_heads(kw, G, NSA_QK_DIM, HEAD_PAD), cq, kva[:, :MLA_KV_RANK], kr, kr_perm,
            vc, vs, vw, gates]
    cols.append(jnp.zeros((D, EV_WIDTH - EV_END + LANES - gates.shape[1]), BF16))
    w = jnp.concatenate(cols, axis=1)
    assert w.shape[1] == EV_WIDTH
    scale = NSA_QK_DIM ** -0.5
    pad_g = lambda g: jnp.pad(g, (0, HEAD_PAD - NSA_QK_DIM))
    gain = jnp.ones((EV_WIDTH,), F32)
    gain = gain.at[EV_Q:EV_Q + _NQ].set(jnp.tile(pad_g(q_gain * scale), NSA_HEADS))
    gain = gain.at[EV_KS:EV_KS + _NK].set(jnp.tile(pad_g(k_gain[1]), G))
    gain = gain.at[EV_KW:EV_KW + _NK].set(jnp.tile(pad_g(k_gain[2]), G))
    col = np.arange(EV_WIDTH)
    flag = ((col < EV_Q + _NQ) | ((col >= EV_KS) & (col < EV_KW + _NK))).astype(np.float32)
    return w, gain, jnp.asarray(flag)


def _even_mixer(xm, x2d, g_a, B, T, w_in, w_out, q_gain, k_gain, pe_k, pe_v, w_ck1, w_ck2, w_cv1, w_cv2,
                g_cq, g_ckv, w_uq, w_ukv, mq_gain, mk_gain):
    N, D = x2d.shape
    G, R = NSA_KV_HEADS, NSA_HEADS // NSA_KV_HEADS
    w_p, gain, flag = _even_in_weights(w_in, q_gain, k_gain)
    h = matmul(xm, w_p, head_norm=(HEAD_PAD, NSA_QK_DIM, gain, flag), name="in_proj_even")

    kc = compress(h, EV_KC, B, T, pe_k, w_ck1, w_ck2, k_gain[0], NSA_QK_DIM, HEAD_PAD)
    vc = compress(h, EV_VC, B, T, pe_v, w_cv1, w_cv2, None, NSA_V_DIM, NSA_V_DIM)

    o_cmp, bits = cmp_attention(h, kc, vc, B, T)
    HV = NSA_HEADS * NSA_V_DIM
    o_slc = flash_attention(h, EV_Q, h, EV_KS, h, EV_VS, B=B, T=T, G=G, R=R, out_cols=HV,
                            n_alibi_heads=NSA_HEADS, bits=bits, tq=512, tk=512, name="nsa_slc_attn")
    o_win = flash_attention(h, EV_Q, h, EV_KW, h, EV_VW, B=B, T=T, G=G, R=R, out_cols=HV,
                            window=NSA_WINDOW, n_alibi_heads=NSA_HEADS, name="nsa_win_attn")
    o_a = nsa_combine(h, o_cmp, o_slc, o_win)

    H = MLA_HEADS
    dqk = MLA_NOPE_DIM + MLA_ROPE_DIM
    half = MLA_ROPE_DIM // 2
    wq = w_uq.reshape(MLA_Q_RANK, H, dqk)
    wq_rope = wq[:, :, MLA_NOPE_DIM:]
    wq_p = jnp.concatenate([wq, wq_rope[:, :, half:], wq_rope[:, :, :half]], axis=2)
    wq_p = wq_p.reshape(MLA_Q_RANK, H * HEAD_PAD).astype(BF16)
    q_raw = matmul(h, wq_p, a_col0=EV_CQ, a_pro="rms", a_gain=g_cq, name="mla_q_up")
    kv_raw = matmul(h, w_ukv.astype(BF16), a_col0=EV_CKV, a_pro="rms", a_gain=g_ckv, name="mla_kv_up")
    qa_lo, qa_hi, qb_hi = _rope_tables(mq_gain, T, dqk ** -0.5)
    ka_lo, ka_hi, kb_hi = _rope_tables(mk_gain, T, 1.0)
    q_m = mla_prep(q_raw, None, 0, qa_lo, qa_hi, qb_hi, T)
    k_m = mla_prep(kv_raw, h, EV_KR, ka_lo, ka_hi, kb_hi, T)
    hp = 4
    kv_w = MLA_NOPE_DIM + MLA_V_DIM
    o_b = flash_attention(q_m, 0, k_m, 0, kv_raw, 0, B=B, T=T, G=H, R=1, HP=hp, v_width=hp * kv_w,
                          v_off=MLA_NOPE_DIM, v_step=kv_w, out_cols=H * MLA_V_DIM, tq=512, tk=512,
                          name="mla_attn")
    return matmul(o_a, w_out.astype(BF16), a2=o_b, resid=(x2d, g_a, T), out_dtype=F32, name="out_proj_even")


def _odd_mixer(xm, x2d, g_a, B, T, w_in, b_in, w_out, b_out, q_gain, k_gain, sinks):
    G, hd = SWA_KV_HEADS, SWA_HEAD_DIM
    nq = SWA_HEADS * hd
    kw = G * hd
    gain = jnp.concatenate([jnp.tile(q_gain * hd ** -0.5, SWA_HEADS), jnp.tile(k_gain, G),
                            jnp.ones((kw,), F32)])
    flag = jnp.concatenate([jnp.ones((nq + kw,), F32), jnp.zeros((kw,), F32)])
    h = matmul(xm, w_in.astype(BF16), bias=b_in, head_norm=(hd, hd, gain, flag), name="in_proj_odd")
    o_c = swa_attention(h, sinks, B, T, nq, nq + kw)
    return matmul(o_c, w_out.astype(BF16), bias=b_out, resid=(x2d, g_a, T), out_dtype=F32,
                  name="out_proj_odd")


def kernel(x, c, w_mod, mod_table, norm_attn, norm_ffn, w_in_even, w_out_even, nsa_q_gain, nsa_k_gain,
           nsa_pe_k, nsa_pe_v, nsa_w_ck1, nsa_w_ck2, nsa_w_cv1, nsa_w_cv2, mla_g_cq, mla_g_ckv, mla_w_uq,
           mla_w_ukv, mla_q_gain, mla_k_gain, w_in_odd, b_in_odd, w_out_odd, b_out_odd, swa_q_gain,
           swa_k_gain, swa_sinks, router_w, router_b, moe_w_gate, moe_b_gate, moe_w_up, moe_b_up,
           moe_w_down, moe_b_down):
    B, T, D = x.shape
    N = B * T
    depth = mod_table.shape[0]
    c_pad = jnp.pad(c, ((0, 8 - B % 8 if B % 8 else 0), (0, 0)))
    cond = matmul(c_pad, w_mod, a_pro="silu", out_dtype=F32, tn=1024, tk=1024, name="adaln_proj")[:B]
    x2d = x.reshape(N, D)
    for layer in range(depth):
        mod = (cond + mod_table[layer]).reshape(B, 6, 1, D)
        sh_a, sc_a, g_a, sh_f, sc_f, g_f = (mod[:, j] for j in range(6))
        xm = norm_mod(x2d, norm_attn[layer], sc_a, sh_a, T)
        i = layer // 2
        if layer % 2 == 0:
            x2d = _even_mixer(xm, x2d, g_a, B, T, w_in_even[i], w_out_even[i], nsa_q_gain[i], nsa_k_gain[i],
                              nsa_pe_k[i], nsa_pe_v[i], nsa_w_ck1[i], nsa_w_ck2[i], nsa_w_cv1[i],
                              nsa_w_cv2[i], mla_g_cq[i], mla_g_ckv[i], mla_w_uq[i], mla_w_ukv[i],
                              mla_q_gain[i], mla_k_gain[i])
        else:
            x2d = _odd_mixer(xm, x2d, g_a, B, T, w_in_odd[i], b_in_odd[i], w_out_odd[i], b_out_odd[i],
                             swa_q_gain[i], swa_k_gain[i], swa_sinks[i])
        x2d = moe_layer(x2d, norm_ffn[layer], sc_f, sh_f, g_f, T, router_w[layer], router_b[layer],
                        layer, moe_w_gate, moe_b_gate, moe_w_up, moe_b_up, moe_w_down, moe_b_down)
    return x2d.reshape(B, T, D)
```

```python
import functools
import math

import numpy as np
import jax
import jax.numpy as jnp
from jax import lax
from jax.experimental import pallas as pl
from jax.experimental.pallas import tpu as pltpu

BF16 = jnp.bfloat16
F32 = jnp.float32

NSA_HEADS = 16
NSA_KV_HEADS = 4
NSA_QK_DIM = 192
NSA_V_DIM = 128
CMP_BLOCK = 32
CMP_STRIDE = 16
SLC_BLOCK = 64
SLC_TOPN = 8
NSA_WINDOW = 512
MLA_HEADS = 16
MLA_Q_RANK = 1024
MLA_KV_RANK = 512
MLA_NOPE_DIM = 128
MLA_ROPE_DIM = 64
MLA_V_DIM = 128
ROPE_THETA = 10000.0
SWA_HEADS = 64
SWA_KV_HEADS = 8
SWA_HEAD_DIM = 64
SWA_WINDOW = 128
N_EXPERTS = 32
TOP_K = 4
SWIGLU_ALPHA = 1.702
SWIGLU_LIMIT = 7.0
MOE_ROW_BLOCK = 256
EPS = 1e-6

LANES = 128
HEAD_PAD = 256
NEG = -1e30
M_FLOOR = -1e29

EVEN_IN_SPLITS = (NSA_HEADS * NSA_QK_DIM,
                  NSA_KV_HEADS * NSA_QK_DIM, NSA_KV_HEADS * NSA_V_DIM,
                  NSA_KV_HEADS * NSA_QK_DIM, NSA_KV_HEADS * NSA_V_DIM,
                  NSA_KV_HEADS * NSA_QK_DIM, NSA_KV_HEADS * NSA_V_DIM,
                  3 * NSA_HEADS, MLA_Q_RANK, MLA_KV_RANK + MLA_ROPE_DIM)

_NQ = NSA_HEADS * HEAD_PAD
_NK = NSA_KV_HEADS * HEAD_PAD
_NV = NSA_KV_HEADS * NSA_V_DIM
EV_Q = 0
EV_KC = EV_Q + _NQ
EV_KS = EV_KC + _NK
EV_KW = EV_KS + _NK
EV_CQ = EV_KW + _NK
EV_CKV = EV_CQ + MLA_Q_RANK
EV_KR = EV_CKV + MLA_KV_RANK
EV_VC = EV_KR + 2 * MLA_ROPE_DIM
EV_VS = EV_VC + _NV
EV_VW = EV_VS + _NV
EV_GATE = EV_VW + _NV
EV_END = EV_GATE + LANES
EV_WIDTH = -(-EV_END // 512) * 512


def _tile(dim, want):
    t = min(dim, want)
    while dim % t:
        t //= 2
    return t


def _alibi_np(n):
    return np.exp2(-8.0 * np.arange(1, n + 1, dtype=np.float32) / n).astype(np.float32)


def _alibi(n):
    return jnp.asarray(_alibi_np(n))


def _head_norm(res, hd, real_d):
    tm, tn = res.shape
    x2 = res * res
    pieces = []
    if hd == HEAD_PAD:
        for s in range(tn // hd):
            ss = jnp.sum(x2[:, s * hd:s * hd + LANES] + x2[:, s * hd + LANES:(s + 1) * hd],
                         axis=1, keepdims=True)
            r = lax.rsqrt(ss * (1.0 / real_d) + EPS)
            pieces.append(jnp.broadcast_to(r, (tm, hd)))
    else:
        lo = lax.broadcasted_iota(jnp.int32, (tm, LANES), 1) < hd
        for s in range(tn // LANES):
            c = x2[:, s * LANES:(s + 1) * LANES]
            ss_lo = jnp.sum(jnp.where(lo, c, 0.0), axis=1, keepdims=True)
            ss_hi = jnp.sum(jnp.where(lo, 0.0, c), axis=1, keepdims=True)
            r_lo = lax.rsqrt(ss_lo * (1.0 / real_d) + EPS)
            r_hi = lax.rsqrt(ss_hi * (1.0 / real_d) + EPS)
            pieces.append(jnp.where(lo, r_lo, r_hi))
    return jnp.concatenate(pieces, axis=1) if len(pieces) > 1 else pieces[0]


def _mm_kernel(*refs, nk, a_pro, has_a2, has_bias, hd, real_d, has_resid):
    it = iter(refs)
    a_ref = next(it)
    a2_ref = next(it) if has_a2 else None
    w_ref = next(it)
    again_ref = next(it) if a_pro == "rms" else None
    b_ref = next(it) if has_bias else None
    gain_ref = flag_ref = None
    if hd:
        gain_ref = next(it)
        flag_ref = next(it)
    x_ref = gate_ref = None
    if has_resid:
        x_ref = next(it)
        gate_ref = next(it)
    o_ref = next(it)
    acc_ref = next(it) if nk > 1 else None

    a = a_ref[...]
    if a_pro == "silu":
        af = a.astype(F32)
        a = af * jax.nn.sigmoid(af)
    elif a_pro == "rms":
        af = a.astype(F32)
        r = lax.rsqrt(jnp.mean(af * af, axis=1, keepdims=True) + EPS)
        a = af * r * again_ref[...]
    if has_a2:
        k1 = a.shape[1]
        part = (jnp.dot(a.astype(BF16), w_ref[:k1, :].astype(BF16), preferred_element_type=F32)
                + jnp.dot(a2_ref[...].astype(BF16), w_ref[k1:, :].astype(BF16), preferred_element_type=F32))
    else:
        part = jnp.dot(a.astype(BF16), w_ref[...].astype(BF16), preferred_element_type=F32)

    def finish(res):
        if has_bias:
            res = res + b_ref[...]
        if hd:
            r = _head_norm(res, hd, real_d)
            res = res * jnp.where(flag_ref[...] > 0.0, r, 1.0) * gain_ref[...]
        if has_resid:
            res = x_ref[...] + gate_ref[...] * res
        o_ref[...] = res.astype(o_ref.dtype)

    if nk == 1:
        finish(part)
    else:
        k = pl.program_id(2)

        @pl.when(k == 0)
        def _():
            acc_ref[...] = part

        @pl.when(k > 0)
        def _():
            acc_ref[...] += part

        @pl.when(k == nk - 1)
        def _():
            finish(acc_ref[...])


def matmul(a, w, *, a2=None, a_col0=0, a_pro=None, a_gain=None, bias=None,
           head_norm=None, resid=None, out_dtype=BF16, tm=1024, tn=512, tk=4096, name="mm"):
    M = a.shape[0]
    K, N = w.shape
    tm, tn, tk = _tile(M, tm), _tile(N, tn), _tile(K, tk)
    if a_pro == "rms" or a2 is not None:
        tk = K
    assert a_col0 % tk == 0 and M % tm == 0 and N % tn == 0 and K % tk == 0
    nk = K // tk
    koff = a_col0 // tk
    hd = head_norm[0] if head_norm else 0
    real_d = head_norm[1] if head_norm else 0
    if hd:
        assert tn % max(hd, LANES) == 0

    if a2 is None:
        ins = [a, w]
        specs = [pl.BlockSpec((tm, tk), lambda i, j, k: (i, koff + k))]
    else:
        assert a_col0 == 0 and a_pro is None and a.shape[1] + a2.shape[1] == K
        ins = [a, a2, w]
        specs = [pl.BlockSpec((tm, a.shape[1]), lambda i, j, k: (i, 0)),
                 pl.BlockSpec((tm, a2.shape[1]), lambda i, j, k: (i, 0))]
    specs.append(pl.BlockSpec((tk, tn), lambda i, j, k: (k, j)))
    if a_pro == "rms":
        ins.append(a_gain.reshape(1, K).astype(F32))
        specs.append(pl.BlockSpec((1, tk), lambda i, j, k: (0, k)))
    if bias is not None:
        ins.append(bias.reshape(1, N).astype(F32))
        specs.append(pl.BlockSpec((1, tn), lambda i, j, k: (0, j)))
    if hd:
        ins += [head_norm[2].reshape(1, N).astype(F32), head_norm[3].reshape(1, N).astype(F32)]
        specs += [pl.BlockSpec((1, tn), lambda i, j, k: (0, j))] * 2
    if resid is not None:
        x, gate, rows_per_batch = resid
        assert rows_per_batch % tm == 0
        bpb = rows_per_batch // tm
        ins += [x, gate]
        specs += [pl.BlockSpec((tm, tn), lambda i, j, k: (i, j)),
                  pl.BlockSpec((None, 1, tn), lambda i, j, k: (i // bpb, 0, j))]
    kern = functools.partial(_mm_kernel, nk=nk, a_pro=a_pro, has_a2=a2 is not None, has_bias=bias is not None,
                             hd=hd, real_d=real_d, has_resid=resid is not None)
    return pl.pallas_call(
        kern,
        out_shape=jax.ShapeDtypeStruct((M, N), out_dtype),
        grid=(M // tm, N // tn, nk),
        in_specs=specs,
        out_specs=pl.BlockSpec((tm, tn), lambda i, j, k: (i, j)),
        scratch_shapes=[pltpu.VMEM((tm, tn), F32)] if nk > 1 else [],
        compiler_params=pltpu.CompilerParams(
            dimension_semantics=("parallel", "parallel", "arbitrary")),
        name=name,
    )(*ins)


def _split_bf16(v):
    hi = v.astype(BF16)
    lo = (v - hi.astype(F32)).astype(BF16)
    return hi, lo


def _pack_bf16_pairs(v):
    half = v.shape[1] // 2
    vb = v.astype(BF16).astype(F32)
    lo = lax.shift_right_logical(lax.bitcast_convert_type(vb[:, :half], jnp.int32), 16)
    hi = lax.bitcast_convert_type(vb[:, half:], jnp.int32) & jnp.int32(-65536)
    return hi | lo


def _unpack_bf16_pairs(u):
    lo = lax.bitcast_convert_type(lax.shift_left(u, 16), F32)
    hi = lax.bitcast_convert_type(u & jnp.int32(-65536), F32)
    return jnp.concatenate([lo, hi], axis=1).astype(BF16)


def _norm_mod_kernel(x_ref, g_ref, sc_ref, sh_ref, *rest, route):
    x = x_ref[...]
    r = lax.rsqrt(jnp.mean(x * x, axis=1, keepdims=True) + EPS)
    xm = (x * r * g_ref[...]) * (1.0 + sc_ref[...]) + sh_ref[...]
    if not route:
        (o_ref,) = rest
        o_ref[...] = xm.astype(o_ref.dtype)
        return
    rw_ref, rb_ref, o_ref, e_ref, p_ref, rank_ref, cnt_ref, run_ref = rest
    o_ref[...] = _pack_bf16_pairs(xm)
    a_hi, a_lo = _split_bf16(xm)
    w = rw_ref[...]
    w_hi, w_lo = _split_bf16(w)
    logits = (jnp.dot(a_hi, w_hi, preferred_element_type=F32)
              + jnp.dot(a_hi, w_lo, preferred_element_type=F32)
              + jnp.dot(a_lo, w_hi, preferred_element_type=F32)) + rb_ref[...]
    tt = logits.shape[0]
    lane = lax.broadcasted_iota(jnp.int32, (tt, LANES), 1)
    lane_f = lane.astype(F32)
    work = jnp.where(lane < N_EXPERTS, logits, -jnp.inf)
    e_out = jnp.zeros((tt, LANES), F32)
    v_out = jnp.full((tt, LANES), -jnp.inf, F32)
    hits = []
    for kk in range(TOP_K):
        m = jnp.max(work, axis=1, keepdims=True)
        idx = jnp.min(jnp.where(work == m, lane_f, float(LANES)), axis=1, keepdims=True)
        e_out = jnp.where(lane == kk, idx, e_out)
        v_out = jnp.where(lane == kk, m, v_out)
        hits.append(lane_f == idx)
        work = jnp.where(hits[-1], -jnp.inf, work)
    e_out = e_out.astype(jnp.int32)
    vmax = jnp.max(v_out, axis=1, keepdims=True)
    pe = jnp.exp(v_out - vmax)
    p_ref[...] = pe / jnp.sum(pe, axis=1, keepdims=True)
    e_ref[...] = e_out

    @pl.when(pl.program_id(0) == 0)
    def _():
        run_ref[...] = jnp.zeros_like(run_ref)

    onehot = jnp.zeros((tt, LANES), F32)
    for hit in hits:
        onehot = jnp.where(hit, 1.0, onehot)
    earlier = (lax.broadcasted_iota(jnp.int32, (tt, tt), 1)
               < lax.broadcasted_iota(jnp.int32, (tt, tt), 0))
    prefix = jnp.dot(jnp.where(earlier, 1.0, 0.0).astype(BF16), onehot.astype(BF16),
                     preferred_element_type=F32)
    base = run_ref[...] + prefix
    rank = jnp.zeros((tt, LANES), F32)
    for kk, hit in enumerate(hits):
        rk = jnp.sum(jnp.where(hit, base, 0.0), axis=1, keepdims=True)
        rank = jnp.where(lane == kk, rk, rank)
    rank_ref[...] = rank.astype(jnp.int32)
    total = run_ref[...] + jnp.sum(onehot, axis=0, keepdims=True)
    run_ref[...] = total
    cnt_ref[...] = total.astype(jnp.int32)


def norm_mod(x2d, gain, sc, sh, rows_per_batch, *, out_dtype=BF16, router=None, tt=256):
    N, D = x2d.shape
    tt = _tile(rows_per_batch, tt)
    bpb = rows_per_batch // tt
    ins = [x2d, gain.reshape(1, D), sc, sh]
    specs = [pl.BlockSpec((tt, D), lambda i: (i, 0)),
             pl.BlockSpec((1, D), lambda i: (0, 0)),
             pl.BlockSpec((None, 1, D), lambda i: (i // bpb, 0, 0)),
             pl.BlockSpec((None, 1, D), lambda i: (i // bpb, 0, 0))]
    if router is None:
        out_shape = [jax.ShapeDtypeStruct((N, D), out_dtype)]
        out_specs = [pl.BlockSpec((tt, D), lambda i: (i, 0))]
        scratch = []
    else:
        rw, rb = router
        E = rw.shape[1]
        rw_p = jnp.pad(rw, ((0, 0), (0, LANES - E)))
        rb_p = jnp.pad(rb, (0, LANES - E)).reshape(1, LANES)
        ins += [rw_p, rb_p]
        specs += [pl.BlockSpec((D, LANES), lambda i: (0, 0)),
                  pl.BlockSpec((1, LANES), lambda i: (0, 0))]
        out_shape = [jax.ShapeDtypeStruct((N, D // 2), jnp.int32),
                     jax.ShapeDtypeStruct((N, LANES), jnp.int32),
                     jax.ShapeDtypeStruct((N, LANES), F32),
                     jax.ShapeDtypeStruct((N, LANES), jnp.int32),
                     jax.ShapeDtypeStruct((1, LANES), jnp.int32)]
        row = pl.BlockSpec((tt, LANES), lambda i: (i, 0))
        out_specs = [pl.BlockSpec((tt, D // 2), lambda i: (i, 0)), row, row, row,
                     pl.BlockSpec((1, LANES), lambda i: (0, 0))]
        scratch = [pltpu.VMEM((1, LANES), F32)]
    res = pl.pallas_call(
        functools.partial(_norm_mod_kernel, route=router is not None),
        out_shape=out_shape, grid=(N // tt,), in_specs=specs, out_specs=out_specs,
        scratch_shapes=scratch,
        compiler_params=pltpu.CompilerParams(
            dimension_semantics=("parallel",) if router is None else ("arbitrary",)),
        name="norm_mod_route" if router is not None else "norm_mod",
    )(*ins)
    return res if router is not None else res[0]


def _compress_kernel(h_ref, pe_ref, w1_ref, w2_ref, g_ref, o_ref, xf_ref, *, real_d, norm, nh):
    n_tiles = xf_ref.shape[0]
    for j in range(n_tiles):
        xf_ref[j] = h_ref[:, j * LANES:(j + 1) * LANES].astype(F32)
    half = CMP_BLOCK // 2
    top = bot = None
    for l in range(half):
        parts = [xf_ref[j, pl.ds(l, nh, stride=CMP_STRIDE), :] for j in range(n_tiles)]
        xl = jnp.concatenate(parts, axis=1) if n_tiles > 1 else parts[0]
        t = jnp.dot((xl + pe_ref[l:l + 1, :]).astype(BF16), w1_ref[l], preferred_element_type=F32)
        b = jnp.dot((xl + pe_ref[half + l:half + l + 1, :]).astype(BF16), w1_ref[half + l],
                    preferred_element_type=F32)
        top = t if top is None else top + t
        bot = b if bot is None else bot + b
    hid = top + pltpu.roll(bot, nh - 1, 0)
    hid = hid * jax.nn.sigmoid(hid)
    y = jnp.dot(hid.astype(BF16), w2_ref[...], preferred_element_type=F32)
    if norm:
        r = lax.rsqrt(jnp.sum(y * y, axis=1, keepdims=True) * (1.0 / real_d) + EPS)
        y = y * r * g_ref[...]
    o_ref[...] = y.astype(o_ref.dtype)


def compress(h, col0, B, T, pe, w1, w2, gain, d, dp):
    G = NSA_KV_HEADS
    nh = T // CMP_STRIDE
    cb = col0 // dp
    pe_p = jnp.pad(pe, ((0, 0), (0, dp - d)))
    w1_p = jnp.pad(w1.reshape(CMP_BLOCK, d, d), ((0, 0), (0, dp - d), (0, dp - d))).astype(BF16)
    w2_p = jnp.pad(w2, ((0, dp - d), (0, dp - d))).astype(BF16)
    g_p = (jnp.ones((dp,), F32) if gain is None else jnp.pad(gain, (0, dp - d))).reshape(1, dp)
    full = lambda shape: pl.BlockSpec(shape, lambda b, g: (0,) * len(shape))
    return pl.pallas_call(
        functools.partial(_compress_kernel, real_d=d, norm=gain is not None, nh=nh),
        out_shape=jax.ShapeDtypeStruct((B * G, nh, dp), BF16),
        grid=(B, G),
        in_specs=[pl.BlockSpec((T, dp), lambda b, g: (b, cb + g)),
                  full((CMP_BLOCK, dp)), full((CMP_BLOCK, dp, dp)), full((dp, dp)), full((1, dp))],
        out_specs=pl.BlockSpec((None, nh, dp), lambda b, g: (b * G + g, 0, 0)),
        scratch_shapes=[pltpu.VMEM((dp // LANES, T, LANES), F32)],
        compiler_params=pltpu.CompilerParams(dimension_semantics=("parallel", "parallel")),
        name="nsa_compress",
    )(h, pe_p, w1_p, w2_p, g_p)


def _stack_heads(q_ref, n, w):
    return jnp.concatenate([q_ref[:, r * w:(r + 1) * w] for r in range(n)], axis=0)


def _row_scalars(vals, tq):
    rows = len(vals) * tq
    rid = lax.broadcasted_iota(jnp.int32, (rows, 1), 0) // tq
    col = jnp.full((rows, 1), vals[-1], F32)
    for r in range(len(vals) - 2, -1, -1):
        col = jnp.where(rid == r, vals[r], col)
    return col


def _nt_dot(a, b):
    return lax.dot_general(a, b, (((1,), (1,)), ((), ())), preferred_element_type=F32)


def _cmp_attn_kernel(slope_ref, q_ref, kc_ref, vc_ref, ovl_ref, o_ref, bits_ref, *, tq, R, n_slc):
    g = pl.program_id(1)
    t0 = pl.program_id(2) * tq
    rows = R * tq
    q = _stack_heads(q_ref, R, HEAD_PAD)
    s = _nt_dot(q, kc_ref[...])
    ncol = s.shape[1]
    row_t = t0 + lax.broadcasted_iota(jnp.int32, (rows, ncol), 0) % tq
    n_id = lax.broadcasted_iota(jnp.int32, (rows, ncol), 1)
    dist = row_t - (n_id * CMP_STRIDE + (CMP_BLOCK - 1))
    slope = _row_scalars([slope_ref[g * R + r] for r in range(R)], tq)
    valid = dist >= 0
    s = jnp.where(valid, s - slope * dist.astype(F32), NEG)
    m = jnp.max(s, axis=1, keepdims=True)
    p = jnp.where(valid, jnp.exp(s - m), 0.0)
    p = p / jnp.maximum(jnp.sum(p, axis=1, keepdims=True), 1e-30)
    o = jnp.dot(p.astype(BF16), vc_ref[...], preferred_element_type=F32)
    for r in range(R):
        o_ref[:, r * NSA_V_DIM:(r + 1) * NSA_V_DIM] = o[r * tq:(r + 1) * tq].astype(o_ref.dtype)

    psum = p[0:tq]
    for r in range(1, R):
        psum = psum + p[r * tq:(r + 1) * tq]
    p_hi, p_lo = _split_bf16(psum)
    imp = (jnp.dot(p_hi, ovl_ref[...], preferred_element_type=F32)
           + jnp.dot(p_lo, ovl_ref[...], preferred_element_type=F32))
    lane = lax.broadcasted_iota(jnp.int32, (tq, LANES), 1)
    cur = (t0 + lax.broadcasted_iota(jnp.int32, (tq, LANES), 0)) // SLC_BLOCK
    forced = (lane == 0) | (lane == cur) | (lane == cur - 1)
    work = jnp.where(forced, jnp.inf, jnp.where(lane > cur, -jnp.inf, imp))
    removed = -3.0e38
    work = jnp.where(lane < n_slc, jnp.where(work == -jnp.inf, -2.0e38, work), removed)
    sel = jnp.zeros((tq, LANES), jnp.bool_)
    lane_f = lane.astype(F32)
    for _ in range(min(SLC_TOPN, n_slc)):
        mx = jnp.max(work, axis=1, keepdims=True)
        idx = jnp.min(jnp.where(work == mx, lane_f, float(LANES)), axis=1, keepdims=True)
        hit = lane_f == idx
        sel = sel | hit
        work = jnp.where(hit, removed, work)
    half = 16
    w_lo = jnp.where(sel & (lane < half), jnp.left_shift(1, jnp.minimum(lane, half - 1)), 0)
    w_hi = jnp.where(sel & (lane >= half), jnp.left_shift(1, jnp.clip(lane - half, 0, half - 1)), 0)
    b_lo = jnp.sum(w_lo.astype(F32), axis=1, keepdims=True).astype(jnp.int32)
    b_hi = jnp.sum(w_hi.astype(F32), axis=1, keepdims=True).astype(jnp.int32)
    bits = b_lo | jnp.left_shift(b_hi, half)
    bits_ref[...] = jnp.broadcast_to(bits, (tq, LANES))


def cmp_attention(h, kc, vc, B, T, tq=256):
    G, R = NSA_KV_HEADS, NSA_HEADS // NSA_KV_HEADS
    tq = _tile(T, tq)
    nq = T // tq
    n_slc = T // SLC_BLOCK
    n_cmp = kc.shape[1]
    assert n_slc <= 32 and n_cmp <= LANES and n_cmp % 8 == 0
    cmp_start = np.arange(n_cmp) * CMP_STRIDE
    slc_start = np.arange(LANES) * SLC_BLOCK
    ovl = ((cmp_start[:, None] < slc_start[None, :] + SLC_BLOCK)
           & (cmp_start[:, None] + CMP_BLOCK > slc_start[None, :])
           & (np.arange(LANES)[None, :] < n_slc) & (np.arange(n_cmp)[:, None] < n_cmp - 1))
    ovl = jnp.asarray(ovl, BF16)
    qw = R * HEAD_PAD
    return pl.pallas_call(
        functools.partial(_cmp_attn_kernel, tq=tq, R=R, n_slc=n_slc),
        out_shape=[jax.ShapeDtypeStruct((B * T, NSA_HEADS * NSA_V_DIM), BF16),
                   jax.ShapeDtypeStruct((B, G, T, LANES), jnp.int32)],
        grid=(B, G, nq),
        in_specs=[pl.BlockSpec(memory_space=pltpu.SMEM),
                  pl.BlockSpec((tq, qw), lambda b, g, i: (b * nq + i, EV_Q // qw + g)),
                  pl.BlockSpec((None, n_cmp, HEAD_PAD), lambda b, g, i: (b * G + g, 0, 0)),
                  pl.BlockSpec((None, n_cmp, NSA_V_DIM), lambda b, g, i: (b * G + g, 0, 0)),
                  pl.BlockSpec((n_cmp, LANES), lambda b, g, i: (0, 0))],
        out_specs=[pl.BlockSpec((tq, R * NSA_V_DIM), lambda b, g, i: (b * nq + i, g)),
                   pl.BlockSpec((None, None, tq, LANES), lambda b, g, i: (b, g, i, 0))],
        compiler_params=pltpu.CompilerParams(dimension_semantics=("parallel", "parallel", "parallel")),
        name="nsa_cmp_attn",
    )(_alibi(NSA_HEADS), h, kc, vc, ovl)


ALIBI_LANE0 = 64
SEL_LANE0 = 70


def _key_table(T):
    s = np.arange(T)
    tab = np.zeros((T, LANES), np.float32)
    tab[:, ALIBI_LANE0:ALIBI_LANE0 + 3] = (s // 256 * 256)[:, None]
    tab[:, ALIBI_LANE0 + 3:ALIBI_LANE0 + 6] = (s % 256)[:, None]
    tab[s, SEL_LANE0 + s // SLC_BLOCK] = 1.0
    return jnp.asarray(tab, BF16)


def _slope_parts(n_heads):
    s = _alibi_np(n_heads)
    h1 = s.astype(BF16).astype(np.float32)
    h2 = (s - h1).astype(BF16).astype(np.float32)
    h3 = (s - h1 - h2).astype(BF16).astype(np.float32)
    return jnp.asarray(np.stack([h1, h2, h3], axis=1).reshape(-1))


def _flash_kernel(*refs, tq, tk, R, HP, v_off, v_step, window, alibi, use_bits):
    it = iter(refs)
    slope_ref = next(it) if alibi else None
    q_ref = next(it)
    k_ref = next(it)
    v_ref = next(it)
    ktab_ref = next(it) if alibi else None
    bits_ref = next(it) if use_bits else None
    o_ref = next(it)
    m_ref = next(it)
    acc_ref = next(it)

    g = pl.program_id(1)
    t0 = pl.program_id(2) * tq
    rows = R * tq
    vd = NSA_V_DIM
    lane = lax.broadcasted_iota(jnp.int32, (tq, LANES), 1)
    sel_add = None
    if use_bits:
        j = jnp.clip(lane - SEL_LANE0, 0, 31)
        in_sel = (lane >= SEL_LANE0) & (lane < SEL_LANE0 + 32)
        picked = (jnp.right_shift(bits_ref[...], j) & 1) != 0
        sel_add = jnp.where(in_sel & jnp.logical_not(picked), NEG, 0.0)

    qs = []
    for hp in range(HP):
        pieces = []
        for r in range(R):
            c0 = (hp * R + r) * HEAD_PAD
            q_lo = q_ref[:, c0:c0 + LANES]
            q_hi = q_ref[:, c0 + LANES:c0 + HEAD_PAD]
            if alibi:
                hd = (g * HP + hp) * R + r
                add = jnp.zeros((tq, LANES), F32) if sel_add is None else sel_add
                for part in range(3):
                    sp = slope_ref[3 * hd + part]
                    add = jnp.where((lane == ALIBI_LANE0 + part) | (lane == ALIBI_LANE0 + 3 + part), sp, add)
                q_hi = (q_hi.astype(F32) + add).astype(BF16)
            pieces.append(jnp.concatenate([q_lo, q_hi], axis=1))
        qs.append(jnp.concatenate(pieces, axis=0) if R > 1 else pieces[0])

    rel0 = (lax.broadcasted_iota(jnp.int32, (rows, tk), 0) % tq
            - lax.broadcasted_iota(jnp.int32, (rows, tk), 1))
    ones = jnp.ones((tk, LANES), BF16)
    m_ref[...] = jnp.full(m_ref.shape, M_FLOOR, F32)
    acc_ref[...] = jnp.zeros(acc_ref.shape, F32)

    def step(c, masked):
        s0 = pl.multiple_of(c * tk, tk)
        off = t0 - s0
        if masked:
            valid = rel0 >= -off
            if window is not None:
                valid = valid & (rel0 < window - off)
        for hp in range(HP):
            kc = k_ref[pl.ds(s0, tk), hp * HEAD_PAD:(hp + 1) * HEAD_PAD]
            if alibi:
                kc = jnp.concatenate([kc[:, :LANES], kc[:, LANES:] + ktab_ref[pl.ds(s0, tk), :]], axis=1)
            vc = v_ref[pl.ds(s0, tk), v_off + hp * v_step:v_off + hp * v_step + vd]
            s = _nt_dot(qs[hp], kc)
            if masked:
                s = jnp.where(valid, s, NEG)
            tiles = [s[:, j * LANES:(j + 1) * LANES] for j in range(tk // LANES)]
            mx = tiles[0]
            for t in tiles[1:]:
                mx = jnp.maximum(mx, t)
            m_old = m_ref[hp]
            m_new = jnp.maximum(m_old, jnp.max(mx, axis=1, keepdims=True))
            alpha = jnp.exp(m_old - m_new)
            p = jnp.concatenate([jnp.exp(t - m_new).astype(BF16) for t in tiles], axis=1)
            pv = jnp.dot(p, jnp.concatenate([vc, ones], axis=1), preferred_element_type=F32)
            acc_ref[hp] = jnp.concatenate([alpha, alpha], axis=1) * acc_ref[hp] + pv
            m_ref[hp] = m_new

    def body(c, carry):
        s0 = c * tk
        full = s0 + tk - 1 <= t0
        if window is not None:
            full = full & (s0 >= t0 + tq - window)
        lax.cond(full, lambda: step(c, False), lambda: step(c, True))
        return carry

    c_hi = (t0 + tq - 1) // tk
    c_lo = 0 if window is None else jnp.maximum((t0 - window + 1) // tk, 0)
    lax.fori_loop(c_lo, c_hi + 1, body, 0)
    for hp in range(HP):
        acc = acc_ref[hp]
        o = acc[:, :vd] / jnp.maximum(acc[:, vd:], 1e-30)
        for r in range(R):
            c0 = (hp * R + r) * vd
            o_ref[:, c0:c0 + vd] = o[r * tq:(r + 1) * tq].astype(o_ref.dtype)


def flash_attention(q_arr, q_col0, k_arr, k_col0, v_arr, v_col0, *, B, T, G, R, HP=1, v_width=NSA_V_DIM,
                    v_off=0, v_step=0, out_cols, out_col0=0, window=None, n_alibi_heads=0, bits=None,
                    tq=256, tk=256, name="flash"):
    vd = NSA_V_DIM
    tq = _tile(T, tq)
    tk = _tile(T, tk)
    nq = T // tq
    qw, kw, ow = HP * R * HEAD_PAD, HP * HEAD_PAD, HP * R * vd
    assert q_col0 % qw == 0 and k_col0 % kw == 0 and v_col0 % v_width == 0 and out_col0 % ow == 0
    assert bits is None or n_alibi_heads
    qb, kb, vb, ob = q_col0 // qw, k_col0 // kw, v_col0 // v_width, out_col0 // ow
    alibi = n_alibi_heads > 0
    ins, specs = [], []
    if alibi:
        ins.append(_slope_parts(n_alibi_heads))
        specs.append(pl.BlockSpec(memory_space=pltpu.SMEM))
    ins += [q_arr, k_arr, v_arr]
    specs += [pl.BlockSpec((tq, qw), lambda b, g, i: (b * nq + i, qb + g)),
              pl.BlockSpec((T, kw), lambda b, g, i: (b, kb + g)),
              pl.BlockSpec((T, v_width), lambda b, g, i: (b, vb + g))]
    if alibi:
        ins.append(_key_table(T))
        specs.append(pl.BlockSpec((T, LANES), lambda b, g, i: (0, 0)))
    if bits is not None:
        ins.append(bits)
        specs.append(pl.BlockSpec((None, None, tq, LANES), lambda b, g, i: (b, g, i, 0)))
    rows = R * tq
    return pl.pallas_call(
        functools.partial(_flash_kernel, tq=tq, tk=tk, R=R, HP=HP, v_off=v_off, v_step=v_step,
                          window=window, alibi=alibi, use_bits=bits is not None),
        out_shape=jax.ShapeDtypeStruct((B * T, out_cols), BF16),
        grid=(B, G // HP, nq),
        in_specs=specs,
        out_specs=pl.BlockSpec((tq, ow), lambda b, g, i: (b * nq + i, ob + g)),
        scratch_shapes=[pltpu.VMEM((HP, rows, LANES), F32), pltpu.VMEM((HP, rows, 2 * vd), F32)],
        compiler_params=pltpu.CompilerParams(dimension_semantics=("parallel", "parallel", "parallel")),
        name=name,
    )(*ins)


def _nsa_combine_kernel(gate_ref, oc_ref, os_ref, ow_ref, o_ref):
    gates = jax.nn.sigmoid(gate_ref[...].astype(F32))
    for h in range(NSA_HEADS):
        sl = slice(h * NSA_V_DIM, (h + 1) * NSA_V_DIM)
        acc = None
        for br, ref in enumerate((oc_ref, os_ref, ow_ref)):
            gcol = gates[:, 3 * h + br:3 * h + br + 1]
            term = gcol * ref[:, sl].astype(F32)
            acc = term if acc is None else acc + term
        o_ref[:, sl] = acc.astype(o_ref.dtype)


def nsa_combine(h, o_cmp, o_slc, o_win, tt=256):
    N, W = o_cmp.shape
    tt = _tile(N, tt)
    gb = EV_GATE // LANES
    blk = pl.BlockSpec((tt, W), lambda i: (i, 0))
    return pl.pallas_call(
        _nsa_combine_kernel,
        out_shape=jax.ShapeDtypeStruct((N, W), BF16),
        grid=(N // tt,),
        in_specs=[pl.BlockSpec((tt, LANES), lambda i: (i, gb)), blk, blk, blk],
        out_specs=blk,
        compiler_params=pltpu.CompilerParams(dimension_semantics=("parallel",)),
        name="nsa_combine",
    )(h, o_cmp, o_slc, o_win)


def _mla_prep_kernel(*refs, shared_rope):
    if shared_rope:
        x_ref, hi_ref, alo_ref, ahi_ref, bhi_ref, o_ref = refs
    else:
        x_ref, alo_ref, ahi_ref, bhi_ref, o_ref = refs
    tt = x_ref.shape[0]
    is_rope = lax.broadcasted_iota(jnp.int32, (tt, LANES), 1) < MLA_ROPE_DIM
    a_hi, b_hi, a_lo = ahi_ref[...], bhi_ref[...], alo_ref[...]

    def rope_part(x_hi):
        ss = jnp.sum(jnp.where(is_rope, x_hi * x_hi, 0.0), axis=1, keepdims=True)
        return x_hi * a_hi + pltpu.roll(x_hi * b_hi, MLA_ROPE_DIM, 1), ss

    if shared_rope:
        y_hi, ss_hi = rope_part(hi_ref[...].astype(F32))
    for h in range(MLA_HEADS):
        c = h * HEAD_PAD
        x_lo = x_ref[:, c:c + LANES].astype(F32)
        if not shared_rope:
            y_hi, ss_hi = rope_part(x_ref[:, c + LANES:c + HEAD_PAD].astype(F32))
        ss = jnp.sum(x_lo * x_lo, axis=1, keepdims=True) + ss_hi
        r = lax.rsqrt(ss * (1.0 / (MLA_NOPE_DIM + MLA_ROPE_DIM)) + EPS)
        o_ref[:, c:c + LANES] = (x_lo * r * a_lo).astype(o_ref.dtype)
        o_ref[:, c + LANES:c + HEAD_PAD] = (y_hi * r).astype(o_ref.dtype)


def mla_prep(x_arr, rope_arr, rope_col0, a_lo, a_hi, b_hi, T, tt=256):
    N = x_arr.shape[0]
    W = MLA_HEADS * HEAD_PAD
    tt = _tile(T, tt)
    nt = T // tt
    shared = rope_arr is not None
    ins = [x_arr]
    specs = [pl.BlockSpec((tt, W), lambda i: (i, 0))]
    if shared:
        rb = rope_col0 // LANES
        ins.append(rope_arr)
        specs.append(pl.BlockSpec((tt, LANES), lambda i: (i, rb)))
    ins += [a_lo, a_hi, b_hi]
    specs += [pl.BlockSpec((1, LANES), lambda i: (0, 0)),
              pl.BlockSpec((tt, LANES), lambda i: (i % nt, 0)),
              pl.BlockSpec((tt, LANES), lambda i: (i % nt, 0))]
    return pl.pallas_call(
        functools.partial(_mla_prep_kernel, shared_rope=shared),
        out_shape=jax.ShapeDtypeStruct((N, W), BF16),
        grid=(N // tt,),
        in_specs=specs,
        out_specs=pl.BlockSpec((tt, W), lambda i: (i, 0)),
        compiler_params=pltpu.CompilerParams(dimension_semantics=("parallel",)),
        name="mla_prep",
    )(*ins)


def _rope_tables(gain, T, scale):
    half = MLA_ROPE_DIM // 2
    inv_freq = ROPE_THETA ** (-jnp.arange(half, dtype=F32) / half)
    ang = jnp.arange(T, dtype=F32)[:, None] * inv_freq[None, :]
    cos2 = jnp.concatenate([jnp.cos(ang), jnp.cos(ang)], axis=1)
    sin_s = jnp.concatenate([-jnp.sin(ang), jnp.sin(ang)], axis=1)
    g_nope, g_rope = gain[:MLA_NOPE_DIM], gain[MLA_NOPE_DIM:]
    g_perm = jnp.concatenate([g_rope[half:], g_rope[:half]])
    zeros = jnp.zeros((T, MLA_ROPE_DIM), F32)
    a_lo = (g_nope * scale).reshape(1, LANES)
    a_hi = jnp.concatenate([g_rope[None, :] * cos2 * scale, zeros], axis=1)
    b_hi = jnp.concatenate([zeros, g_perm[None, :] * sin_s * scale], axis=1)
    return a_lo, a_hi, b_hi


def _swa_kernel(qadd_ref, slope_ref, sink_ref, q_ref, k_ref, v_ref, ktab_ref, o_ref, mask_ref,
                *, tq, tk, R):
    t0 = pl.program_id(2) * tq
    npair = R // 2
    rows = npair * tq
    qp = _stack_heads(q_ref, npair, LANES)
    lane = lax.broadcasted_iota(jnp.int32, (rows, LANES), 1)
    first = lane < SWA_HEAD_DIM
    start = pl.multiple_of(jnp.maximum(t0 - SWA_WINDOW, 0), LANES)
    own_half = lax.broadcasted_iota(jnp.int32, (tk, LANES), 1) // SWA_HEAD_DIM == pl.program_id(1) % 2

    def both_halves(ref):
        pair = ref[pl.ds(start, tk), :].astype(F32)
        return jnp.where(own_half, pair, pltpu.roll(pair, SWA_HEAD_DIM, 1)).astype(BF16)

    kc = jnp.concatenate([both_halves(k_ref), ktab_ref[pl.ds(start, tk), :]], axis=1)
    v_aug = jnp.concatenate([both_halves(v_ref), jnp.ones((tk, LANES), BF16)], axis=1)
    @pl.when(pl.program_id(2) <= 1)
    def _():
        row_in = lax.broadcasted_iota(jnp.int32, (2 * rows, tk), 0) % tq
        rel = t0 - start + row_in - lax.broadcasted_iota(jnp.int32, (2 * rows, tk), 1)
        mask_ref[...] = jnp.where((rel >= 0) & (rel < SWA_WINDOW), 0.0, NEG)

    t_row = (t0 + lax.broadcasted_iota(jnp.int32, (2 * rows, LANES), 0) % tq).astype(F32)
    zero = jnp.zeros_like(qp)
    qu = jnp.concatenate([jnp.where(first, qp, zero), jnp.where(first, zero, qp)], axis=0)
    qa = jnp.concatenate([qu, qadd_ref[...].reshape(2 * rows, LANES)], axis=1)
    sink_t = (sink_ref[...].reshape(2 * rows, LANES)
              + slope_ref[...].reshape(2 * rows, LANES) * t_row)
    s = _nt_dot(qa, kc) + mask_ref[...]
    tiles = [s[:, j * LANES:(j + 1) * LANES] for j in range(tk // LANES)]
    mx = tiles[0]
    for t in tiles[1:]:
        mx = jnp.maximum(mx, t)
    m = jnp.maximum(jnp.max(mx, axis=1, keepdims=True), sink_t)
    p = jnp.concatenate([jnp.exp(t - m).astype(BF16) for t in tiles], axis=1)
    pv = jnp.dot(p, v_aug, preferred_element_type=F32)
    denom = pv[:, LANES:] + jnp.exp(sink_t - m)
    both = pv[:, :LANES] / jnp.maximum(denom, 1e-30)
    o = jnp.where(first, both[:rows], both[rows:])
    for p_ in range(npair):
        o_ref[:, p_ * LANES:(p_ + 1) * LANES] = o[p_ * tq:(p_ + 1) * tq].astype(o_ref.dtype)


def swa_attention(h, sinks, B, T, k_col0, v_col0, tq=128):
    G, R = SWA_KV_HEADS, SWA_HEADS // SWA_KV_HEADS
    tq = _tile(T, tq)
    tk = min(T, tq + SWA_WINDOW)
    assert tk % LANES == 0 and tq % LANES == 0
    nq = T // tq
    qw = R * SWA_HEAD_DIM
    kb, vb = k_col0 // LANES, v_col0 // LANES
    npair = R // 2
    rows = npair * tq
    per_head = lambda v: jnp.repeat(v.reshape(G, npair, 2).transpose(0, 2, 1), tq, axis=2)
    lanes = lambda v: jnp.broadcast_to(v[..., None], (G, 2, rows, LANES))
    parts = _slope_parts(SWA_HEADS).reshape(SWA_HEADS, 3)
    lane_id = np.arange(LANES)
    qadd = jnp.zeros((G, 2, rows, LANES), F32)
    for part in range(3):
        at_lane = (lane_id == ALIBI_LANE0 + part) | (lane_id == ALIBI_LANE0 + 3 + part)
        qadd = jnp.where(at_lane[None, None, None, :], lanes(per_head(parts[:, part])), qadd)
    slope_rows = lanes(per_head(_alibi(SWA_HEADS)))
    sink_rows = lanes(per_head(sinks.astype(F32)))
    const_spec = pl.BlockSpec((None, 2, rows, LANES), lambda b, g, i: (g, 0, 0, 0))
    return pl.pallas_call(
        functools.partial(_swa_kernel, tq=tq, tk=tk, R=R),
        out_shape=jax.ShapeDtypeStruct((B * T, SWA_HEADS * SWA_HEAD_DIM), BF16),
        grid=(B, G, nq),
        in_specs=[const_spec, const_spec, const_spec,
                  pl.BlockSpec((tq, qw), lambda b, g, i: (b * nq + i, g)),
                  pl.BlockSpec((T, LANES), lambda b, g, i: (b, kb + g // 2)),
                  pl.BlockSpec((T, LANES), lambda b, g, i: (b, vb + g // 2)),
                  pl.BlockSpec((T, LANES), lambda b, g, i: (0, 0))],
        out_specs=pl.BlockSpec((tq, qw), lambda b, g, i: (b * nq + i, g)),
        scratch_shapes=[pltpu.VMEM((2 * rows, tk), F32)],
        compiler_params=pltpu.CompilerParams(dimension_semantics=("parallel", "parallel", "arbitrary")),
        name="swa_attn",
    )(qadd.astype(BF16), slope_rows, sink_rows, h, h, h, _key_table(T))


def _dispatch_kernel(zfrom_ref, zto_ref, nu_ref, pos_ref, src_ref, dst_hbm, zbuf, sem, zsem,
                     *, td, n_seg, rb, n_blk):
    @pl.when(pl.program_id(0) == 0)
    def _():
        zbuf[...] = jnp.zeros_like(zbuf)

        def zero_row(r):
            return pltpu.make_async_copy(zbuf.at[pl.ds(0, 1)], dst_hbm.at[pl.ds(r, 1)], zsem)

        def zero_block(b):
            return pltpu.make_async_copy(zbuf, dst_hbm.at[pl.ds(pl.multiple_of(b * rb, rb), rb)], zsem)

        def seg(s, c):
            lax.fori_loop(zfrom_ref[s], zto_ref[s], lambda r, cc: (zero_row(r).start(), cc)[1], 0)
            return c

        def seg_wait(s, c):
            lax.fori_loop(zfrom_ref[s], zto_ref[s], lambda r, cc: (zero_row(r).wait(), cc)[1], 0)
            return c

        lax.fori_loop(0, n_seg, seg, 0)
        lax.fori_loop(nu_ref[0], n_blk, lambda b, cc: (zero_block(b).start(), cc)[1], 0)
        lax.fori_loop(0, n_seg, seg_wait, 0)
        lax.fori_loop(nu_ref[0], n_blk, lambda b, cc: (zero_block(b).wait(), cc)[1], 0)

    def row_copy(t, k):
        return pltpu.make_async_copy(src_ref.at[pl.ds(t, 1)],
                                     dst_hbm.at[pl.ds(pos_ref[0, t * TOP_K + k], 1)], sem)

    def issue(t, c):
        for k in range(TOP_K):
            row_copy(t, k).start()
        return c

    lax.fori_loop(0, td, issue, 0)
    for k in range(TOP_K):
        pltpu.make_async_copy(src_ref, dst_hbm.at[pl.ds(0, td)], sem).wait()


def moe_dispatch(xm_packed, pos, zfrom, zto, n_used, cap, td=128, rb=MOE_ROW_BLOCK):
    N, W = xm_packed.shape
    td = _tile(N, td)
    pos3 = pos.reshape(N // td, 1, td * TOP_K)
    grid_spec = pltpu.PrefetchScalarGridSpec(
        num_scalar_prefetch=3,
        grid=(N // td,),
        in_specs=[pl.BlockSpec((None, 1, td * TOP_K), lambda i, *_: (i, 0, 0), memory_space=pltpu.SMEM),
                  pl.BlockSpec((td, W), lambda i, *_: (i, 0))],
        out_specs=pl.BlockSpec(memory_space=pl.ANY),
        scratch_shapes=[pltpu.VMEM((rb, W), jnp.int32), pltpu.SemaphoreType.DMA(()),
                        pltpu.SemaphoreType.DMA(())],
    )
    return pl.pallas_call(
        functools.partial(_dispatch_kernel, td=td, n_seg=zfrom.shape[0], rb=rb, n_blk=cap // rb),
        out_shape=jax.ShapeDtypeStruct((cap, W), jnp.int32),
        grid_spec=grid_spec,
        compiler_params=pltpu.CompilerParams(dimension_semantics=("arbitrary",), has_side_effects=True),
        name="moe_dispatch",
    )(zfrom, zto, n_used, pos3, xm_packed)


def _expert_kernel(be_ref, slot_ref, nxt_ref, c0_ref, c1_ref, nu_ref, xs_ref, wg_hbm, bg_ref, wu_hbm, bu_ref,
                   wd_hbm, bd_ref, o_ref, wg_s, wu_s, wd_s, stg_a, stg_d, sem, *, ca, cd, ring, layer):
    blk = pl.program_id(0)
    e = be_ref[blk]
    s = slot_ref[blk]
    used = blk < nu_ref[0]
    _, D, F = wg_s.shape
    na, nd = D // ca, F // cd
    n_chunks = 2 * na + nd

    def on_chunk(c, expert, fn):
        k = c % ring

        def go(src, dst, r0, n, stg, sem0):
            copy = pltpu.make_async_copy(src.at[layer, expert, pl.ds(r0, n), :], stg.at[k], sem.at[sem0 + k])
            fn(copy, dst, r0, n, stg.at[k])

        @pl.when(c < na)
        def _():
            go(wg_hbm, wg_s, pl.multiple_of(c * ca, ca), ca, stg_a, 0)

        @pl.when((c >= na) & (c < 2 * na))
        def _():
            go(wu_hbm, wu_s, pl.multiple_of((c - na) * ca, ca), ca, stg_a, 0)

        @pl.when(c >= 2 * na)
        def _():
            go(wd_hbm, wd_s, pl.multiple_of((c - 2 * na) * cd, cd), cd, stg_d, ring)

    def start(c, expert):
        on_chunk(c, expert, lambda copy, dst, r0, n, stg: copy.start())

    def finish(c, expert, dst_slot):
        def fn(copy, dst, r0, n, stg):
            copy.wait()
            dst[dst_slot, pl.ds(r0, n), :] = stg[...].astype(BF16)
        on_chunk(c, expert, fn)

    def prestart(lo, hi, expert):
        for i in range(ring):
            @pl.when(lo + i < hi)
            def _():
                start(lo + i, expert)

    def drain(lo, hi, end, expert, dst_slot):
        def body(c, carry):
            finish(c, expert, dst_slot)

            @pl.when(c + ring < end)
            def _():
                start(c + ring, expert)
            return carry

        lax.fori_loop(lo, hi, body, 0)

    @pl.when(used & (blk == 0))
    def _():
        prestart(0, n_chunks, e)
        drain(0, n_chunks, n_chunks, e, s)

    c0, c1, nxt = c0_ref[blk], c1_ref[blk], nxt_ref[blk]

    @pl.when(used)
    def _():
        prestart(c0, c1, nxt)
        x = _unpack_bf16_pairs(xs_ref[...])
        gg = jnp.dot(x, wg_s[s], preferred_element_type=F32) + bg_ref[...]
        uu = jnp.dot(x, wu_s[s], preferred_element_type=F32) + bu_ref[...]
        gg = jnp.minimum(gg, SWIGLU_LIMIT)
        uu = jnp.clip(uu, -SWIGLU_LIMIT, SWIGLU_LIMIT)
        act = gg * jax.nn.sigmoid(SWIGLU_ALPHA * gg) * (uu + 1.0)
        y = jnp.dot(act.astype(BF16), wd_s[s], preferred_element_type=F32) + bd_ref[...]
        o_ref[...] = _pack_bf16_pairs(y)
        drain(c0, c1, c1, nxt, 1 - s)

    @pl.when(jnp.logical_not(used))
    def _():
        o_ref[...] = jnp.zeros_like(o_ref)


EXPERT_STAGE_RING = 5
VMEM_LIMIT_EXPERTS = 60 * 1024 * 1024


def _expert_schedule(blk_e, n_used, n_chunks):
    n_blk = blk_e.shape[0]
    idx = jnp.arange(n_blk, dtype=jnp.int32)
    first = jnp.concatenate([jnp.ones((1,), jnp.int32), (blk_e[1:] != blk_e[:-1]).astype(jnp.int32)])
    slot = (jnp.cumsum(first) - 1) % 2
    run_start = lax.cummax(jnp.where(first == 1, idx, 0))
    starts_after = jnp.concatenate([jnp.where(first == 1, idx, n_blk)[1:], jnp.full((1,), n_blk, jnp.int32)])
    next_start = lax.cummin(starts_after, reverse=True)
    has_next = next_start < n_used[0]
    nxt = blk_e[jnp.minimum(next_start, n_blk - 1)]
    run_len = jnp.maximum(jnp.minimum(next_start, n_used[0]) - run_start, 1)
    j = idx - run_start
    c0 = jnp.where(has_next, j * n_chunks // run_len, 0)
    c1 = jnp.where(has_next, (j + 1) * n_chunks // run_len, 0)
    as_i32 = lambda v: v.astype(jnp.int32)
    return as_i32(slot), as_i32(nxt), as_i32(c0), as_i32(c1)


def moe_experts(xs, blk_e, n_used, layer, wg, bg, wu, bu, wd, bd, rb=MOE_ROW_BLOCK):
    cap, W = xs.shape
    depth, E, D, F = wg.shape
    n_blk = cap // rb
    ca, cd = _tile(D, 256), _tile(F, 64)
    ring = EXPERT_STAGE_RING
    slot, nxt, c0, c1 = _expert_schedule(blk_e, n_used, 2 * (D // ca) + F // cd)
    by_expert = lambda shape: pl.BlockSpec((None,) + shape,
                                           lambda i, be, *_: (layer * E + be[i], 0, 0))
    grid_spec = pltpu.PrefetchScalarGridSpec(
        num_scalar_prefetch=6,
        grid=(n_blk,),
        in_specs=[pl.BlockSpec((rb, W), lambda i, *sp: (jnp.minimum(i, sp[-1][0] - 1), 0)),
                  pl.BlockSpec(memory_space=pl.ANY), by_expert((1, F)),
                  pl.BlockSpec(memory_space=pl.ANY), by_expert((1, F)),
                  pl.BlockSpec(memory_space=pl.ANY), by_expert((1, D))],
        out_specs=pl.BlockSpec((rb, W), lambda i, *_: (i, 0)),
        scratch_shapes=[pltpu.VMEM((2, D, F), BF16), pltpu.VMEM((2, D, F), BF16), pltpu.VMEM((2, F, D), BF16),
                        pltpu.VMEM((ring, ca, F), F32), pltpu.VMEM((ring, cd, D), F32),
                        pltpu.SemaphoreType.DMA((2 * ring,))],
    )
    return pl.pallas_call(
        functools.partial(_expert_kernel, ca=ca, cd=cd, ring=ring, layer=layer),
        out_shape=jax.ShapeDtypeStruct((cap, W), jnp.int32),
        grid_spec=grid_spec,
        compiler_params=pltpu.CompilerParams(dimension_semantics=("arbitrary",),
                                             vmem_limit_bytes=VMEM_LIMIT_EXPERTS),
        name="moe_experts",
    )(blk_e, slot, nxt, c0, c1, n_used, xs, wg, bg.reshape(depth * E, 1, F), wu, bu.reshape(depth * E, 1, F),
      wd, bd.reshape(depth * E, 1, D))


def _moe_combine_kernel(pos_ref, pos_next_ref, y_hbm, x_ref, w_ref, gate_ref, o_ref, ybuf, sem,
                        *, tt, n_steps):
    i = pl.program_id(0)
    slot = i % 2

    def row_copy(p_ref, t, k, sl):
        return pltpu.make_async_copy(y_hbm.at[pl.ds(p_ref[0, t * TOP_K + k], 1)],
                                     ybuf.at[sl, k, pl.ds(t, 1)], sem.at[sl])

    def issue(p_ref, sl):
        def body(t, c):
            for k in range(TOP_K):
                row_copy(p_ref, t, k, sl).start()
            return c
        lax.fori_loop(0, tt, body, 0)

    @pl.when(i == 0)
    def _():
        issue(pos_ref, 0)

    @pl.when(i + 1 < n_steps)
    def _():
        issue(pos_next_ref, 1 - slot)

    for k in range(TOP_K):
        pltpu.make_async_copy(y_hbm.at[pl.ds(0, tt)], ybuf.at[slot, k], sem.at[slot]).wait()
    w = w_ref[...]
    half = x_ref.shape[1] // 2
    y_lo = y_hi = None
    for k in range(TOP_K):
        u = ybuf[slot, k]
        wk = w[:, k:k + 1]
        lo = wk * lax.bitcast_convert_type(lax.shift_left(u, 16), F32)
        hi = wk * lax.bitcast_convert_type(u & jnp.int32(-65536), F32)
        y_lo = lo if y_lo is None else y_lo + lo
        y_hi = hi if y_hi is None else y_hi + hi
    o_ref[:, :half] = x_ref[:, :half] + gate_ref[:, :half] * y_lo
    o_ref[:, half:] = x_ref[:, half:] + gate_ref[:, half:] * y_hi


def moe_combine(yb, pos, top_w, x2d, gate, rows_per_batch, tt=64):
    N, D = x2d.shape
    tt = _tile(rows_per_batch, tt)
    bpb = rows_per_batch // tt
    n_steps = N // tt
    pos3 = pos.reshape(n_steps, 1, tt * TOP_K)
    pos_spec = lambda f: pl.BlockSpec((None, 1, tt * TOP_K), f, memory_space=pltpu.SMEM)
    return pl.pallas_call(
        functools.partial(_moe_combine_kernel, tt=tt, n_steps=n_steps),
        out_shape=jax.ShapeDtypeStruct((N, D), F32),
        grid=(n_steps,),
        in_specs=[pos_spec(lambda i: (i, 0, 0)),
                  pos_spec(lambda i: (jnp.minimum(i + 1, n_steps - 1), 0, 0)),
                  pl.BlockSpec(memory_space=pl.ANY),
                  pl.BlockSpec((tt, D), lambda i: (i, 0)),
                  pl.BlockSpec((tt, LANES), lambda i: (i, 0)),
                  pl.BlockSpec((None, 1, D), lambda i: (i // bpb, 0, 0))],
        out_specs=pl.BlockSpec((tt, D), lambda i: (i, 0)),
        scratch_shapes=[pltpu.VMEM((2, TOP_K, tt, D // 2), jnp.int32), pltpu.SemaphoreType.DMA((2,))],
        compiler_params=pltpu.CompilerParams(dimension_semantics=("arbitrary",)),
        name="moe_combine",
    )(pos3, pos3, yb, x2d, top_w, gate)


def moe_layer(x2d, gain, sc, sh, gate, rows_per_batch, router_w, router_b, layer, wg, bg, wu, bu, wd, bd):
    N, D = x2d.shape
    RB = MOE_ROW_BLOCK
    E = N_EXPERTS
    xm_packed, top_e, top_w, rank, counts = norm_mod(x2d, gain, sc, sh, rows_per_batch,
                                                     router=(router_w, router_b))
    nk = N * TOP_K
    counts = counts[0, :E]
    padded = (counts + RB - 1) // RB * RB
    pad_end = jnp.cumsum(padded)
    pad_start = pad_end - padded
    e4 = top_e[:, :TOP_K]
    start_of = jnp.sum(jnp.where(e4[:, :, None] == jnp.arange(E)[None, None, :],
                                 pad_start[None, None, :], 0), axis=2)
    pos = (start_of + rank[:, :TOP_K]).astype(jnp.int32)
    cap = (-(-nk // RB)) * RB + E * RB
    n_blk = cap // RB
    blk_start = jnp.arange(n_blk, dtype=jnp.int32) * RB
    blk_e = jnp.minimum(jnp.sum(pad_end[None, :] <= blk_start[:, None], axis=1), E - 1).astype(jnp.int32)
    n_used = (pad_end[-1] // RB).astype(jnp.int32).reshape(1)
    zfrom = (pad_start + counts).astype(jnp.int32)
    zto = pad_end.astype(jnp.int32)
    xs = moe_dispatch(xm_packed, pos, zfrom, zto, n_used, cap)
    yb = moe_experts(xs, blk_e, n_used, layer, wg, bg, wu, bu, wd, bd)
    return moe_combine(yb, pos, top_w, x2d, gate, rows_per_batch)


def _pad_heads(w, n_heads, d, dp):
    lead = w.shape[:-1]
    w = w.reshape(lead + (n_heads, d))
    w = jnp.pad(w, [(0, 0)] * len(lead) + [(0, 0), (0, dp - d)])
    return w.reshape(lead + (n_heads * dp,))


def _even_in_weights(w_in, q_gain, k_gain):
    D = w_in.shape[0]
    cuts = np.cumsum(EVEN_IN_SPLITS)[:-1].tolist()
    q, kc, vc, ks, vs, kw, vw, gates, cq, kva = jnp.split(w_in.astype(BF16), cuts, axis=1)
    half = MLA_ROPE_DIM // 2
    kr = kva[:, MLA_KV_RANK:]
    kr_perm = jnp.concatenate([kr[:, half:], kr[:, :half]], axis=1)
    G = NSA_KV_HEADS
    cols = [_pad_heads(q, NSA_HEADS, NSA_QK_DIM, HEAD_PAD),
            _pad_heads(kc, G, NSA_QK_DIM, HEAD_PAD), _pad_heads(ks, G, NSA_QK_DIM, HEAD_PAD),
            _pad_heads(kw, G, NSA_QK_DIM, HEAD_PAD), cq, kva[:, :MLA_KV_RANK], kr, kr_perm,
            vc, vs, vw, gates]
    cols.append(jnp.zeros((D, EV_WIDTH - EV_END + LANES - gates.shape[1]), BF16))
    w = jnp.concatenate(cols, axis=1)
    assert w.shape[1] == EV_WIDTH
    scale = NSA_QK_DIM ** -0.5
    pad_g = lambda g: jnp.pad(g, (0, HEAD_PAD - NSA_QK_DIM))
    gain = jnp.ones((EV_WIDTH,), F32)
    gain = gain.at[EV_Q:EV_Q + _NQ].set(jnp.tile(pad_g(q_gain * scale), NSA_HEADS))
    gain = gain.at[EV_KS:EV_KS + _NK].set(jnp.tile(pad_g(k_gain[1]), G))
    gain = gain.at[EV_KW:EV_KW + _NK].set(jnp.tile(pad_g(k_gain[2]), G))
    col = np.arange(EV_WIDTH)
    flag = ((col < EV_Q + _NQ) | ((col >= EV_KS) & (col < EV_KW + _NK))).astype(np.float32)
    return w, gain, jnp.asarray(flag)


def _even_mixer(xm, x2d, g_a, B, T, w_in, w_out, q_gain, k_gain, pe_k, pe_v, w_ck1, w_ck2, w_cv1, w_cv2,
                g_cq, g_ckv, w_uq, w_ukv, mq_gain, mk_gain):
    N, D = x2d.shape
    G, R = NSA_KV_HEADS, NSA_HEADS // NSA_KV_HEADS
    w_p, gain, flag = _even_in_weights(w_in, q_gain, k_gain)
    h = matmul(xm, w_p, head_norm=(HEAD_PAD, NSA_QK_DIM, gain, flag), tn=768, name="in_proj_even")

    kc = compress(h, EV_KC, B, T, pe_k, w_ck1, w_ck2, k_gain[0], NSA_QK_DIM, HEAD_PAD)
    vc = compress(h, EV_VC, B, T, pe_v, w_cv1, w_cv2, None, NSA_V_DIM, NSA_V_DIM)

    o_cmp, bits = cmp_attention(h, kc, vc, B, T)
    HV = NSA_HEADS * NSA_V_DIM
    o_slc = flash_attention(h, EV_Q, h, EV_KS, h, EV_VS, B=B, T=T, G=G, R=R, out_cols=HV,
                            n_alibi_heads=NSA_HEADS, bits=bits, tq=512, tk=512, name="nsa_slc_attn")
    o_win = flash_attention(h, EV_Q, h, EV_KW, h, EV_VW, B=B, T=T, G=G, R=R, out_cols=HV,
                            window=NSA_WINDOW, n_alibi_heads=NSA_HEADS, name="nsa_win_attn")
    o_a = nsa_combine(h, o_cmp, o_slc, o_win)

    H = MLA_HEADS
    dqk = MLA_NOPE_DIM + MLA_ROPE_DIM
    half = MLA_ROPE_DIM // 2
    wq = w_uq.reshape(MLA_Q_RANK, H, dqk)
    wq_rope = wq[:, :, MLA_NOPE_DIM:]
    wq_p = jnp.concatenate([wq, wq_rope[:, :, half:], wq_rope[:, :, :half]], axis=2)
    wq_p = wq_p.reshape(MLA_Q_RANK, H * HEAD_PAD).astype(BF16)
    q_raw = matmul(h, wq_p, a_col0=EV_CQ, a_pro="rms", a_gain=g_cq, tn=1024, name="mla_q_up")
    kv_raw = matmul(h, w_ukv.astype(BF16), a_col0=EV_CKV, a_pro="rms", a_gain=g_ckv, tn=1024,
                    name="mla_kv_up")
    qa_lo, qa_hi, qb_hi = _rope_tables(mq_gain, T, dqk ** -0.5)
    ka_lo, ka_hi, kb_hi = _rope_tables(mk_gain, T, 1.0)
    q_m = mla_prep(q_raw, None, 0, qa_lo, qa_hi, qb_hi, T)
    k_m = mla_prep(kv_raw, h, EV_KR, ka_lo, ka_hi, kb_hi, T)
    hp = 4
    kv_w = MLA_NOPE_DIM + MLA_V_DIM
    o_b = flash_attention(q_m, 0, k_m, 0, kv_raw, 0, B=B, T=T, G=H, R=1, HP=hp, v_width=hp * kv_w,
                          v_off=MLA_NOPE_DIM, v_step=kv_w, out_cols=H * MLA_V_DIM, tq=512, tk=512,
                          name="mla_attn")
    return matmul(o_a, w_out.astype(BF16), a2=o_b, resid=(x2d, g_a, T), out_dtype=F32, name="out_proj_even")


def _odd_mixer(xm, x2d, g_a, B, T, w_in, b_in, w_out, b_out, q_gain, k_gain, sinks):
    G, hd = SWA_KV_HEADS, SWA_HEAD_DIM
    nq = SWA_HEADS * hd
    kw = G * hd
    gain = jnp.concatenate([jnp.tile(q_gain * hd ** -0.5, SWA_HEADS), jnp.tile(k_gain, G),
                            jnp.ones((kw,), F32)])
    flag = jnp.concatenate([jnp.ones((nq + kw,), F32), jnp.zeros((kw,), F32)])
    h = matmul(xm, w_in.astype(BF16), bias=b_in, head_norm=(hd, hd, gain, flag), tn=1024,
               name="in_proj_odd")
    o_c = swa_attention(h, sinks, B, T, nq, nq + kw)
    return matmul(o_c, w_out.astype(BF16), bias=b_out, resid=(x2d, g_a, T), out_dtype=F32,
                  name="out_proj_odd")


def kernel(x, c, w_mod, mod_table, norm_attn, norm_ffn, w_in_even, w_out_even, nsa_q_gain, nsa_k_gain,
           nsa_pe_k, nsa_pe_v, nsa_w_ck1, nsa_w_ck2, nsa_w_cv1, nsa_w_cv2, mla_g_cq, mla_g_ckv, mla_w_uq,
           mla_w_ukv, mla_q_gain, mla_k_gain, w_in_odd, b_in_odd, w_out_odd, b_out_odd, swa_q_gain,
           swa_k_gain, swa_sinks, router_w, router_b, moe_w_gate, moe_b_gate, moe_w_up, moe_b_up,
           moe_w_down, moe_b_down):
    B, T, D = x.shape
    N = B * T
    depth = mod_table.shape[0]
    c_pad = jnp.pad(c, ((0, 8 - B % 8 if B % 8 else 0), (0, 0)))
    cond = matmul(c_pad, w_mod, a_pro="silu", out_dtype=F32, tn=1024, tk=2048, name="adaln_proj")[:B]
    x2d = x.reshape(N, D)
    for layer in range(depth):
        mod = (cond + mod_table[layer]).reshape(B, 6, 1, D)
        sh_a, sc_a, g_a, sh_f, sc_f, g_f = (mod[:, j] for j in range(6))
        xm = norm_mod(x2d, norm_attn[layer], sc_a, sh_a, T)
        i = layer // 2
        if layer % 2 == 0:
            x2d = _even_mixer(xm, x2d, g_a, B, T, w_in_even[i], w_out_even[i], nsa_q_gain[i], nsa_k_gain[i],
                              nsa_pe_k[i], nsa_pe_v[i], nsa_w_ck1[i], nsa_w_ck2[i], nsa_w_cv1[i],
                              nsa_w_cv2[i], mla_g_cq[i], mla_g_ckv[i], mla_w_uq[i], mla_w_ukv[i],
                              mla_q_gain[i], mla_k_gain[i])
        else:
            x2d = _odd_mixer(xm, x2d, g_a, B, T, w_in_odd[i], b_in_odd[i], w_out_odd[i], b_out_odd[i],
                             swa_q_gain[i], swa_k_gain[i], swa_sinks[i])
        x2d = moe_layer(x2d, norm_ffn[layer], sc_f, sh_f, g_f, T, router_w[layer], router_b[layer],
                        layer, moe_w_gate, moe_b_gate, moe_w_up, moe_b_up, moe_w_down, moe_b_down)
    return x2d.reshape(B, T, D)
```

```python
import functools
import math

import numpy as np
import jax
import jax.numpy as jnp
from jax import lax
from jax.experimental import pallas as pl
from jax.experimental.pallas import tpu as pltpu

BF16 = jnp.bfloat16
F32 = jnp.float32

NSA_HEADS = 16
NSA_KV_HEADS = 4
NSA_QK_DIM = 192
NSA_V_DIM = 128
CMP_BLOCK = 32
CMP_STRIDE = 16
SLC_BLOCK = 64
SLC_TOPN = 8
NSA_WINDOW = 512
MLA_HEADS = 16
MLA_Q_RANK = 1024
MLA_KV_RANK = 512
MLA_NOPE_DIM = 128
MLA_ROPE_DIM = 64
MLA_V_DIM = 128
ROPE_THETA = 10000.0
SWA_HEADS = 64
SWA_KV_HEADS = 8
SWA_HEAD_DIM = 64
SWA_WINDOW = 128
N_EXPERTS = 32
TOP_K = 4
SWIGLU_ALPHA = 1.702
SWIGLU_LIMIT = 7.0
MOE_ROW_BLOCK = 256
EPS = 1e-6

LANES = 128
HEAD_PAD = 256
NEG = -1e30
M_FLOOR = -1e29

EVEN_IN_SPLITS = (NSA_HEADS * NSA_QK_DIM,
                  NSA_KV_HEADS * NSA_QK_DIM, NSA_KV_HEADS * NSA_V_DIM,
                  NSA_KV_HEADS * NSA_QK_DIM, NSA_KV_HEADS * NSA_V_DIM,
                  NSA_KV_HEADS * NSA_QK_DIM, NSA_KV_HEADS * NSA_V_DIM,
                  3 * NSA_HEADS, MLA_Q_RANK, MLA_KV_RANK + MLA_ROPE_DIM)

_NQ = NSA_HEADS * HEAD_PAD
_NK = NSA_KV_HEADS * HEAD_PAD
_NV = NSA_KV_HEADS * NSA_V_DIM
EV_Q = 0
EV_KC = EV_Q + _NQ
EV_KS = EV_KC + _NK
EV_KW = EV_KS + _NK
EV_CQ = EV_KW + _NK
EV_CKV = EV_CQ + MLA_Q_RANK
EV_KR = EV_CKV + MLA_KV_RANK
EV_VC = EV_KR + 2 * MLA_ROPE_DIM
EV_VS = EV_VC + _NV
EV_VW = EV_VS + _NV
EV_GATE = EV_VW + _NV
EV_END = EV_GATE + LANES
EV_WIDTH = -(-EV_END // 512) * 512


def _tile(dim, want):
    t = min(dim, want)
    while dim % t:
        t //= 2
    return t


def _alibi_np(n):
    return np.exp2(-8.0 * np.arange(1, n + 1, dtype=np.float32) / n).astype(np.float32)


def _alibi(n):
    return jnp.asarray(_alibi_np(n))


def _head_norm(res, hd, real_d):
    tm, tn = res.shape
    x2 = res * res
    pieces = []
    if hd == HEAD_PAD:
        for s in range(tn // hd):
            ss = jnp.sum(x2[:, s * hd:s * hd + LANES] + x2[:, s * hd + LANES:(s + 1) * hd],
                         axis=1, keepdims=True)
            r = lax.rsqrt(ss * (1.0 / real_d) + EPS)
            pieces.append(jnp.broadcast_to(r, (tm, hd)))
    else:
        lo = lax.broadcasted_iota(jnp.int32, (tm, LANES), 1) < hd
        for s in range(tn // LANES):
            c = x2[:, s * LANES:(s + 1) * LANES]
            ss_lo = jnp.sum(jnp.where(lo, c, 0.0), axis=1, keepdims=True)
            ss_hi = jnp.sum(jnp.where(lo, 0.0, c), axis=1, keepdims=True)
            r_lo = lax.rsqrt(ss_lo * (1.0 / real_d) + EPS)
            r_hi = lax.rsqrt(ss_hi * (1.0 / real_d) + EPS)
            pieces.append(jnp.where(lo, r_lo, r_hi))
    return jnp.concatenate(pieces, axis=1) if len(pieces) > 1 else pieces[0]


def _mm_kernel(*refs, nk, a_pro, has_a2, has_bias, hd, real_d, has_resid):
    it = iter(refs)
    a_ref = next(it)
    a2_ref = next(it) if has_a2 else None
    w_ref = next(it)
    again_ref = next(it) if a_pro == "rms" else None
    b_ref = next(it) if has_bias else None
    gain_ref = flag_ref = None
    if hd:
        gain_ref = next(it)
        flag_ref = next(it)
    x_ref = gate_ref = None
    if has_resid:
        x_ref = next(it)
        gate_ref = next(it)
    o_ref = next(it)
    acc_ref = next(it) if nk > 1 else None

    a = a_ref[...]
    if a_pro == "silu":
        af = a.astype(F32)
        a = af * jax.nn.sigmoid(af)
    elif a_pro == "rms":
        af = a.astype(F32)
        r = lax.rsqrt(jnp.mean(af * af, axis=1, keepdims=True) + EPS)
        a = af * r * again_ref[...]
    if has_a2:
        k1 = a.shape[1]
        part = (jnp.dot(a.astype(BF16), w_ref[:k1, :].astype(BF16), preferred_element_type=F32)
                + jnp.dot(a2_ref[...].astype(BF16), w_ref[k1:, :].astype(BF16), preferred_element_type=F32))
    else:
        part = jnp.dot(a.astype(BF16), w_ref[...].astype(BF16), preferred_element_type=F32)

    def finish(res):
        if has_bias:
            res = res + b_ref[...]
        if hd:
            r = _head_norm(res, hd, real_d)
            res = res * jnp.where(flag_ref[...] > 0.0, r, 1.0) * gain_ref[...]
        if has_resid:
            res = x_ref[...] + gate_ref[...] * res
        o_ref[...] = res.astype(o_ref.dtype)

    if nk == 1:
        finish(part)
    else:
        k = pl.program_id(2)

        @pl.when(k == 0)
        def _():
            acc_ref[...] = part

        @pl.when(k > 0)
        def _():
            acc_ref[...] += part

        @pl.when(k == nk - 1)
        def _():
            finish(acc_ref[...])


def matmul(a, w, *, a2=None, a_col0=0, a_pro=None, a_gain=None, bias=None,
           head_norm=None, resid=None, out_dtype=BF16, tm=1024, tn=512, tk=4096, name="mm"):
    M = a.shape[0]
    K, N = w.shape
    tm, tn, tk = _tile(M, tm), _tile(N, tn), _tile(K, tk)
    if a_pro == "rms" or a2 is not None:
        tk = K
    assert a_col0 % tk == 0 and M % tm == 0 and N % tn == 0 and K % tk == 0
    nk = K // tk
    koff = a_col0 // tk
    hd = head_norm[0] if head_norm else 0
    real_d = head_norm[1] if head_norm else 0
    if hd:
        assert tn % max(hd, LANES) == 0

    if a2 is None:
        ins = [a, w]
        specs = [pl.BlockSpec((tm, tk), lambda i, j, k: (i, koff + k))]
    else:
        assert a_col0 == 0 and a_pro is None and a.shape[1] + a2.shape[1] == K
        ins = [a, a2, w]
        specs = [pl.BlockSpec((tm, a.shape[1]), lambda i, j, k: (i, 0)),
                 pl.BlockSpec((tm, a2.shape[1]), lambda i, j, k: (i, 0))]
    specs.append(pl.BlockSpec((tk, tn), lambda i, j, k: (k, j)))
    if a_pro == "rms":
        ins.append(a_gain.reshape(1, K).astype(F32))
        specs.append(pl.BlockSpec((1, tk), lambda i, j, k: (0, k)))
    if bias is not None:
        ins.append(bias.reshape(1, N).astype(F32))
        specs.append(pl.BlockSpec((1, tn), lambda i, j, k: (0, j)))
    if hd:
        ins += [head_norm[2].reshape(1, N).astype(F32), head_norm[3].reshape(1, N).astype(F32)]
        specs += [pl.BlockSpec((1, tn), lambda i, j, k: (0, j))] * 2
    if resid is not None:
        x, gate, rows_per_batch = resid
        assert rows_per_batch % tm == 0
        bpb = rows_per_batch // tm
        ins += [x, gate]
        specs += [pl.BlockSpec((tm, tn), lambda i, j, k: (i, j)),
                  pl.BlockSpec((None, 1, tn), lambda i, j, k: (i // bpb, 0, j))]
    kern = functools.partial(_mm_kernel, nk=nk, a_pro=a_pro, has_a2=a2 is not None, has_bias=bias is not None,
                             hd=hd, real_d=real_d, has_resid=resid is not None)
    return pl.pallas_call(
        kern,
        out_shape=jax.ShapeDtypeStruct((M, N), out_dtype),
        grid=(M // tm, N // tn, nk),
        in_specs=specs,
        out_specs=pl.BlockSpec((tm, tn), lambda i, j, k: (i, j)),
        scratch_shapes=[pltpu.VMEM((tm, tn), F32)] if nk > 1 else [],
        compiler_params=pltpu.CompilerParams(
            dimension_semantics=("parallel", "parallel", "arbitrary")),
        name=name,
    )(*ins)


def _split_bf16(v):
    hi = v.astype(BF16)
    lo = (v - hi.astype(F32)).astype(BF16)
    return hi, lo


def _pack_bf16_pairs(v):
    half = v.shape[1] // 2
    vb = v.astype(BF16).astype(F32)
    lo = lax.shift_right_logical(lax.bitcast_convert_type(vb[:, :half], jnp.int32), 16)
    hi = lax.bitcast_convert_type(vb[:, half:], jnp.int32) & jnp.int32(-65536)
    return hi | lo


def _unpack_bf16_pairs(u):
    lo = lax.bitcast_convert_type(lax.shift_left(u, 16), F32)
    hi = lax.bitcast_convert_type(u & jnp.int32(-65536), F32)
    return jnp.concatenate([lo, hi], axis=1).astype(BF16)


def _norm_mod_kernel(x_ref, g_ref, sc_ref, sh_ref, *rest, route):
    x = x_ref[...]
    r = lax.rsqrt(jnp.mean(x * x, axis=1, keepdims=True) + EPS)
    xm = (x * r * g_ref[...]) * (1.0 + sc_ref[...]) + sh_ref[...]
    if not route:
        (o_ref,) = rest
        o_ref[...] = xm.astype(o_ref.dtype)
        return
    rw_ref, rb_ref, o_ref, e_ref, p_ref, rank_ref, cnt_ref, run_ref = rest
    o_ref[...] = _pack_bf16_pairs(xm)
    a_hi, a_lo = _split_bf16(xm)
    w = rw_ref[...]
    w_hi, w_lo = _split_bf16(w)
    logits = (jnp.dot(a_hi, w_hi, preferred_element_type=F32)
              + jnp.dot(a_hi, w_lo, preferred_element_type=F32)
              + jnp.dot(a_lo, w_hi, preferred_element_type=F32)) + rb_ref[...]
    tt = logits.shape[0]
    lane = lax.broadcasted_iota(jnp.int32, (tt, LANES), 1)
    lane_f = lane.astype(F32)
    work = jnp.where(lane < N_EXPERTS, logits, -jnp.inf)
    e_out = jnp.zeros((tt, LANES), F32)
    v_out = jnp.full((tt, LANES), -jnp.inf, F32)
    hits = []
    for kk in range(TOP_K):
        m = jnp.max(work, axis=1, keepdims=True)
        idx = jnp.min(jnp.where(work == m, lane_f, float(LANES)), axis=1, keepdims=True)
        e_out = jnp.where(lane == kk, idx, e_out)
        v_out = jnp.where(lane == kk, m, v_out)
        hits.append(lane_f == idx)
        work = jnp.where(hits[-1], -jnp.inf, work)
    e_out = e_out.astype(jnp.int32)
    vmax = jnp.max(v_out, axis=1, keepdims=True)
    pe = jnp.exp(v_out - vmax)
    p_ref[...] = pe / jnp.sum(pe, axis=1, keepdims=True)
    e_ref[...] = e_out

    @pl.when(pl.program_id(0) == 0)
    def _():
        run_ref[...] = jnp.zeros_like(run_ref)

    onehot = jnp.zeros((tt, LANES), F32)
    for hit in hits:
        onehot = jnp.where(hit, 1.0, onehot)
    earlier = (lax.broadcasted_iota(jnp.int32, (tt, tt), 1)
               < lax.broadcasted_iota(jnp.int32, (tt, tt), 0))
    prefix = jnp.dot(jnp.where(earlier, 1.0, 0.0).astype(BF16), onehot.astype(BF16),
                     preferred_element_type=F32)
    base = run_ref[...] + prefix
    rank = jnp.zeros((tt, LANES), F32)
    for kk, hit in enumerate(hits):
        rk = jnp.sum(jnp.where(hit, base, 0.0), axis=1, keepdims=True)
        rank = jnp.where(lane == kk, rk, rank)
    rank_ref[...] = rank.astype(jnp.int32)
    total = run_ref[...] + jnp.sum(onehot, axis=0, keepdims=True)
    run_ref[...] = total
    cnt_ref[...] = total.astype(jnp.int32)


def norm_mod(x2d, gain, sc, sh, rows_per_batch, *, out_dtype=BF16, router=None, tt=256):
    N, D = x2d.shape
    tt = _tile(rows_per_batch, tt)
    bpb = rows_per_batch // tt
    ins = [x2d, gain.reshape(1, D), sc, sh]
    specs = [pl.BlockSpec((tt, D), lambda i: (i, 0)),
             pl.BlockSpec((1, D), lambda i: (0, 0)),
             pl.BlockSpec((None, 1, D), lambda i: (i // bpb, 0, 0)),
             pl.BlockSpec((None, 1, D), lambda i: (i // bpb, 0, 0))]
    if router is None:
        out_shape = [jax.ShapeDtypeStruct((N, D), out_dtype)]
        out_specs = [pl.BlockSpec((tt, D), lambda i: (i, 0))]
        scratch = []
    else:
        rw, rb = router
        E = rw.shape[1]
        rw_p = jnp.pad(rw, ((0, 0), (0, LANES - E)))
        rb_p = jnp.pad(rb, (0, LANES - E)).reshape(1, LANES)
        ins += [rw_p, rb_p]
        specs += [pl.BlockSpec((D, LANES), lambda i: (0, 0)),
                  pl.BlockSpec((1, LANES), lambda i: (0, 0))]
        out_shape = [jax.ShapeDtypeStruct((N, D // 2), jnp.int32),
                     jax.ShapeDtypeStruct((N, LANES), jnp.int32),
                     jax.ShapeDtypeStruct((N, LANES), F32),
                     jax.ShapeDtypeStruct((N, LANES), jnp.int32),
                     jax.ShapeDtypeStruct((1, LANES), jnp.int32)]
        row = pl.BlockSpec((tt, LANES), lambda i: (i, 0))
        out_specs = [pl.BlockSpec((tt, D // 2), lambda i: (i, 0)), row, row, row,
                     pl.BlockSpec((1, LANES), lambda i: (0, 0))]
        scratch = [pltpu.VMEM((1, LANES), F32)]
    res = pl.pallas_call(
        functools.partial(_norm_mod_kernel, route=router is not None),
        out_shape=out_shape, grid=(N // tt,), in_specs=specs, out_specs=out_specs,
        scratch_shapes=scratch,
        compiler_params=pltpu.CompilerParams(
            dimension_semantics=("parallel",) if router is None else ("arbitrary",)),
        name="norm_mod_route" if router is not None else "norm_mod",
    )(*ins)
    return res if router is not None else res[0]


def _compress_kernel(h_ref, pe_ref, w1_ref, w2_ref, g_ref, o_ref, xf_ref, *, real_d, norm, nh):
    n_tiles = xf_ref.shape[0]
    for j in range(n_tiles):
        xf_ref[j] = h_ref[:, j * LANES:(j + 1) * LANES].astype(F32)
    half = CMP_BLOCK // 2
    top = bot = None
    for l in range(half):
        parts = [xf_ref[j, pl.ds(l, nh, stride=CMP_STRIDE), :] for j in range(n_tiles)]
        xl = jnp.concatenate(parts, axis=1) if n_tiles > 1 else parts[0]
        t = jnp.dot((xl + pe_ref[l:l + 1, :]).astype(BF16), w1_ref[l], preferred_element_type=F32)
        b = jnp.dot((xl + pe_ref[half + l:half + l + 1, :]).astype(BF16), w1_ref[half + l],
                    preferred_element_type=F32)
        top = t if top is None else top + t
        bot = b if bot is None else bot + b
    hid = top + pltpu.roll(bot, nh - 1, 0)
    hid = hid * jax.nn.sigmoid(hid)
    y = jnp.dot(hid.astype(BF16), w2_ref[...], preferred_element_type=F32)
    if norm:
        r = lax.rsqrt(jnp.sum(y * y, axis=1, keepdims=True) * (1.0 / real_d) + EPS)
        y = y * r * g_ref[...]
    o_ref[...] = y.astype(o_ref.dtype)


def compress(h, col0, B, T, pe, w1, w2, gain, d, dp):
    G = NSA_KV_HEADS
    nh = T // CMP_STRIDE
    cb = col0 // dp
    pe_p = jnp.pad(pe, ((0, 0), (0, dp - d)))
    w1_p = jnp.pad(w1.reshape(CMP_BLOCK, d, d), ((0, 0), (0, dp - d), (0, dp - d))).astype(BF16)
    w2_p = jnp.pad(w2, ((0, dp - d), (0, dp - d))).astype(BF16)
    g_p = (jnp.ones((dp,), F32) if gain is None else jnp.pad(gain, (0, dp - d))).reshape(1, dp)
    full = lambda shape: pl.BlockSpec(shape, lambda b, g: (0,) * len(shape))
    return pl.pallas_call(
        functools.partial(_compress_kernel, real_d=d, norm=gain is not None, nh=nh),
        out_shape=jax.ShapeDtypeStruct((B * G, nh, dp), BF16),
        grid=(B, G),
        in_specs=[pl.BlockSpec((T, dp), lambda b, g: (b, cb + g)),
                  full((CMP_BLOCK, dp)), full((CMP_BLOCK, dp, dp)), full((dp, dp)), full((1, dp))],
        out_specs=pl.BlockSpec((None, nh, dp), lambda b, g: (b * G + g, 0, 0)),
        scratch_shapes=[pltpu.VMEM((dp // LANES, T, LANES), F32)],
        compiler_params=pltpu.CompilerParams(dimension_semantics=("parallel", "parallel")),
        name="nsa_compress",
    )(h, pe_p, w1_p, w2_p, g_p)


def _stack_heads(q_ref, n, w):
    return jnp.concatenate([q_ref[:, r * w:(r + 1) * w] for r in range(n)], axis=0)


def _row_scalars(vals, tq):
    rows = len(vals) * tq
    rid = lax.broadcasted_iota(jnp.int32, (rows, 1), 0) // tq
    col = jnp.full((rows, 1), vals[-1], F32)
    for r in range(len(vals) - 2, -1, -1):
        col = jnp.where(rid == r, vals[r], col)
    return col


def _nt_dot(a, b):
    return lax.dot_general(a, b, (((1,), (1,)), ((), ())), preferred_element_type=F32)


def _cmp_attn_kernel(slope_ref, q_ref, kc_ref, vc_ref, ovl_ref, o_ref, bits_ref, *, tq, R, n_slc):
    g = pl.program_id(1)
    t0 = pl.program_id(2) * tq
    rows = R * tq
    q = _stack_heads(q_ref, R, HEAD_PAD)
    s = _nt_dot(q, kc_ref[...])
    ncol = s.shape[1]
    row_t = t0 + lax.broadcasted_iota(jnp.int32, (rows, ncol), 0) % tq
    n_id = lax.broadcasted_iota(jnp.int32, (rows, ncol), 1)
    dist = row_t - (n_id * CMP_STRIDE + (CMP_BLOCK - 1))
    slope = _row_scalars([slope_ref[g * R + r] for r in range(R)], tq)
    valid = dist >= 0
    s = jnp.where(valid, s - slope * dist.astype(F32), NEG)
    m = jnp.max(s, axis=1, keepdims=True)
    p = jnp.where(valid, jnp.exp(s - m), 0.0)
    p = p / jnp.maximum(jnp.sum(p, axis=1, keepdims=True), 1e-30)
    o = jnp.dot(p.astype(BF16), vc_ref[...], preferred_element_type=F32)
    for r in range(R):
        o_ref[:, r * NSA_V_DIM:(r + 1) * NSA_V_DIM] = o[r * tq:(r + 1) * tq].astype(o_ref.dtype)

    psum = p[0:tq]
    for r in range(1, R):
        psum = psum + p[r * tq:(r + 1) * tq]
    p_hi, p_lo = _split_bf16(psum)
    imp = (jnp.dot(p_hi, ovl_ref[...], preferred_element_type=F32)
           + jnp.dot(p_lo, ovl_ref[...], preferred_element_type=F32))
    lane = lax.broadcasted_iota(jnp.int32, (tq, LANES), 1)
    cur = (t0 + lax.broadcasted_iota(jnp.int32, (tq, LANES), 0)) // SLC_BLOCK
    forced = (lane == 0) | (lane == cur) | (lane == cur - 1)
    work = jnp.where(forced, jnp.inf, jnp.where(lane > cur, -jnp.inf, imp))
    removed = -3.0e38
    work = jnp.where(lane < n_slc, jnp.where(work == -jnp.inf, -2.0e38, work), removed)
    sel = jnp.zeros((tq, LANES), jnp.bool_)
    lane_f = lane.astype(F32)
    for _ in range(min(SLC_TOPN, n_slc)):
        mx = jnp.max(work, axis=1, keepdims=True)
        idx = jnp.min(jnp.where(work == mx, lane_f, float(LANES)), axis=1, keepdims=True)
        hit = lane_f == idx
        sel = sel | hit
        work = jnp.where(hit, removed, work)
    half = 16
    w_lo = jnp.where(sel & (lane < half), jnp.left_shift(1, jnp.minimum(lane, half - 1)), 0)
    w_hi = jnp.where(sel & (lane >= half), jnp.left_shift(1, jnp.clip(lane - half, 0, half - 1)), 0)
    b_lo = jnp.sum(w_lo.astype(F32), axis=1, keepdims=True).astype(jnp.int32)
    b_hi = jnp.sum(w_hi.astype(F32), axis=1, keepdims=True).astype(jnp.int32)
    bits = b_lo | jnp.left_shift(b_hi, half)
    bits_ref[...] = jnp.broadcast_to(bits, (tq, LANES))


def cmp_attention(h, kc, vc, B, T, tq=256):
    G, R = NSA_KV_HEADS, NSA_HEADS // NSA_KV_HEADS
    tq = _tile(T, tq)
    nq = T // tq
    n_slc = T // SLC_BLOCK
    n_cmp = kc.shape[1]
    assert n_slc <= 32 and n_cmp <= LANES and n_cmp % 8 == 0
    cmp_start = np.arange(n_cmp) * CMP_STRIDE
    slc_start = np.arange(LANES) * SLC_BLOCK
    ovl = ((cmp_start[:, None] < slc_start[None, :] + SLC_BLOCK)
           & (cmp_start[:, None] + CMP_BLOCK > slc_start[None, :])
           & (np.arange(LANES)[None, :] < n_slc) & (np.arange(n_cmp)[:, None] < n_cmp - 1))
    ovl = jnp.asarray(ovl, BF16)
    qw = R * HEAD_PAD
    return pl.pallas_call(
        functools.partial(_cmp_attn_kernel, tq=tq, R=R, n_slc=n_slc),
        out_shape=[jax.ShapeDtypeStruct((B * T, NSA_HEADS * NSA_V_DIM), BF16),
                   jax.ShapeDtypeStruct((B, G, T, LANES), jnp.int32)],
        grid=(B, G, nq),
        in_specs=[pl.BlockSpec(memory_space=pltpu.SMEM),
                  pl.BlockSpec((tq, qw), lambda b, g, i: (b * nq + i, EV_Q // qw + g)),
                  pl.BlockSpec((None, n_cmp, HEAD_PAD), lambda b, g, i: (b * G + g, 0, 0)),
                  pl.BlockSpec((None, n_cmp, NSA_V_DIM), lambda b, g, i: (b * G + g, 0, 0)),
                  pl.BlockSpec((n_cmp, LANES), lambda b, g, i: (0, 0))],
        out_specs=[pl.BlockSpec((tq, R * NSA_V_DIM), lambda b, g, i: (b * nq + i, g)),
                   pl.BlockSpec((None, None, tq, LANES), lambda b, g, i: (b, g, i, 0))],
        compiler_params=pltpu.CompilerParams(dimension_semantics=("parallel", "parallel", "parallel")),
        name="nsa_cmp_attn",
    )(_alibi(NSA_HEADS), h, kc, vc, ovl)


ALIBI_LANE0 = 64
SEL_LANE0 = 70


def _key_table(T):
    s = np.arange(T)
    tab = np.zeros((T, LANES), np.float32)
    tab[:, ALIBI_LANE0:ALIBI_LANE0 + 3] = (s // 256 * 256)[:, None]
    tab[:, ALIBI_LANE0 + 3:ALIBI_LANE0 + 6] = (s % 256)[:, None]
    tab[s, SEL_LANE0 + s // SLC_BLOCK] = 1.0
    return jnp.asarray(tab, BF16)


def _slope_parts(n_heads):
    s = _alibi_np(n_heads)
    h1 = s.astype(BF16).astype(np.float32)
    h2 = (s - h1).astype(BF16).astype(np.float32)
    h3 = (s - h1 - h2).astype(BF16).astype(np.float32)
    return jnp.asarray(np.stack([h1, h2, h3], axis=1).reshape(-1))


def _flash_kernel(*refs, tq, tk, R, HP, v_off, v_step, window, alibi, use_bits):
    it = iter(refs)
    slope_ref = next(it) if alibi else None
    q_ref = next(it)
    k_ref = next(it)
    v_ref = next(it)
    ktab_ref = next(it) if alibi else None
    bits_ref = next(it) if use_bits else None
    o_ref = next(it)
    m_ref = next(it)
    acc_ref = next(it)

    g = pl.program_id(1)
    t0 = pl.program_id(2) * tq
    rows = R * tq
    vd = NSA_V_DIM
    lane = lax.broadcasted_iota(jnp.int32, (tq, LANES), 1)
    sel_add = None
    if use_bits:
        j = jnp.clip(lane - SEL_LANE0, 0, 31)
        in_sel = (lane >= SEL_LANE0) & (lane < SEL_LANE0 + 32)
        picked = (jnp.right_shift(bits_ref[...], j) & 1) != 0
        sel_add = jnp.where(in_sel & jnp.logical_not(picked), NEG, 0.0)

    qs = []
    for hp in range(HP):
        pieces = []
        for r in range(R):
            c0 = (hp * R + r) * HEAD_PAD
            q_lo = q_ref[:, c0:c0 + LANES]
            q_hi = q_ref[:, c0 + LANES:c0 + HEAD_PAD]
            if alibi:
                hd = (g * HP + hp) * R + r
                add = jnp.zeros((tq, LANES), F32) if sel_add is None else sel_add
                for part in range(3):
                    sp = slope_ref[3 * hd + part]
                    add = jnp.where((lane == ALIBI_LANE0 + part) | (lane == ALIBI_LANE0 + 3 + part), sp, add)
                q_hi = (q_hi.astype(F32) + add).astype(BF16)
            pieces.append(jnp.concatenate([q_lo, q_hi], axis=1))
        qs.append(jnp.concatenate(pieces, axis=0) if R > 1 else pieces[0])

    rel0 = (lax.broadcasted_iota(jnp.int32, (rows, tk), 0) % tq
            - lax.broadcasted_iota(jnp.int32, (rows, tk), 1))
    ones = jnp.ones((tk, LANES), BF16)
    m_ref[...] = jnp.full(m_ref.shape, M_FLOOR, F32)
    acc_ref[...] = jnp.zeros(acc_ref.shape, F32)

    def step(c, masked):
        s0 = pl.multiple_of(c * tk, tk)
        off = t0 - s0
        if masked:
            valid = rel0 >= -off
            if window is not None:
                valid = valid & (rel0 < window - off)
        for hp in range(HP):
            kc = k_ref[pl.ds(s0, tk), hp * HEAD_PAD:(hp + 1) * HEAD_PAD]
            if alibi:
                kc = jnp.concatenate([kc[:, :LANES], kc[:, LANES:] + ktab_ref[pl.ds(s0, tk), :]], axis=1)
            vc = v_ref[pl.ds(s0, tk), v_off + hp * v_step:v_off + hp * v_step + vd]
            s = _nt_dot(qs[hp], kc)
            if masked:
                s = jnp.where(valid, s, NEG)
            tiles = [s[:, j * LANES:(j + 1) * LANES] for j in range(tk // LANES)]
            mx = tiles[0]
            for t in tiles[1:]:
                mx = jnp.maximum(mx, t)
            m_old = m_ref[hp]
            m_new = jnp.maximum(m_old, jnp.max(mx, axis=1, keepdims=True))
            alpha = jnp.exp(m_old - m_new)
            p = jnp.concatenate([jnp.exp(t - m_new).astype(BF16) for t in tiles], axis=1)
            pv = jnp.dot(p, jnp.concatenate([vc, ones], axis=1), preferred_element_type=F32)
            acc_ref[hp] = jnp.concatenate([alpha, alpha], axis=1) * acc_ref[hp] + pv
            m_ref[hp] = m_new

    def body(c, carry):
        s0 = c * tk
        full = s0 + tk - 1 <= t0
        if window is not None:
            full = full & (s0 >= t0 + tq - window)
        lax.cond(full, lambda: step(c, False), lambda: step(c, True))
        return carry

    c_hi = (t0 + tq - 1) // tk
    c_lo = 0 if window is None else jnp.maximum((t0 - window + 1) // tk, 0)
    lax.fori_loop(c_lo, c_hi + 1, body, 0)
    for hp in range(HP):
        acc = acc_ref[hp]
        o = acc[:, :vd] / jnp.maximum(acc[:, vd:], 1e-30)
        for r in range(R):
            c0 = (hp * R + r) * vd
            o_ref[:, c0:c0 + vd] = o[r * tq:(r + 1) * tq].astype(o_ref.dtype)


def flash_attention(q_arr, q_col0, k_arr, k_col0, v_arr, v_col0, *, B, T, G, R, HP=1, v_width=NSA_V_DIM,
                    v_off=0, v_step=0, out_cols, out_col0=0, window=None, n_alibi_heads=0, bits=None,
                    tq=256, tk=256, name="flash"):
    vd = NSA_V_DIM
    tq = _tile(T, tq)
    tk = _tile(T, tk)
    nq = T // tq
    qw, kw, ow = HP * R * HEAD_PAD, HP * HEAD_PAD, HP * R * vd
    assert q_col0 % qw == 0 and k_col0 % kw == 0 and v_col0 % v_width == 0 and out_col0 % ow == 0
    assert bits is None or n_alibi_heads
    qb, kb, vb, ob = q_col0 // qw, k_col0 // kw, v_col0 // v_width, out_col0 // ow
    alibi = n_alibi_heads > 0
    ins, specs = [], []
    if alibi:
        ins.append(_slope_parts(n_alibi_heads))
        specs.append(pl.BlockSpec(memory_space=pltpu.SMEM))
    ins += [q_arr, k_arr, v_arr]
    specs += [pl.BlockSpec((tq, qw), lambda b, g, i: (b * nq + i, qb + g)),
              pl.BlockSpec((T, kw), lambda b, g, i: (b, kb + g)),
              pl.BlockSpec((T, v_width), lambda b, g, i: (b, vb + g))]
    if alibi:
        ins.append(_key_table(T))
        specs.append(pl.BlockSpec((T, LANES), lambda b, g, i: (0, 0)))
    if bits is not None:
        ins.append(bits)
        specs.append(pl.BlockSpec((None, None, tq, LANES), lambda b, g, i: (b, g, i, 0)))
    rows = R * tq
    return pl.pallas_call(
        functools.partial(_flash_kernel, tq=tq, tk=tk, R=R, HP=HP, v_off=v_off, v_step=v_step,
                          window=window, alibi=alibi, use_bits=bits is not None),
        out_shape=jax.ShapeDtypeStruct((B * T, out_cols), BF16),
        grid=(B, G // HP, nq),
        in_specs=specs,
        out_specs=pl.BlockSpec((tq, ow), lambda b, g, i: (b * nq + i, ob + g)),
        scratch_shapes=[pltpu.VMEM((HP, rows, LANES), F32), pltpu.VMEM((HP, rows, 2 * vd), F32)],
        compiler_params=pltpu.CompilerParams(dimension_semantics=("parallel", "parallel", "parallel")),
        name=name,
    )(*ins)


def _nsa_combine_kernel(gate_ref, oc_ref, os_ref, ow_ref, o_ref):
    gates = jax.nn.sigmoid(gate_ref[...].astype(F32))
    for h in range(NSA_HEADS):
        sl = slice(h * NSA_V_DIM, (h + 1) * NSA_V_DIM)
        acc = None
        for br, ref in enumerate((oc_ref, os_ref, ow_ref)):
            gcol = gates[:, 3 * h + br:3 * h + br + 1]
            term = gcol * ref[:, sl].astype(F32)
            acc = term if acc is None else acc + term
        o_ref[:, sl] = acc.astype(o_ref.dtype)


def nsa_combine(h, o_cmp, o_slc, o_win, tt=256):
    N, W = o_cmp.shape
    tt = _tile(N, tt)
    gb = EV_GATE // LANES
    blk = pl.BlockSpec((tt, W), lambda i: (i, 0))
    return pl.pallas_call(
        _nsa_combine_kernel,
        out_shape=jax.ShapeDtypeStruct((N, W), BF16),
        grid=(N // tt,),
        in_specs=[pl.BlockSpec((tt, LANES), lambda i: (i, gb)), blk, blk, blk],
        out_specs=blk,
        compiler_params=pltpu.CompilerParams(dimension_semantics=("parallel",)),
        name="nsa_combine",
    )(h, o_cmp, o_slc, o_win)


def _mla_prep_kernel(*refs, shared_rope):
    if shared_rope:
        x_ref, hi_ref, alo_ref, ahi_ref, bhi_ref, o_ref = refs
    else:
        x_ref, alo_ref, ahi_ref, bhi_ref, o_ref = refs
    tt = x_ref.shape[0]
    is_rope = lax.broadcasted_iota(jnp.int32, (tt, LANES), 1) < MLA_ROPE_DIM
    a_hi, b_hi, a_lo = ahi_ref[...], bhi_ref[...], alo_ref[...]

    def rope_part(x_hi):
        ss = jnp.sum(jnp.where(is_rope, x_hi * x_hi, 0.0), axis=1, keepdims=True)
        return x_hi * a_hi + pltpu.roll(x_hi * b_hi, MLA_ROPE_DIM, 1), ss

    if shared_rope:
        y_hi, ss_hi = rope_part(hi_ref[...].astype(F32))
    for h in range(MLA_HEADS):
        c = h * HEAD_PAD
        x_lo = x_ref[:, c:c + LANES].astype(F32)
        if not shared_rope:
            y_hi, ss_hi = rope_part(x_ref[:, c + LANES:c + HEAD_PAD].astype(F32))
        ss = jnp.sum(x_lo * x_lo, axis=1, keepdims=True) + ss_hi
        r = lax.rsqrt(ss * (1.0 / (MLA_NOPE_DIM + MLA_ROPE_DIM)) + EPS)
        o_ref[:, c:c + LANES] = (x_lo * r * a_lo).astype(o_ref.dtype)
        o_ref[:, c + LANES:c + HEAD_PAD] = (y_hi * r).astype(o_ref.dtype)


def mla_prep(x_arr, rope_arr, rope_col0, a_lo, a_hi, b_hi, T, tt=256):
    N = x_arr.shape[0]
    W = MLA_HEADS * HEAD_PAD
    tt = _tile(T, tt)
    nt = T // tt
    shared = rope_arr is not None
    ins = [x_arr]
    specs = [pl.BlockSpec((tt, W), lambda i: (i, 0))]
    if shared:
        rb = rope_col0 // LANES
        ins.append(rope_arr)
        specs.append(pl.BlockSpec((tt, LANES), lambda i: (i, rb)))
    ins += [a_lo, a_hi, b_hi]
    specs += [pl.BlockSpec((1, LANES), lambda i: (0, 0)),
              pl.BlockSpec((tt, LANES), lambda i: (i % nt, 0)),
              pl.BlockSpec((tt, LANES), lambda i: (i % nt, 0))]
    return pl.pallas_call(
        functools.partial(_mla_prep_kernel, shared_rope=shared),
        out_shape=jax.ShapeDtypeStruct((N, W), BF16),
        grid=(N // tt,),
        in_specs=specs,
        out_specs=pl.BlockSpec((tt, W), lambda i: (i, 0)),
        compiler_params=pltpu.CompilerParams(dimension_semantics=("parallel",)),
        name="mla_prep",
    )(*ins)


def _rope_tables(gain, T, scale):
    half = MLA_ROPE_DIM // 2
    inv_freq = ROPE_THETA ** (-jnp.arange(half, dtype=F32) / half)
    ang = jnp.arange(T, dtype=F32)[:, None] * inv_freq[None, :]
    cos2 = jnp.concatenate([jnp.cos(ang), jnp.cos(ang)], axis=1)
    sin_s = jnp.concatenate([-jnp.sin(ang), jnp.sin(ang)], axis=1)
    g_nope, g_rope = gain[:MLA_NOPE_DIM], gain[MLA_NOPE_DIM:]
    g_perm = jnp.concatenate([g_rope[half:], g_rope[:half]])
    zeros = jnp.zeros((T, MLA_ROPE_DIM), F32)
    a_lo = (g_nope * scale).reshape(1, LANES)
    a_hi = jnp.concatenate([g_rope[None, :] * cos2 * scale, zeros], axis=1)
    b_hi = jnp.concatenate([zeros, g_perm[None, :] * sin_s * scale], axis=1)
    return a_lo, a_hi, b_hi


def _swa_kernel(qadd_ref, slope_ref, sink_ref, q_ref, k_ref, v_ref, ktab_ref, o_ref, mask_ref,
                *, tq, tk, R):
    t0 = pl.program_id(2) * tq
    npair = R // 2
    rows = npair * tq
    qp = _stack_heads(q_ref, npair, LANES)
    lane = lax.broadcasted_iota(jnp.int32, (rows, LANES), 1)
    first = lane < SWA_HEAD_DIM
    start = pl.multiple_of(jnp.maximum(t0 - SWA_WINDOW, 0), LANES)
    own_half = lax.broadcasted_iota(jnp.int32, (tk, LANES), 1) // SWA_HEAD_DIM == pl.program_id(1) % 2

    def both_halves(ref):
        pair = ref[pl.ds(start, tk), :].astype(F32)
        return jnp.where(own_half, pair, pltpu.roll(pair, SWA_HEAD_DIM, 1)).astype(BF16)

    kc = jnp.concatenate([both_halves(k_ref), ktab_ref[pl.ds(start, tk), :]], axis=1)
    v_aug = jnp.concatenate([both_halves(v_ref), jnp.ones((tk, LANES), BF16)], axis=1)
    @pl.when(pl.program_id(2) <= 1)
    def _():
        row_in = lax.broadcasted_iota(jnp.int32, (2 * rows, tk), 0) % tq
        rel = t0 - start + row_in - lax.broadcasted_iota(jnp.int32, (2 * rows, tk), 1)
        mask_ref[...] = jnp.where((rel >= 0) & (rel < SWA_WINDOW), 0.0, NEG)

    t_row = (t0 + lax.broadcasted_iota(jnp.int32, (2 * rows, LANES), 0) % tq).astype(F32)
    zero = jnp.zeros_like(qp)
    qu = jnp.concatenate([jnp.where(first, qp, zero), jnp.where(first, zero, qp)], axis=0)
    qa = jnp.concatenate([qu, qadd_ref[...].reshape(2 * rows, LANES)], axis=1)
    sink_t = (sink_ref[...].reshape(2 * rows, LANES)
              + slope_ref[...].reshape(2 * rows, LANES) * t_row)
    s = _nt_dot(qa, kc) + mask_ref[...]
    tiles = [s[:, j * LANES:(j + 1) * LANES] for j in range(tk // LANES)]
    mx = tiles[0]
    for t in tiles[1:]:
        mx = jnp.maximum(mx, t)
    m = jnp.maximum(jnp.max(mx, axis=1, keepdims=True), sink_t)
    p = jnp.concatenate([jnp.exp(t - m).astype(BF16) for t in tiles], axis=1)
    pv = jnp.dot(p, v_aug, preferred_element_type=F32)
    denom = pv[:, LANES:] + jnp.exp(sink_t - m)
    both = pv[:, :LANES] / jnp.maximum(denom, 1e-30)
    o = jnp.where(first, both[:rows], both[rows:])
    for p_ in range(npair):
        o_ref[:, p_ * LANES:(p_ + 1) * LANES] = o[p_ * tq:(p_ + 1) * tq].astype(o_ref.dtype)


def swa_attention(h, sinks, B, T, k_col0, v_col0, tq=128):
    G, R = SWA_KV_HEADS, SWA_HEADS // SWA_KV_HEADS
    tq = _tile(T, tq)
    tk = min(T, tq + SWA_WINDOW)
    assert tk % LANES == 0 and tq % LANES == 0
    nq = T // tq
    qw = R * SWA_HEAD_DIM
    kb, vb = k_col0 // LANES, v_col0 // LANES
    npair = R // 2
    rows = npair * tq
    per_head = lambda v: jnp.repeat(v.reshape(G, npair, 2).transpose(0, 2, 1), tq, axis=2)
    lanes = lambda v: jnp.broadcast_to(v[..., None], (G, 2, rows, LANES))
    parts = _slope_parts(SWA_HEADS).reshape(SWA_HEADS, 3)
    lane_id = np.arange(LANES)
    qadd = jnp.zeros((G, 2, rows, LANES), F32)
    for part in range(3):
        at_lane = (lane_id == ALIBI_LANE0 + part) | (lane_id == ALIBI_LANE0 + 3 + part)
        qadd = jnp.where(at_lane[None, None, None, :], lanes(per_head(parts[:, part])), qadd)
    slope_rows = lanes(per_head(_alibi(SWA_HEADS)))
    sink_rows = lanes(per_head(sinks.astype(F32)))
    const_spec = pl.BlockSpec((None, 2, rows, LANES), lambda b, g, i: (g, 0, 0, 0))
    return pl.pallas_call(
        functools.partial(_swa_kernel, tq=tq, tk=tk, R=R),
        out_shape=jax.ShapeDtypeStruct((B * T, SWA_HEADS * SWA_HEAD_DIM), BF16),
        grid=(B, G, nq),
        in_specs=[const_spec, const_spec, const_spec,
                  pl.BlockSpec((tq, qw), lambda b, g, i: (b * nq + i, g)),
                  pl.BlockSpec((T, LANES), lambda b, g, i: (b, kb + g // 2)),
                  pl.BlockSpec((T, LANES), lambda b, g, i: (b, vb + g // 2)),
                  pl.BlockSpec((T, LANES), lambda b, g, i: (0, 0))],
        out_specs=pl.BlockSpec((tq, qw), lambda b, g, i: (b * nq + i, g)),
        scratch_shapes=[pltpu.VMEM((2 * rows, tk), F32)],
        compiler_params=pltpu.CompilerParams(dimension_semantics=("parallel", "parallel", "arbitrary")),
        name="swa_attn",
    )(qadd.astype(BF16), slope_rows, sink_rows, h, h, h, _key_table(T))


def _dispatch_kernel(zfrom_ref, zto_ref, nu_ref, pos_ref, src_ref, dst_hbm, zbuf, sem, zsem,
                     *, td, n_seg, rb, n_blk):
    @pl.when(pl.program_id(0) == 0)
    def _():
        zbuf[...] = jnp.zeros_like(zbuf)

        def zero_row(r):
            return pltpu.make_async_copy(zbuf.at[pl.ds(0, 1)], dst_hbm.at[pl.ds(r, 1)], zsem)

        def zero_block(b):
            return pltpu.make_async_copy(zbuf, dst_hbm.at[pl.ds(pl.multiple_of(b * rb, rb), rb)], zsem)

        def seg(s, c):
            lax.fori_loop(zfrom_ref[s], zto_ref[s], lambda r, cc: (zero_row(r).start(), cc)[1], 0)
            return c

        def seg_wait(s, c):
            lax.fori_loop(zfrom_ref[s], zto_ref[s], lambda r, cc: (zero_row(r).wait(), cc)[1], 0)
            return c

        lax.fori_loop(0, n_seg, seg, 0)
        lax.fori_loop(nu_ref[0], n_blk, lambda b, cc: (zero_block(b).start(), cc)[1], 0)
        lax.fori_loop(0, n_seg, seg_wait, 0)
        lax.fori_loop(nu_ref[0], n_blk, lambda b, cc: (zero_block(b).wait(), cc)[1], 0)

    def row_copy(t, k):
        return pltpu.make_async_copy(src_ref.at[pl.ds(t, 1)],
                                     dst_hbm.at[pl.ds(pos_ref[0, t * TOP_K + k], 1)], sem)

    def issue(t, c):
        for k in range(TOP_K):
            row_copy(t, k).start()
        return c

    lax.fori_loop(0, td, issue, 0)
    for k in range(TOP_K):
        pltpu.make_async_copy(src_ref, dst_hbm.at[pl.ds(0, td)], sem).wait()


def moe_dispatch(xm_packed, pos, zfrom, zto, n_used, cap, td=128, rb=MOE_ROW_BLOCK):
    N, W = xm_packed.shape
    td = _tile(N, td)
    pos3 = pos.reshape(N // td, 1, td * TOP_K)
    grid_spec = pltpu.PrefetchScalarGridSpec(
        num_scalar_prefetch=3,
        grid=(N // td,),
        in_specs=[pl.BlockSpec((None, 1, td * TOP_K), lambda i, *_: (i, 0, 0), memory_space=pltpu.SMEM),
                  pl.BlockSpec((td, W), lambda i, *_: (i, 0))],
        out_specs=pl.BlockSpec(memory_space=pl.ANY),
        scratch_shapes=[pltpu.VMEM((rb, W), jnp.int32), pltpu.SemaphoreType.DMA(()),
                        pltpu.SemaphoreType.DMA(())],
    )
    return pl.pallas_call(
        functools.partial(_dispatch_kernel, td=td, n_seg=zfrom.shape[0], rb=rb, n_blk=cap // rb),
        out_shape=jax.ShapeDtypeStruct((cap, W), jnp.int32),
        grid_spec=grid_spec,
        compiler_params=pltpu.CompilerParams(dimension_semantics=("arbitrary",), has_side_effects=True),
        name="moe_dispatch",
    )(zfrom, zto, n_used, pos3, xm_packed)


def _expert_kernel(be_ref, slot_ref, nxt_ref, c0_ref, c1_ref, nu_ref, xs_ref, wg_hbm, bg_ref, wu_hbm, bu_ref,
                   wd_hbm, bd_ref, o_ref, wg_s, wu_s, wd_s, stg_a, stg_d, sem, *, ca, cd, ring, layer):
    blk = pl.program_id(0)
    e = be_ref[blk]
    s = slot_ref[blk]
    used = blk < nu_ref[0]
    _, D, F = wg_s.shape
    na, nd = D // ca, F // cd
    n_chunks = 2 * na + nd

    def on_chunk(c, expert, fn):
        k = c % ring

        def go(src, dst, r0, n, stg, sem0):
            copy = pltpu.make_async_copy(src.at[layer, expert, pl.ds(r0, n), :], stg.at[k], sem.at[sem0 + k])
            fn(copy, dst, r0, n, stg.at[k])

        @pl.when(c < na)
        def _():
            go(wg_hbm, wg_s, pl.multiple_of(c * ca, ca), ca, stg_a, 0)

        @pl.when((c >= na) & (c < 2 * na))
        def _():
            go(wu_hbm, wu_s, pl.multiple_of((c - na) * ca, ca), ca, stg_a, 0)

        @pl.when(c >= 2 * na)
        def _():
            go(wd_hbm, wd_s, pl.multiple_of((c - 2 * na) * cd, cd), cd, stg_d, ring)

    def start(c, expert):
        on_chunk(c, expert, lambda copy, dst, r0, n, stg: copy.start())

    def finish(c, expert, dst_slot):
        def fn(copy, dst, r0, n, stg):
            copy.wait()
            dst[dst_slot, pl.ds(r0, n), :] = stg[...].astype(BF16)
        on_chunk(c, expert, fn)

    def prestart(lo, hi, expert):
        for i in range(ring):
            @pl.when(lo + i < hi)
            def _():
                start(lo + i, expert)

    def drain(lo, hi, end, expert, dst_slot):
        def body(c, carry):
            finish(c, expert, dst_slot)

            @pl.when(c + ring < end)
            def _():
                start(c + ring, expert)
            return carry

        lax.fori_loop(lo, hi, body, 0)

    @pl.when(used & (blk == 0))
    def _():
        prestart(0, n_chunks, e)
        drain(0, n_chunks, n_chunks, e, s)

    c0, c1, nxt = c0_ref[blk], c1_ref[blk], nxt_ref[blk]

    @pl.when(used)
    def _():
        prestart(c0, c1, nxt)
        x = _unpack_bf16_pairs(xs_ref[...])
        gg = jnp.dot(x, wg_s[s], preferred_element_type=F32) + bg_ref[...]
        uu = jnp.dot(x, wu_s[s], preferred_element_type=F32) + bu_ref[...]
        gg = jnp.minimum(gg, SWIGLU_LIMIT)
        uu = jnp.clip(uu, -SWIGLU_LIMIT, SWIGLU_LIMIT)
        act = gg * jax.nn.sigmoid(SWIGLU_ALPHA * gg) * (uu + 1.0)
        y = jnp.dot(act.astype(BF16), wd_s[s], preferred_element_type=F32) + bd_ref[...]
        o_ref[...] = _pack_bf16_pairs(y)
        drain(c0, c1, c1, nxt, 1 - s)

    @pl.when(jnp.logical_not(used))
    def _():
        o_ref[...] = jnp.zeros_like(o_ref)


EXPERT_STAGE_RING = 6
VMEM_LIMIT_EXPERTS = 60 * 1024 * 1024


def _expert_schedule(blk_e, n_used, n_chunks):
    n_blk = blk_e.shape[0]
    idx = jnp.arange(n_blk, dtype=jnp.int32)
    first = jnp.concatenate([jnp.ones((1,), jnp.int32), (blk_e[1:] != blk_e[:-1]).astype(jnp.int32)])
    slot = (jnp.cumsum(first) - 1) % 2
    run_start = lax.cummax(jnp.where(first == 1, idx, 0))
    starts_after = jnp.concatenate([jnp.where(first == 1, idx, n_blk)[1:], jnp.full((1,), n_blk, jnp.int32)])
    next_start = lax.cummin(starts_after, reverse=True)
    has_next = next_start < n_used[0]
    nxt = blk_e[jnp.minimum(next_start, n_blk - 1)]
    run_len = jnp.maximum(jnp.minimum(next_start, n_used[0]) - run_start, 1)
    j = idx - run_start
    c0 = jnp.where(has_next, j * n_chunks // run_len, 0)
    c1 = jnp.where(has_next, (j + 1) * n_chunks // run_len, 0)
    as_i32 = lambda v: v.astype(jnp.int32)
    return as_i32(slot), as_i32(nxt), as_i32(c0), as_i32(c1)


def moe_experts(xs, blk_e, n_used, layer, wg, bg, wu, bu, wd, bd, rb=MOE_ROW_BLOCK):
    cap, W = xs.shape
    depth, E, D, F = wg.shape
    n_blk = cap // rb
    ca, cd = _tile(D, 256), _tile(F, 64)
    ring = EXPERT_STAGE_RING
    slot, nxt, c0, c1 = _expert_schedule(blk_e, n_used, 2 * (D // ca) + F // cd)
    by_expert = lambda shape: pl.BlockSpec((None,) + shape,
                                           lambda i, be, *_: (layer * E + be[i], 0, 0))
    grid_spec = pltpu.PrefetchScalarGridSpec(
        num_scalar_prefetch=6,
        grid=(n_blk,),
        in_specs=[pl.BlockSpec((rb, W), lambda i, *sp: (jnp.minimum(i, sp[-1][0] - 1), 0)),
                  pl.BlockSpec(memory_space=pl.ANY), by_expert((1, F)),
                  pl.BlockSpec(memory_space=pl.ANY), by_expert((1, F)),
                  pl.BlockSpec(memory_space=pl.ANY), by_expert((1, D))],
        out_specs=pl.BlockSpec((rb, W), lambda i, *_: (i, 0)),
        scratch_shapes=[pltpu.VMEM((2, D, F), BF16), pltpu.VMEM((2, D, F), BF16), pltpu.VMEM((2, F, D), BF16),
                        pltpu.VMEM((ring, ca, F), F32), pltpu.VMEM((ring, cd, D), F32),
                        pltpu.SemaphoreType.DMA((2 * ring,))],
    )
    return pl.pallas_call(
        functools.partial(_expert_kernel, ca=ca, cd=cd, ring=ring, layer=layer),
        out_shape=jax.ShapeDtypeStruct((cap, W), jnp.int32),
        grid_spec=grid_spec,
        compiler_params=pltpu.CompilerParams(dimension_semantics=("arbitrary",),
                                             vmem_limit_bytes=VMEM_LIMIT_EXPERTS),
        name="moe_experts",
    )(blk_e, slot, nxt, c0, c1, n_used, xs, wg, bg.reshape(depth * E, 1, F), wu, bu.reshape(depth * E, 1, F),
      wd, bd.reshape(depth * E, 1, D))


def _moe_combine_kernel(pos_ref, pos_next_ref, y_hbm, x_ref, w_ref, gate_ref, o_ref, ybuf, sem,
                        *, tt, n_steps):
    i = pl.program_id(0)
    slot = i % 2

    def row_copy(p_ref, t, k, sl):
        return pltpu.make_async_copy(y_hbm.at[pl.ds(p_ref[0, t * TOP_K + k], 1)],
                                     ybuf.at[sl, k, pl.ds(t, 1)], sem.at[sl])

    def issue(p_ref, sl):
        def body(t, c):
            for k in range(TOP_K):
                row_copy(p_ref, t, k, sl).start()
            return c
        lax.fori_loop(0, tt, body, 0)

    @pl.when(i == 0)
    def _():
        issue(pos_ref, 0)

    @pl.when(i + 1 < n_steps)
    def _():
        issue(pos_next_ref, 1 - slot)

    for k in range(TOP_K):
        pltpu.make_async_copy(y_hbm.at[pl.ds(0, tt)], ybuf.at[slot, k], sem.at[slot]).wait()
    w = w_ref[...]
    half = x_ref.shape[1] // 2
    y_lo = y_hi = None
    for k in range(TOP_K):
        u = ybuf[slot, k]
        wk = w[:, k:k + 1]
        lo = wk * lax.bitcast_convert_type(lax.shift_left(u, 16), F32)
        hi = wk * lax.bitcast_convert_type(u & jnp.int32(-65536), F32)
        y_lo = lo if y_lo is None else y_lo + lo
        y_hi = hi if y_hi is None else y_hi + hi
    o_ref[:, :half] = x_ref[:, :half] + gate_ref[:, :half] * y_lo
    o_ref[:, half:] = x_ref[:, half:] + gate_ref[:, half:] * y_hi


def moe_combine(yb, pos, top_w, x2d, gate, rows_per_batch, tt=64):
    N, D = x2d.shape
    tt = _tile(rows_per_batch, tt)
    bpb = rows_per_batch // tt
    n_steps = N // tt
    pos3 = pos.reshape(n_steps, 1, tt * TOP_K)
    pos_spec = lambda f: pl.BlockSpec((None, 1, tt * TOP_K), f, memory_space=pltpu.SMEM)
    return pl.pallas_call(
        functools.partial(_moe_combine_kernel, tt=tt, n_steps=n_steps),
        out_shape=jax.ShapeDtypeStruct((N, D), F32),
        grid=(n_steps,),
        in_specs=[pos_spec(lambda i: (i, 0, 0)),
                  pos_spec(lambda i: (jnp.minimum(i + 1, n_steps - 1), 0, 0)),
                  pl.BlockSpec(memory_space=pl.ANY),
                  pl.BlockSpec((tt, D), lambda i: (i, 0)),
                  pl.BlockSpec((tt, LANES), lambda i: (i, 0)),
                  pl.BlockSpec((None, 1, D), lambda i: (i // bpb, 0, 0))],
        out_specs=pl.BlockSpec((tt, D), lambda i: (i, 0)),
        scratch_shapes=[pltpu.VMEM((2, TOP_K, tt, D // 2), jnp.int32), pltpu.SemaphoreType.DMA((2,))],
        compiler_params=pltpu.CompilerParams(dimension_semantics=("arbitrary",)),
        name="moe_combine",
    )(pos3, pos3, yb, x2d, top_w, gate)


def moe_layer(x2d, gain, sc, sh, gate, rows_per_batch, router_w, router_b, layer, wg, bg, wu, bu, wd, bd):
    N, D = x2d.shape
    RB = MOE_ROW_BLOCK
    E = N_EXPERTS
    xm_packed, top_e, top_w, rank, counts = norm_mod(x2d, gain, sc, sh, rows_per_batch,
                                                     router=(router_w, router_b))
    nk = N * TOP_K
    counts = counts[0, :E]
    padded = (counts + RB - 1) // RB * RB
    pad_end = jnp.cumsum(padded)
    pad_start = pad_end - padded
    e4 = top_e[:, :TOP_K]
    start_of = jnp.sum(jnp.where(e4[:, :, None] == jnp.arange(E)[None, None, :],
                                 pad_start[None, None, :], 0), axis=2)
    pos = (start_of + rank[:, :TOP_K]).astype(jnp.int32)
    cap = (-(-nk // RB)) * RB + E * RB
    n_blk = cap // RB
    blk_start = jnp.arange(n_blk, dtype=jnp.int32) * RB
    blk_e = jnp.minimum(jnp.sum(pad_end[None, :] <= blk_start[:, None], axis=1), E - 1).astype(jnp.int32)
    n_used = (pad_end[-1] // RB).astype(jnp.int32).reshape(1)
    zfrom = (pad_start + counts).astype(jnp.int32)
    zto = pad_end.astype(jnp.int32)
    xs = moe_dispatch(xm_packed, pos, zfrom, zto, n_used, cap)
    yb = moe_experts(xs, blk_e, n_used, layer, wg, bg, wu, bu, wd, bd)
    return moe_combine(yb, pos, top_w, x2d, gate, rows_per_batch)


def _pad_heads(w, n_heads, d, dp):
    lead = w.shape[:-1]
    w = w.reshape(lead + (n_heads, d))
    w = jnp.pad(w, [(0, 0)] * len(lead) + [(0, 0), (0, dp - d)])
    return w.reshape(lead + (n_heads * dp,))


def _even_in_weights(w_in, q_gain, k_gain):
    D = w_in.shape[0]
    cuts = np.cumsum(EVEN_IN_SPLITS)[:-1].tolist()
    q, kc, vc, ks, vs, kw, vw, gates, cq, kva = jnp.split(w_in.astype(BF16), cuts, axis=1)
    half = MLA_ROPE_DIM // 2
    kr = kva[:, MLA_KV_RANK:]
    kr_perm = jnp.concatenate([kr[:, half:], kr[:, :half]], axis=1)
    G = NSA_KV_HEADS
    cols = [_pad_heads(q, NSA_HEADS, NSA_QK_DIM, HEAD_PAD),
            _pad_heads(kc, G, NSA_QK_DIM, HEAD_PAD), _pad_heads(ks, G, NSA_QK_DIM, HEAD_PAD),
            _pad_heads(kw, G, NSA_QK_DIM, HEAD_PAD), cq, kva[:, :MLA_KV_RANK], kr, kr_perm,
            vc, vs, vw, gates]
    cols.append(jnp.zeros((D, EV_WIDTH - EV_END + LANES - gates.shape[1]), BF16))
    w = jnp.concatenate(cols, axis=1)
    assert w.shape[1] == EV_WIDTH
    scale = NSA_QK_DIM ** -0.5
    pad_g = lambda g: jnp.pad(g, (0, HEAD_PAD - NSA_QK_DIM))
    gain = jnp.ones((EV_WIDTH,), F32)
    gain = gain.at[EV_Q:EV_Q + _NQ].set(jnp.tile(pad_g(q_gain * scale), NSA_HEADS))
    gain = gain.at[EV_KS:EV_KS + _NK].set(jnp.tile(pad_g(k_gain[1]), G))
    gain = gain.at[EV_KW:EV_KW + _NK].set(jnp.tile(pad_g(k_gain[2]), G))
    col = np.arange(EV_WIDTH)
    flag = ((col < EV_Q + _NQ) | ((col >= EV_KS) & (col < EV_KW + _NK))).astype(np.float32)
    return w, gain, jnp.asarray(flag)


def _even_mixer(xm, x2d, g_a, B, T, w_in, w_out, q_gain, k_gain, pe_k, pe_v, w_ck1, w_ck2, w_cv1, w_cv2,
                g_cq, g_ckv, w_uq, w_ukv, mq_gain, mk_gain):
    N, D = x2d.shape
    G, R = NSA_KV_HEADS, NSA_HEADS // NSA_KV_HEADS
    w_p, gain, flag = _even_in_weights(w_in, q_gain, k_gain)
    h = matmul(xm, w_p, head_norm=(HEAD_PAD, NSA_QK_DIM, gain, flag), tn=768, name="in_proj_even")

    kc = compress(h, EV_KC, B, T, pe_k, w_ck1, w_ck2, k_gain[0], NSA_QK_DIM, HEAD_PAD)
    vc = compress(h, EV_VC, B, T, pe_v, w_cv1, w_cv2, None, NSA_V_DIM, NSA_V_DIM)

    o_cmp, bits = cmp_attention(h, kc, vc, B, T)
    HV = NSA_HEADS * NSA_V_DIM
    o_slc = flash_attention(h, EV_Q, h, EV_KS, h, EV_VS, B=B, T=T, G=G, R=R, out_cols=HV,
                            n_alibi_heads=NSA_HEADS, bits=bits, tq=512, tk=512, name="nsa_slc_attn")
    o_win = flash_attention(h, EV_Q, h, EV_KW, h, EV_VW, B=B, T=T, G=G, R=R, out_cols=HV,
                            window=NSA_WINDOW, n_alibi_heads=NSA_HEADS, name="nsa_win_attn")
    o_a = nsa_combine(h, o_cmp, o_slc, o_win)

    H = MLA_HEADS
    dqk = MLA_NOPE_DIM + MLA_ROPE_DIM
    half = MLA_ROPE_DIM // 2
    wq = w_uq.reshape(MLA_Q_RANK, H, dqk)
    wq_rope = wq[:, :, MLA_NOPE_DIM:]
    wq_p = jnp.concatenate([wq, wq_rope[:, :, half:], wq_rope[:, :, :half]], axis=2)
    wq_p = wq_p.reshape(MLA_Q_RANK, H * HEAD_PAD).astype(BF16)
    q_raw = matmul(h, wq_p, a_col0=EV_CQ, a_pro="rms", a_gain=g_cq, tn=1024, name="mla_q_up")
    kv_raw = matmul(h, w_ukv.astype(BF16), a_col0=EV_CKV, a_pro="rms", a_gain=g_ckv, tn=1024,
                    name="mla_kv_up")
    qa_lo, qa_hi, qb_hi = _rope_tables(mq_gain, T, dqk ** -0.5)
    ka_lo, ka_hi, kb_hi = _rope_tables(mk_gain, T, 1.0)
    q_m = mla_prep(q_raw, None, 0, qa_lo, qa_hi, qb_hi, T)
    k_m = mla_prep(kv_raw, h, EV_KR, ka_lo, ka_hi, kb_hi, T)
    hp = 4
    kv_w = MLA_NOPE_DIM + MLA_V_DIM
    o_b = flash_attention(q_m, 0, k_m, 0, kv_raw, 0, B=B, T=T, G=H, R=1, HP=hp, v_width=hp * kv_w,
                          v_off=MLA_NOPE_DIM, v_step=kv_w, out_cols=H * MLA_V_DIM, tq=512, tk=512,
                          name="mla_attn")
    return matmul(o_a, w_out.astype(BF16), a2=o_b, resid=(x2d, g_a, T), out_dtype=F32, name="out_proj_even")


def _odd_mixer(xm, x2d, g_a, B, T, w_in, b_in, w_out, b_out, q_gain, k_gain, sinks):
    G, hd = SWA_KV_HEADS, SWA_HEAD_DIM
    nq = SWA_HEADS * hd
    kw = G * hd
    gain = jnp.concatenate([jnp.tile(q_gain * hd ** -0.5, SWA_HEADS), jnp.tile(k_gain, G),
                            jnp.ones((kw,), F32)])
    flag = jnp.concatenate([jnp.ones((nq + kw,), F32), jnp.zeros((kw,), F32)])
    h = matmul(xm, w_in.astype(BF16), bias=b_in, head_norm=(hd, hd, gain, flag), tn=1024,
               name="in_proj_odd")
    o_c = swa_attention(h, sinks, B, T, nq, nq + kw)
    return matmul(o_c, w_out.astype(BF16), bias=b_out, resid=(x2d, g_a, T), out_dtype=F32,
                  name="out_proj_odd")


def kernel(x, c, w_mod, mod_table, norm_attn, norm_ffn, w_in_even, w_out_even, nsa_q_gain, nsa_k_gain,
           nsa_pe_k, nsa_pe_v, nsa_w_ck1, nsa_w_ck2, nsa_w_cv1, nsa_w_cv2, mla_g_cq, mla_g_ckv, mla_w_uq,
           mla_w_ukv, mla_q_gain, mla_k_gain, w_in_odd, b_in_odd, w_out_odd, b_out_odd, swa_q_gain,
           swa_k_gain, swa_sinks, router_w, router_b, moe_w_gate, moe_b_gate, moe_w_up, moe_b_up,
           moe_w_down, moe_b_down):
    B, T, D = x.shape
    N = B * T
    depth = mod_table.shape[0]
    c_pad = jnp.pad(c, ((0, 8 - B % 8 if B % 8 else 0), (0, 0)))
    cond = matmul(c_pad, w_mod, a_pro="silu", out_dtype=F32, tn=1024, tk=2048, name="adaln_proj")[:B]
    x2d = x.reshape(N, D)
    for layer in range(depth):
        mod = (cond + mod_table[layer]).reshape(B, 6, 1, D)
        sh_a, sc_a, g_a, sh_f, sc_f, g_f = (mod[:, j] for j in range(6))
        xm = norm_mod(x2d, norm_attn[layer], sc_a, sh_a, T)
        i = layer // 2
        if layer % 2 == 0:
            x2d = _even_mixer(xm, x2d, g_a, B, T, w_in_even[i], w_out_even[i], nsa_q_gain[i], nsa_k_gain[i],
                              nsa_pe_k[i], nsa_pe_v[i], nsa_w_ck1[i], nsa_w_ck2[i], nsa_w_cv1[i],
                              nsa_w_cv2[i], mla_g_cq[i], mla_g_ckv[i], mla_w_uq[i], mla_w_ukv[i],
                              mla_q_gain[i], mla_k_gain[i])
        else:
            x2d = _odd_mixer(xm, x2d, g_a, B, T, w_in_odd[i], b_in_odd[i], w_out_odd[i], b_out_odd[i],
                             swa_q_gain[i], swa_k_gain[i], swa_sinks[i])
        x2d = moe_layer(x2d, norm_ffn[layer], sc_f, sh_f, g_f, T, router_w[layer], router_b[layer],
                        layer, moe_w_gate, moe_b_gate, moe_w_up, moe_b_up, moe_w_down, moe_b_down)
    return x2d.reshape(B, T, D)
```

```python
import functools
import math

import numpy as np
import jax
import jax.numpy as jnp
from jax import lax
from jax.experimental import pallas as pl
from jax.experimental.pallas import tpu as pltpu

BF16 = jnp.bfloat16
F32 = jnp.float32

NSA_HEADS = 16
NSA_KV_HEADS = 4
NSA_QK_DIM = 192
NSA_V_DIM = 128
CMP_BLOCK = 32
CMP_STRIDE = 16
SLC_BLOCK = 64
SLC_TOPN = 8
NSA_WINDOW = 512
MLA_HEADS = 16
MLA_Q_RANK = 1024
MLA_KV_RANK = 512
MLA_NOPE_DIM = 128
MLA_ROPE_DIM = 64
MLA_V_DIM = 128
ROPE_THETA = 10000.0
SWA_HEADS = 64
SWA_KV_HEADS = 8
SWA_HEAD_DIM = 64
SWA_WINDOW = 128
N_EXPERTS = 32
TOP_K = 4
SWIGLU_ALPHA = 1.702
SWIGLU_LIMIT = 7.0
MOE_ROW_BLOCK = 256
EPS = 1e-6

LANES = 128
HEAD_PAD = 256
NEG = -1e30
M_FLOOR = -1e29

EVEN_IN_SPLITS = (NSA_HEADS * NSA_QK_DIM,
                  NSA_KV_HEADS * NSA_QK_DIM, NSA_KV_HEADS * NSA_V_DIM,
                  NSA_KV_HEADS * NSA_QK_DIM, NSA_KV_HEADS * NSA_V_DIM,
                  NSA_KV_HEADS * NSA_QK_DIM, NSA_KV_HEADS * NSA_V_DIM,
                  3 * NSA_HEADS, MLA_Q_RANK, MLA_KV_RANK + MLA_ROPE_DIM)

_NQ = NSA_HEADS * HEAD_PAD
_NK = NSA_KV_HEADS * HEAD_PAD
_NV = NSA_KV_HEADS * NSA_V_DIM
EV_Q = 0
EV_KC = EV_Q + _NQ
EV_KS = EV_KC + _NK
EV_KW = EV_KS + _NK
EV_CQ = EV_KW + _NK
EV_CKV = EV_CQ + MLA_Q_RANK
EV_KR = EV_CKV + MLA_KV_RANK
EV_VC = EV_KR + 2 * MLA_ROPE_DIM
EV_VS = EV_VC + _NV
EV_VW = EV_VS + _NV
EV_GATE = EV_VW + _NV
EV_END = EV_GATE + LANES
EV_WIDTH = -(-EV_END // 512) * 512


def _tile(dim, want):
    t = min(dim, want)
    while dim % t:
        t //= 2
    return t


def _alibi_np(n):
    return np.exp2(-8.0 * np.arange(1, n + 1, dtype=np.float32) / n).astype(np.float32)


def _alibi(n):
    return jnp.asarray(_alibi_np(n))


def _head_norm(res, hd, real_d):
    tm, tn = res.shape
    x2 = res * res
    pieces = []
    if hd == HEAD_PAD:
        for s in range(tn // hd):
            ss = jnp.sum(x2[:, s * hd:s * hd + LANES] + x2[:, s * hd + LANES:(s + 1) * hd],
                         axis=1, keepdims=True)
            r = lax.rsqrt(ss * (1.0 / real_d) + EPS)
            pieces.append(jnp.broadcast_to(r, (tm, hd)))
    else:
        lo = lax.broadcasted_iota(jnp.int32, (tm, LANES), 1) < hd
        for s in range(tn // LANES):
            c = x2[:, s * LANES:(s + 1) * LANES]
            ss_lo = jnp.sum(jnp.where(lo, c, 0.0), axis=1, keepdims=True)
            ss_hi = jnp.sum(jnp.where(lo, 0.0, c), axis=1, keepdims=True)
            r_lo = lax.rsqrt(ss_lo * (1.0 / real_d) + EPS)
            r_hi = lax.rsqrt(ss_hi * (1.0 / real_d) + EPS)
            pieces.append(jnp.where(lo, r_lo, r_hi))
    return jnp.concatenate(pieces, axis=1) if len(pieces) > 1 else pieces[0]


def _mm_kernel(*refs, nk, a_pro, has_a2, has_bias, hd, real_d, has_resid):
    it = iter(refs)
    a_ref = next(it)
    a2_ref = next(it) if has_a2 else None
    w_ref = next(it)
    again_ref = next(it) if a_pro == "rms" else None
    b_ref = next(it) if has_bias else None
    gain_ref = flag_ref = None
    if hd:
        gain_ref = next(it)
        flag_ref = next(it)
    x_ref = gate_ref = None
    if has_resid:
        x_ref = next(it)
        gate_ref = next(it)
    o_ref = next(it)
    acc_ref = next(it) if nk > 1 else None

    a = a_ref[...]
    if a_pro == "silu":
        af = a.astype(F32)
        a = af * jax.nn.sigmoid(af)
    elif a_pro == "rms":
        af = a.astype(F32)
        r = lax.rsqrt(jnp.mean(af * af, axis=1, keepdims=True) + EPS)
        a = af * r * again_ref[...]
    if has_a2:
        k1 = a.shape[1]
        part = (jnp.dot(a.astype(BF16), w_ref[:k1, :].astype(BF16), preferred_element_type=F32)
                + jnp.dot(a2_ref[...].astype(BF16), w_ref[k1:, :].astype(BF16), preferred_element_type=F32))
    else:
        part = jnp.dot(a.astype(BF16), w_ref[...].astype(BF16), preferred_element_type=F32)

    def finish(res):
        if has_bias:
            res = res + b_ref[...]
        if hd:
            r = _head_norm(res, hd, real_d)
            res = res * jnp.where(flag_ref[...] > 0.0, r, 1.0) * gain_ref[...]
        if has_resid:
            res = x_ref[...] + gate_ref[...] * res
        o_ref[...] = res.astype(o_ref.dtype)

    if nk == 1:
        finish(part)
    else:
        k = pl.program_id(2)

        @pl.when(k == 0)
        def _():
            acc_ref[...] = part

        @pl.when(k > 0)
        def _():
            acc_ref[...] += part

        @pl.when(k == nk - 1)
        def _():
            finish(acc_ref[...])


def matmul(a, w, *, a2=None, a_col0=0, a_pro=None, a_gain=None, bias=None,
           head_norm=None, resid=None, out_dtype=BF16, tm=1024, tn=512, tk=4096, name="mm"):
    M = a.shape[0]
    K, N = w.shape
    tm, tn, tk = _tile(M, tm), _tile(N, tn), _tile(K, tk)
    if a_pro == "rms" or a2 is not None:
        tk = K
    assert a_col0 % tk == 0 and M % tm == 0 and N % tn == 0 and K % tk == 0
    nk = K // tk
    koff = a_col0 // tk
    hd = head_norm[0] if head_norm else 0
    real_d = head_norm[1] if head_norm else 0
    if hd:
        assert tn % max(hd, LANES) == 0

    if a2 is None:
        ins = [a, w]
        specs = [pl.BlockSpec((tm, tk), lambda i, j, k: (i, koff + k))]
    else:
        assert a_col0 == 0 and a_pro is None and a.shape[1] + a2.shape[1] == K
        ins = [a, a2, w]
        specs = [pl.BlockSpec((tm, a.shape[1]), lambda i, j, k: (i, 0)),
                 pl.BlockSpec((tm, a2.shape[1]), lambda i, j, k: (i, 0))]
    specs.append(pl.BlockSpec((tk, tn), lambda i, j, k: (k, j)))
    if a_pro == "rms":
        ins.append(a_gain.reshape(1, K).astype(F32))
        specs.append(pl.BlockSpec((1, tk), lambda i, j, k: (0, k)))
    if bias is not None:
        ins.append(bias.reshape(1, N).astype(F32))
        specs.append(pl.BlockSpec((1, tn), lambda i, j, k: (0, j)))
    if hd:
        ins += [head_norm[2].reshape(1, N).astype(F32), head_norm[3].reshape(1, N).astype(F32)]
        specs += [pl.BlockSpec((1, tn), lambda i, j, k: (0, j))] * 2
    if resid is not None:
        x, gate, rows_per_batch = resid
        assert rows_per_batch % tm == 0
        bpb = rows_per_batch // tm
        ins += [x, gate]
        specs += [pl.BlockSpec((tm, tn), lambda i, j, k: (i, j)),
                  pl.BlockSpec((None, 1, tn), lambda i, j, k: (i // bpb, 0, j))]
    kern = functools.partial(_mm_kernel, nk=nk, a_pro=a_pro, has_a2=a2 is not None, has_bias=bias is not None,
                             hd=hd, real_d=real_d, has_resid=resid is not None)
    return pl.pallas_call(
        kern,
        out_shape=jax.ShapeDtypeStruct((M, N), out_dtype),
        grid=(M // tm, N // tn, nk),
        in_specs=specs,
        out_specs=pl.BlockSpec((tm, tn), lambda i, j, k: (i, j)),
        scratch_shapes=[pltpu.VMEM((tm, tn), F32)] if nk > 1 else [],
        compiler_params=pltpu.CompilerParams(
            dimension_semantics=("parallel", "parallel", "arbitrary")),
        name=name,
    )(*ins)


def _split_bf16(v):
    hi = v.astype(BF16)
    lo = (v - hi.astype(F32)).astype(BF16)
    return hi, lo


def _pack_bf16_pairs(v):
    half = v.shape[1] // 2
    vb = v.astype(BF16).astype(F32)
    lo = lax.shift_right_logical(lax.bitcast_convert_type(vb[:, :half], jnp.int32), 16)
    hi = lax.bitcast_convert_type(vb[:, half:], jnp.int32) & jnp.int32(-65536)
    return hi | lo


def _unpack_bf16_pairs(u):
    lo = lax.bitcast_convert_type(lax.shift_left(u, 16), F32)
    hi = lax.bitcast_convert_type(u & jnp.int32(-65536), F32)
    return jnp.concatenate([lo, hi], axis=1).astype(BF16)


def _norm_mod_kernel(x_ref, g_ref, sc_ref, sh_ref, *rest, route):
    x = x_ref[...]
    r = lax.rsqrt(jnp.mean(x * x, axis=1, keepdims=True) + EPS)
    xm = (x * r * g_ref[...]) * (1.0 + sc_ref[...]) + sh_ref[...]
    if not route:
        (o_ref,) = rest
        o_ref[...] = xm.astype(o_ref.dtype)
        return
    rw_ref, rb_ref, o_ref, e_ref, p_ref, rank_ref, cnt_ref, run_ref = rest
    o_ref[...] = _pack_bf16_pairs(xm)
    a_hi, a_lo = _split_bf16(xm)
    w = rw_ref[...]
    w_hi, w_lo = _split_bf16(w)
    logits = (jnp.dot(a_hi, w_hi, preferred_element_type=F32)
              + jnp.dot(a_hi, w_lo, preferred_element_type=F32)
              + jnp.dot(a_lo, w_hi, preferred_element_type=F32)) + rb_ref[...]
    tt = logits.shape[0]
    lane = lax.broadcasted_iota(jnp.int32, (tt, LANES), 1)
    lane_f = lane.astype(F32)
    work = jnp.where(lane < N_EXPERTS, logits, -jnp.inf)
    e_out = jnp.zeros((tt, LANES), F32)
    v_out = jnp.full((tt, LANES), -jnp.inf, F32)
    hits = []
    for kk in range(TOP_K):
        m = jnp.max(work, axis=1, keepdims=True)
        idx = jnp.min(jnp.where(work == m, lane_f, float(LANES)), axis=1, keepdims=True)
        e_out = jnp.where(lane == kk, idx, e_out)
        v_out = jnp.where(lane == kk, m, v_out)
        hits.append(lane_f == idx)
        work = jnp.where(hits[-1], -jnp.inf, work)
    e_out = e_out.astype(jnp.int32)
    vmax = jnp.max(v_out, axis=1, keepdims=True)
    pe = jnp.exp(v_out - vmax)
    p_ref[...] = pe / jnp.sum(pe, axis=1, keepdims=True)
    e_ref[...] = e_out

    @pl.when(pl.program_id(0) == 0)
    def _():
        run_ref[...] = jnp.zeros_like(run_ref)

    onehot = jnp.zeros((tt, LANES), F32)
    for hit in hits:
        onehot = jnp.where(hit, 1.0, onehot)
    earlier = (lax.broadcasted_iota(jnp.int32, (tt, tt), 1)
               < lax.broadcasted_iota(jnp.int32, (tt, tt), 0))
    prefix = jnp.dot(jnp.where(earlier, 1.0, 0.0).astype(BF16), onehot.astype(BF16),
                     preferred_element_type=F32)
    base = run_ref[...] + prefix
    rank = jnp.zeros((tt, LANES), F32)
    for kk, hit in enumerate(hits):
        rk = jnp.sum(jnp.where(hit, base, 0.0), axis=1, keepdims=True)
        rank = jnp.where(lane == kk, rk, rank)
    rank_ref[...] = rank.astype(jnp.int32)
    total = run_ref[...] + jnp.sum(onehot, axis=0, keepdims=True)
    run_ref[...] = total
    cnt_ref[...] = total.astype(jnp.int32)


def norm_mod(x2d, gain, sc, sh, rows_per_batch, *, out_dtype=BF16, router=None, tt=256):
    N, D = x2d.shape
    tt = _tile(rows_per_batch, tt)
    bpb = rows_per_batch // tt
    ins = [x2d, gain.reshape(1, D), sc, sh]
    specs = [pl.BlockSpec((tt, D), lambda i: (i, 0)),
             pl.BlockSpec((1, D), lambda i: (0, 0)),
             pl.BlockSpec((None, 1, D), lambda i: (i // bpb, 0, 0)),
             pl.BlockSpec((None, 1, D), lambda i: (i // bpb, 0, 0))]
    if router is None:
        out_shape = [jax.ShapeDtypeStruct((N, D), out_dtype)]
        out_specs = [pl.BlockSpec((tt, D), lambda i: (i, 0))]
        scratch = []
    else:
        rw, rb = router
        E = rw.shape[1]
        rw_p = jnp.pad(rw, ((0, 0), (0, LANES - E)))
        rb_p = jnp.pad(rb, (0, LANES - E)).reshape(1, LANES)
        ins += [rw_p, rb_p]
        specs += [pl.BlockSpec((D, LANES), lambda i: (0, 0)),
                  pl.BlockSpec((1, LANES), lambda i: (0, 0))]
        out_shape = [jax.ShapeDtypeStruct((N, D // 2), jnp.int32),
                     jax.ShapeDtypeStruct((N, LANES), jnp.int32),
                     jax.ShapeDtypeStruct((N, LANES), F32),
                     jax.ShapeDtypeStruct((N, LANES), jnp.int32),
                     jax.ShapeDtypeStruct((1, LANES), jnp.int32)]
        row = pl.BlockSpec((tt, LANES), lambda i: (i, 0))
        out_specs = [pl.BlockSpec((tt, D // 2), lambda i: (i, 0)), row, row, row,
                     pl.BlockSpec((1, LANES), lambda i: (0, 0))]
        scratch = [pltpu.VMEM((1, LANES), F32)]
    res = pl.pallas_call(
        functools.partial(_norm_mod_kernel, route=router is not None),
        out_shape=out_shape, grid=(N // tt,), in_specs=specs, out_specs=out_specs,
        scratch_shapes=scratch,
        compiler_params=pltpu.CompilerParams(
            dimension_semantics=("parallel",) if router is None else ("arbitrary",)),
        name="norm_mod_route" if router is not None else "norm_mod",
    )(*ins)
    return res if router is not None else res[0]


def _compress_kernel(h_ref, pe_ref, w1_ref, w2_ref, g_ref, o_ref, xf_ref, *, real_d, norm, nh):
    n_tiles = xf_ref.shape[0]
    for j in range(n_tiles):
        xf_ref[j] = h_ref[:, j * LANES:(j + 1) * LANES].astype(F32)
    half = CMP_BLOCK // 2
    top = bot = None
    for l in range(half):
        parts = [xf_ref[j, pl.ds(l, nh, stride=CMP_STRIDE), :] for j in range(n_tiles)]
        xl = jnp.concatenate(parts, axis=1) if n_tiles > 1 else parts[0]
        t = jnp.dot((xl + pe_ref[l:l + 1, :]).astype(BF16), w1_ref[l], preferred_element_type=F32)
        b = jnp.dot((xl + pe_ref[half + l:half + l + 1, :]).astype(BF16), w1_ref[half + l],
                    preferred_element_type=F32)
        top = t if top is None else top + t
        bot = b if bot is None else bot + b
    hid = top + pltpu.roll(bot, nh - 1, 0)
    hid = hid * jax.nn.sigmoid(hid)
    y = jnp.dot(hid.astype(BF16), w2_ref[...], preferred_element_type=F32)
    if norm:
        r = lax.rsqrt(jnp.sum(y * y, axis=1, keepdims=True) * (1.0 / real_d) + EPS)
        y = y * r * g_ref[...]
    o_ref[...] = y.astype(o_ref.dtype)


def compress(h, col0, B, T, pe, w1, w2, gain, d, dp):
    G = NSA_KV_HEADS
    nh = T // CMP_STRIDE
    cb = col0 // dp
    pe_p = jnp.pad(pe, ((0, 0), (0, dp - d)))
    w1_p = jnp.pad(w1.reshape(CMP_BLOCK, d, d), ((0, 0), (0, dp - d), (0, dp - d))).astype(BF16)
    w2_p = jnp.pad(w2, ((0, dp - d), (0, dp - d))).astype(BF16)
    g_p = (jnp.ones((dp,), F32) if gain is None else jnp.pad(gain, (0, dp - d))).reshape(1, dp)
    full = lambda shape: pl.BlockSpec(shape, lambda b, g: (0,) * len(shape))
    return pl.pallas_call(
        functools.partial(_compress_kernel, real_d=d, norm=gain is not None, nh=nh),
        out_shape=jax.ShapeDtypeStruct((B * G, nh, dp), BF16),
        grid=(B, G),
        in_specs=[pl.BlockSpec((T, dp), lambda b, g: (b, cb + g)),
                  full((CMP_BLOCK, dp)), full((CMP_BLOCK, dp, dp)), full((dp, dp)), full((1, dp))],
        out_specs=pl.BlockSpec((None, nh, dp), lambda b, g: (b * G + g, 0, 0)),
        scratch_shapes=[pltpu.VMEM((dp // LANES, T, LANES), F32)],
        compiler_params=pltpu.CompilerParams(dimension_semantics=("parallel", "parallel")),
        name="nsa_compress",
    )(h, pe_p, w1_p, w2_p, g_p)


def _stack_heads(q_ref, n, w):
    return jnp.concatenate([q_ref[:, r * w:(r + 1) * w] for r in range(n)], axis=0)


def _row_scalars(vals, tq):
    rows = len(vals) * tq
    rid = lax.broadcasted_iota(jnp.int32, (rows, 1), 0) // tq
    col = jnp.full((rows, 1), vals[-1], F32)
    for r in range(len(vals) - 2, -1, -1):
        col = jnp.where(rid == r, vals[r], col)
    return col


def _nt_dot(a, b):
    return lax.dot_general(a, b, (((1,), (1,)), ((), ())), preferred_element_type=F32)


def _cmp_attn_kernel(slope_ref, q_ref, kc_ref, vc_ref, ovl_ref, o_ref, bits_ref, *, tq, R, n_slc):
    g = pl.program_id(1)
    t0 = pl.program_id(2) * tq
    rows = R * tq
    q = _stack_heads(q_ref, R, HEAD_PAD)
    s = _nt_dot(q, kc_ref[...])
    ncol = s.shape[1]
    row_t = t0 + lax.broadcasted_iota(jnp.int32, (rows, ncol), 0) % tq
    n_id = lax.broadcasted_iota(jnp.int32, (rows, ncol), 1)
    dist = row_t - (n_id * CMP_STRIDE + (CMP_BLOCK - 1))
    slope = _row_scalars([slope_ref[g * R + r] for r in range(R)], tq)
    valid = dist >= 0
    s = jnp.where(valid, s - slope * dist.astype(F32), NEG)
    m = jnp.max(s, axis=1, keepdims=True)
    p = jnp.where(valid, jnp.exp(s - m), 0.0)
    p = p / jnp.maximum(jnp.sum(p, axis=1, keepdims=True), 1e-30)
    o = jnp.dot(p.astype(BF16), vc_ref[...], preferred_element_type=F32)
    for r in range(R):
        o_ref[:, r * NSA_V_DIM:(r + 1) * NSA_V_DIM] = o[r * tq:(r + 1) * tq].astype(o_ref.dtype)

    psum = p[0:tq]
    for r in range(1, R):
        psum = psum + p[r * tq:(r + 1) * tq]
    p_hi, p_lo = _split_bf16(psum)
    imp = (jnp.dot(p_hi, ovl_ref[...], preferred_element_type=F32)
           + jnp.dot(p_lo, ovl_ref[...], preferred_element_type=F32))
    lane = lax.broadcasted_iota(jnp.int32, (tq, LANES), 1)
    cur = (t0 + lax.broadcasted_iota(jnp.int32, (tq, LANES), 0)) // SLC_BLOCK
    forced = (lane == 0) | (lane == cur) | (lane == cur - 1)
    work = jnp.where(forced, jnp.inf, jnp.where(lane > cur, -jnp.inf, imp))
    removed = -3.0e38
    work = jnp.where(lane < n_slc, jnp.where(work == -jnp.inf, -2.0e38, work), removed)
    sel = jnp.zeros((tq, LANES), jnp.bool_)
    lane_f = lane.astype(F32)
    for _ in range(min(SLC_TOPN, n_slc)):
        mx = jnp.max(work, axis=1, keepdims=True)
        idx = jnp.min(jnp.where(work == mx, lane_f, float(LANES)), axis=1, keepdims=True)
        hit = lane_f == idx
        sel = sel | hit
        work = jnp.where(hit, removed, work)
    half = 16
    w_lo = jnp.where(sel & (lane < half), jnp.left_shift(1, jnp.minimum(lane, half - 1)), 0)
    w_hi = jnp.where(sel & (lane >= half), jnp.left_shift(1, jnp.clip(lane - half, 0, half - 1)), 0)
    b_lo = jnp.sum(w_lo.astype(F32), axis=1, keepdims=True).astype(jnp.int32)
    b_hi = jnp.sum(w_hi.astype(F32), axis=1, keepdims=True).astype(jnp.int32)
    bits = b_lo | jnp.left_shift(b_hi, half)
    bits_ref[...] = jnp.broadcast_to(bits, (tq, LANES))


def cmp_attention(h, kc, vc, B, T, tq=256):
    G, R = NSA_KV_HEADS, NSA_HEADS // NSA_KV_HEADS
    tq = _tile(T, tq)
    nq = T // tq
    n_slc = T // SLC_BLOCK
    n_cmp = kc.shape[1]
    assert n_slc <= 32 and n_cmp <= LANES and n_cmp % 8 == 0
    cmp_start = np.arange(n_cmp) * CMP_STRIDE
    slc_start = np.arange(LANES) * SLC_BLOCK
    ovl = ((cmp_start[:, None] < slc_start[None, :] + SLC_BLOCK)
           & (cmp_start[:, None] + CMP_BLOCK > slc_start[None, :])
           & (np.arange(LANES)[None, :] < n_slc) & (np.arange(n_cmp)[:, None] < n_cmp - 1))
    ovl = jnp.asarray(ovl, BF16)
    qw = R * HEAD_PAD
    return pl.pallas_call(
        functools.partial(_cmp_attn_kernel, tq=tq, R=R, n_slc=n_slc),
        out_shape=[jax.ShapeDtypeStruct((B * T, NSA_HEADS * NSA_V_DIM), BF16),
                   jax.ShapeDtypeStruct((B, G, T, LANES), jnp.int32)],
        grid=(B, G, nq),
        in_specs=[pl.BlockSpec(memory_space=pltpu.SMEM),
                  pl.BlockSpec((tq, qw), lambda b, g, i: (b * nq + i, EV_Q // qw + g)),
                  pl.BlockSpec((None, n_cmp, HEAD_PAD), lambda b, g, i: (b * G + g, 0, 0)),
                  pl.BlockSpec((None, n_cmp, NSA_V_DIM), lambda b, g, i: (b * G + g, 0, 0)),
                  pl.BlockSpec((n_cmp, LANES), lambda b, g, i: (0, 0))],
        out_specs=[pl.BlockSpec((tq, R * NSA_V_DIM), lambda b, g, i: (b * nq + i, g)),
                   pl.BlockSpec((None, None, tq, LANES), lambda b, g, i: (b, g, i, 0))],
        compiler_params=pltpu.CompilerParams(dimension_semantics=("parallel", "parallel", "parallel")),
        name="nsa_cmp_attn",
    )(_alibi(NSA_HEADS), h, kc, vc, ovl)


ALIBI_LANE0 = 64
SEL_LANE0 = 70


def _key_table(T):
    s = np.arange(T)
    tab = np.zeros((T, LANES), np.float32)
    tab[:, ALIBI_LANE0:ALIBI_LANE0 + 3] = (s // 256 * 256)[:, None]
    tab[:, ALIBI_LANE0 + 3:ALIBI_LANE0 + 6] = (s % 256)[:, None]
    tab[s, SEL_LANE0 + s // SLC_BLOCK] = 1.0
    return jnp.asarray(tab, BF16)


def _slope_parts(n_heads):
    s = _alibi_np(n_heads)
    h1 = s.astype(BF16).astype(np.float32)
    h2 = (s - h1).astype(BF16).astype(np.float32)
    h3 = (s - h1 - h2).astype(BF16).astype(np.float32)
    return jnp.asarray(np.stack([h1, h2, h3], axis=1).reshape(-1))


def _flash_kernel(*refs, tq, tk, R, HP, v_off, v_step, window, alibi, use_bits):
    it = iter(refs)
    slope_ref = next(it) if alibi else None
    q_ref = next(it)
    k_ref = next(it)
    v_ref = next(it)
    ktab_ref = next(it) if alibi else None
    bits_ref = next(it) if use_bits else None
    o_ref = next(it)
    m_ref = next(it)
    acc_ref = next(it)

    g = pl.program_id(1)
    t0 = pl.program_id(2) * tq
    rows = R * tq
    vd = NSA_V_DIM
    lane = lax.broadcasted_iota(jnp.int32, (tq, LANES), 1)
    sel_add = None
    if use_bits:
        j = jnp.clip(lane - SEL_LANE0, 0, 31)
        in_sel = (lane >= SEL_LANE0) & (lane < SEL_LANE0 + 32)
        picked = (jnp.right_shift(bits_ref[...], j) & 1) != 0
        sel_add = jnp.where(in_sel & jnp.logical_not(picked), NEG, 0.0)

    qs = []
    for hp in range(HP):
        pieces = []
        for r in range(R):
            c0 = (hp * R + r) * HEAD_PAD
            q_lo = q_ref[:, c0:c0 + LANES]
            q_hi = q_ref[:, c0 + LANES:c0 + HEAD_PAD]
            if alibi:
                hd = (g * HP + hp) * R + r
                add = jnp.zeros((tq, LANES), F32) if sel_add is None else sel_add
                for part in range(3):
                    sp = slope_ref[3 * hd + part]
                    add = jnp.where((lane == ALIBI_LANE0 + part) | (lane == ALIBI_LANE0 + 3 + part), sp, add)
                q_hi = (q_hi.astype(F32) + add).astype(BF16)
            pieces.append(jnp.concatenate([q_lo, q_hi], axis=1))
        qs.append(jnp.concatenate(pieces, axis=0) if R > 1 else pieces[0])

    rel0 = (lax.broadcasted_iota(jnp.int32, (rows, tk), 0) % tq
            - lax.broadcasted_iota(jnp.int32, (rows, tk), 1))
    ones = jnp.ones((tk, LANES), BF16)
    m_ref[...] = jnp.full(m_ref.shape, M_FLOOR, F32)
    acc_ref[...] = jnp.zeros(acc_ref.shape, F32)

    def step(c, masked):
        s0 = pl.multiple_of(c * tk, tk)
        off = t0 - s0
        if masked:
            valid = rel0 >= -off
            if window is not None:
                valid = valid & (rel0 < window - off)
        for hp in range(HP):
            kc = k_ref[pl.ds(s0, tk), hp * HEAD_PAD:(hp + 1) * HEAD_PAD]
            if alibi:
                kc = jnp.concatenate([kc[:, :LANES], kc[:, LANES:] + ktab_ref[pl.ds(s0, tk), :]], axis=1)
            vc = v_ref[pl.ds(s0, tk), v_off + hp * v_step:v_off + hp * v_step + vd]
            s = _nt_dot(qs[hp], kc)
            if masked:
                s = jnp.where(valid, s, NEG)
            tiles = [s[:, j * LANES:(j + 1) * LANES] for j in range(tk // LANES)]
            mx = tiles[0]
            for t in tiles[1:]:
                mx = jnp.maximum(mx, t)
            m_old = m_ref[hp]
            m_new = jnp.maximum(m_old, jnp.max(mx, axis=1, keepdims=True))
            alpha = jnp.exp(m_old - m_new)
            p = jnp.concatenate([jnp.exp(t - m_new).astype(BF16) for t in tiles], axis=1)
            pv = jnp.dot(p, jnp.concatenate([vc, ones], axis=1), preferred_element_type=F32)
            acc_ref[hp] = jnp.concatenate([alpha, alpha], axis=1) * acc_ref[hp] + pv
            m_ref[hp] = m_new

    def body(c, carry):
        s0 = c * tk
        full = s0 + tk - 1 <= t0
        if window is not None:
            full = full & (s0 >= t0 + tq - window)
        lax.cond(full, lambda: step(c, False), lambda: step(c, True))
        return carry

    c_hi = (t0 + tq - 1) // tk
    c_lo = 0 if window is None else jnp.maximum((t0 - window + 1) // tk, 0)
    lax.fori_loop(c_lo, c_hi + 1, body, 0)
    for hp in range(HP):
        acc = acc_ref[hp]
        o = acc[:, :vd] / jnp.maximum(acc[:, vd:], 1e-30)
        for r in range(R):
            c0 = (hp * R + r) * vd
            o_ref[:, c0:c0 + vd] = o[r * tq:(r + 1) * tq].astype(o_ref.dtype)


def flash_attention(q_arr, q_col0, k_arr, k_col0, v_arr, v_col0, *, B, T, G, R, HP=1, v_width=NSA_V_DIM,
                    v_off=0, v_step=0, out_cols, out_col0=0, window=None, n_alibi_heads=0, bits=None,
                    tq=256, tk=256, name="flash"):
    vd = NSA_V_DIM
    tq = _tile(T, tq)
    tk = _tile(T, tk)
    nq = T // tq
    qw, kw, ow = HP * R * HEAD_PAD, HP * HEAD_PAD, HP * R * vd
    assert q_col0 % qw == 0 and k_col0 % kw == 0 and v_col0 % v_width == 0 and out_col0 % ow == 0
    assert bits is None or n_alibi_heads
    qb, kb, vb, ob = q_col0 // qw, k_col0 // kw, v_col0 // v_width, out_col0 // ow
    alibi = n_alibi_heads > 0
    ins, specs = [], []
    if alibi:
        ins.append(_slope_parts(n_alibi_heads))
        specs.append(pl.BlockSpec(memory_space=pltpu.SMEM))
    ins += [q_arr, k_arr, v_arr]
    specs += [pl.BlockSpec((tq, qw), lambda b, g, i: (b * nq + i, qb + g)),
              pl.BlockSpec((T, kw), lambda b, g, i: (b, kb + g)),
              pl.BlockSpec((T, v_width), lambda b, g, i: (b, vb + g))]
    if alibi:
        ins.append(_key_table(T))
        specs.append(pl.BlockSpec((T, LANES), lambda b, g, i: (0, 0)))
    if bits is not None:
        ins.append(bits)
        specs.append(pl.BlockSpec((None, None, tq, LANES), lambda b, g, i: (b, g, i, 0)))
    rows = R * tq
    return pl.pallas_call(
        functools.partial(_flash_kernel, tq=tq, tk=tk, R=R, HP=HP, v_off=v_off, v_step=v_step,
                          window=window, alibi=alibi, use_bits=bits is not None),
        out_shape=jax.ShapeDtypeStruct((B * T, out_cols), BF16),
        grid=(B, G // HP, nq),
        in_specs=specs,
        out_specs=pl.BlockSpec((tq, ow), lambda b, g, i: (b * nq + i, ob + g)),
        scratch_shapes=[pltpu.VMEM((HP, rows, LANES), F32), pltpu.VMEM((HP, rows, 2 * vd), F32)],
        compiler_params=pltpu.CompilerParams(dimension_semantics=("parallel", "parallel", "parallel")),
        name=name,
    )(*ins)


def _nsa_combine_kernel(gate_ref, oc_ref, os_ref, ow_ref, o_ref):
    gates = jax.nn.sigmoid(gate_ref[...].astype(F32))
    for h in range(NSA_HEADS):
        sl = slice(h * NSA_V_DIM, (h + 1) * NSA_V_DIM)
        acc = None
        for br, ref in enumerate((oc_ref, os_ref, ow_ref)):
            gcol = gates[:, 3 * h + br:3 * h + br + 1]
            term = gcol * ref[:, sl].astype(F32)
            acc = term if acc is None else acc + term
        o_ref[:, sl] = acc.astype(o_ref.dtype)


def nsa_combine(h, o_cmp, o_slc, o_win, tt=256):
    N, W = o_cmp.shape
    tt = _tile(N, tt)
    gb = EV_GATE // LANES
    blk = pl.BlockSpec((tt, W), lambda i: (i, 0))
    return pl.pallas_call(
        _nsa_combine_kernel,
        out_shape=jax.ShapeDtypeStruct((N, W), BF16),
        grid=(N // tt,),
        in_specs=[pl.BlockSpec((tt, LANES), lambda i: (i, gb)), blk, blk, blk],
        out_specs=blk,
        compiler_params=pltpu.CompilerParams(dimension_semantics=("parallel",)),
        name="nsa_combine",
    )(h, o_cmp, o_slc, o_win)


def _mla_prep_kernel(*refs, shared_rope):
    if shared_rope:
        x_ref, hi_ref, alo_ref, ahi_ref, bhi_ref, o_ref = refs
    else:
        x_ref, alo_ref, ahi_ref, bhi_ref, o_ref = refs
    tt = x_ref.shape[0]
    is_rope = lax.broadcasted_iota(jnp.int32, (tt, LANES), 1) < MLA_ROPE_DIM
    a_hi, b_hi, a_lo = ahi_ref[...], bhi_ref[...], alo_ref[...]

    def rope_part(x_hi):
        ss = jnp.sum(jnp.where(is_rope, x_hi * x_hi, 0.0), axis=1, keepdims=True)
        return x_hi * a_hi + pltpu.roll(x_hi * b_hi, MLA_ROPE_DIM, 1), ss

    if shared_rope:
        y_hi, ss_hi = rope_part(hi_ref[...].astype(F32))
    for h in range(MLA_HEADS):
        c = h * HEAD_PAD
        x_lo = x_ref[:, c:c + LANES].astype(F32)
        if not shared_rope:
            y_hi, ss_hi = rope_part(x_ref[:, c + LANES:c + HEAD_PAD].astype(F32))
        ss = jnp.sum(x_lo * x_lo, axis=1, keepdims=True) + ss_hi
        r = lax.rsqrt(ss * (1.0 / (MLA_NOPE_DIM + MLA_ROPE_DIM)) + EPS)
        o_ref[:, c:c + LANES] = (x_lo * r * a_lo).astype(o_ref.dtype)
        o_ref[:, c + LANES:c + HEAD_PAD] = (y_hi * r).astype(o_ref.dtype)


def mla_prep(x_arr, rope_arr, rope_col0, a_lo, a_hi, b_hi, T, tt=256):
    N = x_arr.shape[0]
    W = MLA_HEADS * HEAD_PAD
    tt = _tile(T, tt)
    nt = T // tt
    shared = rope_arr is not None
    ins = [x_arr]
    specs = [pl.BlockSpec((tt, W), lambda i: (i, 0))]
    if shared:
        rb = rope_col0 // LANES
        ins.append(rope_arr)
        specs.append(pl.BlockSpec((tt, LANES), lambda i: (i, rb)))
    ins += [a_lo, a_hi, b_hi]
    specs += [pl.BlockSpec((1, LANES), lambda i: (0, 0)),
              pl.BlockSpec((tt, LANES), lambda i: (i % nt, 0)),
              pl.BlockSpec((tt, LANES), lambda i: (i % nt, 0))]
    return pl.pallas_call(
        functools.partial(_mla_prep_kernel, shared_rope=shared),
        out_shape=jax.ShapeDtypeStruct((N, W), BF16),
        grid=(N // tt,),
        in_specs=specs,
        out_specs=pl.BlockSpec((tt, W), lambda i: (i, 0)),
        compiler_params=pltpu.CompilerParams(dimension_semantics=("parallel",)),
        name="mla_prep",
    )(*ins)


def _rope_tables(gain, T, scale):
    half = MLA_ROPE_DIM // 2
    inv_freq = ROPE_THETA ** (-jnp.arange(half, dtype=F32) / half)
    ang = jnp.arange(T, dtype=F32)[:, None] * inv_freq[None, :]
    cos2 = jnp.concatenate([jnp.cos(ang), jnp.cos(ang)], axis=1)
    sin_s = jnp.concatenate([-jnp.sin(ang), jnp.sin(ang)], axis=1)
    g_nope, g_rope = gain[:MLA_NOPE_DIM], gain[MLA_NOPE_DIM:]
    g_perm = jnp.concatenate([g_rope[half:], g_rope[:half]])
    zeros = jnp.zeros((T, MLA_ROPE_DIM), F32)
    a_lo = (g_nope * scale).reshape(1, LANES)
    a_hi = jnp.concatenate([g_rope[None, :] * cos2 * scale, zeros], axis=1)
    b_hi = jnp.concatenate([zeros, g_perm[None, :] * sin_s * scale], axis=1)
    return a_lo, a_hi, b_hi


def _swa_kernel(qadd_ref, slope_ref, sink_ref, q_ref, k_ref, v_ref, ktab_ref, o_ref, mask_ref,
                *, tq, tk, R):
    t0 = pl.program_id(2) * tq
    npair = R // 2
    rows = npair * tq
    qp = _stack_heads(q_ref, npair, LANES)
    lane = lax.broadcasted_iota(jnp.int32, (rows, LANES), 1)
    first = lane < SWA_HEAD_DIM
    start = pl.multiple_of(jnp.maximum(t0 - SWA_WINDOW, 0), LANES)
    own_half = lax.broadcasted_iota(jnp.int32, (tk, LANES), 1) // SWA_HEAD_DIM == pl.program_id(1) % 2

    def both_halves(ref):
        pair = ref[pl.ds(start, tk), :].astype(F32)
        return jnp.where(own_half, pair, pltpu.roll(pair, SWA_HEAD_DIM, 1)).astype(BF16)

    kc = jnp.concatenate([both_halves(k_ref), ktab_ref[pl.ds(start, tk), :]], axis=1)
    v_aug = jnp.concatenate([both_halves(v_ref), jnp.ones((tk, LANES), BF16)], axis=1)
    @pl.when(pl.program_id(2) <= 1)
    def _():
        row_in = lax.broadcasted_iota(jnp.int32, (2 * rows, tk), 0) % tq
        rel = t0 - start + row_in - lax.broadcasted_iota(jnp.int32, (2 * rows, tk), 1)
        mask_ref[...] = jnp.where((rel >= 0) & (rel < SWA_WINDOW), 0.0, NEG)

    t_row = (t0 + lax.broadcasted_iota(jnp.int32, (2 * rows, LANES), 0) % tq).astype(F32)
    zero = jnp.zeros_like(qp)
    qu = jnp.concatenate([jnp.where(first, qp, zero), jnp.where(first, zero, qp)], axis=0)
    qa = jnp.concatenate([qu, qadd_ref[...].reshape(2 * rows, LANES)], axis=1)
    sink_t = (sink_ref[...].reshape(2 * rows, LANES)
              + slope_ref[...].reshape(2 * rows, LANES) * t_row)
    s = _nt_dot(qa, kc) + mask_ref[...]
    tiles = [s[:, j * LANES:(j + 1) * LANES] for j in range(tk // LANES)]
    mx = tiles[0]
    for t in tiles[1:]:
        mx = jnp.maximum(mx, t)
    m = jnp.maximum(jnp.max(mx, axis=1, keepdims=True), sink_t)
    p = jnp.concatenate([jnp.exp(t - m).astype(BF16) for t in tiles], axis=1)
    pv = jnp.dot(p, v_aug, preferred_element_type=F32)
    denom = pv[:, LANES:] + jnp.exp(sink_t - m)
    both = pv[:, :LANES] / jnp.maximum(denom, 1e-30)
    o = jnp.where(first, both[:rows], both[rows:])
    for p_ in range(npair):
        o_ref[:, p_ * LANES:(p_ + 1) * LANES] = o[p_ * tq:(p_ + 1) * tq].astype(o_ref.dtype)


def swa_attention(h, sinks, B, T, k_col0, v_col0, tq=128):
    G, R = SWA_KV_HEADS, SWA_HEADS // SWA_KV_HEADS
    tq = _tile(T, tq)
    tk = min(T, tq + SWA_WINDOW)
    assert tk % LANES == 0 and tq % LANES == 0
    nq = T // tq
    qw = R * SWA_HEAD_DIM
    kb, vb = k_col0 // LANES, v_col0 // LANES
    npair = R // 2
    rows = npair * tq
    per_head = lambda v: jnp.repeat(v.reshape(G, npair, 2).transpose(0, 2, 1), tq, axis=2)
    lanes = lambda v: jnp.broadcast_to(v[..., None], (G, 2, rows, LANES))
    parts = _slope_parts(SWA_HEADS).reshape(SWA_HEADS, 3)
    lane_id = np.arange(LANES)
    qadd = jnp.zeros((G, 2, rows, LANES), F32)
    for part in range(3):
        at_lane = (lane_id == ALIBI_LANE0 + part) | (lane_id == ALIBI_LANE0 + 3 + part)
        qadd = jnp.where(at_lane[None, None, None, :], lanes(per_head(parts[:, part])), qadd)
    slope_rows = lanes(per_head(_alibi(SWA_HEADS)))
    sink_rows = lanes(per_head(sinks.astype(F32)))
    const_spec = pl.BlockSpec((None, 2, rows, LANES), lambda b, g, i: (g, 0, 0, 0))
    return pl.pallas_call(
        functools.partial(_swa_kernel, tq=tq, tk=tk, R=R),
        out_shape=jax.ShapeDtypeStruct((B * T, SWA_HEADS * SWA_HEAD_DIM), BF16),
        grid=(B, G, nq),
        in_specs=[const_spec, const_spec, const_spec,
                  pl.BlockSpec((tq, qw), lambda b, g, i: (b * nq + i, g)),
                  pl.BlockSpec((T, LANES), lambda b, g, i: (b, kb + g // 2)),
                  pl.BlockSpec((T, LANES), lambda b, g, i: (b, vb + g // 2)),
                  pl.BlockSpec((T, LANES), lambda b, g, i: (0, 0))],
        out_specs=pl.BlockSpec((tq, qw), lambda b, g, i: (b * nq + i, g)),
        scratch_shapes=[pltpu.VMEM((2 * rows, tk), F32)],
        compiler_params=pltpu.CompilerParams(dimension_semantics=("parallel", "parallel", "arbitrary")),
        name="swa_attn",
    )(qadd.astype(BF16), slope_rows, sink_rows, h, h, h, _key_table(T))


def _dispatch_kernel(zfrom_ref, zto_ref, nu_ref, pos_ref, src_ref, dst_hbm, zbuf, sem, zsem,
                     *, td, n_seg, rb, n_blk):
    @pl.when(pl.program_id(0) == 0)
    def _():
        zbuf[...] = jnp.zeros_like(zbuf)

        def zero_row(r):
            return pltpu.make_async_copy(zbuf.at[pl.ds(0, 1)], dst_hbm.at[pl.ds(r, 1)], zsem)

        def zero_block(b):
            return pltpu.make_async_copy(zbuf, dst_hbm.at[pl.ds(pl.multiple_of(b * rb, rb), rb)], zsem)

        def seg(s, c):
            lax.fori_loop(zfrom_ref[s], zto_ref[s], lambda r, cc: (zero_row(r).start(), cc)[1], 0)
            return c

        def seg_wait(s, c):
            lax.fori_loop(zfrom_ref[s], zto_ref[s], lambda r, cc: (zero_row(r).wait(), cc)[1], 0)
            return c

        lax.fori_loop(0, n_seg, seg, 0)
        lax.fori_loop(nu_ref[0], n_blk, lambda b, cc: (zero_block(b).start(), cc)[1], 0)
        lax.fori_loop(0, n_seg, seg_wait, 0)
        lax.fori_loop(nu_ref[0], n_blk, lambda b, cc: (zero_block(b).wait(), cc)[1], 0)

    def row_copy(t, k):
        return pltpu.make_async_copy(src_ref.at[pl.ds(t, 1)],
                                     dst_hbm.at[pl.ds(pos_ref[0, t * TOP_K + k], 1)], sem)

    def issue(t, c):
        for k in range(TOP_K):
            row_copy(t, k).start(priority=k % 2)
        return c

    lax.fori_loop(0, td, issue, 0)
    for k in range(TOP_K):
        pltpu.make_async_copy(src_ref, dst_hbm.at[pl.ds(0, td)], sem).wait()


def moe_dispatch(xm_packed, pos, zfrom, zto, n_used, cap, td=128, rb=MOE_ROW_BLOCK):
    N, W = xm_packed.shape
    td = _tile(N, td)
    pos3 = pos.reshape(N // td, 1, td * TOP_K)
    grid_spec = pltpu.PrefetchScalarGridSpec(
        num_scalar_prefetch=3,
        grid=(N // td,),
        in_specs=[pl.BlockSpec((None, 1, td * TOP_K), lambda i, *_: (i, 0, 0), memory_space=pltpu.SMEM),
                  pl.BlockSpec((td, W), lambda i, *_: (i, 0))],
        out_specs=pl.BlockSpec(memory_space=pl.ANY),
        scratch_shapes=[pltpu.VMEM((rb, W), jnp.int32), pltpu.SemaphoreType.DMA(()),
                        pltpu.SemaphoreType.DMA(())],
    )
    return pl.pallas_call(
        functools.partial(_dispatch_kernel, td=td, n_seg=zfrom.shape[0], rb=rb, n_blk=cap // rb),
        out_shape=jax.ShapeDtypeStruct((cap, W), jnp.int32),
        grid_spec=grid_spec,
        compiler_params=pltpu.CompilerParams(dimension_semantics=("arbitrary",), has_side_effects=True),
        name="moe_dispatch",
    )(zfrom, zto, n_used, pos3, xm_packed)


def _expert_kernel(be_ref, slot_ref, nxt_ref, c0_ref, c1_ref, nu_ref, xs_ref, wg_hbm, bg_ref, wu_hbm, bu_ref,
                   wd_hbm, bd_ref, o_ref, wg_s, wu_s, wd_s, stg_a, stg_d, sem, *, ca, cd, ring, layer):
    blk = pl.program_id(0)
    e = be_ref[blk]
    s = slot_ref[blk]
    used = blk < nu_ref[0]
    _, D, F = wg_s.shape
    na, nd = D // ca, F // cd
    n_chunks = 2 * na + nd

    def on_chunk(c, expert, fn):
        k = c % ring

        def go(src, dst, r0, n, stg, sem0):
            copy = pltpu.make_async_copy(src.at[layer, expert, pl.ds(r0, n), :], stg.at[k], sem.at[sem0 + k])
            fn(copy, dst, r0, n, stg.at[k])

        @pl.when(c < na)
        def _():
            go(wg_hbm, wg_s, pl.multiple_of(c * ca, ca), ca, stg_a, 0)

        @pl.when((c >= na) & (c < 2 * na))
        def _():
            go(wu_hbm, wu_s, pl.multiple_of((c - na) * ca, ca), ca, stg_a, 0)

        @pl.when(c >= 2 * na)
        def _():
            go(wd_hbm, wd_s, pl.multiple_of((c - 2 * na) * cd, cd), cd, stg_d, ring)

    def start(c, expert):
        on_chunk(c, expert, lambda copy, dst, r0, n, stg: copy.start())

    def finish(c, expert, dst_slot):
        def fn(copy, dst, r0, n, stg):
            copy.wait()
            dst[dst_slot, pl.ds(r0, n), :] = stg[...].astype(BF16)
        on_chunk(c, expert, fn)

    def prestart(lo, hi, expert):
        for i in range(ring):
            @pl.when(lo + i < hi)
            def _():
                start(lo + i, expert)

    def drain(lo, hi, end, expert, dst_slot):
        def body(c, carry):
            finish(c, expert, dst_slot)

            @pl.when(c + ring < end)
            def _():
                start(c + ring, expert)
            return carry

        lax.fori_loop(lo, hi, body, 0)

    @pl.when(used & (blk == 0))
    def _():
        prestart(0, n_chunks, e)
        drain(0, n_chunks, n_chunks, e, s)

    c0, c1, nxt = c0_ref[blk], c1_ref[blk], nxt_ref[blk]

    @pl.when(used)
    def _():
        prestart(c0, c1, nxt)
        x = _unpack_bf16_pairs(xs_ref[...])
        gg = jnp.dot(x, wg_s[s], preferred_element_type=F32) + bg_ref[...]
        uu = jnp.dot(x, wu_s[s], preferred_element_type=F32) + bu_ref[...]
        gg = jnp.minimum(gg, SWIGLU_LIMIT)
        uu = jnp.clip(uu, -SWIGLU_LIMIT, SWIGLU_LIMIT)
        act = gg * jax.nn.sigmoid(SWIGLU_ALPHA * gg) * (uu + 1.0)
        y = jnp.dot(act.astype(BF16), wd_s[s], preferred_element_type=F32) + bd_ref[...]
        o_ref[...] = _pack_bf16_pairs(y)
        drain(c0, c1, c1, nxt, 1 - s)

    @pl.when(jnp.logical_not(used))
    def _():
        o_ref[...] = jnp.zeros_like(o_ref)


EXPERT_STAGE_RING = 6
VMEM_LIMIT_EXPERTS = 60 * 1024 * 1024


def _expert_schedule(blk_e, n_used, n_chunks):
    n_blk = blk_e.shape[0]
    idx = jnp.arange(n_blk, dtype=jnp.int32)
    first = jnp.concatenate([jnp.ones((1,), jnp.int32), (blk_e[1:] != blk_e[:-1]).astype(jnp.int32)])
    slot = (jnp.cumsum(first) - 1) % 2
    run_start = lax.cummax(jnp.where(first == 1, idx, 0))
    starts_after = jnp.concatenate([jnp.where(first == 1, idx, n_blk)[1:], jnp.full((1,), n_blk, jnp.int32)])
    next_start = lax.cummin(starts_after, reverse=True)
    has_next = next_start < n_used[0]
    nxt = blk_e[jnp.minimum(next_start, n_blk - 1)]
    run_len = jnp.maximum(jnp.minimum(next_start, n_used[0]) - run_start, 1)
    j = idx - run_start
    c0 = jnp.where(has_next, j * n_chunks // run_len, 0)
    c1 = jnp.where(has_next, (j + 1) * n_chunks // run_len, 0)
    as_i32 = lambda v: v.astype(jnp.int32)
    return as_i32(slot), as_i32(nxt), as_i32(c0), as_i32(c1)


def moe_experts(xs, blk_e, n_used, layer, wg, bg, wu, bu, wd, bd, rb=MOE_ROW_BLOCK):
    cap, W = xs.shape
    depth, E, D, F = wg.shape
    n_blk = cap // rb
    ca, cd = _tile(D, 256), _tile(F, 64)
    ring = EXPERT_STAGE_RING
    slot, nxt, c0, c1 = _expert_schedule(blk_e, n_used, 2 * (D // ca) + F // cd)
    by_expert = lambda shape: pl.BlockSpec((None,) + shape,
                                           lambda i, be, *_: (layer * E + be[i], 0, 0))
    grid_spec = pltpu.PrefetchScalarGridSpec(
        num_scalar_prefetch=6,
        grid=(n_blk,),
        in_specs=[pl.BlockSpec((rb, W), lambda i, *sp: (jnp.minimum(i, sp[-1][0] - 1), 0)),
                  pl.BlockSpec(memory_space=pl.ANY), by_expert((1, F)),
                  pl.BlockSpec(memory_space=pl.ANY), by_expert((1, F)),
                  pl.BlockSpec(memory_space=pl.ANY), by_expert((1, D))],
        out_specs=pl.BlockSpec((rb, W), lambda i, *_: (i, 0)),
        scratch_shapes=[pltpu.VMEM((2, D, F), BF16), pltpu.VMEM((2, D, F), BF16), pltpu.VMEM((2, F, D), BF16),
                        pltpu.VMEM((ring, ca, F), F32), pltpu.VMEM((ring, cd, D), F32),
                        pltpu.SemaphoreType.DMA((2 * ring,))],
    )
    return pl.pallas_call(
        functools.partial(_expert_kernel, ca=ca, cd=cd, ring=ring, layer=layer),
        out_shape=jax.ShapeDtypeStruct((cap, W), jnp.int32),
        grid_spec=grid_spec,
        compiler_params=pltpu.CompilerParams(dimension_semantics=("arbitrary",),
                                             vmem_limit_bytes=VMEM_LIMIT_EXPERTS),
        name="moe_experts",
    )(blk_e, slot, nxt, c0, c1, n_used, xs, wg, bg.reshape(depth * E, 1, F), wu, bu.reshape(depth * E, 1, F),
      wd, bd.reshape(depth * E, 1, D))


def _moe_combine_kernel(pos_ref, pos_next_ref, y_hbm, x_ref, w_ref, gate_ref, o_ref, ybuf, sem,
                        *, tt, n_steps):
    i = pl.program_id(0)
    slot = i % 2

    def row_copy(p_ref, t, k, sl):
        return pltpu.make_async_copy(y_hbm.at[pl.ds(p_ref[0, t * TOP_K + k], 1)],
                                     ybuf.at[sl, k, pl.ds(t, 1)], sem.at[sl])

    def issue(p_ref, sl):
        def body(t, c):
            for k in range(TOP_K):
                row_copy(p_ref, t, k, sl).start(priority=k % 2)
            return c
        lax.fori_loop(0, tt, body, 0)

    @pl.when(i == 0)
    def _():
        issue(pos_ref, 0)

    @pl.when(i + 1 < n_steps)
    def _():
        issue(pos_next_ref, 1 - slot)

    for k in range(TOP_K):
        pltpu.make_async_copy(y_hbm.at[pl.ds(0, tt)], ybuf.at[slot, k], sem.at[slot]).wait()
    w = w_ref[...]
    half = x_ref.shape[1] // 2
    y_lo = y_hi = None
    for k in range(TOP_K):
        u = ybuf[slot, k]
        wk = w[:, k:k + 1]
        lo = wk * lax.bitcast_convert_type(lax.shift_left(u, 16), F32)
        hi = wk * lax.bitcast_convert_type(u & jnp.int32(-65536), F32)
        y_lo = lo if y_lo is None else y_lo + lo
        y_hi = hi if y_hi is None else y_hi + hi
    o_ref[:, :half] = x_ref[:, :half] + gate_ref[:, :half] * y_lo
    o_ref[:, half:] = x_ref[:, half:] + gate_ref[:, half:] * y_hi


def moe_combine(yb, pos, top_w, x2d, gate, rows_per_batch, tt=64):
    N, D = x2d.shape
    tt = _tile(rows_per_batch, tt)
    bpb = rows_per_batch // tt
    n_steps = N // tt
    pos3 = pos.reshape(n_steps, 1, tt * TOP_K)
    pos_spec = lambda f: pl.BlockSpec((None, 1, tt * TOP_K), f, memory_space=pltpu.SMEM)
    return pl.pallas_call(
        functools.partial(_moe_combine_kernel, tt=tt, n_steps=n_steps),
        out_shape=jax.ShapeDtypeStruct((N, D), F32),
        grid=(n_steps,),
        in_specs=[pos_spec(lambda i: (i, 0, 0)),
                  pos_spec(lambda i: (jnp.minimum(i + 1, n_steps - 1), 0, 0)),
                  pl.BlockSpec(memory_space=pl.ANY),
                  pl.BlockSpec((tt, D), lambda i: (i, 0)),
                  pl.BlockSpec((tt, LANES), lambda i: (i, 0)),
                  pl.BlockSpec((None, 1, D), lambda i: (i // bpb, 0, 0))],
        out_specs=pl.BlockSpec((tt, D), lambda i: (i, 0)),
        scratch_shapes=[pltpu.VMEM((2, TOP_K, tt, D // 2), jnp.int32), pltpu.SemaphoreType.DMA((2,))],
        compiler_params=pltpu.CompilerParams(dimension_semantics=("arbitrary",)),
        name="moe_combine",
    )(pos3, pos3, yb, x2d, top_w, gate)


def moe_layer(x2d, gain, sc, sh, gate, rows_per_batch, router_w, router_b, layer, wg, bg, wu, bu, wd, bd):
    N, D = x2d.shape
    RB = MOE_ROW_BLOCK
    E = N_EXPERTS
    xm_packed, top_e, top_w, rank, counts = norm_mod(x2d, gain, sc, sh, rows_per_batch,
                                                     router=(router_w, router_b))
    nk = N * TOP_K
    counts = counts[0, :E]
    padded = (counts + RB - 1) // RB * RB
    pad_end = jnp.cumsum(padded)
    pad_start = pad_end - padded
    e4 = top_e[:, :TOP_K]
    start_of = jnp.sum(jnp.where(e4[:, :, None] == jnp.arange(E)[None, None, :],
                                 pad_start[None, None, :], 0), axis=2)
    pos = (start_of + rank[:, :TOP_K]).astype(jnp.int32)
    cap = (-(-nk // RB)) * RB + E * RB
    n_blk = cap // RB
    blk_start = jnp.arange(n_blk, dtype=jnp.int32) * RB
    blk_e = jnp.minimum(jnp.sum(pad_end[None, :] <= blk_start[:, None], axis=1), E - 1).astype(jnp.int32)
    n_used = (pad_end[-1] // RB).astype(jnp.int32).reshape(1)
    zfrom = (pad_start + counts).astype(jnp.int32)
    zto = pad_end.astype(jnp.int32)
    xs = moe_dispatch(xm_packed, pos, zfrom, zto, n_used, cap)
    yb = moe_experts(xs, blk_e, n_used, layer, wg, bg, wu, bu, wd, bd)
    return moe_combine(yb, pos, top_w, x2d, gate, rows_per_batch)


def _pad_heads(w, n_heads, d, dp):
    lead = w.shape[:-1]
    w = w.reshape(lead + (n_heads, d))
    w = jnp.pad(w, [(0, 0)] * len(lead) + [(0, 0), (0, dp - d)])
    return w.reshape(lead + (n_heads * dp,))


def _even_in_weights(w_in, q_gain, k_gain):
    D = w_in.shape[0]
    cuts = np.cumsum(EVEN_IN_SPLITS)[:-1].tolist()
    q, kc, vc, ks, vs, kw, vw, gates, cq, kva = jnp.split(w_in.astype(BF16), cuts, axis=1)
    half = MLA_ROPE_DIM // 2
    kr = kva[:, MLA_KV_RANK:]
    kr_perm = jnp.concatenate([kr[:, half:], kr[:, :half]], axis=1)
    G = NSA_KV_HEADS
    cols = [_pad_heads(q, NSA_HEADS, NSA_QK_DIM, HEAD_PAD),
            _pad_heads(kc, G, NSA_QK_DIM, HEAD_PAD), _pad_heads(ks, G, NSA_QK_DIM, HEAD_PAD),
            _pad_heads(kw, G, NSA_QK_DIM, HEAD_PAD), cq, kva[:, :MLA_KV_RANK], kr, kr_perm,
            vc, vs, vw, gates]
    cols.append(jnp.zeros((D, EV_WIDTH - EV_END + LANES - gates.shape[1]), BF16))
    w = jnp.concatenate(cols, axis=1)
    assert w.shape[1] == EV_WIDTH
    scale = NSA_QK_DIM ** -0.5
    pad_g = lambda g: jnp.pad(g, (0, HEAD_PAD - NSA_QK_DIM))
    gain = jnp.ones((EV_WIDTH,), F32)
    gain = gain.at[EV_Q:EV_Q + _NQ].set(jnp.tile(pad_g(q_gain * scale), NSA_HEADS))
    gain = gain.at[EV_KS:EV_KS + _NK].set(jnp.tile(pad_g(k_gain[1]), G))
    gain = gain.at[EV_KW:EV_KW + _NK].set(jnp.tile(pad_g(k_gain[2]), G))
    col = np.arange(EV_WIDTH)
    flag = ((col < EV_Q + _NQ) | ((col >= EV_KS) & (col < EV_KW + _NK))).astype(np.float32)
    return w, gain, jnp.asarray(flag)


def _even_mixer(xm, x2d, g_a, B, T, w_in, w_out, q_gain, k_gain, pe_k, pe_v, w_ck1, w_ck2, w_cv1, w_cv2,
                g_cq, g_ckv, w_uq, w_ukv, mq_gain, mk_gain):
    N, D = x2d.shape
    G, R = NSA_KV_HEADS, NSA_HEADS // NSA_KV_HEADS
    w_p, gain, flag = _even_in_weights(w_in, q_gain, k_gain)
    h = matmul(xm, w_p, head_norm=(HEAD_PAD, NSA_QK_DIM, gain, flag), tn=768, name="in_proj_even")

    kc = compress(h, EV_KC, B, T, pe_k, w_ck1, w_ck2, k_gain[0], NSA_QK_DIM, HEAD_PAD)
    vc = compress(h, EV_VC, B, T, pe_v, w_cv1, w_cv2, None, NSA_V_DIM, NSA_V_DIM)

    o_cmp, bits = cmp_attention(h, kc, vc, B, T)
    HV = NSA_HEADS * NSA_V_DIM
    o_slc = flash_attention(h, EV_Q, h, EV_KS, h, EV_VS, B=B, T=T, G=G, R=R, out_cols=HV,
                            n_alibi_heads=NSA_HEADS, bits=bits, tq=512, tk=512, name="nsa_slc_attn")
    o_win = flash_attention(h, EV_Q, h, EV_KW, h, EV_VW, B=B, T=T, G=G, R=R, out_cols=HV,
                            window=NSA_WINDOW, n_alibi_heads=NSA_HEADS, name="nsa_win_attn")
    o_a = nsa_combine(h, o_cmp, o_slc, o_win)

    H = MLA_HEADS
    dqk = MLA_NOPE_DIM + MLA_ROPE_DIM
    half = MLA_ROPE_DIM // 2
    wq = w_uq.reshape(MLA_Q_RANK, H, dqk)
    wq_rope = wq[:, :, MLA_NOPE_DIM:]
    wq_p = jnp.concatenate([wq, wq_rope[:, :, half:], wq_rope[:, :, :half]], axis=2)
    wq_p = wq_p.reshape(MLA_Q_RANK, H * HEAD_PAD).astype(BF16)
    q_raw = matmul(h, wq_p, a_col0=EV_CQ, a_pro="rms", a_gain=g_cq, tn=1024, name="mla_q_up")
    kv_raw = matmul(h, w_ukv.astype(BF16), a_col0=EV_CKV, a_pro="rms", a_gain=g_ckv, tn=1024,
                    name="mla_kv_up")
    qa_lo, qa_hi, qb_hi = _rope_tables(mq_gain, T, dqk ** -0.5)
    ka_lo, ka_hi, kb_hi = _rope_tables(mk_gain, T, 1.0)
    q_m = mla_prep(q_raw, None, 0, qa_lo, qa_hi, qb_hi, T)
    k_m = mla_prep(kv_raw, h, EV_KR, ka_lo, ka_hi, kb_hi, T)
    hp = 4
    kv_w = MLA_NOPE_DIM + MLA_V_DIM
    o_b = flash_attention(q_m, 0, k_m, 0, kv_raw, 0, B=B, T=T, G=H, R=1, HP=hp, v_width=hp * kv_w,
                          v_off=MLA_NOPE_DIM, v_step=kv_w, out_cols=H * MLA_V_DIM, tq=512, tk=512,
                          name="mla_attn")
    return matmul(o_a, w_out.astype(BF16), a2=o_b, resid=(x2d, g_a, T), out_dtype=F32, name="out_proj_even")


def _odd_mixer(xm, x2d, g_a, B, T, w_in, b_in, w_out, b_out, q_gain, k_gain, sinks):
    G, hd = SWA_KV_HEADS, SWA_HEAD_DIM
    nq = SWA_HEADS * hd
    kw = G * hd
    gain = jnp.concatenate([jnp.tile(q_gain * hd ** -0.5, SWA_HEADS), jnp.tile(k_gain, G),
                            jnp.ones((kw,), F32)])
    flag = jnp.concatenate([jnp.ones((nq + kw,), F32), jnp.zeros((kw,), F32)])
    h = matmul(xm, w_in.astype(BF16), bias=b_in, head_norm=(hd, hd, gain, flag), tn=1024,
               name="in_proj_odd")
    o_c = swa_attention(h, sinks, B, T, nq, nq + kw)
    return matmul(o_c, w_out.astype(BF16), bias=b_out, resid=(x2d, g_a, T), out_dtype=F32,
                  name="out_proj_odd")


def kernel(x, c, w_mod, mod_table, norm_attn, norm_ffn, w_in_even, w_out_even, nsa_q_gain, nsa_k_gain,
           nsa_pe_k, nsa_pe_v, nsa_w_ck1, nsa_w_ck2, nsa_w_cv1, nsa_w_cv2, mla_g_cq, mla_g_ckv, mla_w_uq,
           mla_w_ukv, mla_q_gain, mla_k_gain, w_in_odd, b_in_odd, w_out_odd, b_out_odd, swa_q_gain,
           swa_k_gain, swa_sinks, router_w, router_b, moe_w_gate, moe_b_gate, moe_w_up, moe_b_up,
           moe_w_down, moe_b_down):
    B, T, D = x.shape
    N = B * T
    depth = mod_table.shape[0]
    c_pad = jnp.pad(c, ((0, 8 - B % 8 if B % 8 else 0), (0, 0)))
    cond = matmul(c_pad, w_mod, a_pro="silu", out_dtype=F32, tn=1024, tk=2048, name="adaln_proj")[:B]
    x2d = x.reshape(N, D)
    for layer in range(depth):
        mod = (cond + mod_table[layer]).reshape(B, 6, 1, D)
        sh_a, sc_a, g_a, sh_f, sc_f, g_f = (mod[:, j] for j in range(6))
        xm = norm_mod(x2d, norm_attn[layer], sc_a, sh_a, T)
        i = layer // 2
        if layer % 2 == 0:
            x2d = _even_mixer(xm, x2d, g_a, B, T, w_in_even[i], w_out_even[i], nsa_q_gain[i], nsa_k_gain[i],
                              nsa_pe_k[i], nsa_pe_v[i], nsa_w_ck1[i], nsa_w_ck2[i], nsa_w_cv1[i],
                              nsa_w_cv2[i], mla_g_cq[i], mla_g_ckv[i], mla_w_uq[i], mla_w_ukv[i],
                              mla_q_gain[i], mla_k_gain[i])
        else:
            x2d = _odd_mixer(xm, x2d, g_a, B, T, w_in_odd[i], b_in_odd[i], w_out_odd[i], b_out_odd[i],
                             swa_q_gain[i], swa_k_gain[i], swa_sinks[i])
        x2d = moe_layer(x2d, norm_ffn[layer], sc_f, sh_f, g_f, T, router_w[layer], router_b[layer],
                        layer, moe_w_gate, moe_b_gate, moe_w_up, moe_b_up, moe_w_down, moe_b_down)
    return x2d.reshape(B, T, D)
```
